```python
import math
import jax, jax.numpy as jnp
from jax import lax
import numpy as np

D_MODEL = 1024
BATCH = 8
SEQ = 4096
DEPTH = 1

HEAD_DIM = 64
FOX_HEADS = 8
NSA_HEADS = 8
NSA_KV_GROUPS = 2
NSA_HPG = NSA_HEADS // NSA_KV_GROUPS
FOX_WIDTH = FOX_HEADS * HEAD_DIM
NSA_WIDTH = NSA_HEADS * HEAD_DIM
NSA_KV_WIDTH = NSA_KV_GROUPS * HEAD_DIM
CMP_BLOCK = 32
CMP_STRIDE = 16
SEL_BLOCK = 64
SEL_TOPN = 16
WINDOW = 512
FOX_Q_BLOCK = 128
NSA_Q_BLOCK = 64
REL_BUCKETS = 32
REL_MAX_DIST = 128
N_EXPERTS = 32
TOP_K = 4
D_EXPERT = D_MODEL
SWIGLU_LIMIT = 7.0
SWIGLU_ALPHA = 1.702
MOE_BLOCK = 256
RMS_EPS = 1e-6
NEG = -1e30
BIG = 1e9
IN_SIZES = (FOX_WIDTH, FOX_WIDTH, FOX_WIDTH, FOX_HEADS, NSA_WIDTH,
            NSA_KV_WIDTH, NSA_KV_WIDTH, NSA_KV_WIDTH, NSA_KV_WIDTH, NSA_KV_WIDTH, NSA_KV_WIDTH,
            3 * NSA_HEADS, D_MODEL, D_MODEL)
IN_WIDTH = sum(IN_SIZES)

kernel_name = "hybrid_fox_nsa_moe_block"


def rms_norm(x, g):
    xf = x.astype(jnp.float32)
    y = xf * lax.rsqrt(jnp.mean(xf * xf, axis=-1, keepdims=True) + RMS_EPS)
    return (y * g.astype(jnp.float32)).astype(x.dtype)


def t5_bucket(dist):
    n = jnp.maximum(dist, 0)
    max_exact = REL_BUCKETS // 2
    nf = jnp.maximum(n, 1).astype(jnp.float32)
    large = max_exact + (jnp.log(nf / max_exact) / math.log(REL_MAX_DIST / max_exact)
                         * (REL_BUCKETS - max_exact)).astype(jnp.int32)
    large = jnp.minimum(large, REL_BUCKETS - 1)
    return jnp.where(n < max_exact, n, large)


def fox_attention(q, k, v, log_f):
    B, S, H, dh = q.shape
    scale = dh ** -0.5
    cum = jnp.cumsum(log_f, axis=1).transpose(0, 2, 1)
    qh = q.transpose(0, 2, 1, 3)
    kh = k.transpose(0, 2, 1, 3)
    vh = v.transpose(0, 2, 1, 3)
    kpos = jnp.arange(S)

    def block(b):
        q0 = b * FOX_Q_BLOCK
        qb = lax.dynamic_slice_in_dim(qh, q0, FOX_Q_BLOCK, axis=2)
        cb = lax.dynamic_slice_in_dim(cum, q0, FOX_Q_BLOCK, axis=2)
        tpos = q0 + jnp.arange(FOX_Q_BLOCK)
        logits = (jnp.einsum('bhqd,bhkd->bhqk', qb, kh).astype(jnp.float32) * scale
                  + (cb[..., :, None] - cum[..., None, :]))
        logits = jnp.where(kpos[None, :] <= tpos[:, None], logits, NEG)
        p = jax.nn.softmax(logits, axis=-1)
        return jnp.einsum('bhqk,bhkd->bhqd', p.astype(vh.dtype), vh)

    out = lax.map(block, jnp.arange(S // FOX_Q_BLOCK))
    return out.transpose(1, 0, 3, 2, 4).reshape(B, S, H * dh)


def nsa_attention(q, k_cmp, v_cmp, k_slc, v_slc, k_win, v_win, gates,
                  pe_k, pe_v, w_ck, w_cv, rel_bias):
    B, S, H, dh = q.shape
    G, hpg, Tq = NSA_KV_GROUPS, NSA_HPG, NSA_Q_BLOCK
    scale = dh ** -0.5
    n_cmp = (S - CMP_BLOCK) // CMP_STRIDE + 1
    cidx = jnp.arange(n_cmp)[:, None] * CMP_STRIDE + jnp.arange(CMP_BLOCK)[None, :]

    def compress(kv, pe, w):
        blk = kv[:, cidx] + pe[None, None, :, None, :]
        blk = blk.transpose(0, 3, 1, 2, 4).reshape(B, G, n_cmp, CMP_BLOCK * dh)
        return blk @ w

    kc = compress(k_cmp, pe_k, w_ck)
    vc = compress(v_cmp, pe_v, w_cv)
    cmp_start = jnp.arange(n_cmp) * CMP_STRIDE
    cmp_end = cmp_start + CMP_BLOCK - 1
    n_sel = S // SEL_BLOCK
    top_n = min(SEL_TOPN, n_sel)
    ks = k_slc.reshape(B, n_sel, SEL_BLOCK, G, dh).transpose(0, 3, 1, 2, 4)
    vs = v_slc.reshape(B, n_sel, SEL_BLOCK, G, dh).transpose(0, 3, 1, 2, 4)
    sel_start = jnp.arange(n_sel) * SEL_BLOCK
    overlap = ((cmp_start[:, None] < sel_start[None, :] + SEL_BLOCK)
               & (cmp_start[:, None] + CMP_BLOCK > sel_start[None, :])).astype(jnp.float32)
    kw = jnp.pad(k_win, ((0, 0), (WINDOW, 0), (0, 0), (0, 0))).transpose(0, 2, 1, 3)
    vw = jnp.pad(v_win, ((0, 0), (WINDOW, 0), (0, 0), (0, 0))).transpose(0, 2, 1, 3)
    wk = WINDOW + Tq
    rel_d = (jnp.arange(Tq)[:, None] + WINDOW) - jnp.arange(wk)[None, :]
    win_band = (rel_d >= 0) & (rel_d < WINDOW)
    win_bias = rel_bias[t5_bucket(rel_d)].transpose(2, 0, 1).reshape(G, hpg, Tq, wk)
    tbl = rel_bias.reshape(REL_BUCKETS, G, hpg)
    qg = q.reshape(B, S, G, hpg, dh).transpose(0, 2, 3, 1, 4)
    gts = gates.reshape(B, S, G, hpg, 3).transpose(0, 2, 3, 1, 4)
    gather_blocks = jax.vmap(jax.vmap(lambda kv, ix: kv[ix]))
    garr = jnp.arange(G)[None, :, None, None, None]
    jblk = jnp.arange(n_sel)

    def block(b):
        q0 = b * Tq
        t = q0 + jnp.arange(Tq)
        qb = lax.dynamic_slice_in_dim(qg, q0, Tq, axis=3)
        cvalid = cmp_end[None, :] <= t[:, None]
        cbias = tbl[t5_bucket(t[:, None] - cmp_end[None, :])].transpose(2, 3, 0, 1)
        lc = jnp.einsum('bghtd,bgcd->bghtc', qb, kc).astype(jnp.float32) * scale + cbias
        pc = jax.nn.softmax(jnp.where(cvalid, lc, NEG), axis=-1) * cvalid
        o_cmp = jnp.einsum('bghtc,bgcd->bghtd', pc.astype(vc.dtype), vc)
        imp = jnp.einsum('bghtc,cj->bgtj', pc, overlap)
        cur = t // SEL_BLOCK
        forced = (jblk[None, :] == 0) | (jblk[None, :] == cur[:, None]) | (jblk[None, :] == cur[:, None] - 1)
        svalid = jblk[None, :] * SEL_BLOCK <= t[:, None]
        score = jnp.where(forced, BIG, jnp.where(svalid, imp, -BIG))
        _, sel = lax.top_k(score, top_n)
        kb = gather_blocks(ks, sel)
        vb = gather_blocks(vs, sel).reshape(B, G, Tq, top_n * SEL_BLOCK, dh)
        pos = sel[..., None] * SEL_BLOCK + jnp.arange(SEL_BLOCK)
        dist = t[None, None, :, None, None] - pos
        sbias = jnp.moveaxis(tbl[t5_bucket(dist), garr], -1, 2).reshape(B, G, hpg, Tq, top_n * SEL_BLOCK)
        ls = (jnp.einsum('bghtd,bgtnkd->bghtnk', qb, kb).reshape(B, G, hpg, Tq, top_n * SEL_BLOCK)
              .astype(jnp.float32) * scale + sbias)
        smask = (dist >= 0).reshape(B, G, Tq, top_n * SEL_BLOCK)[:, :, None]
        ps = jax.nn.softmax(jnp.where(smask, ls, NEG), axis=-1)
        o_slc = jnp.einsum('bghtm,bgtmd->bghtd', ps.astype(vb.dtype), vb)
        kwb = lax.dynamic_slice_in_dim(kw, q0, wk, axis=2)
        vwb = lax.dynamic_slice_in_dim(vw, q0, wk, axis=2)
        kpos = q0 - WINDOW + jnp.arange(wk)
        wmask = win_band & (kpos[None, :] >= 0)
        lw = jnp.einsum('bghtd,bgkd->bghtk', qb, kwb).astype(jnp.float32) * scale + win_bias
        pw = jax.nn.softmax(jnp.where(wmask, lw, NEG), axis=-1)
        o_win = jnp.einsum('bghtk,bgkd->bghtd', pw.astype(vwb.dtype), vwb)
        g = lax.dynamic_slice_in_dim(gts, q0, Tq, axis=3)
        return g[..., 0:1] * o_cmp + g[..., 1:2] * o_slc + g[..., 2:3] * o_win

    out = lax.map(block, jnp.arange(S // Tq))
    return out.transpose(1, 0, 4, 2, 3, 5).reshape(B, S, H * dh)


def moe_ffn(h, w_router, b_router, w_gate, b_gate, w_up, b_up, w_down, b_down):
    B, S, D = h.shape
    xt = h.reshape(-1, D)
    T = xt.shape[0]
    logits = (xt @ w_router).astype(jnp.float32) + b_router
    top_v, top_i = lax.top_k(logits, TOP_K)
    top_w = jax.nn.softmax(top_v, axis=-1)
    A = T * TOP_K
    flat_e = top_i.reshape(-1)
    order = jnp.argsort(flat_e)
    sorted_e = flat_e[order]
    token_of = order // TOP_K
    counts = jax.ops.segment_sum(jnp.ones_like(flat_e), flat_e, num_segments=N_EXPERTS)
    padded = ((counts + MOE_BLOCK - 1) // MOE_BLOCK) * MOE_BLOCK
    pad_end = jnp.cumsum(padded)
    pad_start = pad_end - padded
    grp_start = jnp.cumsum(counts) - counts
    dest = pad_start[sorted_e] + jnp.arange(A) - grp_start[sorted_e]
    n_blocks = -(-A // MOE_BLOCK) + N_EXPERTS
    P = n_blocks * MOE_BLOCK
    xs = jnp.zeros((P, D), h.dtype).at[dest].set(xt[token_of])
    blk_e = jnp.minimum(jnp.searchsorted(pad_end, jnp.arange(n_blocks) * MOE_BLOCK, side='right'),
                        N_EXPERTS - 1)

    def expert_block(args):
        xb, e = args
        gt = jnp.minimum(xb @ w_gate[e] + b_gate[e], SWIGLU_LIMIT)
        up = jnp.clip(xb @ w_up[e] + b_up[e], -SWIGLU_LIMIT, SWIGLU_LIMIT)
        glu = gt * jax.nn.sigmoid(SWIGLU_ALPHA * gt)
        return (glu * (up + 1.0)) @ w_down[e] + b_down[e]

    ys = lax.map(expert_block, (xs.reshape(n_blocks, MOE_BLOCK, D), blk_e)).reshape(P, D)
    w_sorted = top_w.reshape(-1)[order]
    contrib = ys[dest] * w_sorted[:, None].astype(ys.dtype)
    return jax.ops.segment_sum(contrib, token_of, num_segments=T).reshape(B, S, D)


def setup_inputs(seed: int = 0) -> dict:
    key = jax.random.key(seed)
    ks = jax.random.split(key, 32)
    f32 = jnp.float32
    L, D, E, F = DEPTH, D_MODEL, N_EXPERTS, D_EXPERT

    def nrm(k, shape, s):
        return jax.random.normal(k, shape, f32) * s

    return {
        "x": nrm(ks[0], (BATCH, SEQ, D), 1.0),
        "c": nrm(ks[1], (BATCH, D), 1.0),
        "w_ada": nrm(ks[2], (L, D, 6 * D), 0.5 * D ** -0.5),
        "b_ada": nrm(ks[3], (L, 6 * D), 0.02),
        "g_mix_pre": 1.0 + nrm(ks[4], (L, D), 0.05),
        "g_mix_post": 1.0 + nrm(ks[5], (L, D), 0.05),
        "w_in": nrm(ks[6], (L, D, IN_WIDTH), D ** -0.5),
        "b_forget": jax.random.uniform(ks[7], (L, FOX_HEADS), f32, 3.0, 6.0),
        "pe_k": nrm(ks[8], (L, CMP_BLOCK, HEAD_DIM), 0.5),
        "pe_v": nrm(ks[9], (L, CMP_BLOCK, HEAD_DIM), 0.5),
        "w_cmp_k": nrm(ks[10], (L, CMP_BLOCK * HEAD_DIM, HEAD_DIM), (CMP_BLOCK * HEAD_DIM) ** -0.5),
        "w_cmp_v": nrm(ks[11], (L, CMP_BLOCK * HEAD_DIM, HEAD_DIM), (CMP_BLOCK * HEAD_DIM) ** -0.5),
        "w_fox_proj": nrm(ks[12], (L, FOX_WIDTH, D), FOX_WIDTH ** -0.5),
        "w_nsa_proj": nrm(ks[13], (L, NSA_WIDTH, D), NSA_WIDTH ** -0.5),
        "w_mix_out": nrm(ks[14], (L, D, D), D ** -0.5),
        "rel_bias": nrm(ks[15], (REL_BUCKETS, NSA_HEADS), 0.5),
        "g_ffn_pre": 1.0 + nrm(ks[16], (L, D), 0.05),
        "g_ffn_post": 1.0 + nrm(ks[17], (L, D), 0.05),
        "w_router": nrm(ks[18], (L, D, E), D ** -0.5),
        "b_router": nrm(ks[19], (L, E), 0.01),
        "w_gate": nrm(ks[20], (L, E, D, F), D ** -0.5),
        "b_gate": nrm(ks[21], (L, E, F), 0.01),
        "w_up": nrm(ks[22], (L, E, D, F), D ** -0.5),
        "b_up": nrm(ks[23], (L, E, F), 0.01),
        "w_down": nrm(ks[24], (L, E, F, D), F ** -0.5),
        "b_down": nrm(ks[25], (L, E, D), 0.01),
    }


def reference(x, c, w_ada, b_ada, g_mix_pre, g_mix_post, w_in, b_forget, pe_k, pe_v,
              w_cmp_k, w_cmp_v, w_fox_proj, w_nsa_proj, w_mix_out, rel_bias,
              g_ffn_pre, g_ffn_post, w_router, b_router, w_gate, b_gate, w_up, b_up, w_down, b_down):
    B, S, D = x.shape
    offs = []
    acc = 0
    for s in IN_SIZES[:-1]:
        acc += s
        offs.append(acc)
    for l in range(DEPTH):
        ada = (jax.nn.silu(c) @ w_ada[l] + b_ada[l])[:, None, :]
        sh1, sc1, ga1, sh2, sc2, ga2 = jnp.split(ada, 6, axis=-1)
        h = rms_norm(x, g_mix_pre[l]) * (1.0 + sc1) + sh1
        proj = h @ w_in[l]
        (fq, fk, fv, ff, nq, kcm, vcm, ksl, vsl, kwn, vwn, ng, mg_fox, mg_nsa) = jnp.split(proj, offs, axis=-1)
        log_f = jax.nn.log_sigmoid(ff.astype(jnp.float32) + b_forget[l])
        hd = lambda t, n: t.reshape(B, S, n, HEAD_DIM)
        y_fox = fox_attention(hd(fq, FOX_HEADS), hd(fk, FOX_HEADS), hd(fv, FOX_HEADS), log_f) @ w_fox_proj[l]
        y_nsa = nsa_attention(hd(nq, NSA_HEADS), hd(kcm, NSA_KV_GROUPS), hd(vcm, NSA_KV_GROUPS),
                              hd(ksl, NSA_KV_GROUPS), hd(vsl, NSA_KV_GROUPS),
                              hd(kwn, NSA_KV_GROUPS), hd(vwn, NSA_KV_GROUPS),
                              jax.nn.sigmoid(ng).reshape(B, S, NSA_HEADS, 3),
                              pe_k[l], pe_v[l], w_cmp_k[l], w_cmp_v[l], rel_bias) @ w_nsa_proj[l]
        mixed = (jax.nn.sigmoid(mg_fox) * y_fox + jax.nn.sigmoid(mg_nsa) * y_nsa) @ w_mix_out[l]
        x = x + ga1 * rms_norm(mixed, g_mix_post[l])
        h = rms_norm(x, g_ffn_pre[l]) * (1.0 + sc2) + sh2
        y = moe_ffn(h, w_router[l], b_router[l], w_gate[l], b_gate[l], w_up[l], b_up[l], w_down[l], b_down[l])
        x = x + ga2 * rms_norm(y, g_ffn_post[l])
    return x
```

```python
import functools

import numpy as np
import jax
import jax.numpy as jnp
from jax import lax
from jax.experimental import pallas as pl
from jax.experimental.pallas import tpu as pltpu

f32 = jnp.float32
bf16 = jnp.bfloat16
i32 = jnp.int32

D_MODEL = 1024
HEAD_DIM = 64
FOX_HEADS = 8
NSA_HEADS = 8
NSA_KV_GROUPS = 2
NSA_HPG = NSA_HEADS // NSA_KV_GROUPS
FOX_WIDTH = FOX_HEADS * HEAD_DIM
NSA_WIDTH = NSA_HEADS * HEAD_DIM
NSA_KV_WIDTH = NSA_KV_GROUPS * HEAD_DIM
CMP_BLOCK = 32
CMP_STRIDE = 16
SEL_BLOCK = 64
SEL_TOPN = 16
WINDOW = 512
REL_BUCKETS = 32
REL_MAX_DIST = 128
N_EXPERTS = 32
TOP_K = 4
SWIGLU_LIMIT = 7.0
SWIGLU_ALPHA = 1.702
RMS_EPS = 1e-6
NEG = -1e30
BIG = 1e9
IN_SIZES = (FOX_WIDTH, FOX_WIDTH, FOX_WIDTH, FOX_HEADS, NSA_WIDTH,
            NSA_KV_WIDTH, NSA_KV_WIDTH, NSA_KV_WIDTH, NSA_KV_WIDTH, NSA_KV_WIDTH, NSA_KV_WIDTH,
            3 * NSA_HEADS, D_MODEL, D_MODEL)

LANES = 128
VMEM_LIMIT = 56 * 1024 * 1024
ATT_TILE = 256
ROW_TILE = 512
MOE_BM = 512
MOE_ROWS = 256
SEL_MASK = 1e9

_NT = (((1,), (1,)), ((), ()))


def _cparams(*sem):
    return pltpu.CompilerParams(dimension_semantics=sem, vmem_limit_bytes=VMEM_LIMIT)


def _const_spec(shape):
    nd = len(shape)
    return pl.BlockSpec(shape, lambda *_: (0,) * nd, pipeline_mode=pl.Buffered(1))


def _split3(a):
    a1 = a.astype(bf16)
    r1 = a - a1.astype(f32)
    a2 = r1.astype(bf16)
    a3 = (r1 - a2.astype(f32)).astype(bf16)
    return a1, a2, a3


def _rms(x, g):
    ms = jnp.mean(x * x, axis=-1, keepdims=True)
    return x * lax.rsqrt(ms + RMS_EPS) * g


def _ada_kernel(c_ref, w_ref, b_ref, o_ref):
    c = c_ref[...]
    s = c * jax.nn.sigmoid(c)
    w = w_ref[...]
    s1, s2, _ = _split3(s)
    w1, w2, _ = _split3(w)
    acc = jnp.dot(s1, w1, preferred_element_type=f32)
    acc += jnp.dot(s1, w2, preferred_element_type=f32)
    acc += jnp.dot(s2, w1, preferred_element_type=f32)
    o_ref[...] = acc + b_ref[...]


def _ada(c, w_ada, b_ada):
    B, D = c.shape
    N = w_ada.shape[1]
    tn = 1024
    return pl.pallas_call(
        _ada_kernel,
        out_shape=jax.ShapeDtypeStruct((B, N), f32),
        grid=(N // tn,),
        in_specs=[pl.BlockSpec((B, D), lambda j: (0, 0)),
                  pl.BlockSpec((D, tn), lambda j: (0, j)),
                  pl.BlockSpec((1, tn), lambda j: (0, j))],
        out_specs=pl.BlockSpec((B, tn), lambda j: (0, j)),
        compiler_params=_cparams("arbitrary"),
        name="ada",
    )(c, w_ada, b_ada.reshape(1, N))


def _inproj_kernel(x_ref, sc_ref, sh_ref, g_ref, bfg_ref, tri_ref, esel_ref,
                   wfq, wfk, wfv, wnq, wcm, wksl, wvsl, wkwn, wvwn, wmg, wsm,
                   bq, bv, bvs,
                   ofq, ofk, ofv, onq, ocm, oksl, ovsl, okwn, ovwn, omg, osm,
                   carry_ref, *, tm, tiles_per_seq):
    i = pl.program_id(0)
    x = x_ref[...]
    h = _rms(x, g_ref[...]) * (1.0 + sc_ref[0]) + sh_ref[0]
    hb = h.astype(bf16)

    def proj(w):
        return jnp.dot(hb, w[...], preferred_element_type=f32)

    ofq[...] = (proj(wfq) + bq[...]).astype(bf16)
    ofv[...] = (proj(wfv) + bv[...]).astype(bf16)
    onq[...] = proj(wnq).astype(bf16)
    ocm[...] = proj(wcm).astype(bf16)
    ovsl[...] = (proj(wvsl) + bvs[...]).astype(bf16)
    okwn[...] = proj(wkwn).astype(bf16)
    ovwn[...] = (proj(wvwn) + bvs[...]).astype(bf16)
    omg[...] = jax.nn.sigmoid(proj(wmg)).astype(bf16)

    row = lax.broadcasted_iota(i32, (tm, 2 * LANES), 0)
    lane = lax.broadcasted_iota(i32, (tm, 2 * LANES), 1)
    blk = ((i % tiles_per_seq) * tm + row) // SEL_BLOCK
    onehot = jnp.where((lane & (LANES - 1)) == blk, 1.0, 0.0)
    oksl[...] = (proj(wksl) + onehot).astype(bf16)

    sm = proj(wsm)
    osm[...] = sm
    z = sm + bfg_ref[...]
    lane1 = lax.broadcasted_iota(i32, (tm, LANES), 1)
    logf = jnp.where(lane1 < FOX_HEADS, jnp.minimum(z, 0.0) - jnp.log(1.0 + jnp.exp(-jnp.abs(z))), 0.0)

    @pl.when(i % tiles_per_seq == 0)
    def _():
        carry_ref[...] = jnp.zeros_like(carry_ref)

    tri = tri_ref[...]
    cum = carry_ref[0:1, :]
    for piece in _split3(logf):
        cum = cum + jnp.dot(tri, piece, preferred_element_type=f32)
    carry_ref[0:1, :] = cum[tm - 1:tm, :]
    ncat = jnp.concatenate(_split3(-cum), axis=1)
    ofk[...] = (proj(wfk) + jnp.dot(ncat, esel_ref[...], preferred_element_type=f32)).astype(bf16)


def _heads_to_lanes(w, lo):
    D = w.shape[0]
    nh = w.shape[1] // HEAD_DIM
    w3 = w.reshape(D, nh, HEAD_DIM)
    z = jnp.zeros_like(w3)
    return jnp.concatenate([w3, z] if lo else [z, w3], axis=2).reshape(D, nh * LANES)


def _heads_even_odd(w):
    D = w.shape[0]
    nh = w.shape[1] // HEAD_DIM
    w4 = w.reshape(D, nh // 2, 2, HEAD_DIM)
    z = jnp.zeros((D, nh // 2, HEAD_DIM), w.dtype)
    even = jnp.concatenate([w4[:, :, 0], z], axis=2)
    odd = jnp.concatenate([z, w4[:, :, 1]], axis=2)
    return jnp.stack([even, odd], axis=2).reshape(D, nh * LANES)


def _group_even_odd(w):
    D = w.shape[0]
    w3 = w.reshape(D, NSA_KV_GROUPS, HEAD_DIM)
    z = jnp.zeros_like(w3)
    return jnp.concatenate([w3, z, z, w3], axis=2).reshape(D, NSA_KV_GROUPS * 2 * LANES)


def _inproj(x2, sc1, sh1, g_pre, w_in, b_forget, S):
    T, D = x2.shape
    tm = min(ROW_TILE, S)
    tiles_per_seq = S // tm
    offs = np.cumsum(IN_SIZES)[:-1].tolist()
    (wfq, wfk, wfv, wff, wnq, wkcm, wvcm, wksl, wvsl, wkwn, wvwn, wng, wmgf, wmgn) = jnp.split(w_in, offs, axis=1)
    scale = HEAD_DIM ** -0.5
    cast = lambda w: w.astype(bf16)
    weights = [
        cast(_heads_to_lanes(wfq * scale, True)),
        cast(_heads_to_lanes(wfk, True)),
        cast(_heads_even_odd(wfv)),
        cast(_heads_to_lanes(wnq * scale, False)),
        cast(jnp.concatenate([wkcm, wvcm], axis=1)),
        cast(_heads_to_lanes(wksl, False)),
        cast(_group_even_odd(wvsl)),
        cast(_heads_to_lanes(wkwn, False)),
        cast(_group_even_odd(wvwn)),
        cast(jnp.concatenate([wmgf, wmgn], axis=1)),
        cast(jnp.concatenate([wff, wng, jnp.zeros((D, LANES - FOX_HEADS - 3 * NSA_HEADS), f32)], axis=1)),
    ]
    bq = np.zeros((1, FOX_HEADS * LANES), np.float32)
    bv = np.zeros((1, FOX_HEADS * LANES), np.float32)
    for h in range(FOX_HEADS):
        bq[0, h * LANES + HEAD_DIM:h * LANES + HEAD_DIM + 3] = 1.0
        bv[0, h * LANES + (HEAD_DIM if h % 2 == 0 else 0)] = 1.0
    bvs = np.zeros((1, NSA_KV_GROUPS * 2 * LANES), np.float32)
    for g in range(NSA_KV_GROUPS):
        bvs[0, g * 2 * LANES + HEAD_DIM] = 1.0
        bvs[0, g * 2 * LANES + LANES] = 1.0
    esel = np.zeros((3 * LANES, FOX_HEADS * LANES), np.float32)
    for j in range(3):
        for h in range(FOX_HEADS):
            esel[j * LANES + h, h * LANES + HEAD_DIM + j] = 1.0
    tri = np.tril(np.ones((tm, tm), np.float32))
    bfg = jnp.concatenate([b_forget, jnp.zeros((LANES - FOX_HEADS,), f32)]).reshape(1, LANES)

    widths = [w.shape[1] for w in weights]
    out_dtypes = [bf16] * 10 + [f32]
    row_spec = lambda n: pl.BlockSpec((tm, n), lambda i: (i, 0))
    mod_spec = pl.BlockSpec((1, 1, D), lambda i: (i // tiles_per_seq, 0, 0))
    consts = [jnp.asarray(tri, bf16), jnp.asarray(esel, bf16)]
    biases = [jnp.asarray(bq), jnp.asarray(bv), jnp.asarray(bvs)]
    order = [0, 1, 2, 3, 4, 5, 6, 7, 8, 9, 10]
    outs = pl.pallas_call(
        functools.partial(_inproj_kernel, tm=tm, tiles_per_seq=tiles_per_seq),
        out_shape=[jax.ShapeDtypeStruct((T, widths[k]), out_dtypes[k]) for k in order],
        grid=(T // tm,),
        in_specs=[row_spec(D), mod_spec, mod_spec, _const_spec((1, D)), _const_spec((1, LANES))]
        + [_const_spec(c.shape) for c in consts]
        + [_const_spec(weights[k].shape) for k in order]
        + [_const_spec(b.shape) for b in biases],
        out_specs=[row_spec(widths[k]) for k in order],
        scratch_shapes=[pltpu.VMEM((8, LANES), f32)],
        compiler_params=_cparams("arbitrary"),
        name="inproj",
    )(x2, sc1, sh1, g_pre.reshape(1, D), bfg, *consts, *[weights[k] for k in order], *biases)
    return outs


def _fox_kernel(q_ref, k_ref, v_ref, o_ref, *, tq):
    i = pl.program_id(2)
    row = lax.broadcasted_iota(i32, (tq, tq), 0)
    col = lax.broadcasted_iota(i32, (tq, tq), 1)
    accs = []
    for hh in range(2):
        ls = slice(hh * LANES, (hh + 1) * LANES)
        q = q_ref[0, :, ls]

        def tile(j, carry, diag):
            m, acc = carry
            start = pl.multiple_of(j * tq, tq)
            k = k_ref[0, pl.ds(start, tq), ls]
            v = v_ref[0, pl.ds(start, tq), ls]
            s = lax.dot_general(q, k, _NT, preferred_element_type=f32)
            if diag:
                s = jnp.where(col <= row, s, NEG)
            m_new = jnp.maximum(m, jnp.max(s, axis=-1, keepdims=True))
            p = jnp.exp(s - m_new)
            alpha = jnp.exp(m - m_new)
            acc = alpha * acc + jnp.dot(p.astype(bf16), v, preferred_element_type=f32)
            return m_new, acc

        carry = (jnp.full((tq, 1), NEG, f32), jnp.zeros((tq, LANES), f32))
        carry = tile(i, carry, True)
        carry = lax.fori_loop(0, i, lambda j, c: tile(j, c, False), carry)
        accs.append(carry[1])
    acc_e, acc_o = accs
    lane = lax.broadcasted_iota(i32, (tq, LANES), 1)
    o = jnp.where(lane < HEAD_DIM, acc_e / acc_e[:, HEAD_DIM:HEAD_DIM + 1], acc_o / acc_o[:, 0:1])
    o_ref[0] = o.astype(bf16)


def _fox(fq, fk, fv, B, S):
    tq = min(ATT_TILE, S)
    nq = S // tq
    q3 = fq.reshape(B, S, FOX_HEADS * LANES)
    k3 = fk.reshape(B, S, FOX_HEADS * LANES)
    v3 = fv.reshape(B, S, FOX_HEADS * LANES)
    return pl.pallas_call(
        functools.partial(_fox_kernel, tq=tq),
        out_shape=jax.ShapeDtypeStruct((B, S, FOX_WIDTH), bf16),
        grid=(B, FOX_HEADS // 2, nq),
        in_specs=[pl.BlockSpec((1, tq, 2 * LANES), lambda b, hp, i: (b, i, hp)),
                  pl.BlockSpec((1, S, 2 * LANES), lambda b, hp, i: (b, 0, hp)),
                  pl.BlockSpec((1, S, 2 * LANES), lambda b, hp, i: (b, 0, hp))],
        out_specs=pl.BlockSpec((1, tq, LANES), lambda b, hp, i: (b, i, hp)),
        compiler_params=_cparams("parallel", "parallel", "arbitrary"),
        name="fox",
    )(q3, k3, v3)


def _compress_kernel(x_ref, pea_ref, peb_ref, wa_ref, wb_ref, okc, ovc, *, nc):
    x = x_ref[0].astype(f32)
    xa = (x + pea_ref[...]).astype(bf16)
    xb = (x + peb_ref[...]).astype(bf16)
    a = jnp.dot(xa, wa_ref[...], preferred_element_type=f32)
    b = jnp.dot(xb, wb_ref[...], preferred_element_type=f32)
    out = a + pltpu.roll(b, nc - 1, 0)
    okc[0] = out[:, :2 * LANES].astype(bf16)
    ovc[0] = out[:, 2 * LANES:].astype(bf16)


def _compress(cm, pe_k, pe_v, w_cmp_k, w_cmp_v, B, S):
    nc = S // CMP_STRIDE
    half = CMP_BLOCK // 2
    win = half * 2 * LANES
    x = cm.reshape(B, nc, win)
    wk = w_cmp_k.reshape(CMP_BLOCK, HEAD_DIM, HEAD_DIM)
    wv = w_cmp_v.reshape(CMP_BLOCK, HEAD_DIM, HEAD_DIM)
    H = HEAD_DIM

    def build(wk_h, wv_h):
        w = jnp.zeros((half, 2 * LANES, 6 * LANES), f32)
        w = w.at[:, 0:H, H:2 * H].set(wk_h)
        w = w.at[:, H:2 * H, 3 * H:4 * H].set(wk_h)
        w = w.at[:, 2 * H:3 * H, 4 * H:5 * H].set(wv_h)
        w = w.at[:, 2 * H:3 * H, 7 * H:8 * H].set(wv_h)
        w = w.at[:, 3 * H:4 * H, 8 * H:9 * H].set(wv_h)
        w = w.at[:, 3 * H:4 * H, 11 * H:12 * H].set(wv_h)
        return w.reshape(win, 6 * LANES).astype(bf16)

    wa = build(wk[:half], wv[:half])
    wb = build(wk[half:], wv[half:])

    def pe_row(pk, pv):
        return jnp.concatenate([pk, pk, pv, pv], axis=1).reshape(1, win)

    pea = pe_row(pe_k[:half], pe_v[:half])
    peb = pe_row(pe_k[half:], pe_v[half:])
    return pl.pallas_call(
        functools.partial(_compress_kernel, nc=nc),
        out_shape=[jax.ShapeDtypeStruct((B, nc, 2 * LANES), bf16),
                   jax.ShapeDtypeStruct((B, nc, 4 * LANES), bf16)],
        grid=(B,),
        in_specs=[pl.BlockSpec((1, nc, win), lambda b: (b, 0, 0)),
                  _const_spec((1, win)), _const_spec((1, win)),
                  _const_spec((win, 6 * LANES)), _const_spec((win, 6 * LANES))],
        out_specs=[pl.BlockSpec((1, nc, 2 * LANES), lambda b: (b, 0, 0)),
                   pl.BlockSpec((1, nc, 4 * LANES), lambda b: (b, 0, 0))],
        compiler_params=_cparams("parallel"),
        name="compress",
    )(x, pea, peb, wa, wb)


def _cmpsel_kernel(q_ref, kc_ref, vc_ref, cb_ref, ov_ref, ocmp, osel, *, tq, nc, n_sel, top_n):
    i = pl.program_id(1)
    kc = kc_ref[0]
    pcs = jnp.zeros((tq, nc), f32)
    outs = []
    for hh in range(NSA_HPG):
        q = q_ref[0, :, hh * LANES:(hh + 1) * LANES]
        cb = cb_ref[hh]
        lc = lax.dot_general(q, kc, _NT, preferred_element_type=f32) + cb
        m = jnp.max(lc, axis=-1, keepdims=True)
        p = jnp.where(cb > 0.5 * NEG, jnp.exp(lc - m), 0.0)
        l = jnp.sum(p, axis=-1, keepdims=True)
        pc = p * jnp.where(l > 0.0, 1.0 / l, 0.0)
        pcs = pcs + pc
        v = vc_ref[0, :, (hh % 2) * LANES:(hh % 2 + 1) * LANES]
        outs.append(jnp.dot(pc.astype(bf16), v, preferred_element_type=f32))
    ocmp[0] = jnp.concatenate([outs[0] + outs[1], outs[2] + outs[3]], axis=1).astype(bf16)

    hi = pcs.astype(bf16)
    lo = (pcs - hi.astype(f32)).astype(bf16)
    ov = ov_ref[...]
    imp = (lax.dot_general(ov, hi, _NT, preferred_element_type=f32)
           + lax.dot_general(ov, lo, _NT, preferred_element_type=f32))
    jio = lax.broadcasted_iota(i32, (n_sel, tq), 0)
    t = i * tq + lax.broadcasted_iota(i32, (n_sel, tq), 1)
    cur = t // SEL_BLOCK
    forced = (jio == 0) | (jio == cur) | (jio == cur - 1)
    score = jnp.where(forced, BIG, jnp.where(jio <= cur, imp, -BIG))
    rank = jnp.zeros((n_sel, tq), f32)
    for jp in range(n_sel):
        r = score[jp:jp + 1, :]
        tie = jnp.where(jio > jp, 1.0, 0.0)
        rank = rank + jnp.where(r > score, 1.0, jnp.where(r == score, tie, 0.0))
    selb = jnp.where(rank < top_n, 0.0, -SEL_MASK)
    padded = jnp.concatenate([selb, jnp.zeros((LANES - n_sel, tq), f32)], axis=0)
    osel[0, 0] = padded.T.astype(bf16)


def _t5_bucket(dist):
    n = jnp.maximum(dist, 0)
    max_exact = REL_BUCKETS // 2
    nf = jnp.maximum(n, 1).astype(f32)
    large = max_exact + (jnp.log(nf / max_exact) / np.log(REL_MAX_DIST / max_exact)
                         * (REL_BUCKETS - max_exact)).astype(i32)
    large = jnp.minimum(large, REL_BUCKETS - 1)
    return jnp.where(n < max_exact, n, large)


def _bias_table(rel_bias, n):
    tab = rel_bias[_t5_bucket(jnp.arange(n))]
    return (tab - rel_bias[REL_BUCKETS - 1][None, :]).T


def _cmpsel(nq_arr, kc, vc, rel_bias, B, S):
    tq = min(ATT_TILE, S)
    nqt = S // tq
    nc = S // CMP_STRIDE
    n_sel = S // SEL_BLOCK
    top_n = min(SEL_TOPN, n_sel)
    G = NSA_KV_GROUPS
    assert n_sel <= HEAD_DIM
    t = np.arange(S)[:, None]
    cend = np.arange(nc)[None, :] * CMP_STRIDE + CMP_BLOCK - 1
    d = t - cend
    valid = (d >= 0) & (np.arange(nc)[None, :] < nc - 1)
    dmax = 2 * REL_MAX_DIST
    tab = _bias_table(rel_bias, dmax)
    cb = jnp.where(jnp.asarray(valid)[None], tab[:, np.clip(d, 0, dmax - 1)], NEG)
    c = np.arange(nc)[None, :]
    j = np.arange(n_sel)[:, None]
    ov = ((c * CMP_STRIDE < j * SEL_BLOCK + SEL_BLOCK) & (c * CMP_STRIDE + CMP_BLOCK > j * SEL_BLOCK)
          & (c < nc - 1)).astype(np.float32)
    q3 = nq_arr.reshape(B, S, NSA_HEADS * LANES)
    return pl.pallas_call(
        functools.partial(_cmpsel_kernel, tq=tq, nc=nc, n_sel=n_sel, top_n=top_n),
        out_shape=[jax.ShapeDtypeStruct((B, S, NSA_HEADS * HEAD_DIM), bf16),
                   jax.ShapeDtypeStruct((B, G, S, LANES), bf16)],
        grid=(G, nqt, B),
        in_specs=[pl.BlockSpec((1, tq, NSA_HPG * LANES), lambda g, i, b: (b, i, g)),
                  pl.BlockSpec((1, nc, LANES), lambda g, i, b: (b, 0, g)),
                  pl.BlockSpec((1, nc, 2 * LANES), lambda g, i, b: (b, 0, g)),
                  pl.BlockSpec((NSA_HPG, tq, nc), lambda g, i, b: (g, i, 0)),
                  _const_spec((n_sel, nc))],
        out_specs=[pl.BlockSpec((1, tq, 2 * LANES), lambda g, i, b: (b, i, g)),
                   pl.BlockSpec((1, 1, tq, LANES), lambda g, i, b: (b, g, i, 0))],
        compiler_params=_cparams("parallel", "arbitrary", "arbitrary"),
        name="cmpsel",
    )(q3, kc, vc, cb, jnp.asarray(ov, bf16))


def _flash_tile(q4, k, v, bias, m_scr, acc_scr, half):
    s = lax.dot_general(q4, k, _NT, preferred_element_type=f32)
    if bias is not None:
        s = s + bias
    m_old = m_scr[...]
    m_new = jnp.maximum(m_old, jnp.max(s, axis=-1, keepdims=True))
    p = jnp.exp(s - m_new).astype(bf16)
    alpha = jnp.exp(m_old - m_new)
    pv = jnp.concatenate([jnp.dot(p[:half], v[:, :LANES], preferred_element_type=f32),
                          jnp.dot(p[half:], v[:, LANES:], preferred_element_type=f32)], axis=0)
    acc_scr[...] = alpha * acc_scr[...] + pv
    m_scr[...] = m_new


def _flash_finish(acc_scr, o_ref, tq):
    acc = acc_scr[...]
    acc_e = acc[:2 * tq]
    acc_o = acc[2 * tq:]
    lane = lax.broadcasted_iota(i32, (2 * tq, LANES), 1)
    o = jnp.where(lane < HEAD_DIM, acc_e / acc_e[:, HEAD_DIM:HEAD_DIM + 1], acc_o / acc_o[:, 0:1])
    o_ref[0] = jnp.concatenate([o[:tq], o[tq:]], axis=1).astype(bf16)


def _stack_heads(q, extra=None):
    parts = [q[:, h * LANES:(h + 1) * LANES] for h in (0, 2, 1, 3)]
    if extra is not None:
        parts = [p + extra for p in parts]
    return jnp.concatenate(parts, axis=0)


def _slc_kernel(q_ref, sb_ref, k_ref, v_ref, bias_ref, o_ref, m_scr, acc_scr, *, tq):
    i = pl.program_id(2)
    q4 = _stack_heads(q_ref[0], sb_ref[0, 0])
    m_scr[...] = jnp.full(m_scr.shape, NEG, f32)
    acc_scr[...] = jnp.zeros(acc_scr.shape, f32)

    def kv(j):
        start = pl.multiple_of(j * tq, tq)
        return k_ref[0, pl.ds(start, tq), :], v_ref[0, pl.ds(start, tq), :]

    _flash_tile(q4, *kv(i), bias_ref[0, 0], m_scr, acc_scr, 2 * tq)

    @pl.when(i >= 1)
    def _():
        _flash_tile(q4, *kv(i - 1), bias_ref[0, 1], m_scr, acc_scr, 2 * tq)

    def far(j, c):
        _flash_tile(q4, *kv(j), None, m_scr, acc_scr, 2 * tq)
        return c

    lax.fori_loop(0, jnp.maximum(i - 1, 0), far, 0)
    _flash_finish(acc_scr, o_ref, tq)


def _win_kernel(q_ref, k_ref, v_ref, bias_ref, o_ref, m_scr, acc_scr, *, tq, n_tiles):
    i = pl.program_id(2)
    q4 = _stack_heads(q_ref[0])
    m_scr[...] = jnp.full(m_scr.shape, NEG, f32)
    acc_scr[...] = jnp.zeros(acc_scr.shape, f32)

    def step(dl):
        start = pl.multiple_of((i - dl) * tq, tq)
        _flash_tile(q4, k_ref[0, pl.ds(start, tq), :], v_ref[0, pl.ds(start, tq), :],
                    bias_ref[0, dl], m_scr, acc_scr, 2 * tq)

    step(0)
    for dl in range(1, n_tiles):
        pl.when(i >= dl)(functools.partial(step, dl))
    _flash_finish(acc_scr, o_ref, tq)


def _near_bias(rel_bias, tq, n_tiles, window):
    r = np.arange(tq)[:, None]
    c = np.arange(tq)[None, :]
    dmax = (n_tiles + 1) * tq
    tab = _bias_table(rel_bias, dmax)
    tiles = []
    for dl in range(n_tiles):
        d = r - c + dl * tq
        valid = d >= 0
        if window is not None:
            valid = valid & (d < window)
        tiles.append(jnp.where(jnp.asarray(valid)[None], tab[:, np.clip(d, 0, dmax - 1)], NEG))
    t = jnp.stack(tiles, axis=1)
    t = t.reshape(NSA_KV_GROUPS, NSA_HPG, n_tiles, tq, tq)[:, np.array([0, 2, 1, 3])]
    return t.transpose(0, 2, 1, 3, 4).reshape(NSA_KV_GROUPS, n_tiles, NSA_HPG * tq, tq)


def _nsa_flash(kind, nq_arr, selb, k_arr, v_arr, rel_bias, B, S):
    tq = min(ATT_TILE, S)
    nqt = S // tq
    G = NSA_KV_GROUPS
    assert tq + 1 >= REL_MAX_DIST
    q3 = nq_arr.reshape(B, S, NSA_HEADS * LANES)
    k3 = k_arr.reshape(B, S, G * LANES)
    v3 = v_arr.reshape(B, S, G * 2 * LANES)
    q_spec = pl.BlockSpec((1, tq, NSA_HPG * LANES), lambda b, g, i: (b, i, g))
    k_spec = pl.BlockSpec((1, S, LANES), lambda b, g, i: (b, 0, g))
    v_spec = pl.BlockSpec((1, S, 2 * LANES), lambda b, g, i: (b, 0, g))
    if kind == "slc":
        n_tiles = 2
        bias = _near_bias(rel_bias, tq, n_tiles, None)
        kern = functools.partial(_slc_kernel, tq=tq)
        extra_specs = [pl.BlockSpec((1, 1, tq, LANES), lambda b, g, i: (b, g, i, 0))]
        extra = [selb]
    else:
        n_tiles = min(WINDOW // tq + 1, nqt) if WINDOW % tq == 0 else None
        bias = _near_bias(rel_bias, tq, n_tiles, WINDOW)
        kern = functools.partial(_win_kernel, tq=tq, n_tiles=n_tiles)
        extra_specs, extra = [], []
    b_spec = pl.BlockSpec((1, n_tiles, NSA_HPG * tq, tq), lambda b, g, i: (g, 0, 0, 0))
    return pl.pallas_call(
        kern,
        out_shape=jax.ShapeDtypeStruct((B, S, NSA_HEADS * HEAD_DIM), bf16),
        grid=(B, G, nqt),
        in_specs=[q_spec] + extra_specs + [k_spec, v_spec, b_spec],
        out_specs=pl.BlockSpec((1, tq, 2 * LANES), lambda b, g, i: (b, i, g)),
        scratch_shapes=[pltpu.VMEM((NSA_HPG * tq, 1), f32), pltpu.VMEM((NSA_HPG * tq, LANES), f32)],
        compiler_params=_cparams("parallel", "parallel", "arbitrary"),
        name=kind,
    )(q3, *extra, k3, v3, bias)


def _post_kernel(x_ref, ofox, ocmp, oslc, owin, mg_ref, sm_ref, ga1, sc2, sh2, gpost, gpre,
                 wfp, wnp_, wmo, wr, br, eg, tris,
                 x1_ref, h2_ref, route_ref, cnt_ref, carry_ref, *, tm):
    i = pl.program_id(0)
    W = NSA_WIDTH
    gates = jax.nn.sigmoid(sm_ref[...]).astype(bf16)
    gx = jnp.dot(gates, eg[...], preferred_element_type=f32)
    nsa = (gx[:, :W] * ocmp[...].astype(f32) + gx[:, W:2 * W] * oslc[...].astype(f32)
           + gx[:, 2 * W:] * owin[...].astype(f32))
    y_nsa = jnp.dot(nsa.astype(bf16), wnp_[...], preferred_element_type=f32)
    y_fox = jnp.dot(ofox[...], wfp[...], preferred_element_type=f32)
    mg = mg_ref[...].astype(f32)
    mix = (mg[:, :D_MODEL] * y_fox + mg[:, D_MODEL:] * y_nsa).astype(bf16)
    mixed = jnp.dot(mix, wmo[...], preferred_element_type=f32)
    x1 = x_ref[...] + ga1[0] * _rms(mixed, gpost[...])
    x1_ref[...] = x1
    h2 = _rms(x1, gpre[...]) * (1.0 + sc2[0]) + sh2[0]
    h2_ref[...] = h2

    lane = lax.broadcasted_iota(i32, (tm, LANES), 1)
    logits = jnp.dot(h2.astype(bf16), wr[...], preferred_element_type=f32) + br[...]
    l = jnp.where(lane < N_EXPERTS, logits, NEG)
    vals, idxs = [], []
    for _ in range(TOP_K):
        m = jnp.max(l, axis=-1, keepdims=True)
        idx = jnp.min(jnp.where(l == m, lane, LANES), axis=-1, keepdims=True)
        vals.append(m)
        idxs.append(idx)
        l = jnp.where(lane == idx, NEG, l)
    es = [jnp.exp(v - vals[0]) for v in vals]
    den = es[0] + es[1] + es[2] + es[3]

    @pl.when(i == 0)
    def _():
        carry_ref[...] = jnp.zeros_like(carry_ref)

    hot = [lane == idx for idx in idxs]
    cnt = sum(jnp.where(h, 1.0, 0.0) for h in hot)
    base = jnp.dot(tris[...], cnt.astype(bf16), preferred_element_type=f32) + carry_ref[0:1, :]
    new_carry = base[tm - 1:tm, :] + cnt[tm - 1:tm, :]
    carry_ref[0:1, :] = new_carry
    cnt_ref[...] = jnp.broadcast_to(new_carry, cnt_ref.shape)
    route = jnp.zeros((tm, LANES), f32)
    for k in range(TOP_K):
        pos = jnp.sum(jnp.where(hot[k], base, 0.0), axis=-1, keepdims=True)
        route = jnp.where(lane == k, idxs[k].astype(f32), route)
        route = jnp.where(lane == TOP_K + k, pos, route)
        route = jnp.where(lane == 2 * TOP_K + k, es[k] / den, route)
    route_ref[...] = route


def _post(x2, ofox, ocmp, oslc, owin, mg, sm, ga1, sc2, sh2, g_post, g_pre2,
          w_fox_proj, w_nsa_proj, w_mix_out, w_router, b_router, S):
    T, D = x2.shape
    tm = min(ROW_TILE, S)
    tiles_per_seq = S // tm
    W = NSA_WIDTH
    eg = np.zeros((LANES, 3 * W), np.float32)
    for h in range(NSA_HEADS):
        for k in range(3):
            eg[FOX_HEADS + 3 * h + k, k * W + h * HEAD_DIM:k * W + (h + 1) * HEAD_DIM] = 1.0
    tris = np.tril(np.ones((tm, tm), np.float32), -1)
    wr = jnp.concatenate([w_router, jnp.zeros((D, LANES - N_EXPERTS), f32)], axis=1).astype(bf16)
    br = jnp.concatenate([b_router, jnp.zeros((LANES - N_EXPERTS,), f32)]).reshape(1, LANES)
    row = lambda n: pl.BlockSpec((tm, n), lambda i: (i, 0))
    mod = pl.BlockSpec((1, 1, D), lambda i: (i // tiles_per_seq, 0, 0))
    consts = [w_fox_proj.astype(bf16), w_nsa_proj.astype(bf16), w_mix_out.astype(bf16), wr, br,
              jnp.asarray(eg, bf16), jnp.asarray(tris, bf16)]
    return pl.pallas_call(
        functools.partial(_post_kernel, tm=tm),
        out_shape=[jax.ShapeDtypeStruct((T, D), f32), jax.ShapeDtypeStruct((T, D), f32),
                   jax.ShapeDtypeStruct((T, LANES), f32), jax.ShapeDtypeStruct((8, LANES), f32)],
        grid=(T // tm,),
        in_specs=[row(D), row(FOX_WIDTH), row(W), row(W), row(W), row(2 * D), row(LANES),
                  mod, mod, mod, _const_spec((1, D)), _const_spec((1, D))]
        + [_const_spec(c.shape) for c in consts],
        out_specs=[row(D), row(D), row(LANES), pl.BlockSpec((8, LANES), lambda i: (0, 0))],
        scratch_shapes=[pltpu.VMEM((8, LANES), f32)],
        compiler_params=_cparams("arbitrary"),
        name="post",
    )(x2, ofox, ocmp, oslc, owin, mg, sm, ga1, sc2, sh2, g_post.reshape(1, D), g_pre2.reshape(1, D), *consts)


def _scatter_kernel(dest_ref, h_ref, xs_in, xs_ref, sem, *, tm):
    del xs_in
    i = pl.program_id(0)

    def row_copy(r, k):
        d = dest_ref[(i * tm + r) * TOP_K + k]
        return pltpu.make_async_copy(h_ref.at[pl.ds(r, 1), :], xs_ref.at[pl.ds(d, 1), :], sem)

    def issue(r, c):
        for k in range(TOP_K):
            row_copy(r, k).start()
        return c

    lax.fori_loop(0, tm, issue, 0)
    for _ in range(TOP_K):
        pltpu.make_async_copy(h_ref, xs_ref.at[pl.ds(0, tm), :], sem).wait()


def _scatter(dest, h2, P):
    T, D = h2.shape
    tm = min(MOE_ROWS, T)
    return pl.pallas_call(
        functools.partial(_scatter_kernel, tm=tm),
        out_shape=jax.ShapeDtypeStruct((P, D), f32),
        grid_spec=pltpu.PrefetchScalarGridSpec(
            num_scalar_prefetch=1, grid=(T // tm,),
            in_specs=[pl.BlockSpec((tm, D), lambda i, d: (i, 0)), pl.BlockSpec(memory_space=pl.ANY)],
            out_specs=pl.BlockSpec(memory_space=pl.ANY),
            scratch_shapes=[pltpu.SemaphoreType.DMA]),
        input_output_aliases={2: 0},
        compiler_params=pltpu.CompilerParams(dimension_semantics=("arbitrary",), vmem_limit_bytes=VMEM_LIMIT,
                                             has_side_effects=True),
        name="scatter",
    )(dest, h2, jnp.zeros((P, D), f32))


def _expert_kernel(be_ref, nu_ref, xs_ref, wg, bg, wu, bu, wd, bd, ys_ref):
    i = pl.program_id(0)

    @pl.when(i < nu_ref[0])
    def _():
        x = xs_ref[...].astype(bf16)
        g = jnp.dot(x, wg[0], preferred_element_type=f32) + bg[0]
        u = jnp.dot(x, wu[0], preferred_element_type=f32) + bu[0]
        gt = jnp.minimum(g, SWIGLU_LIMIT)
        up = jnp.clip(u, -SWIGLU_LIMIT, SWIGLU_LIMIT)
        a = (gt * jax.nn.sigmoid(SWIGLU_ALPHA * gt) * (up + 1.0)).astype(bf16)
        ys_ref[...] = jnp.dot(a, wd[0], preferred_element_type=f32) + bd[0]

    @pl.when(i >= nu_ref[0])
    def _():
        ys_ref[...] = jnp.zeros_like(ys_ref)


def _experts(blk_e, n_used, xs, w_gate, b_gate, w_up, b_up, w_down, b_down):
    P, D = xs.shape
    E, _, F = w_gate.shape
    nb = P // MOE_BM
    blk = lambda i, be, nu: (jnp.minimum(i, nu[0] - 1), 0)
    wsel = lambda i, be, nu: (be[i], 0, 0)
    return pl.pallas_call(
        _expert_kernel,
        out_shape=jax.ShapeDtypeStruct((P, D), f32),
        grid_spec=pltpu.PrefetchScalarGridSpec(
            num_scalar_prefetch=2, grid=(nb,),
            in_specs=[pl.BlockSpec((MOE_BM, D), blk),
                      pl.BlockSpec((1, D, F), wsel), pl.BlockSpec((1, 1, F), wsel),
                      pl.BlockSpec((1, D, F), wsel), pl.BlockSpec((1, 1, F), wsel),
                      pl.BlockSpec((1, F, D), wsel), pl.BlockSpec((1, 1, D), wsel)],
            out_specs=pl.BlockSpec((MOE_BM, D), lambda i, be, nu: (i, 0))),
        compiler_params=_cparams("arbitrary"),
        name="experts",
    )(blk_e, n_used, xs, w_gate.astype(bf16), b_gate.reshape(E, 1, F), w_up.astype(bf16), b_up.reshape(E, 1, F),
      w_down.astype(bf16), b_down.reshape(E, 1, D))


def _combine_kernel(dest_ref, ys_ref, route_ref, x1_ref, ga2, gpost, o_ref, buf, sem, *, tm):
    i = pl.program_id(0)

    def row_copy(r, k):
        d = dest_ref[(i * tm + r) * TOP_K + k]
        return pltpu.make_async_copy(ys_ref.at[pl.ds(d, 1), :], buf.at[k, pl.ds(r, 1), :], sem)

    def issue(r, c):
        for k in range(TOP_K):
            row_copy(r, k).start()
        return c

    lax.fori_loop(0, tm, issue, 0)
    for k in range(TOP_K):
        pltpu.make_async_copy(ys_ref.at[pl.ds(0, tm), :], buf.at[k], sem).wait()
    route = route_ref[...]
    y = jnp.zeros((tm, D_MODEL), f32)
    for k in range(TOP_K):
        y = y + route[:, 2 * TOP_K + k:2 * TOP_K + k + 1] * buf[k]
    o_ref[...] = x1_ref[...] + ga2[0] * _rms(y, gpost[...])


def _combine(dest, ys, route, x1, ga2, g_post2, S):
    T, D = x1.shape
    tm = min(MOE_ROWS, S)
    tiles_per_seq = S // tm
    return pl.pallas_call(
        functools.partial(_combine_kernel, tm=tm),
        out_shape=jax.ShapeDtypeStruct((T, D), f32),
        grid_spec=pltpu.PrefetchScalarGridSpec(
            num_scalar_prefetch=1, grid=(T // tm,),
            in_specs=[pl.BlockSpec(memory_space=pl.ANY),
                      pl.BlockSpec((tm, LANES), lambda i, d: (i, 0)),
                      pl.BlockSpec((tm, D), lambda i, d: (i, 0)),
                      pl.BlockSpec((1, 1, D), lambda i, d: (i // tiles_per_seq, 0, 0)),
                      pl.BlockSpec((1, D), lambda i, d: (0, 0))],
            out_specs=pl.BlockSpec((tm, D), lambda i, d: (i, 0)),
            scratch_shapes=[pltpu.VMEM((TOP_K, tm, D), f32), pltpu.SemaphoreType.DMA]),
        compiler_params=_cparams("arbitrary"),
        name="combine",
    )(dest, ys, route, x1, ga2, g_post2.reshape(1, D))


def kernel(x, c, w_ada, b_ada, g_mix_pre, g_mix_post, w_in, b_forget, pe_k, pe_v, w_cmp_k, w_cmp_v, w_fox_proj, w_nsa_proj, w_mix_out, rel_bias, g_ffn_pre, g_ffn_post, w_router, b_router, w_gate, b_gate, w_up, b_up, w_down, b_down):
    B, S, D = x.shape
    T = B * S
    for l in range(w_ada.shape[0]):
        x2 = x.reshape(T, D)
        ada = _ada(c, w_ada[l], b_ada[l])
        sh1, sc1, ga1, sh2, sc2, ga2 = [a.reshape(B, 1, D) for a in jnp.split(ada, 6, axis=-1)]
        fq, fk, fv, nq, cm, ksl, vsl, kwn, vwn, mg, sm = _inproj(x2, sc1, sh1, g_mix_pre[l], w_in[l], b_forget[l], S)
        o_fox = _fox(fq, fk, fv, B, S)
        kc, vc = _compress(cm, pe_k[l], pe_v[l], w_cmp_k[l], w_cmp_v[l], B, S)
        o_cmp, selb = _cmpsel(nq, kc, vc, rel_bias, B, S)
        o_slc = _nsa_flash("slc", nq, selb, ksl, vsl, rel_bias, B, S)
        o_win = _nsa_flash("win", nq, None, kwn, vwn, rel_bias, B, S)
        x1, h2, route, cnt = _post(x2, o_fox.reshape(T, -1), o_cmp.reshape(T, -1), o_slc.reshape(T, -1),
                                   o_win.reshape(T, -1), mg, sm, ga1, sc2, sh2, g_mix_post[l], g_ffn_pre[l],
                                   w_fox_proj[l], w_nsa_proj[l], w_mix_out[l], w_router[l], b_router[l], S)
        counts = cnt[0, :N_EXPERTS].astype(i32)
        nblk = (counts + MOE_BM - 1) // MOE_BM
        blk_end = jnp.cumsum(nblk)
        pad_start = (blk_end - nblk) * MOE_BM
        top_i = route[:, :TOP_K].astype(i32)
        pos = route[:, TOP_K:2 * TOP_K].astype(i32)
        dest = (pad_start[top_i] + pos).reshape(-1)
        nb = -(-(T * TOP_K) // MOE_BM) + N_EXPERTS
        n_used = blk_end[-1:]
        blk_raw = jnp.minimum(jnp.searchsorted(blk_end, jnp.arange(nb), side="right"), N_EXPERTS - 1)
        blk_e = blk_raw[jnp.minimum(jnp.arange(nb), n_used[0] - 1)].astype(i32)
        xs = _scatter(dest, h2, nb * MOE_BM)
        ys = _experts(blk_e, n_used.astype(i32), xs, w_gate[l], b_gate[l], w_up[l], b_up[l], w_down[l], b_down[l])
        x = _combine(dest, ys, route, x1, ga2, g_ffn_post[l], S).reshape(B, S, D)
    return x
```

```python
import functools

import numpy as np
import jax
import jax.numpy as jnp
from jax import lax
from jax.experimental import pallas as pl
from jax.experimental.pallas import tpu as pltpu

f32 = jnp.float32
bf16 = jnp.bfloat16
i32 = jnp.int32

D_MODEL = 1024
HEAD_DIM = 64
FOX_HEADS = 8
NSA_HEADS = 8
NSA_KV_GROUPS = 2
NSA_HPG = NSA_HEADS // NSA_KV_GROUPS
FOX_WIDTH = FOX_HEADS * HEAD_DIM
NSA_WIDTH = NSA_HEADS * HEAD_DIM
NSA_KV_WIDTH = NSA_KV_GROUPS * HEAD_DIM
CMP_BLOCK = 32
CMP_STRIDE = 16
SEL_BLOCK = 64
SEL_TOPN = 16
WINDOW = 512
REL_BUCKETS = 32
REL_MAX_DIST = 128
N_EXPERTS = 32
TOP_K = 4
SWIGLU_LIMIT = 7.0
SWIGLU_ALPHA = 1.702
RMS_EPS = 1e-6
NEG = -1e30
BIG = 1e9
IN_SIZES = (FOX_WIDTH, FOX_WIDTH, FOX_WIDTH, FOX_HEADS, NSA_WIDTH,
            NSA_KV_WIDTH, NSA_KV_WIDTH, NSA_KV_WIDTH, NSA_KV_WIDTH, NSA_KV_WIDTH, NSA_KV_WIDTH,
            3 * NSA_HEADS, D_MODEL, D_MODEL)

LANES = 128
VMEM_LIMIT = 56 * 1024 * 1024
FOX_TILE = 512
SLC_TILE = 512
WIN_TILE = 256
CMP_TILE = 256
ROW_TILE = 512
MOE_BM = 512
MOE_ROWS = 256
SEL_MASK = 1e9

_NT = (((1,), (1,)), ((), ()))


def _cparams(*sem):
    return pltpu.CompilerParams(dimension_semantics=sem, vmem_limit_bytes=VMEM_LIMIT)


def _const_spec(shape):
    nd = len(shape)
    return pl.BlockSpec(shape, lambda *_: (0,) * nd, pipeline_mode=pl.Buffered(1))


def _split3(a):
    a1 = a.astype(bf16)
    r1 = a - a1.astype(f32)
    a2 = r1.astype(bf16)
    a3 = (r1 - a2.astype(f32)).astype(bf16)
    return a1, a2, a3


def _rms(x, g):
    ms = jnp.mean(x * x, axis=-1, keepdims=True)
    return x * lax.rsqrt(ms + RMS_EPS) * g


def _ada_kernel(c_ref, w_ref, b_ref, o_ref):
    c = c_ref[...]
    s = c * jax.nn.sigmoid(c)
    w = w_ref[...]
    s1, s2, _ = _split3(s)
    w1, w2, _ = _split3(w)
    acc = jnp.dot(s1, w1, preferred_element_type=f32)
    acc += jnp.dot(s1, w2, preferred_element_type=f32)
    acc += jnp.dot(s2, w1, preferred_element_type=f32)
    o_ref[...] = acc + b_ref[...]


def _ada(c, w_ada, b_ada):
    B, D = c.shape
    N = w_ada.shape[1]
    tn = 1024
    return pl.pallas_call(
        _ada_kernel,
        out_shape=jax.ShapeDtypeStruct((B, N), f32),
        grid=(N // tn,),
        in_specs=[pl.BlockSpec((B, D), lambda j: (0, 0)),
                  pl.BlockSpec((D, tn), lambda j: (0, j)),
                  pl.BlockSpec((1, tn), lambda j: (0, j))],
        out_specs=pl.BlockSpec((B, tn), lambda j: (0, j)),
        compiler_params=_cparams("arbitrary"),
        name="ada",
    )(c, w_ada, b_ada.reshape(1, N))


def _inproj_kernel(x_ref, sc_ref, sh_ref, g_ref, bfg_ref, tri_ref, esel_ref,
                   wfq, wfk, wfv, wnq, wcm, wksl, wvsl, wkwn, wvwn, wmg, wsm,
                   bq, bv, bvs,
                   ofq, ofk, ofv, onq, ocm, oksl, ovsl, okwn, ovwn, omg, osm,
                   carry_ref, *, tm, tiles_per_seq):
    i = pl.program_id(0)
    x = x_ref[...]
    h = _rms(x, g_ref[...]) * (1.0 + sc_ref[0]) + sh_ref[0]
    hb = h.astype(bf16)

    def proj(w):
        return jnp.dot(hb, w[...], preferred_element_type=f32)

    ofq[...] = (proj(wfq) + bq[...]).astype(bf16)
    ofv[...] = (proj(wfv) + bv[...]).astype(bf16)
    onq[...] = proj(wnq).astype(bf16)
    ocm[...] = proj(wcm).astype(bf16)
    ovsl[...] = (proj(wvsl) + bvs[...]).astype(bf16)
    okwn[...] = proj(wkwn).astype(bf16)
    ovwn[...] = (proj(wvwn) + bvs[...]).astype(bf16)
    omg[...] = jax.nn.sigmoid(proj(wmg)).astype(bf16)

    row = lax.broadcasted_iota(i32, (tm, 2 * LANES), 0)
    lane = lax.broadcasted_iota(i32, (tm, 2 * LANES), 1)
    blk = ((i % tiles_per_seq) * tm + row) // SEL_BLOCK
    onehot = jnp.where((lane & (LANES - 1)) == blk, 1.0, 0.0)
    oksl[...] = (proj(wksl) + onehot).astype(bf16)

    sm = proj(wsm)
    osm[...] = sm
    z = sm + bfg_ref[...]
    lane1 = lax.broadcasted_iota(i32, (tm, LANES), 1)
    logf = jnp.where(lane1 < FOX_HEADS, jnp.minimum(z, 0.0) - jnp.log(1.0 + jnp.exp(-jnp.abs(z))), 0.0)

    @pl.when(i % tiles_per_seq == 0)
    def _():
        carry_ref[...] = jnp.zeros_like(carry_ref)

    tri = tri_ref[...]
    cum = carry_ref[0:1, :]
    for piece in _split3(logf):
        cum = cum + jnp.dot(tri, piece, preferred_element_type=f32)
    carry_ref[0:1, :] = cum[tm - 1:tm, :]
    ncat = jnp.concatenate(_split3(-cum), axis=1)
    ofk[...] = (proj(wfk) + jnp.dot(ncat, esel_ref[...], preferred_element_type=f32)).astype(bf16)


def _heads_to_lanes(w, lo):
    D = w.shape[0]
    nh = w.shape[1] // HEAD_DIM
    w3 = w.reshape(D, nh, HEAD_DIM)
    z = jnp.zeros_like(w3)
    return jnp.concatenate([w3, z] if lo else [z, w3], axis=2).reshape(D, nh * LANES)


def _heads_even_odd(w):
    D = w.shape[0]
    nh = w.shape[1] // HEAD_DIM
    w4 = w.reshape(D, nh // 2, 2, HEAD_DIM)
    z = jnp.zeros((D, nh // 2, HEAD_DIM), w.dtype)
    even = jnp.concatenate([w4[:, :, 0], z], axis=2)
    odd = jnp.concatenate([z, w4[:, :, 1]], axis=2)
    return jnp.stack([even, odd], axis=2).reshape(D, nh * LANES)


def _group_even_odd(w):
    D = w.shape[0]
    w3 = w.reshape(D, NSA_KV_GROUPS, HEAD_DIM)
    z = jnp.zeros_like(w3)
    return jnp.concatenate([w3, z, z, w3], axis=2).reshape(D, NSA_KV_GROUPS * 2 * LANES)


def _inproj(x2, sc1, sh1, g_pre, w_in, b_forget, S):
    T, D = x2.shape
    tm = min(ROW_TILE, S)
    tiles_per_seq = S // tm
    offs = np.cumsum(IN_SIZES)[:-1].tolist()
    (wfq, wfk, wfv, wff, wnq, wkcm, wvcm, wksl, wvsl, wkwn, wvwn, wng, wmgf, wmgn) = jnp.split(w_in, offs, axis=1)
    scale = HEAD_DIM ** -0.5
    cast = lambda w: w.astype(bf16)
    weights = [
        cast(_heads_to_lanes(wfq * scale, True)),
        cast(_heads_to_lanes(wfk, True)),
        cast(_heads_even_odd(wfv)),
        cast(_heads_to_lanes(wnq * scale, False)),
        cast(jnp.concatenate([wkcm, wvcm], axis=1)),
        cast(_heads_to_lanes(wksl, False)),
        cast(_group_even_odd(wvsl)),
        cast(_heads_to_lanes(wkwn, False)),
        cast(_group_even_odd(wvwn)),
        cast(jnp.concatenate([wmgf, wmgn], axis=1)),
        cast(jnp.concatenate([wff, wng, jnp.zeros((D, LANES - FOX_HEADS - 3 * NSA_HEADS), f32)], axis=1)),
    ]
    bq = np.zeros((1, FOX_HEADS * LANES), np.float32)
    bv = np.zeros((1, FOX_HEADS * LANES), np.float32)
    for h in range(FOX_HEADS):
        bq[0, h * LANES + HEAD_DIM:h * LANES + HEAD_DIM + 3] = 1.0
        bv[0, h * LANES + (HEAD_DIM if h % 2 == 0 else 0)] = 1.0
    bvs = np.zeros((1, NSA_KV_GROUPS * 2 * LANES), np.float32)
    for g in range(NSA_KV_GROUPS):
        bvs[0, g * 2 * LANES + HEAD_DIM] = 1.0
        bvs[0, g * 2 * LANES + LANES] = 1.0
    esel = np.zeros((3 * LANES, FOX_HEADS * LANES), np.float32)
    for j in range(3):
        for h in range(FOX_HEADS):
            esel[j * LANES + h, h * LANES + HEAD_DIM + j] = 1.0
    tri = np.tril(np.ones((tm, tm), np.float32))
    bfg = jnp.concatenate([b_forget, jnp.zeros((LANES - FOX_HEADS,), f32)]).reshape(1, LANES)

    widths = [w.shape[1] for w in weights]
    out_dtypes = [bf16] * 10 + [f32]
    row_spec = lambda n: pl.BlockSpec((tm, n), lambda i: (i, 0))
    mod_spec = pl.BlockSpec((1, 1, D), lambda i: (i // tiles_per_seq, 0, 0))
    consts = [jnp.asarray(tri, bf16), jnp.asarray(esel, bf16)]
    biases = [jnp.asarray(bq), jnp.asarray(bv), jnp.asarray(bvs)]
    order = [0, 1, 2, 3, 4, 5, 6, 7, 8, 9, 10]
    outs = pl.pallas_call(
        functools.partial(_inproj_kernel, tm=tm, tiles_per_seq=tiles_per_seq),
        out_shape=[jax.ShapeDtypeStruct((T, widths[k]), out_dtypes[k]) for k in order],
        grid=(T // tm,),
        in_specs=[row_spec(D), mod_spec, mod_spec, _const_spec((1, D)), _const_spec((1, LANES))]
        + [_const_spec(c.shape) for c in consts]
        + [_const_spec(weights[k].shape) for k in order]
        + [_const_spec(b.shape) for b in biases],
        out_specs=[row_spec(widths[k]) for k in order],
        scratch_shapes=[pltpu.VMEM((8, LANES), f32)],
        compiler_params=_cparams("arbitrary"),
        name="inproj",
    )(x2, sc1, sh1, g_pre.reshape(1, D), bfg, *consts, *[weights[k] for k in order], *biases)
    return outs


def _fox_kernel(q_ref, k_ref, v_ref, o_ref, *, tq):
    i = pl.program_id(2)
    row = lax.broadcasted_iota(i32, (tq, tq), 0)
    col = lax.broadcasted_iota(i32, (tq, tq), 1)
    qs = [q_ref[0, :, hh * LANES:(hh + 1) * LANES] for hh in range(2)]

    def tile(j, carry, diag):
        start = pl.multiple_of(j * tq, tq)
        new = []
        for hh in range(2):
            m, acc = carry[hh]
            k = k_ref[0, pl.ds(start, tq), hh * LANES:(hh + 1) * LANES]
            v = v_ref[0, pl.ds(start, tq), hh * LANES:(hh + 1) * LANES]
            s = lax.dot_general(qs[hh], k, _NT, preferred_element_type=f32)
            if diag:
                s = jnp.where(col <= row, s, NEG)
            m_new = jnp.maximum(m, jnp.max(s, axis=-1, keepdims=True))
            p = jnp.exp(s - m_new).astype(bf16)
            acc = jnp.exp(m - m_new) * acc + jnp.dot(p, v, preferred_element_type=f32)
            new.append((m_new, acc))
        return tuple(new)

    carry = tuple((jnp.full((tq, 1), NEG, f32), jnp.zeros((tq, LANES), f32)) for _ in range(2))
    carry = tile(i, carry, True)
    carry = lax.fori_loop(0, i, lambda j, c: tile(j, c, False), carry)
    o_ref[0] = _pair_out(carry[0][1], carry[1][1]).astype(bf16)


def _fox(fq, fk, fv, B, S):
    tq = min(FOX_TILE, S)
    nq = S // tq
    q3 = fq.reshape(B, S, FOX_HEADS * LANES)
    k3 = fk.reshape(B, S, FOX_HEADS * LANES)
    v3 = fv.reshape(B, S, FOX_HEADS * LANES)
    return pl.pallas_call(
        functools.partial(_fox_kernel, tq=tq),
        out_shape=jax.ShapeDtypeStruct((B, S, FOX_WIDTH), bf16),
        grid=(B, FOX_HEADS // 2, nq),
        in_specs=[pl.BlockSpec((1, tq, 2 * LANES), lambda b, hp, i: (b, i, hp)),
                  pl.BlockSpec((1, S, 2 * LANES), lambda b, hp, i: (b, 0, hp)),
                  pl.BlockSpec((1, S, 2 * LANES), lambda b, hp, i: (b, 0, hp))],
        out_specs=pl.BlockSpec((1, tq, LANES), lambda b, hp, i: (b, i, hp)),
        compiler_params=_cparams("parallel", "parallel", "arbitrary"),
        name="fox",
    )(q3, k3, v3)


def _compress_kernel(x_ref, pea_ref, peb_ref, wa_ref, wb_ref, okc, ovc, *, nc):
    x = x_ref[0].astype(f32)
    xa = (x + pea_ref[...]).astype(bf16)
    xb = (x + peb_ref[...]).astype(bf16)
    a = jnp.dot(xa, wa_ref[...], preferred_element_type=f32)
    b = jnp.dot(xb, wb_ref[...], preferred_element_type=f32)
    out = a + pltpu.roll(b, nc - 1, 0)
    okc[0] = out[:, :2 * LANES].astype(bf16)
    ovc[0] = out[:, 2 * LANES:].astype(bf16)


def _compress(cm, pe_k, pe_v, w_cmp_k, w_cmp_v, B, S):
    nc = S // CMP_STRIDE
    half = CMP_BLOCK // 2
    win = half * 2 * LANES
    x = cm.reshape(B, nc, win)
    wk = w_cmp_k.reshape(CMP_BLOCK, HEAD_DIM, HEAD_DIM)
    wv = w_cmp_v.reshape(CMP_BLOCK, HEAD_DIM, HEAD_DIM)
    H = HEAD_DIM

    def build(wk_h, wv_h):
        w = jnp.zeros((half, 2 * LANES, 6 * LANES), f32)
        w = w.at[:, 0:H, H:2 * H].set(wk_h)
        w = w.at[:, H:2 * H, 3 * H:4 * H].set(wk_h)
        w = w.at[:, 2 * H:3 * H, 4 * H:5 * H].set(wv_h)
        w = w.at[:, 2 * H:3 * H, 7 * H:8 * H].set(wv_h)
        w = w.at[:, 3 * H:4 * H, 8 * H:9 * H].set(wv_h)
        w = w.at[:, 3 * H:4 * H, 11 * H:12 * H].set(wv_h)
        return w.reshape(win, 6 * LANES).astype(bf16)

    wa = build(wk[:half], wv[:half])
    wb = build(wk[half:], wv[half:])

    def pe_row(pk, pv):
        return jnp.concatenate([pk, pk, pv, pv], axis=1).reshape(1, win)

    pea = pe_row(pe_k[:half], pe_v[:half])
    peb = pe_row(pe_k[half:], pe_v[half:])
    return pl.pallas_call(
        functools.partial(_compress_kernel, nc=nc),
        out_shape=[jax.ShapeDtypeStruct((B, nc, 2 * LANES), bf16),
                   jax.ShapeDtypeStruct((B, nc, 4 * LANES), bf16)],
        grid=(B,),
        in_specs=[pl.BlockSpec((1, nc, win), lambda b: (b, 0, 0)),
                  _const_spec((1, win)), _const_spec((1, win)),
                  _const_spec((win, 6 * LANES)), _const_spec((win, 6 * LANES))],
        out_specs=[pl.BlockSpec((1, nc, 2 * LANES), lambda b: (b, 0, 0)),
                   pl.BlockSpec((1, nc, 4 * LANES), lambda b: (b, 0, 0))],
        compiler_params=_cparams("parallel"),
        name="compress",
    )(x, pea, peb, wa, wb)


def _cmpsel_kernel(q_ref, kc_ref, vc_ref, pat_ref, ov_ref, ocmp, osel, *, tq, nc, n_sel, top_n, past):
    i = pl.program_id(1)
    kc = kc_ref[0]
    c0 = i * (tq // CMP_STRIDE) - past
    wio = lax.broadcasted_iota(i32, (LANES, nc), 0)
    cio = lax.broadcasted_iota(i32, (LANES, nc), 1)
    shift = jnp.where(cio == wio + c0, 1.0, 0.0).astype(bf16)
    t = i * tq + lax.broadcasted_iota(i32, (tq, nc), 0)
    cend = lax.broadcasted_iota(i32, (tq, nc), 1) * CMP_STRIDE + (CMP_BLOCK - 1)
    valid = cend <= t
    pcs = jnp.zeros((tq, nc), f32)
    outs = []
    for hh in range(NSA_HPG):
        q = q_ref[0, :, hh * LANES:(hh + 1) * LANES]
        pat = pat_ref[hh]
        pat_hi = pat.astype(bf16)
        pat_lo = (pat - pat_hi.astype(f32)).astype(bf16)
        cb = (jnp.dot(pat_hi, shift, preferred_element_type=f32)
              + jnp.dot(pat_lo, shift, preferred_element_type=f32))
        lc = jnp.where(valid, lax.dot_general(q, kc, _NT, preferred_element_type=f32) + cb, NEG)
        m = jnp.max(lc, axis=-1, keepdims=True)
        p = jnp.where(valid, jnp.exp(lc - m), 0.0)
        l = jnp.sum(p, axis=-1, keepdims=True)
        pc = p * jnp.where(l > 0.0, 1.0 / l, 0.0)
        pcs = pcs + pc
        v = vc_ref[0, :, (hh % 2) * LANES:(hh % 2 + 1) * LANES]
        outs.append(jnp.dot(pc.astype(bf16), v, preferred_element_type=f32))
    ocmp[0] = jnp.concatenate([outs[0] + outs[1], outs[2] + outs[3]], axis=1).astype(bf16)

    hi = pcs.astype(bf16)
    lo = (pcs - hi.astype(f32)).astype(bf16)
    ov = ov_ref[...]
    imp = (lax.dot_general(ov, hi, _NT, preferred_element_type=f32)
           + lax.dot_general(ov, lo, _NT, preferred_element_type=f32))
    jio = lax.broadcasted_iota(i32, (n_sel, tq), 0)
    t = i * tq + lax.broadcasted_iota(i32, (n_sel, tq), 1)
    cur = t // SEL_BLOCK
    forced = (jio == 0) | (jio == cur) | (jio == cur - 1)
    score = jnp.where(forced, BIG, jnp.where(jio <= cur, imp, -BIG))
    rank = jnp.zeros((n_sel, tq), f32)
    for jp in range(n_sel):
        r = score[jp:jp + 1, :]
        tie = jnp.where(jio > jp, 1.0, 0.0)
        rank = rank + jnp.where(r > score, 1.0, jnp.where(r == score, tie, 0.0))
    selb = jnp.where(rank < top_n, 0.0, -SEL_MASK)
    padded = jnp.concatenate([selb, jnp.zeros((LANES - n_sel, tq), f32)], axis=0)
    osel[0, 0] = padded.T.astype(bf16)


def _bucket_bounds():
    n = np.arange(0, 4 * REL_MAX_DIST)
    max_exact = REL_BUCKETS // 2
    nf = np.maximum(n, 1).astype(np.float32)
    large = max_exact + (np.log(nf / np.float32(max_exact)) / np.float32(np.log(REL_MAX_DIST / max_exact))
                         * np.float32(REL_BUCKETS - max_exact)).astype(np.int32)
    bucket = np.where(n < max_exact, n, np.minimum(large, REL_BUCKETS - 1))
    return [int(n[bucket > b].min()) for b in range(REL_BUCKETS - 1)]


_BOUNDS = _bucket_bounds()
_CMP_PAST = (_BOUNDS[-1] + CMP_BLOCK - 1 + CMP_STRIDE - 1) // CMP_STRIDE - 1


def _rel_bias_of(d, rb_ref, h):
    far = rb_ref[(REL_BUCKETS - 1) * NSA_HEADS + h]
    v = jnp.zeros(d.shape, f32)
    for b in reversed(range(REL_BUCKETS - 1)):
        v = jnp.where(d < _BOUNDS[b], rb_ref[b * NSA_HEADS + h] - far, v)
    return v


def _biasgen_kernel(rb_ref, oslc, owin, ocmp, *, ts, tw, tc, nk):
    h = pl.program_id(0)

    def tile(t, off, window):
        lo, hi = off - (t - 1), off + (t - 1)
        if hi < 0 or (window is not None and lo >= window):
            return jnp.full((t, t), NEG, f32)
        d = lax.broadcasted_iota(i32, (t, t), 0) - lax.broadcasted_iota(i32, (t, t), 1) + off
        val = _rel_bias_of(d, rb_ref, h) if lo < _BOUNDS[-1] else jnp.zeros((t, t), f32)
        if lo < 0:
            val = jnp.where(d >= 0, val, NEG)
        if window is not None and hi >= window:
            val = jnp.where(d < window, val, NEG)
        return val

    oslc[0, 0, 0] = tile(ts, 0, None)
    oslc[0, 1, 0] = tile(ts, ts, None)
    oslc[0, 2, 0] = jnp.full((ts, ts), NEG, f32)
    for v in range(nk):
        for cc in range(nk):
            owin[0, v, 0, :, cc * tw:(cc + 1) * tw] = tile(tw, (v - cc) * tw, WINDOW)
    rr = lax.broadcasted_iota(i32, (tc, LANES), 0)
    w = lax.broadcasted_iota(i32, (tc, LANES), 1)
    d = rr - CMP_STRIDE * (w - _CMP_PAST) - (CMP_BLOCK - 1)
    ocmp[0] = jnp.where(d >= 0, _rel_bias_of(d, rb_ref, h), 0.0)


def _biasgen(rel_bias, S):
    ts, tw, tc = min(SLC_TILE, S), min(WIN_TILE, S), min(CMP_TILE, S)
    nk = WINDOW // tw + 1
    assert WINDOW % tw == 0 and S >= nk * tw and min(ts, tw) + 1 >= _BOUNDS[-1]
    assert tc // CMP_STRIDE + _CMP_PAST <= LANES
    G = NSA_KV_GROUPS
    return pl.pallas_call(
        functools.partial(_biasgen_kernel, ts=ts, tw=tw, tc=tc, nk=nk),
        out_shape=[jax.ShapeDtypeStruct((G, 3, NSA_HPG, ts, ts), f32),
                   jax.ShapeDtypeStruct((G, nk, NSA_HPG, tw, nk * tw), f32),
                   jax.ShapeDtypeStruct((NSA_HEADS, tc, LANES), f32)],
        grid=(NSA_HEADS,),
        in_specs=[pl.BlockSpec(memory_space=pltpu.SMEM)],
        out_specs=[pl.BlockSpec((1, 3, 1, ts, ts), lambda h: (h // NSA_HPG, 0, h % NSA_HPG, 0, 0)),
                   pl.BlockSpec((1, nk, 1, tw, nk * tw), lambda h: (h // NSA_HPG, 0, h % NSA_HPG, 0, 0)),
                   pl.BlockSpec((1, tc, LANES), lambda h: (h, 0, 0))],
        compiler_params=_cparams("arbitrary"),
        name="biasgen",
    )(rel_bias.reshape(-1))


def _cmpsel(nq_arr, kc, vc, pat, B, S):
    tq = min(CMP_TILE, S)
    nqt = S // tq
    nc = S // CMP_STRIDE
    n_sel = S // SEL_BLOCK
    top_n = min(SEL_TOPN, n_sel)
    G = NSA_KV_GROUPS
    assert n_sel <= HEAD_DIM
    c = np.arange(nc)[None, :]
    j = np.arange(n_sel)[:, None]
    ov = ((c * CMP_STRIDE < j * SEL_BLOCK + SEL_BLOCK) & (c * CMP_STRIDE + CMP_BLOCK > j * SEL_BLOCK)
          & (c < nc - 1)).astype(np.float32)
    q3 = nq_arr.reshape(B, S, NSA_HEADS * LANES)
    return pl.pallas_call(
        functools.partial(_cmpsel_kernel, tq=tq, nc=nc, n_sel=n_sel, top_n=top_n, past=_CMP_PAST),
        out_shape=[jax.ShapeDtypeStruct((B, S, NSA_HEADS * HEAD_DIM), bf16),
                   jax.ShapeDtypeStruct((B, G, S, LANES), bf16)],
        grid=(G, nqt, B),
        in_specs=[pl.BlockSpec((1, tq, NSA_HPG * LANES), lambda g, i, b: (b, i, g)),
                  pl.BlockSpec((1, nc, LANES), lambda g, i, b: (b, 0, g)),
                  pl.BlockSpec((1, nc, 2 * LANES), lambda g, i, b: (b, 0, g)),
                  pl.BlockSpec((NSA_HPG, tq, LANES), lambda g, i, b: (g, 0, 0)),
                  _const_spec((n_sel, nc))],
        out_specs=[pl.BlockSpec((1, tq, 2 * LANES), lambda g, i, b: (b, i, g)),
                   pl.BlockSpec((1, 1, tq, LANES), lambda g, i, b: (b, g, i, 0))],
        compiler_params=_cparams("parallel", "arbitrary", "arbitrary"),
        name="cmpsel",
    )(q3, kc, vc, pat, jnp.asarray(ov, bf16))


def _pair_out(acc_e, acc_o):
    lane = lax.broadcasted_iota(i32, acc_e.shape, 1)
    return jnp.where(lane < HEAD_DIM, acc_e / acc_e[:, HEAD_DIM:HEAD_DIM + 1], acc_o / acc_o[:, 0:1])


def _slc_kernel(q_ref, sb_ref, k_ref, v_ref, bias_ref, o_ref, *, tq):
    i = pl.program_id(2)
    sb = sb_ref[0, 0]
    qs = [q_ref[0, :, h * LANES:(h + 1) * LANES] + sb for h in range(NSA_HPG)]

    def tile(j, carry, bias_idx):
        start = pl.multiple_of(j * tq, tq)
        k = k_ref[0, pl.ds(start, tq), :]
        v = v_ref[0, pl.ds(start, tq), :]
        new = []
        for h in range(NSA_HPG):
            m, acc = carry[h]
            s = lax.dot_general(qs[h], k, _NT, preferred_element_type=f32)
            if bias_idx is not None:
                s = s + bias_ref[0, bias_idx, h]
            m_new = jnp.maximum(m, jnp.max(s, axis=-1, keepdims=True))
            p = jnp.exp(s - m_new).astype(bf16)
            vh = v[:, (h % 2) * LANES:(h % 2 + 1) * LANES]
            acc = jnp.exp(m - m_new) * acc + jnp.dot(p, vh, preferred_element_type=f32)
            new.append((m_new, acc))
        return tuple(new)

    carry = tuple((jnp.full((tq, 1), NEG, f32), jnp.zeros((tq, LANES), f32)) for _ in range(NSA_HPG))
    carry = tile(i, carry, 0)
    carry = tile(jnp.maximum(i - 1, 0), carry, jnp.where(i >= 1, 1, 2))
    carry = lax.fori_loop(0, jnp.maximum(i - 1, 0), lambda j, c: tile(j, c, None), carry)
    o_ref[0] = jnp.concatenate([_pair_out(carry[0][1], carry[1][1]),
                                _pair_out(carry[2][1], carry[3][1])], axis=1).astype(bf16)


def _win_kernel(q_ref, k_ref, v_ref, bias_ref, o_ref, *, tq, nk):
    i = pl.program_id(2)
    start = pl.multiple_of(jnp.maximum(i - (nk - 1), 0) * tq, tq)
    k = k_ref[0, pl.ds(start, nk * tq), :]
    v = v_ref[0, pl.ds(start, nk * tq), :]
    accs = []
    for h in range(NSA_HPG):
        q = q_ref[0, :, h * LANES:(h + 1) * LANES]
        s = lax.dot_general(q, k, _NT, preferred_element_type=f32) + bias_ref[0, 0, h]
        p = jnp.exp(s - jnp.max(s, axis=-1, keepdims=True)).astype(bf16)
        accs.append(jnp.dot(p, v[:, (h % 2) * LANES:(h % 2 + 1) * LANES], preferred_element_type=f32))
    o_ref[0] = jnp.concatenate([_pair_out(accs[0], accs[1]), _pair_out(accs[2], accs[3])], axis=1).astype(bf16)


def _nsa_flash(kind, nq_arr, selb, k_arr, v_arr, bias, B, S):
    tq = bias.shape[3]
    nqt = S // tq
    G = NSA_KV_GROUPS
    q3 = nq_arr.reshape(B, S, NSA_HEADS * LANES)
    k3 = k_arr.reshape(B, S, G * LANES)
    v3 = v_arr.reshape(B, S, G * 2 * LANES)
    q_spec = pl.BlockSpec((1, tq, NSA_HPG * LANES), lambda g, b, i: (b, i, g))
    k_spec = pl.BlockSpec((1, S, LANES), lambda g, b, i: (b, 0, g))
    v_spec = pl.BlockSpec((1, S, 2 * LANES), lambda g, b, i: (b, 0, g))
    if kind == "slc":
        kern = functools.partial(_slc_kernel, tq=tq)
        extra_specs = [pl.BlockSpec((1, 1, tq, LANES), lambda g, b, i: (b, g, i, 0))]
        extra = [selb]
        b_spec = pl.BlockSpec((1, 3, NSA_HPG, tq, tq), lambda g, b, i: (g, 0, 0, 0, 0),
                              pipeline_mode=pl.Buffered(1))
    else:
        nk = bias.shape[1]
        kern = functools.partial(_win_kernel, tq=tq, nk=nk)
        extra_specs, extra = [], []
        b_spec = pl.BlockSpec((1, 1, NSA_HPG, tq, nk * tq), lambda g, b, i: (g, jnp.minimum(i, nk - 1), 0, 0, 0))
    return pl.pallas_call(
        kern,
        out_shape=jax.ShapeDtypeStruct((B, S, NSA_HEADS * HEAD_DIM), bf16),
        grid=(G, B, nqt),
        in_specs=[q_spec] + extra_specs + [k_spec, v_spec, b_spec],
        out_specs=pl.BlockSpec((1, tq, 2 * LANES), lambda g, b, i: (b, i, g)),
        compiler_params=_cparams("parallel", "parallel", "arbitrary"),
        name=kind,
    )(q3, *extra, k3, v3, bias)


def _post_kernel(x_ref, ofox, ocmp, oslc, owin, mg_ref, sm_ref, ga1, sc2, sh2, gpost, gpre,
                 wfp, wnp_, wmo, wr, br, eg, tris,
                 x1_ref, h2_ref, route_ref, cnt_ref, carry_ref, *, tm):
    i = pl.program_id(0)
    W = NSA_WIDTH
    gates = jax.nn.sigmoid(sm_ref[...]).astype(bf16)
    gx = jnp.dot(gates, eg[...], preferred_element_type=f32)
    nsa = (gx[:, :W] * ocmp[...].astype(f32) + gx[:, W:2 * W] * oslc[...].astype(f32)
           + gx[:, 2 * W:] * owin[...].astype(f32))
    y_nsa = jnp.dot(nsa.astype(bf16), wnp_[...], preferred_element_type=f32)
    y_fox = jnp.dot(ofox[...], wfp[...], preferred_element_type=f32)
    mg = mg_ref[...].astype(f32)
    mix = (mg[:, :D_MODEL] * y_fox + mg[:, D_MODEL:] * y_nsa).astype(bf16)
    mixed = jnp.dot(mix, wmo[...], preferred_element_type=f32)
    x1 = x_ref[...] + ga1[0] * _rms(mixed, gpost[...])
    x1_ref[...] = x1
    h2 = _rms(x1, gpre[...]) * (1.0 + sc2[0]) + sh2[0]
    h2_ref[...] = h2

    lane = lax.broadcasted_iota(i32, (tm, LANES), 1)
    logits = jnp.dot(h2.astype(bf16), wr[...], preferred_element_type=f32) + br[...]
    l = jnp.where(lane < N_EXPERTS, logits, NEG)
    vals, idxs = [], []
    for _ in range(TOP_K):
        m = jnp.max(l, axis=-1, keepdims=True)
        idx = jnp.min(jnp.where(l == m, lane, LANES), axis=-1, keepdims=True)
        vals.append(m)
        idxs.append(idx)
        l = jnp.where(lane == idx, NEG, l)
    es = [jnp.exp(v - vals[0]) for v in vals]
    den = es[0] + es[1] + es[2] + es[3]

    @pl.when(i == 0)
    def _():
        carry_ref[...] = jnp.zeros_like(carry_ref)

    hot = [lane == idx for idx in idxs]
    cnt = sum(jnp.where(h, 1.0, 0.0) for h in hot)
    base = jnp.dot(tris[...], cnt.astype(bf16), preferred_element_type=f32) + carry_ref[0:1, :]
    new_carry = base[tm - 1:tm, :] + cnt[tm - 1:tm, :]
    carry_ref[0:1, :] = new_carry
    cnt_ref[...] = jnp.broadcast_to(new_carry, cnt_ref.shape)
    route = jnp.zeros((tm, LANES), f32)
    for k in range(TOP_K):
        pos = jnp.sum(jnp.where(hot[k], base, 0.0), axis=-1, keepdims=True)
        route = jnp.where(lane == k, idxs[k].astype(f32), route)
        route = jnp.where(lane == TOP_K + k, pos, route)
        route = jnp.where(lane == 2 * TOP_K + k, es[k] / den, route)
    route_ref[...] = route


def _post(x2, ofox, ocmp, oslc, owin, mg, sm, ga1, sc2, sh2, g_post, g_pre2,
          w_fox_proj, w_nsa_proj, w_mix_out, w_router, b_router, S):
    T, D = x2.shape
    tm = min(ROW_TILE, S)
    tiles_per_seq = S // tm
    W = NSA_WIDTH
    eg = np.zeros((LANES, 3 * W), np.float32)
    for h in range(NSA_HEADS):
        for k in range(3):
            eg[FOX_HEADS + 3 * h + k, k * W + h * HEAD_DIM:k * W + (h + 1) * HEAD_DIM] = 1.0
    tris = np.tril(np.ones((tm, tm), np.float32), -1)
    wr = jnp.concatenate([w_router, jnp.zeros((D, LANES - N_EXPERTS), f32)], axis=1).astype(bf16)
    br = jnp.concatenate([b_router, jnp.zeros((LANES - N_EXPERTS,), f32)]).reshape(1, LANES)
    row = lambda n: pl.BlockSpec((tm, n), lambda i: (i, 0))
    mod = pl.BlockSpec((1, 1, D), lambda i: (i // tiles_per_seq, 0, 0))
    consts = [w_fox_proj.astype(bf16), w_nsa_proj.astype(bf16), w_mix_out.astype(bf16), wr, br,
              jnp.asarray(eg, bf16), jnp.asarray(tris, bf16)]
    return pl.pallas_call(
        functools.partial(_post_kernel, tm=tm),
        out_shape=[jax.ShapeDtypeStruct((T, D), f32), jax.ShapeDtypeStruct((T, D), f32),
                   jax.ShapeDtypeStruct((T, LANES), f32), jax.ShapeDtypeStruct((8, LANES), f32)],
        grid=(T // tm,),
        in_specs=[row(D), row(FOX_WIDTH), row(W), row(W), row(W), row(2 * D), row(LANES),
                  mod, mod, mod, _const_spec((1, D)), _const_spec((1, D))]
        + [_const_spec(c.shape) for c in consts],
        out_specs=[row(D), row(D), row(LANES), pl.BlockSpec((8, LANES), lambda i: (0, 0))],
        scratch_shapes=[pltpu.VMEM((8, LANES), f32)],
        compiler_params=_cparams("arbitrary"),
        name="post",
    )(x2, ofox, ocmp, oslc, owin, mg, sm, ga1, sc2, sh2, g_post.reshape(1, D), g_pre2.reshape(1, D), *consts)


def _scatter_kernel(dest_ref, h_ref, xs_in, xs_ref, sem, *, tm):
    del xs_in
    i = pl.program_id(0)

    def row_copy(r, k):
        d = dest_ref[(i * tm + r) * TOP_K + k]
        return pltpu.make_async_copy(h_ref.at[pl.ds(r, 1), :], xs_ref.at[pl.ds(d, 1), :], sem)

    def issue(r, c):
        for k in range(TOP_K):
            row_copy(r, k).start()
        return c

    lax.fori_loop(0, tm, issue, 0)
    for _ in range(TOP_K):
        pltpu.make_async_copy(h_ref, xs_ref.at[pl.ds(0, tm), :], sem).wait()


def _scatter(dest, h2, P):
    T, D = h2.shape
    tm = min(MOE_ROWS, T)
    return pl.pallas_call(
        functools.partial(_scatter_kernel, tm=tm),
        out_shape=jax.ShapeDtypeStruct((P, D), f32),
        grid_spec=pltpu.PrefetchScalarGridSpec(
            num_scalar_prefetch=1, grid=(T // tm,),
            in_specs=[pl.BlockSpec((tm, D), lambda i, d: (i, 0)), pl.BlockSpec(memory_space=pl.ANY)],
            out_specs=pl.BlockSpec(memory_space=pl.ANY),
            scratch_shapes=[pltpu.SemaphoreType.DMA]),
        input_output_aliases={2: 0},
        compiler_params=pltpu.CompilerParams(dimension_semantics=("arbitrary",), vmem_limit_bytes=VMEM_LIMIT,
                                             has_side_effects=True),
        name="scatter",
    )(dest, h2, jnp.zeros((P, D), f32))


def _expert_kernel(be_ref, nu_ref, xs_ref, wg, bg, wu, bu, wd, bd, ys_ref):
    i = pl.program_id(0)

    @pl.when(i < nu_ref[0])
    def _():
        x = xs_ref[...].astype(bf16)
        g = jnp.dot(x, wg[0], preferred_element_type=f32) + bg[0]
        u = jnp.dot(x, wu[0], preferred_element_type=f32) + bu[0]
        gt = jnp.minimum(g, SWIGLU_LIMIT)
        up = jnp.clip(u, -SWIGLU_LIMIT, SWIGLU_LIMIT)
        a = (gt * jax.nn.sigmoid(SWIGLU_ALPHA * gt) * (up + 1.0)).astype(bf16)
        ys_ref[...] = jnp.dot(a, wd[0], preferred_element_type=f32) + bd[0]

    @pl.when(i >= nu_ref[0])
    def _():
        ys_ref[...] = jnp.zeros_like(ys_ref)


def _experts(blk_e, n_used, xs, w_gate, b_gate, w_up, b_up, w_down, b_down):
    P, D = xs.shape
    E, _, F = w_gate.shape
    nb = P // MOE_BM
    blk = lambda i, be, nu: (jnp.minimum(i, nu[0] - 1), 0)
    wsel = lambda i, be, nu: (be[i], 0, 0)
    return pl.pallas_call(
        _expert_kernel,
        out_shape=jax.ShapeDtypeStruct((P, D), f32),
        grid_spec=pltpu.PrefetchScalarGridSpec(
            num_scalar_prefetch=2, grid=(nb,),
            in_specs=[pl.BlockSpec((MOE_BM, D), blk),
                      pl.BlockSpec((1, D, F), wsel), pl.BlockSpec((1, 1, F), wsel),
                      pl.BlockSpec((1, D, F), wsel), pl.BlockSpec((1, 1, F), wsel),
                      pl.BlockSpec((1, F, D), wsel), pl.BlockSpec((1, 1, D), wsel)],
            out_specs=pl.BlockSpec((MOE_BM, D), lambda i, be, nu: (i, 0))),
        compiler_params=_cparams("arbitrary"),
        name="experts",
    )(blk_e, n_used, xs, w_gate.astype(bf16), b_gate.reshape(E, 1, F), w_up.astype(bf16), b_up.reshape(E, 1, F),
      w_down.astype(bf16), b_down.reshape(E, 1, D))


def _combine_kernel(dest_ref, ys_ref, route_ref, x1_ref, ga2, gpost, o_ref, buf, sem, *, tm):
    i = pl.program_id(0)

    def row_copy(r, k):
        d = dest_ref[(i * tm + r) * TOP_K + k]
        return pltpu.make_async_copy(ys_ref.at[pl.ds(d, 1), :], buf.at[k, pl.ds(r, 1), :], sem)

    def issue(r, c):
        for k in range(TOP_K):
            row_copy(r, k).start()
        return c

    lax.fori_loop(0, tm, issue, 0)
    for k in range(TOP_K):
        pltpu.make_async_copy(ys_ref.at[pl.ds(0, tm), :], buf.at[k], sem).wait()
    route = route_ref[...]
    y = jnp.zeros((tm, D_MODEL), f32)
    for k in range(TOP_K):
        y = y + route[:, 2 * TOP_K + k:2 * TOP_K + k + 1] * buf[k]
    o_ref[...] = x1_ref[...] + ga2[0] * _rms(y, gpost[...])


def _combine(dest, ys, route, x1, ga2, g_post2, S):
    T, D = x1.shape
    tm = min(MOE_ROWS, S)
    tiles_per_seq = S // tm
    return pl.pallas_call(
        functools.partial(_combine_kernel, tm=tm),
        out_shape=jax.ShapeDtypeStruct((T, D), f32),
        grid_spec=pltpu.PrefetchScalarGridSpec(
            num_scalar_prefetch=1, grid=(T // tm,),
            in_specs=[pl.BlockSpec(memory_space=pl.ANY),
                      pl.BlockSpec((tm, LANES), lambda i, d: (i, 0)),
                      pl.BlockSpec((tm, D), lambda i, d: (i, 0)),
                      pl.BlockSpec((1, 1, D), lambda i, d: (i // tiles_per_seq, 0, 0)),
                      pl.BlockSpec((1, D), lambda i, d: (0, 0))],
            out_specs=pl.BlockSpec((tm, D), lambda i, d: (i, 0)),
            scratch_shapes=[pltpu.VMEM((TOP_K, tm, D), f32), pltpu.SemaphoreType.DMA]),
        compiler_params=_cparams("arbitrary"),
        name="combine",
    )(dest, ys, route, x1, ga2, g_post2.reshape(1, D))


def kernel(x, c, w_ada, b_ada, g_mix_pre, g_mix_post, w_in, b_forget, pe_k, pe_v, w_cmp_k, w_cmp_v, w_fox_proj, w_nsa_proj, w_mix_out, rel_bias, g_ffn_pre, g_ffn_post, w_router, b_router, w_gate, b_gate, w_up, b_up, w_down, b_down):
    B, S, D = x.shape
    T = B * S
    for l in range(w_ada.shape[0]):
        x2 = x.reshape(T, D)
        ada = _ada(c, w_ada[l], b_ada[l])
        sh1, sc1, ga1, sh2, sc2, ga2 = [a.reshape(B, 1, D) for a in jnp.split(ada, 6, axis=-1)]
        fq, fk, fv, nq, cm, ksl, vsl, kwn, vwn, mg, sm = _inproj(x2, sc1, sh1, g_mix_pre[l], w_in[l], b_forget[l], S)
        o_fox = _fox(fq, fk, fv, B, S)
        kc, vc = _compress(cm, pe_k[l], pe_v[l], w_cmp_k[l], w_cmp_v[l], B, S)
        bias_slc, bias_win, pat_cmp = _biasgen(rel_bias, S)
        o_cmp, selb = _cmpsel(nq, kc, vc, pat_cmp, B, S)
        o_slc = _nsa_flash("slc", nq, selb, ksl, vsl, bias_slc, B, S)
        o_win = _nsa_flash("win", nq, None, kwn, vwn, bias_win, B, S)
        x1, h2, route, cnt = _post(x2, o_fox.reshape(T, -1), o_cmp.reshape(T, -1), o_slc.reshape(T, -1),
                                   o_win.reshape(T, -1), mg, sm, ga1, sc2, sh2, g_mix_post[l], g_ffn_pre[l],
                                   w_fox_proj[l], w_nsa_proj[l], w_mix_out[l], w_router[l], b_router[l], S)
        counts = cnt[0, :N_EXPERTS].astype(i32)
        nblk = (counts + MOE_BM - 1) // MOE_BM
        blk_end = jnp.cumsum(nblk)
        pad_start = (blk_end - nblk) * MOE_BM
        top_i = route[:, :TOP_K].astype(i32)
        pos = route[:, TOP_K:2 * TOP_K].astype(i32)
        dest = (pad_start[top_i] + pos).reshape(-1)
        nb = -(-(T * TOP_K) // MOE_BM) + N_EXPERTS
        n_used = blk_end[-1:]
        blk_raw = jnp.minimum(jnp.searchsorted(blk_end, jnp.arange(nb), side="right"), N_EXPERTS - 1)
        blk_e = blk_raw[jnp.minimum(jnp.arange(nb), n_used[0] - 1)].astype(i32)
        xs = _scatter(dest, h2, nb * MOE_BM)
        ys = _experts(blk_e, n_used.astype(i32), xs, w_gate[l], b_gate[l], w_up[l], b_up[l], w_down[l], b_down[l])
        x = _combine(dest, ys, route, x1, ga2, g_ffn_post[l], S).reshape(B, S, D)
    return x
```

```python
import functools

import numpy as np
import jax
import jax.numpy as jnp
from jax import lax
from jax.experimental import pallas as pl
from jax.experimental.pallas import tpu as pltpu

f32 = jnp.float32
bf16 = jnp.bfloat16
i32 = jnp.int32

D_MODEL = 1024
HEAD_DIM = 64
FOX_HEADS = 8
NSA_HEADS = 8
NSA_KV_GROUPS = 2
NSA_HPG = NSA_HEADS // NSA_KV_GROUPS
FOX_WIDTH = FOX_HEADS * HEAD_DIM
NSA_WIDTH = NSA_HEADS * HEAD_DIM
NSA_KV_WIDTH = NSA_KV_GROUPS * HEAD_DIM
CMP_BLOCK = 32
CMP_STRIDE = 16
SEL_BLOCK = 64
SEL_TOPN = 16
WINDOW = 512
REL_BUCKETS = 32
REL_MAX_DIST = 128
N_EXPERTS = 32
TOP_K = 4
SWIGLU_LIMIT = 7.0
SWIGLU_ALPHA = 1.702
RMS_EPS = 1e-6
NEG = -1e30
BIG = 1e9
IN_SIZES = (FOX_WIDTH, FOX_WIDTH, FOX_WIDTH, FOX_HEADS, NSA_WIDTH,
            NSA_KV_WIDTH, NSA_KV_WIDTH, NSA_KV_WIDTH, NSA_KV_WIDTH, NSA_KV_WIDTH, NSA_KV_WIDTH,
            3 * NSA_HEADS, D_MODEL, D_MODEL)

LANES = 128
VMEM_LIMIT = 56 * 1024 * 1024
FOX_TILE = 512
SLC_TILE = 512
WIN_TILE = 256
CMP_TILE = 256
ROW_TILE = 512
MOE_BM = 512
MOE_ROWS = 256
SEL_MASK = 1e9

_NT = (((1,), (1,)), ((), ()))


def _cparams(*sem):
    return pltpu.CompilerParams(dimension_semantics=sem, vmem_limit_bytes=VMEM_LIMIT)


def _const_spec(shape):
    nd = len(shape)
    return pl.BlockSpec(shape, lambda *_: (0,) * nd, pipeline_mode=pl.Buffered(1))


def _split3(a):
    a1 = a.astype(bf16)
    r1 = a - a1.astype(f32)
    a2 = r1.astype(bf16)
    a3 = (r1 - a2.astype(f32)).astype(bf16)
    return a1, a2, a3


ROW_TILES = D_MODEL // LANES


def _store_tile_rows(ref, val):
    n = val.shape[0]
    for c in range(ROW_TILES):
        ref[pl.ds(c, n, stride=ROW_TILES), :] = val[:, c * LANES:(c + 1) * LANES]


def _load_tile_rows(ref, n):
    return jnp.concatenate([ref[pl.ds(c, n, stride=ROW_TILES), :] for c in range(ROW_TILES)], axis=1)


def _rms(x, g):
    ms = jnp.mean(x * x, axis=-1, keepdims=True)
    return x * lax.rsqrt(ms + RMS_EPS) * g


def _ada_kernel(c_ref, w_ref, b_ref, o_ref):
    c = c_ref[...]
    s = c * jax.nn.sigmoid(c)
    w = w_ref[...]
    s1, s2, _ = _split3(s)
    w1, w2, _ = _split3(w)
    acc = jnp.dot(s1, w1, preferred_element_type=f32)
    acc += jnp.dot(s1, w2, preferred_element_type=f32)
    acc += jnp.dot(s2, w1, preferred_element_type=f32)
    o_ref[...] = acc + b_ref[...]


def _ada(c, w_ada, b_ada):
    B, D = c.shape
    N = w_ada.shape[1]
    tn = 1024
    return pl.pallas_call(
        _ada_kernel,
        out_shape=jax.ShapeDtypeStruct((B, N), f32),
        grid=(N // tn,),
        in_specs=[pl.BlockSpec((B, D), lambda j: (0, 0)),
                  pl.BlockSpec((D, tn), lambda j: (0, j)),
                  pl.BlockSpec((1, tn), lambda j: (0, j))],
        out_specs=pl.BlockSpec((B, tn), lambda j: (0, j)),
        compiler_params=_cparams("arbitrary"),
        name="ada",
    )(c, w_ada, b_ada.reshape(1, N))


def _inproj_kernel(x_ref, sc_ref, sh_ref, g_ref, bfg_ref, tri_ref, esel_ref,
                   wfq, wfk, wfv, wnq, wcm, wksl, wvsl, wkwn, wvwn, wmg, wsm,
                   bq, bv, bvs,
                   ofq, ofk, ofv, onq, ocm, oksl, ovsl, okwn, ovwn, omg, osm,
                   carry_ref, *, tm, tiles_per_seq):
    i = pl.program_id(0)
    x = x_ref[...]
    h = _rms(x, g_ref[...]) * (1.0 + sc_ref[0]) + sh_ref[0]
    hb = h.astype(bf16)

    def proj(w):
        return jnp.dot(hb, w[...], preferred_element_type=f32)

    ofq[...] = (proj(wfq) + bq[...]).astype(bf16)
    ofv[...] = (proj(wfv) + bv[...]).astype(bf16)
    onq[...] = proj(wnq).astype(bf16)
    ocm[...] = proj(wcm).astype(bf16)
    ovsl[...] = (proj(wvsl) + bvs[...]).astype(bf16)
    okwn[...] = proj(wkwn).astype(bf16)
    ovwn[...] = (proj(wvwn) + bvs[...]).astype(bf16)
    omg[...] = jax.nn.sigmoid(proj(wmg)).astype(bf16)

    row = lax.broadcasted_iota(i32, (tm, 2 * LANES), 0)
    lane = lax.broadcasted_iota(i32, (tm, 2 * LANES), 1)
    blk = ((i % tiles_per_seq) * tm + row) // SEL_BLOCK
    onehot = jnp.where((lane & (LANES - 1)) == blk, 1.0, 0.0)
    oksl[...] = (proj(wksl) + onehot).astype(bf16)

    sm = proj(wsm)
    osm[...] = sm
    z = sm + bfg_ref[...]
    lane1 = lax.broadcasted_iota(i32, (tm, LANES), 1)
    logf = jnp.where(lane1 < FOX_HEADS, jnp.minimum(z, 0.0) - jnp.log(1.0 + jnp.exp(-jnp.abs(z))), 0.0)

    @pl.when(i % tiles_per_seq == 0)
    def _():
        carry_ref[...] = jnp.zeros_like(carry_ref)

    tri = tri_ref[...]
    cum = carry_ref[0:1, :]
    for piece in _split3(logf):
        cum = cum + jnp.dot(tri, piece, preferred_element_type=f32)
    carry_ref[0:1, :] = cum[tm - 1:tm, :]
    ncat = jnp.concatenate(_split3(-cum), axis=1)
    ofk[...] = (proj(wfk) + jnp.dot(ncat, esel_ref[...], preferred_element_type=f32)).astype(bf16)


def _heads_to_lanes(w, lo):
    D = w.shape[0]
    nh = w.shape[1] // HEAD_DIM
    w3 = w.reshape(D, nh, HEAD_DIM)
    z = jnp.zeros_like(w3)
    return jnp.concatenate([w3, z] if lo else [z, w3], axis=2).reshape(D, nh * LANES)


def _heads_even_odd(w):
    D = w.shape[0]
    nh = w.shape[1] // HEAD_DIM
    w4 = w.reshape(D, nh // 2, 2, HEAD_DIM)
    z = jnp.zeros((D, nh // 2, HEAD_DIM), w.dtype)
    even = jnp.concatenate([w4[:, :, 0], z], axis=2)
    odd = jnp.concatenate([z, w4[:, :, 1]], axis=2)
    return jnp.stack([even, odd], axis=2).reshape(D, nh * LANES)


def _group_even_odd(w):
    D = w.shape[0]
    w3 = w.reshape(D, NSA_KV_GROUPS, HEAD_DIM)
    z = jnp.zeros_like(w3)
    return jnp.concatenate([w3, z, z, w3], axis=2).reshape(D, NSA_KV_GROUPS * 2 * LANES)


def _inproj(x2, sc1, sh1, g_pre, w_in, b_forget, S):
    T, D = x2.shape
    tm = min(ROW_TILE, S)
    tiles_per_seq = S // tm
    offs = np.cumsum(IN_SIZES)[:-1].tolist()
    (wfq, wfk, wfv, wff, wnq, wkcm, wvcm, wksl, wvsl, wkwn, wvwn, wng, wmgf, wmgn) = jnp.split(w_in, offs, axis=1)
    scale = HEAD_DIM ** -0.5
    cast = lambda w: w.astype(bf16)
    weights = [
        cast(_heads_to_lanes(wfq * scale, True)),
        cast(_heads_to_lanes(wfk, True)),
        cast(_heads_even_odd(wfv)),
        cast(_heads_to_lanes(wnq * scale, False)),
        cast(jnp.concatenate([wkcm, wvcm], axis=1)),
        cast(_heads_to_lanes(wksl, False)),
        cast(_group_even_odd(wvsl)),
        cast(_heads_to_lanes(wkwn, False)),
        cast(_group_even_odd(wvwn)),
        cast(jnp.concatenate([wmgf, wmgn], axis=1)),
        cast(jnp.concatenate([wff, wng, jnp.zeros((D, LANES - FOX_HEADS - 3 * NSA_HEADS), f32)], axis=1)),
    ]
    bq = np.zeros((1, FOX_HEADS * LANES), np.float32)
    bv = np.zeros((1, FOX_HEADS * LANES), np.float32)
    for h in range(FOX_HEADS):
        bq[0, h * LANES + HEAD_DIM:h * LANES + HEAD_DIM + 3] = 1.0
        bv[0, h * LANES + (HEAD_DIM if h % 2 == 0 else 0)] = 1.0
    bvs = np.zeros((1, NSA_KV_GROUPS * 2 * LANES), np.float32)
    for g in range(NSA_KV_GROUPS):
        bvs[0, g * 2 * LANES + HEAD_DIM] = 1.0
        bvs[0, g * 2 * LANES + LANES] = 1.0
    esel = np.zeros((3 * LANES, FOX_HEADS * LANES), np.float32)
    for j in range(3):
        for h in range(FOX_HEADS):
            esel[j * LANES + h, h * LANES + HEAD_DIM + j] = 1.0
    tri = np.tril(np.ones((tm, tm), np.float32))
    bfg = jnp.concatenate([b_forget, jnp.zeros((LANES - FOX_HEADS,), f32)]).reshape(1, LANES)

    widths = [w.shape[1] for w in weights]
    out_dtypes = [bf16] * 10 + [f32]
    row_spec = lambda n: pl.BlockSpec((tm, n), lambda i: (i, 0))
    mod_spec = pl.BlockSpec((1, 1, D), lambda i: (i // tiles_per_seq, 0, 0))
    consts = [jnp.asarray(tri, bf16), jnp.asarray(esel, bf16)]
    biases = [jnp.asarray(bq), jnp.asarray(bv), jnp.asarray(bvs)]
    order = [0, 1, 2, 3, 4, 5, 6, 7, 8, 9, 10]
    outs = pl.pallas_call(
        functools.partial(_inproj_kernel, tm=tm, tiles_per_seq=tiles_per_seq),
        out_shape=[jax.ShapeDtypeStruct((T, widths[k]), out_dtypes[k]) for k in order],
        grid=(T // tm,),
        in_specs=[row_spec(D), mod_spec, mod_spec, _const_spec((1, D)), _const_spec((1, LANES))]
        + [_const_spec(c.shape) for c in consts]
        + [_const_spec(weights[k].shape) for k in order]
        + [_const_spec(b.shape) for b in biases],
        out_specs=[row_spec(widths[k]) for k in order],
        scratch_shapes=[pltpu.VMEM((8, LANES), f32)],
        compiler_params=_cparams("arbitrary"),
        name="inproj",
    )(x2, sc1, sh1, g_pre.reshape(1, D), bfg, *consts, *[weights[k] for k in order], *biases)
    return outs


def _fox_kernel(q_ref, k_ref, v_ref, o_ref, *, tq):
    i = pl.program_id(2)
    row = lax.broadcasted_iota(i32, (tq, tq), 0)
    col = lax.broadcasted_iota(i32, (tq, tq), 1)
    qs = [q_ref[0, :, hh * LANES:(hh + 1) * LANES] for hh in range(2)]

    def tile(j, carry, diag):
        start = pl.multiple_of(j * tq, tq)
        new = []
        for hh in range(2):
            m, acc = carry[hh]
            k = k_ref[0, pl.ds(start, tq), hh * LANES:(hh + 1) * LANES]
            v = v_ref[0, pl.ds(start, tq), hh * LANES:(hh + 1) * LANES]
            s = lax.dot_general(qs[hh], k, _NT, preferred_element_type=f32)
            if diag:
                s = jnp.where(col <= row, s, NEG)
            m_new = jnp.maximum(m, jnp.max(s, axis=-1, keepdims=True))
            p = jnp.exp(s - m_new).astype(bf16)
            acc = jnp.exp(m - m_new) * acc + jnp.dot(p, v, preferred_element_type=f32)
            new.append((m_new, acc))
        return tuple(new)

    carry = tuple((jnp.full((tq, 1), NEG, f32), jnp.zeros((tq, LANES), f32)) for _ in range(2))
    carry = tile(i, carry, True)
    carry = lax.fori_loop(0, i, lambda j, c: tile(j, c, False), carry)
    o_ref[0] = _pair_out(carry[0][1], carry[1][1]).astype(bf16)


def _fox(fq, fk, fv, B, S):
    tq = min(FOX_TILE, S)
    nq = S // tq
    q3 = fq.reshape(B, S, FOX_HEADS * LANES)
    k3 = fk.reshape(B, S, FOX_HEADS * LANES)
    v3 = fv.reshape(B, S, FOX_HEADS * LANES)
    return pl.pallas_call(
        functools.partial(_fox_kernel, tq=tq),
        out_shape=jax.ShapeDtypeStruct((B, S, FOX_WIDTH), bf16),
        grid=(B, FOX_HEADS // 2, nq),
        in_specs=[pl.BlockSpec((1, tq, 2 * LANES), lambda b, hp, i: (b, i, hp)),
                  pl.BlockSpec((1, S, 2 * LANES), lambda b, hp, i: (b, 0, hp)),
                  pl.BlockSpec((1, S, 2 * LANES), lambda b, hp, i: (b, 0, hp))],
        out_specs=pl.BlockSpec((1, tq, LANES), lambda b, hp, i: (b, i, hp)),
        compiler_params=_cparams("parallel", "parallel", "arbitrary"),
        name="fox",
    )(q3, k3, v3)


def _compress_kernel(x_ref, pea_ref, peb_ref, wa_ref, wb_ref, okc, ovc, *, nc):
    x = x_ref[0].astype(f32)
    xa = (x + pea_ref[...]).astype(bf16)
    xb = (x + peb_ref[...]).astype(bf16)
    a = jnp.dot(xa, wa_ref[...], preferred_element_type=f32)
    b = jnp.dot(xb, wb_ref[...], preferred_element_type=f32)
    out = a + pltpu.roll(b, nc - 1, 0)
    okc[0] = out[:, :2 * LANES].astype(bf16)
    ovc[0] = out[:, 2 * LANES:].astype(bf16)


def _compress(cm, pe_k, pe_v, w_cmp_k, w_cmp_v, B, S):
    nc = S // CMP_STRIDE
    half = CMP_BLOCK // 2
    win = half * 2 * LANES
    x = cm.reshape(B, nc, win)
    wk = w_cmp_k.reshape(CMP_BLOCK, HEAD_DIM, HEAD_DIM)
    wv = w_cmp_v.reshape(CMP_BLOCK, HEAD_DIM, HEAD_DIM)
    H = HEAD_DIM

    def build(wk_h, wv_h):
        w = jnp.zeros((half, 2 * LANES, 6 * LANES), f32)
        w = w.at[:, 0:H, H:2 * H].set(wk_h)
        w = w.at[:, H:2 * H, 3 * H:4 * H].set(wk_h)
        w = w.at[:, 2 * H:3 * H, 4 * H:5 * H].set(wv_h)
        w = w.at[:, 2 * H:3 * H, 7 * H:8 * H].set(wv_h)
        w = w.at[:, 3 * H:4 * H, 8 * H:9 * H].set(wv_h)
        w = w.at[:, 3 * H:4 * H, 11 * H:12 * H].set(wv_h)
        return w.reshape(win, 6 * LANES).astype(bf16)

    wa = build(wk[:half], wv[:half])
    wb = build(wk[half:], wv[half:])

    def pe_row(pk, pv):
        return jnp.concatenate([pk, pk, pv, pv], axis=1).reshape(1, win)

    pea = pe_row(pe_k[:half], pe_v[:half])
    peb = pe_row(pe_k[half:], pe_v[half:])
    return pl.pallas_call(
        functools.partial(_compress_kernel, nc=nc),
        out_shape=[jax.ShapeDtypeStruct((B, nc, 2 * LANES), bf16),
                   jax.ShapeDtypeStruct((B, nc, 4 * LANES), bf16)],
        grid=(B,),
        in_specs=[pl.BlockSpec((1, nc, win), lambda b: (b, 0, 0)),
                  _const_spec((1, win)), _const_spec((1, win)),
                  _const_spec((win, 6 * LANES)), _const_spec((win, 6 * LANES))],
        out_specs=[pl.BlockSpec((1, nc, 2 * LANES), lambda b: (b, 0, 0)),
                   pl.BlockSpec((1, nc, 4 * LANES), lambda b: (b, 0, 0))],
        compiler_params=_cparams("parallel"),
        name="compress",
    )(x, pea, peb, wa, wb)


def _cmpsel_kernel(q_ref, kc_ref, vc_ref, pat_ref, ov_ref, ocmp, osel, *, tq, nc, n_sel, top_n, past):
    i = pl.program_id(1)
    kc = kc_ref[0]
    c0 = i * (tq // CMP_STRIDE) - past
    wio = lax.broadcasted_iota(i32, (LANES, nc), 0)
    cio = lax.broadcasted_iota(i32, (LANES, nc), 1)
    shift = jnp.where(cio == wio + c0, 1.0, 0.0).astype(bf16)
    t = i * tq + lax.broadcasted_iota(i32, (tq, nc), 0)
    cend = lax.broadcasted_iota(i32, (tq, nc), 1) * CMP_STRIDE + (CMP_BLOCK - 1)
    valid = cend <= t
    pcs = jnp.zeros((tq, nc), f32)
    outs = []
    for hh in range(NSA_HPG):
        q = q_ref[0, :, hh * LANES:(hh + 1) * LANES]
        pat = pat_ref[hh]
        pat_hi = pat.astype(bf16)
        pat_lo = (pat - pat_hi.astype(f32)).astype(bf16)
        cb = (jnp.dot(pat_hi, shift, preferred_element_type=f32)
              + jnp.dot(pat_lo, shift, preferred_element_type=f32))
        lc = jnp.where(valid, lax.dot_general(q, kc, _NT, preferred_element_type=f32) + cb, NEG)
        m = jnp.max(lc, axis=-1, keepdims=True)
        p = jnp.where(valid, jnp.exp(lc - m), 0.0)
        l = jnp.sum(p, axis=-1, keepdims=True)
        pc = p * jnp.where(l > 0.0, 1.0 / l, 0.0)
        pcs = pcs + pc
        v = vc_ref[0, :, (hh % 2) * LANES:(hh % 2 + 1) * LANES]
        outs.append(jnp.dot(pc.astype(bf16), v, preferred_element_type=f32))
    ocmp[0] = jnp.concatenate([outs[0] + outs[1], outs[2] + outs[3]], axis=1).astype(bf16)

    hi = pcs.astype(bf16)
    lo = (pcs - hi.astype(f32)).astype(bf16)
    ov = ov_ref[...]
    imp = (lax.dot_general(ov, hi, _NT, preferred_element_type=f32)
           + lax.dot_general(ov, lo, _NT, preferred_element_type=f32))
    jio = lax.broadcasted_iota(i32, (n_sel, tq), 0)
    t = i * tq + lax.broadcasted_iota(i32, (n_sel, tq), 1)
    cur = t // SEL_BLOCK
    forced = (jio == 0) | (jio == cur) | (jio == cur - 1)
    score = jnp.where(forced, BIG, jnp.where(jio <= cur, imp, -BIG))
    rank = jnp.zeros((n_sel, tq), f32)
    for jp in range(n_sel):
        r = score[jp:jp + 1, :]
        tie = jnp.where(jio > jp, 1.0, 0.0)
        rank = rank + jnp.where(r > score, 1.0, jnp.where(r == score, tie, 0.0))
    selb = jnp.where(rank < top_n, 0.0, -SEL_MASK)
    padded = jnp.concatenate([selb, jnp.zeros((LANES - n_sel, tq), f32)], axis=0)
    osel[0, 0] = padded.T.astype(bf16)


def _bucket_bounds():
    n = np.arange(0, 4 * REL_MAX_DIST)
    max_exact = REL_BUCKETS // 2
    nf = np.maximum(n, 1).astype(np.float32)
    large = max_exact + (np.log(nf / np.float32(max_exact)) / np.float32(np.log(REL_MAX_DIST / max_exact))
                         * np.float32(REL_BUCKETS - max_exact)).astype(np.int32)
    bucket = np.where(n < max_exact, n, np.minimum(large, REL_BUCKETS - 1))
    return [int(n[bucket > b].min()) for b in range(REL_BUCKETS - 1)]


_BOUNDS = _bucket_bounds()
_CMP_PAST = (_BOUNDS[-1] + CMP_BLOCK - 1 + CMP_STRIDE - 1) // CMP_STRIDE - 1


def _rel_bias_of(d, rb_ref, h):
    far = rb_ref[(REL_BUCKETS - 1) * NSA_HEADS + h]
    v = jnp.zeros(d.shape, f32)
    for b in reversed(range(REL_BUCKETS - 1)):
        v = jnp.where(d < _BOUNDS[b], rb_ref[b * NSA_HEADS + h] - far, v)
    return v


def _biasgen_kernel(rb_ref, oslc, owin, ocmp, *, ts, tw, tc, nk):
    h = pl.program_id(0)

    def tile(t, off, window):
        lo, hi = off - (t - 1), off + (t - 1)
        if hi < 0 or (window is not None and lo >= window):
            return jnp.full((t, t), NEG, f32)
        d = lax.broadcasted_iota(i32, (t, t), 0) - lax.broadcasted_iota(i32, (t, t), 1) + off
        val = _rel_bias_of(d, rb_ref, h) if lo < _BOUNDS[-1] else jnp.zeros((t, t), f32)
        if lo < 0:
            val = jnp.where(d >= 0, val, NEG)
        if window is not None and hi >= window:
            val = jnp.where(d < window, val, NEG)
        return val

    oslc[0, 0, 0] = tile(ts, 0, None)
    oslc[0, 1, 0] = tile(ts, ts, None)
    oslc[0, 2, 0] = jnp.full((ts, ts), NEG, f32)
    for v in range(nk):
        for cc in range(nk):
            owin[0, v, 0, :, cc * tw:(cc + 1) * tw] = tile(tw, (v - cc) * tw, WINDOW)
    rr = lax.broadcasted_iota(i32, (tc, LANES), 0)
    w = lax.broadcasted_iota(i32, (tc, LANES), 1)
    d = rr - CMP_STRIDE * (w - _CMP_PAST) - (CMP_BLOCK - 1)
    ocmp[0] = jnp.where(d >= 0, _rel_bias_of(d, rb_ref, h), 0.0)


def _biasgen(rel_bias, S):
    ts, tw, tc = min(SLC_TILE, S), min(WIN_TILE, S), min(CMP_TILE, S)
    nk = WINDOW // tw + 1
    assert WINDOW % tw == 0 and S >= nk * tw and min(ts, tw) + 1 >= _BOUNDS[-1]
    assert tc // CMP_STRIDE + _CMP_PAST <= LANES
    G = NSA_KV_GROUPS
    return pl.pallas_call(
        functools.partial(_biasgen_kernel, ts=ts, tw=tw, tc=tc, nk=nk),
        out_shape=[jax.ShapeDtypeStruct((G, 3, NSA_HPG, ts, ts), f32),
                   jax.ShapeDtypeStruct((G, nk, NSA_HPG, tw, nk * tw), f32),
                   jax.ShapeDtypeStruct((NSA_HEADS, tc, LANES), f32)],
        grid=(NSA_HEADS,),
        in_specs=[pl.BlockSpec(memory_space=pltpu.SMEM)],
        out_specs=[pl.BlockSpec((1, 3, 1, ts, ts), lambda h: (h // NSA_HPG, 0, h % NSA_HPG, 0, 0)),
                   pl.BlockSpec((1, nk, 1, tw, nk * tw), lambda h: (h // NSA_HPG, 0, h % NSA_HPG, 0, 0)),
                   pl.BlockSpec((1, tc, LANES), lambda h: (h, 0, 0))],
        compiler_params=_cparams("arbitrary"),
        name="biasgen",
    )(rel_bias.reshape(-1))


def _cmpsel(nq_arr, kc, vc, pat, B, S):
    tq = min(CMP_TILE, S)
    nqt = S // tq
    nc = S // CMP_STRIDE
    n_sel = S // SEL_BLOCK
    top_n = min(SEL_TOPN, n_sel)
    G = NSA_KV_GROUPS
    assert n_sel <= HEAD_DIM
    c = np.arange(nc)[None, :]
    j = np.arange(n_sel)[:, None]
    ov = ((c * CMP_STRIDE < j * SEL_BLOCK + SEL_BLOCK) & (c * CMP_STRIDE + CMP_BLOCK > j * SEL_BLOCK)
          & (c < nc - 1)).astype(np.float32)
    q3 = nq_arr.reshape(B, S, NSA_HEADS * LANES)
    return pl.pallas_call(
        functools.partial(_cmpsel_kernel, tq=tq, nc=nc, n_sel=n_sel, top_n=top_n, past=_CMP_PAST),
        out_shape=[jax.ShapeDtypeStruct((B, S, NSA_HEADS * HEAD_DIM), bf16),
                   jax.ShapeDtypeStruct((B, G, S, LANES), bf16)],
        grid=(G, nqt, B),
        in_specs=[pl.BlockSpec((1, tq, NSA_HPG * LANES), lambda g, i, b: (b, i, g)),
                  pl.BlockSpec((1, nc, LANES), lambda g, i, b: (b, 0, g)),
                  pl.BlockSpec((1, nc, 2 * LANES), lambda g, i, b: (b, 0, g)),
                  pl.BlockSpec((NSA_HPG, tq, LANES), lambda g, i, b: (g, 0, 0)),
                  _const_spec((n_sel, nc))],
        out_specs=[pl.BlockSpec((1, tq, 2 * LANES), lambda g, i, b: (b, i, g)),
                   pl.BlockSpec((1, 1, tq, LANES), lambda g, i, b: (b, g, i, 0))],
        compiler_params=_cparams("parallel", "arbitrary", "arbitrary"),
        name="cmpsel",
    )(q3, kc, vc, pat, jnp.asarray(ov, bf16))


def _pair_out(acc_e, acc_o):
    lane = lax.broadcasted_iota(i32, acc_e.shape, 1)
    return jnp.where(lane < HEAD_DIM, acc_e / acc_e[:, HEAD_DIM:HEAD_DIM + 1], acc_o / acc_o[:, 0:1])


def _slc_kernel(q_ref, sb_ref, k_ref, v_ref, bias_ref, o_ref, *, tq):
    i = pl.program_id(2)
    sb = sb_ref[0, 0]
    qs = [q_ref[0, :, h * LANES:(h + 1) * LANES] + sb for h in range(NSA_HPG)]

    def tile(j, carry, bias_idx):
        start = pl.multiple_of(j * tq, tq)
        k = k_ref[0, pl.ds(start, tq), :]
        v = v_ref[0, pl.ds(start, tq), :]
        new = []
        for h in range(NSA_HPG):
            m, acc = carry[h]
            s = lax.dot_general(qs[h], k, _NT, preferred_element_type=f32)
            if bias_idx is not None:
                s = s + bias_ref[0, bias_idx, h]
            m_new = jnp.maximum(m, jnp.max(s, axis=-1, keepdims=True))
            p = jnp.exp(s - m_new).astype(bf16)
            vh = v[:, (h % 2) * LANES:(h % 2 + 1) * LANES]
            acc = jnp.exp(m - m_new) * acc + jnp.dot(p, vh, preferred_element_type=f32)
            new.append((m_new, acc))
        return tuple(new)

    carry = tuple((jnp.full((tq, 1), NEG, f32), jnp.zeros((tq, LANES), f32)) for _ in range(NSA_HPG))
    carry = tile(i, carry, 0)
    carry = tile(jnp.maximum(i - 1, 0), carry, jnp.where(i >= 1, 1, 2))
    carry = lax.fori_loop(0, jnp.maximum(i - 1, 0), lambda j, c: tile(j, c, None), carry)
    o_ref[0] = jnp.concatenate([_pair_out(carry[0][1], carry[1][1]),
                                _pair_out(carry[2][1], carry[3][1])], axis=1).astype(bf16)


def _win_kernel(q_ref, k_ref, v_ref, bias_ref, o_ref, *, tq, nk):
    i = pl.program_id(2)
    start = pl.multiple_of(jnp.maximum(i - (nk - 1), 0) * tq, tq)
    k = k_ref[0, pl.ds(start, nk * tq), :]
    v = v_ref[0, pl.ds(start, nk * tq), :]
    accs = []
    for h in range(NSA_HPG):
        q = q_ref[0, :, h * LANES:(h + 1) * LANES]
        s = lax.dot_general(q, k, _NT, preferred_element_type=f32) + bias_ref[0, 0, h]
        p = jnp.exp(s - jnp.max(s, axis=-1, keepdims=True)).astype(bf16)
        accs.append(jnp.dot(p, v[:, (h % 2) * LANES:(h % 2 + 1) * LANES], preferred_element_type=f32))
    o_ref[0] = jnp.concatenate([_pair_out(accs[0], accs[1]), _pair_out(accs[2], accs[3])], axis=1).astype(bf16)


def _nsa_flash(kind, nq_arr, selb, k_arr, v_arr, bias, B, S):
    tq = bias.shape[3]
    nqt = S // tq
    G = NSA_KV_GROUPS
    q3 = nq_arr.reshape(B, S, NSA_HEADS * LANES)
    k3 = k_arr.reshape(B, S, G * LANES)
    v3 = v_arr.reshape(B, S, G * 2 * LANES)
    q_spec = pl.BlockSpec((1, tq, NSA_HPG * LANES), lambda g, b, i: (b, i, g))
    k_spec = pl.BlockSpec((1, S, LANES), lambda g, b, i: (b, 0, g))
    v_spec = pl.BlockSpec((1, S, 2 * LANES), lambda g, b, i: (b, 0, g))
    if kind == "slc":
        kern = functools.partial(_slc_kernel, tq=tq)
        extra_specs = [pl.BlockSpec((1, 1, tq, LANES), lambda g, b, i: (b, g, i, 0))]
        extra = [selb]
        b_spec = pl.BlockSpec((1, 3, NSA_HPG, tq, tq), lambda g, b, i: (g, 0, 0, 0, 0),
                              pipeline_mode=pl.Buffered(1))
    else:
        nk = bias.shape[1]
        kern = functools.partial(_win_kernel, tq=tq, nk=nk)
        extra_specs, extra = [], []
        b_spec = pl.BlockSpec((1, 1, NSA_HPG, tq, nk * tq), lambda g, b, i: (g, jnp.minimum(i, nk - 1), 0, 0, 0))
    return pl.pallas_call(
        kern,
        out_shape=jax.ShapeDtypeStruct((B, S, NSA_HEADS * HEAD_DIM), bf16),
        grid=(G, B, nqt),
        in_specs=[q_spec] + extra_specs + [k_spec, v_spec, b_spec],
        out_specs=pl.BlockSpec((1, tq, 2 * LANES), lambda g, b, i: (b, i, g)),
        compiler_params=_cparams("parallel", "parallel", "arbitrary"),
        name=kind,
    )(q3, *extra, k3, v3, bias)


def _post_kernel(x_ref, ofox, ocmp, oslc, owin, mg_ref, sm_ref, ga1, sc2, sh2, gpost, gpre,
                 wfp, wnp_, wmo, wr, br, eg, tris,
                 x1_ref, h2_ref, route_ref, cnt_ref, carry_ref, *, tm):
    i = pl.program_id(0)
    W = NSA_WIDTH
    gates = jax.nn.sigmoid(sm_ref[...]).astype(bf16)
    gx = jnp.dot(gates, eg[...], preferred_element_type=f32)
    nsa = (gx[:, :W] * ocmp[...].astype(f32) + gx[:, W:2 * W] * oslc[...].astype(f32)
           + gx[:, 2 * W:] * owin[...].astype(f32))
    y_nsa = jnp.dot(nsa.astype(bf16), wnp_[...], preferred_element_type=f32)
    y_fox = jnp.dot(ofox[...], wfp[...], preferred_element_type=f32)
    mg = mg_ref[...].astype(f32)
    mix = (mg[:, :D_MODEL] * y_fox + mg[:, D_MODEL:] * y_nsa).astype(bf16)
    mixed = jnp.dot(mix, wmo[...], preferred_element_type=f32)
    x1 = x_ref[...] + ga1[0] * _rms(mixed, gpost[...])
    x1_ref[...] = x1
    h2 = _rms(x1, gpre[...]) * (1.0 + sc2[0]) + sh2[0]
    _store_tile_rows(h2_ref, h2)

    lane = lax.broadcasted_iota(i32, (tm, LANES), 1)
    logits = jnp.dot(h2.astype(bf16), wr[...], preferred_element_type=f32) + br[...]
    l = jnp.where(lane < N_EXPERTS, logits, NEG)
    vals, idxs = [], []
    for _ in range(TOP_K):
        m = jnp.max(l, axis=-1, keepdims=True)
        idx = jnp.min(jnp.where(l == m, lane, LANES), axis=-1, keepdims=True)
        vals.append(m)
        idxs.append(idx)
        l = jnp.where(lane == idx, NEG, l)
    es = [jnp.exp(v - vals[0]) for v in vals]
    den = es[0] + es[1] + es[2] + es[3]

    @pl.when(i == 0)
    def _():
        carry_ref[...] = jnp.zeros_like(carry_ref)

    hot = [lane == idx for idx in idxs]
    cnt = sum(jnp.where(h, 1.0, 0.0) for h in hot)
    base = jnp.dot(tris[...], cnt.astype(bf16), preferred_element_type=f32) + carry_ref[0:1, :]
    new_carry = base[tm - 1:tm, :] + cnt[tm - 1:tm, :]
    carry_ref[0:1, :] = new_carry
    cnt_ref[...] = jnp.broadcast_to(new_carry, cnt_ref.shape)
    route = jnp.zeros((tm, LANES), f32)
    for k in range(TOP_K):
        pos = jnp.sum(jnp.where(hot[k], base, 0.0), axis=-1, keepdims=True)
        route = jnp.where(lane == k, idxs[k].astype(f32), route)
        route = jnp.where(lane == TOP_K + k, pos, route)
        route = jnp.where(lane == 2 * TOP_K + k, es[k] / den, route)
    route_ref[...] = route


def _post(x2, ofox, ocmp, oslc, owin, mg, sm, ga1, sc2, sh2, g_post, g_pre2,
          w_fox_proj, w_nsa_proj, w_mix_out, w_router, b_router, S):
    T, D = x2.shape
    tm = min(ROW_TILE, S)
    tiles_per_seq = S // tm
    W = NSA_WIDTH
    eg = np.zeros((LANES, 3 * W), np.float32)
    for h in range(NSA_HEADS):
        for k in range(3):
            eg[FOX_HEADS + 3 * h + k, k * W + h * HEAD_DIM:k * W + (h + 1) * HEAD_DIM] = 1.0
    tris = np.tril(np.ones((tm, tm), np.float32), -1)
    wr = jnp.concatenate([w_router, jnp.zeros((D, LANES - N_EXPERTS), f32)], axis=1).astype(bf16)
    br = jnp.concatenate([b_router, jnp.zeros((LANES - N_EXPERTS,), f32)]).reshape(1, LANES)
    row = lambda n: pl.BlockSpec((tm, n), lambda i: (i, 0))
    mod = pl.BlockSpec((1, 1, D), lambda i: (i // tiles_per_seq, 0, 0))
    consts = [w_fox_proj.astype(bf16), w_nsa_proj.astype(bf16), w_mix_out.astype(bf16), wr, br,
              jnp.asarray(eg, bf16), jnp.asarray(tris, bf16)]
    return pl.pallas_call(
        functools.partial(_post_kernel, tm=tm),
        out_shape=[jax.ShapeDtypeStruct((T, D), f32), jax.ShapeDtypeStruct((T * ROW_TILES, LANES), f32),
                   jax.ShapeDtypeStruct((T, LANES), f32), jax.ShapeDtypeStruct((8, LANES), f32)],
        grid=(T // tm,),
        in_specs=[row(D), row(FOX_WIDTH), row(W), row(W), row(W), row(2 * D), row(LANES),
                  mod, mod, mod, _const_spec((1, D)), _const_spec((1, D))]
        + [_const_spec(c.shape) for c in consts],
        out_specs=[row(D), pl.BlockSpec((tm * ROW_TILES, LANES), lambda i: (i, 0)), row(LANES),
                   pl.BlockSpec((8, LANES), lambda i: (0, 0))],
        scratch_shapes=[pltpu.VMEM((8, LANES), f32)],
        compiler_params=_cparams("arbitrary"),
        name="post",
    )(x2, ofox, ocmp, oslc, owin, mg, sm, ga1, sc2, sh2, g_post.reshape(1, D), g_pre2.reshape(1, D), *consts)


def _moe_kernel(be_ref, nu_ref, tok0_ref, tokn_ref, dstp_ref, h_hbm, wg, bg, wu, bu, wd, bd,
                y_hbm, xbuf, ybuf, gsem, ssem, *, bm):
    del be_ref
    i = pl.program_id(0)
    nu = nu_ref[0]
    rt = ROW_TILES

    def rows(r):
        return pl.ds(r * rt, rt) if isinstance(r, int) else pl.ds(pl.multiple_of(r * rt, rt), rt)

    def gather_copy(tok_ref, r, s):
        return pltpu.make_async_copy(h_hbm.at[tok_ref[0, 0, r]], xbuf.at[s, rows(r), :], gsem.at[s])

    def scatter_copy(r, s):
        return pltpu.make_async_copy(ybuf.at[s, rows(r), :], y_hbm.at[dstp_ref[0, 0, r]], ssem.at[s])

    def wait_gather(s):
        pltpu.make_async_copy(xbuf.at[s], xbuf.at[s], gsem.at[s]).wait()

    def wait_scatter(s):
        pltpu.make_async_copy(ybuf.at[s], ybuf.at[s], ssem.at[s]).wait()

    @pl.when(i == 0)
    def _():
        ybuf[1] = jnp.zeros((bm * rt, LANES), f32)
        n_real = y_hbm.shape[0] - 2 * bm

        def issue(r, c):
            gather_copy(tok0_ref, r, 0).start()
            for half in range(2):
                pltpu.make_async_copy(ybuf.at[1, rows(r), :], y_hbm.at[n_real + half * bm + r],
                                      ssem.at[1]).start()
            return c

        lax.fori_loop(0, bm, issue, 0)
        for half in range(2):
            wait_scatter(1)

    def step(slot):
        other = 1 - slot
        wait_gather(slot)

        @pl.when(i >= 1)
        def _():
            wait_scatter(slot)

        for r in range(bm):
            gather_copy(tokn_ref, r, other).start()
            scatter_copy(r, other).start()
        x = _load_tile_rows(xbuf.at[slot], bm).astype(bf16)
        g = jnp.dot(x, wg[0], preferred_element_type=f32) + bg[0]
        u = jnp.dot(x, wu[0], preferred_element_type=f32) + bu[0]
        gt = jnp.minimum(g, SWIGLU_LIMIT)
        up = jnp.clip(u, -SWIGLU_LIMIT, SWIGLU_LIMIT)
        a = (gt * jax.nn.sigmoid(SWIGLU_ALPHA * gt) * (up + 1.0)).astype(bf16)
        _store_tile_rows(ybuf.at[slot], jnp.dot(a, wd[0], preferred_element_type=f32) + bd[0])

    def drain(slot):
        other = 1 - slot
        wait_gather(slot)
        wait_scatter(slot)

        def issue(r, c):
            scatter_copy(r, other).start()
            return c

        lax.fori_loop(0, bm, issue, 0)
        wait_scatter(other)

    for s in range(2):
        pl.when(jnp.logical_and(i < nu, i % 2 == s))(functools.partial(step, s))
        pl.when(jnp.logical_and(i == nu, i % 2 == s))(functools.partial(drain, s))


def _moe(blk_e, n_used, row_tok, row_dst, h2t, w_gate, b_gate, w_up, b_up, w_down, b_down, n_rows):
    D = D_MODEL
    E, _, F = w_gate.shape
    nbt = row_tok.shape[0]
    bm = MOE_BM
    wsel = lambda i, be, nu: (be[jnp.minimum(i, nu[0] - 1)], 0, 0)
    idx_spec = lambda f: pl.BlockSpec((1, 1, bm), lambda i, be, nu: (f(i), 0, 0), memory_space=pltpu.SMEM)
    return pl.pallas_call(
        functools.partial(_moe_kernel, bm=bm),
        out_shape=jax.ShapeDtypeStruct((n_rows, ROW_TILES, LANES), f32),
        grid_spec=pltpu.PrefetchScalarGridSpec(
            num_scalar_prefetch=2, grid=(nbt,),
            in_specs=[idx_spec(lambda i: 0),
                      idx_spec(lambda i: jnp.minimum(i + 1, nbt - 1)),
                      idx_spec(lambda i: jnp.maximum(i - 1, 0)),
                      pl.BlockSpec(memory_space=pl.ANY),
                      pl.BlockSpec((1, D, F), wsel), pl.BlockSpec((1, 1, F), wsel),
                      pl.BlockSpec((1, D, F), wsel), pl.BlockSpec((1, 1, F), wsel),
                      pl.BlockSpec((1, F, D), wsel), pl.BlockSpec((1, 1, D), wsel)],
            out_specs=pl.BlockSpec(memory_space=pl.ANY),
            scratch_shapes=[pltpu.VMEM((2, bm * ROW_TILES, LANES), f32), pltpu.VMEM((2, bm * ROW_TILES, LANES), f32),
                            pltpu.SemaphoreType.DMA((2,)), pltpu.SemaphoreType.DMA((2,))]),
        compiler_params=_cparams("arbitrary"),
        name="moe",
    )(blk_e, n_used, row_tok, row_tok, row_dst, h2t.reshape(-1, ROW_TILES, LANES),
      w_gate.astype(bf16), b_gate.reshape(E, 1, F), w_up.astype(bf16), b_up.reshape(E, 1, F),
      w_down.astype(bf16), b_down.reshape(E, 1, D))


def _combine_kernel(y0, y1, y2, y3, route_ref, x1_ref, ga2, gpost, o_ref):
    route = route_ref[...]
    tm = o_ref.shape[0]
    y = jnp.zeros(o_ref.shape, f32)
    for k, yk in enumerate((y0, y1, y2, y3)):
        y = y + route[:, 2 * TOP_K + k:2 * TOP_K + k + 1] * _load_tile_rows(yk, tm)
    o_ref[...] = x1_ref[...] + ga2[0] * _rms(y, gpost[...])


def _combine(y4, route, x1, ga2, g_post2, S):
    T, D = x1.shape
    tm = min(MOE_ROWS, S)
    tiles_per_seq = S // tm
    nt = T // tm
    y4 = y4.reshape(-1, LANES)
    y_spec = lambda k: pl.BlockSpec((tm * ROW_TILES, LANES), lambda i: (k * nt + i, 0))
    return pl.pallas_call(
        _combine_kernel,
        out_shape=jax.ShapeDtypeStruct((T, D), f32),
        grid=(nt,),
        in_specs=[y_spec(k) for k in range(TOP_K)]
        + [pl.BlockSpec((tm, LANES), lambda i: (i, 0)),
           pl.BlockSpec((tm, D), lambda i: (i, 0)),
           pl.BlockSpec((1, 1, D), lambda i: (i // tiles_per_seq, 0, 0)),
           pl.BlockSpec((1, D), lambda i: (0, 0))],
        out_specs=pl.BlockSpec((tm, D), lambda i: (i, 0)),
        compiler_params=_cparams("parallel"),
        name="combine",
    )(y4, y4, y4, y4, route, x1, ga2, g_post2.reshape(1, D))


def kernel(x, c, w_ada, b_ada, g_mix_pre, g_mix_post, w_in, b_forget, pe_k, pe_v, w_cmp_k, w_cmp_v, w_fox_proj, w_nsa_proj, w_mix_out, rel_bias, g_ffn_pre, g_ffn_post, w_router, b_router, w_gate, b_gate, w_up, b_up, w_down, b_down):
    B, S, D = x.shape
    T = B * S
    for l in range(w_ada.shape[0]):
        x2 = x.reshape(T, D)
        ada = _ada(c, w_ada[l], b_ada[l])
        sh1, sc1, ga1, sh2, sc2, ga2 = [a.reshape(B, 1, D) for a in jnp.split(ada, 6, axis=-1)]
        fq, fk, fv, nq, cm, ksl, vsl, kwn, vwn, mg, sm = _inproj(x2, sc1, sh1, g_mix_pre[l], w_in[l], b_forget[l], S)
        o_fox = _fox(fq, fk, fv, B, S)
        kc, vc = _compress(cm, pe_k[l], pe_v[l], w_cmp_k[l], w_cmp_v[l], B, S)
        bias_slc, bias_win, pat_cmp = _biasgen(rel_bias, S)
        o_cmp, selb = _cmpsel(nq, kc, vc, pat_cmp, B, S)
        o_slc = _nsa_flash("slc", nq, selb, ksl, vsl, bias_slc, B, S)
        o_win = _nsa_flash("win", nq, None, kwn, vwn, bias_win, B, S)
        x1, h2, route, cnt = _post(x2, o_fox.reshape(T, -1), o_cmp.reshape(T, -1), o_slc.reshape(T, -1),
                                   o_win.reshape(T, -1), mg, sm, ga1, sc2, sh2, g_mix_post[l], g_ffn_pre[l],
                                   w_fox_proj[l], w_nsa_proj[l], w_mix_out[l], w_router[l], b_router[l], S)
        counts = cnt[0, :N_EXPERTS].astype(i32)
        nblk = (counts + MOE_BM - 1) // MOE_BM
        blk_end = jnp.cumsum(nblk)
        pad_start = (blk_end - nblk) * MOE_BM
        top_i = route[:, :TOP_K].astype(i32)
        pos = route[:, TOP_K:2 * TOP_K].astype(i32)
        dest = (pad_start[top_i] + pos).reshape(-1)
        A = T * TOP_K
        nbt = -(-A // MOE_BM) + N_EXPERTS + 1
        n_used = blk_end[-1:].astype(i32)
        blk_e = jnp.minimum(jnp.sum(jnp.arange(nbt)[:, None] >= blk_end[None, :], axis=1), N_EXPERTS - 1).astype(i32)
        row_a = jnp.full((nbt * MOE_BM,), -1, i32).at[dest].set(jnp.arange(A, dtype=i32), unique_indices=True)
        p = jnp.arange(nbt * MOE_BM, dtype=i32)
        row_tok = jnp.where(row_a >= 0, row_a // TOP_K, 0)
        row_dst = jnp.where(row_a >= 0, (row_a % TOP_K) * T + row_a // TOP_K,
                            A + ((p // MOE_BM) % 2) * MOE_BM + p % MOE_BM)
        y4 = _moe(blk_e, n_used, row_tok.reshape(nbt, 1, MOE_BM), row_dst.reshape(nbt, 1, MOE_BM), h2,
                  w_gate[l], b_gate[l], w_up[l], b_up[l], w_down[l], b_down[l], A + 2 * MOE_BM)
        x = _combine(y4, route, x1, ga2, g_ffn_post[l], S).reshape(B, S, D)
    return x
```

```python
import functools

import numpy as np
import jax
import jax.numpy as jnp
from jax import lax
from jax.experimental import pallas as pl
from jax.experimental.pallas import tpu as pltpu

f32 = jnp.float32
bf16 = jnp.bfloat16
i32 = jnp.int32

D_MODEL = 1024
HEAD_DIM = 64
FOX_HEADS = 8
NSA_HEADS = 8
NSA_KV_GROUPS = 2
NSA_HPG = NSA_HEADS // NSA_KV_GROUPS
FOX_WIDTH = FOX_HEADS * HEAD_DIM
NSA_WIDTH = NSA_HEADS * HEAD_DIM
NSA_KV_WIDTH = NSA_KV_GROUPS * HEAD_DIM
CMP_BLOCK = 32
CMP_STRIDE = 16
SEL_BLOCK = 64
SEL_TOPN = 16
WINDOW = 512
REL_BUCKETS = 32
REL_MAX_DIST = 128
N_EXPERTS = 32
TOP_K = 4
SWIGLU_LIMIT = 7.0
SWIGLU_ALPHA = 1.702
RMS_EPS = 1e-6
NEG = -1e30
BIG = 1e9
IN_SIZES = (FOX_WIDTH, FOX_WIDTH, FOX_WIDTH, FOX_HEADS, NSA_WIDTH,
            NSA_KV_WIDTH, NSA_KV_WIDTH, NSA_KV_WIDTH, NSA_KV_WIDTH, NSA_KV_WIDTH, NSA_KV_WIDTH,
            3 * NSA_HEADS, D_MODEL, D_MODEL)

LANES = 128
VMEM_LIMIT = 56 * 1024 * 1024
FOX_TILE = 512
SLC_TILE = 512
WIN_TILE = 256
CMP_TILE = 256
ROW_TILE = 512
MOE_BM = 512
MOE_ROWS = 256
SEL_MASK = 1e9

_NT = (((1,), (1,)), ((), ()))


def _cparams(*sem):
    return pltpu.CompilerParams(dimension_semantics=sem, vmem_limit_bytes=VMEM_LIMIT)


def _const_spec(shape):
    nd = len(shape)
    return pl.BlockSpec(shape, lambda *_: (0,) * nd, pipeline_mode=pl.Buffered(1))


def _split3(a):
    a1 = a.astype(bf16)
    r1 = a - a1.astype(f32)
    a2 = r1.astype(bf16)
    a3 = (r1 - a2.astype(f32)).astype(bf16)
    return a1, a2, a3


ROW_TILES = D_MODEL // LANES


def _store_tile_rows(ref, val):
    n = val.shape[0]
    for c in range(ROW_TILES):
        ref[pl.ds(c, n, stride=ROW_TILES), :] = val[:, c * LANES:(c + 1) * LANES]


def _load_tile_rows(ref, n):
    return jnp.concatenate([ref[pl.ds(c, n, stride=ROW_TILES), :] for c in range(ROW_TILES)], axis=1)


def _rms(x, g):
    ms = jnp.mean(x * x, axis=-1, keepdims=True)
    return x * lax.rsqrt(ms + RMS_EPS) * g


def _ada_kernel(c_ref, w_ref, b_ref, o_ref):
    c = c_ref[...]
    s = c * jax.nn.sigmoid(c)
    w = w_ref[...]
    s1, s2, _ = _split3(s)
    w1, w2, _ = _split3(w)
    acc = jnp.dot(s1, w1, preferred_element_type=f32)
    acc += jnp.dot(s1, w2, preferred_element_type=f32)
    acc += jnp.dot(s2, w1, preferred_element_type=f32)
    o_ref[...] = acc + b_ref[...]


def _ada(c, w_ada, b_ada):
    B, D = c.shape
    N = w_ada.shape[1]
    tn = 1024
    return pl.pallas_call(
        _ada_kernel,
        out_shape=jax.ShapeDtypeStruct((B, N), f32),
        grid=(N // tn,),
        in_specs=[pl.BlockSpec((B, D), lambda j: (0, 0)),
                  pl.BlockSpec((D, tn), lambda j: (0, j)),
                  pl.BlockSpec((1, tn), lambda j: (0, j))],
        out_specs=pl.BlockSpec((B, tn), lambda j: (0, j)),
        compiler_params=_cparams("arbitrary"),
        name="ada",
    )(c, w_ada, b_ada.reshape(1, N))


def _inproj_kernel(x_ref, sc_ref, sh_ref, g_ref, bfg_ref, tri_ref, esel_ref,
                   wfq, wfk, wfv, wnq, wcm, wksl, wvsl, wkwn, wvwn, wmg, wsm,
                   bq, bv, bvs,
                   ofq, ofk, ofv, onq, ocm, oksl, ovsl, okwn, ovwn, omg, osm,
                   carry_ref, *, tm, tiles_per_seq):
    i = pl.program_id(0)
    x = x_ref[...]
    h = _rms(x, g_ref[...]) * (1.0 + sc_ref[0]) + sh_ref[0]
    hb = h.astype(bf16)

    def proj(w):
        return jnp.dot(hb, w[...], preferred_element_type=f32)

    ofq[...] = (proj(wfq) + bq[...]).astype(bf16)
    ofv[...] = (proj(wfv) + bv[...]).astype(bf16)
    onq[...] = proj(wnq).astype(bf16)
    ocm[...] = proj(wcm).astype(bf16)
    ovsl[...] = (proj(wvsl) + bvs[...]).astype(bf16)
    okwn[...] = proj(wkwn).astype(bf16)
    ovwn[...] = (proj(wvwn) + bvs[...]).astype(bf16)
    omg[...] = jax.nn.sigmoid(proj(wmg)).astype(bf16)

    row = lax.broadcasted_iota(i32, (tm, 2 * LANES), 0)
    lane = lax.broadcasted_iota(i32, (tm, 2 * LANES), 1)
    blk = ((i % tiles_per_seq) * tm + row) // SEL_BLOCK
    onehot = jnp.where((lane & (LANES - 1)) == blk, 1.0, 0.0)
    oksl[...] = (proj(wksl) + onehot).astype(bf16)

    sm = proj(wsm)
    osm[...] = sm
    z = sm + bfg_ref[...]
    lane1 = lax.broadcasted_iota(i32, (tm, LANES), 1)
    logf = jnp.where(lane1 < FOX_HEADS, jnp.minimum(z, 0.0) - jnp.log(1.0 + jnp.exp(-jnp.abs(z))), 0.0)

    @pl.when(i % tiles_per_seq == 0)
    def _():
        carry_ref[...] = jnp.zeros_like(carry_ref)

    tri = tri_ref[...]
    cum = carry_ref[0:1, :]
    for piece in _split3(logf):
        cum = cum + jnp.dot(tri, piece, preferred_element_type=f32)
    carry_ref[0:1, :] = cum[tm - 1:tm, :]
    ncat = jnp.concatenate(_split3(-cum), axis=1)
    ofk[...] = (proj(wfk) + jnp.dot(ncat, esel_ref[...], preferred_element_type=f32)).astype(bf16)


def _heads_to_lanes(w, lo):
    D = w.shape[0]
    nh = w.shape[1] // HEAD_DIM
    w3 = w.reshape(D, nh, HEAD_DIM)
    z = jnp.zeros_like(w3)
    return jnp.concatenate([w3, z] if lo else [z, w3], axis=2).reshape(D, nh * LANES)


def _heads_even_odd(w):
    D = w.shape[0]
    nh = w.shape[1] // HEAD_DIM
    w4 = w.reshape(D, nh // 2, 2, HEAD_DIM)
    z = jnp.zeros((D, nh // 2, HEAD_DIM), w.dtype)
    even = jnp.concatenate([w4[:, :, 0], z], axis=2)
    odd = jnp.concatenate([z, w4[:, :, 1]], axis=2)
    return jnp.stack([even, odd], axis=2).reshape(D, nh * LANES)


def _group_even_odd(w):
    D = w.shape[0]
    w3 = w.reshape(D, NSA_KV_GROUPS, HEAD_DIM)
    z = jnp.zeros_like(w3)
    return jnp.concatenate([w3, z, z, w3], axis=2).reshape(D, NSA_KV_GROUPS * 2 * LANES)


def _inproj(x2, sc1, sh1, g_pre, w_in, b_forget, S):
    T, D = x2.shape
    tm = min(ROW_TILE, S)
    tiles_per_seq = S // tm
    offs = np.cumsum(IN_SIZES)[:-1].tolist()
    (wfq, wfk, wfv, wff, wnq, wkcm, wvcm, wksl, wvsl, wkwn, wvwn, wng, wmgf, wmgn) = jnp.split(w_in, offs, axis=1)
    scale = HEAD_DIM ** -0.5
    cast = lambda w: w.astype(bf16)
    weights = [
        cast(_heads_to_lanes(wfq * scale, True)),
        cast(_heads_to_lanes(wfk, True)),
        cast(_heads_even_odd(wfv)),
        cast(_heads_to_lanes(wnq * scale, False)),
        cast(jnp.concatenate([wkcm, wvcm], axis=1)),
        cast(_heads_to_lanes(wksl, False)),
        cast(_group_even_odd(wvsl)),
        cast(_heads_to_lanes(wkwn, False)),
        cast(_group_even_odd(wvwn)),
        cast(jnp.concatenate([wmgf, wmgn], axis=1)),
        cast(jnp.concatenate([wff, wng, jnp.zeros((D, LANES - FOX_HEADS - 3 * NSA_HEADS), f32)], axis=1)),
    ]
    bq = np.zeros((1, FOX_HEADS * LANES), np.float32)
    bv = np.zeros((1, FOX_HEADS * LANES), np.float32)
    for h in range(FOX_HEADS):
        bq[0, h * LANES + HEAD_DIM:h * LANES + HEAD_DIM + 3] = 1.0
        bv[0, h * LANES + (HEAD_DIM if h % 2 == 0 else 0)] = 1.0
    bvs = np.zeros((1, NSA_KV_GROUPS * 2 * LANES), np.float32)
    for g in range(NSA_KV_GROUPS):
        bvs[0, g * 2 * LANES + HEAD_DIM] = 1.0
        bvs[0, g * 2 * LANES + LANES] = 1.0
    esel = np.zeros((3 * LANES, FOX_HEADS * LANES), np.float32)
    for j in range(3):
        for h in range(FOX_HEADS):
            esel[j * LANES + h, h * LANES + HEAD_DIM + j] = 1.0
    tri = np.tril(np.ones((tm, tm), np.float32))
    bfg = jnp.concatenate([b_forget, jnp.zeros((LANES - FOX_HEADS,), f32)]).reshape(1, LANES)

    widths = [w.shape[1] for w in weights]
    out_dtypes = [bf16] * 10 + [f32]
    row_spec = lambda n: pl.BlockSpec((tm, n), lambda i: (i, 0))
    mod_spec = pl.BlockSpec((1, 1, D), lambda i: (i // tiles_per_seq, 0, 0))
    consts = [jnp.asarray(tri, bf16), jnp.asarray(esel, bf16)]
    biases = [jnp.asarray(bq), jnp.asarray(bv), jnp.asarray(bvs)]
    order = [0, 1, 2, 3, 4, 5, 6, 7, 8, 9, 10]
    outs = pl.pallas_call(
        functools.partial(_inproj_kernel, tm=tm, tiles_per_seq=tiles_per_seq),
        out_shape=[jax.ShapeDtypeStruct((T, widths[k]), out_dtypes[k]) for k in order],
        grid=(T // tm,),
        in_specs=[row_spec(D), mod_spec, mod_spec, _const_spec((1, D)), _const_spec((1, LANES))]
        + [_const_spec(c.shape) for c in consts]
        + [_const_spec(weights[k].shape) for k in order]
        + [_const_spec(b.shape) for b in biases],
        out_specs=[row_spec(widths[k]) for k in order],
        scratch_shapes=[pltpu.VMEM((8, LANES), f32)],
        compiler_params=_cparams("arbitrary"),
        name="inproj",
    )(x2, sc1, sh1, g_pre.reshape(1, D), bfg, *consts, *[weights[k] for k in order], *biases)
    return outs


def _fox_kernel(q_ref, k_ref, v_ref, o_ref, *, tq):
    i = pl.program_id(2)
    row = lax.broadcasted_iota(i32, (tq, tq), 0)
    col = lax.broadcasted_iota(i32, (tq, tq), 1)
    qs = [q_ref[0, :, hh * LANES:(hh + 1) * LANES] for hh in range(2)]

    def tile(j, carry, diag):
        start = pl.multiple_of(j * tq, tq)
        new = []
        for hh in range(2):
            m, acc = carry[hh]
            k = k_ref[0, pl.ds(start, tq), hh * LANES:(hh + 1) * LANES]
            v = v_ref[0, pl.ds(start, tq), hh * LANES:(hh + 1) * LANES]
            s = lax.dot_general(qs[hh], k, _NT, preferred_element_type=f32)
            if diag:
                s = jnp.where(col <= row, s, NEG)
            m_new = jnp.maximum(m, jnp.max(s, axis=-1, keepdims=True))
            p = jnp.exp(s - m_new).astype(bf16)
            acc = jnp.exp(m - m_new) * acc + jnp.dot(p, v, preferred_element_type=f32)
            new.append((m_new, acc))
        return tuple(new)

    carry = tuple((jnp.full((tq, 1), NEG, f32), jnp.zeros((tq, LANES), f32)) for _ in range(2))
    carry = tile(i, carry, True)
    carry = lax.fori_loop(0, i, lambda j, c: tile(j, c, False), carry)
    o_ref[0] = _pair_out(carry[0][1], carry[1][1]).astype(bf16)


def _fox(fq, fk, fv, B, S):
    tq = min(FOX_TILE, S)
    nq = S // tq
    q3 = fq.reshape(B, S, FOX_HEADS * LANES)
    k3 = fk.reshape(B, S, FOX_HEADS * LANES)
    v3 = fv.reshape(B, S, FOX_HEADS * LANES)
    return pl.pallas_call(
        functools.partial(_fox_kernel, tq=tq),
        out_shape=jax.ShapeDtypeStruct((B, S, FOX_WIDTH), bf16),
        grid=(B, FOX_HEADS // 2, nq),
        in_specs=[pl.BlockSpec((1, tq, 2 * LANES), lambda b, hp, i: (b, i, hp)),
                  pl.BlockSpec((1, S, 2 * LANES), lambda b, hp, i: (b, 0, hp)),
                  pl.BlockSpec((1, S, 2 * LANES), lambda b, hp, i: (b, 0, hp))],
        out_specs=pl.BlockSpec((1, tq, LANES), lambda b, hp, i: (b, i, hp)),
        compiler_params=_cparams("parallel", "parallel", "arbitrary"),
        name="fox",
    )(q3, k3, v3)


def _compress_kernel(x_ref, pea_ref, peb_ref, wa_ref, wb_ref, okc, ovc, *, nc):
    x = x_ref[0].astype(f32)
    xa = (x + pea_ref[...]).astype(bf16)
    xb = (x + peb_ref[...]).astype(bf16)
    a = jnp.dot(xa, wa_ref[...], preferred_element_type=f32)
    b = jnp.dot(xb, wb_ref[...], preferred_element_type=f32)
    out = a + pltpu.roll(b, nc - 1, 0)
    okc[0] = out[:, :2 * LANES].astype(bf16)
    ovc[0] = out[:, 2 * LANES:].astype(bf16)


def _compress(cm, pe_k, pe_v, w_cmp_k, w_cmp_v, B, S):
    nc = S // CMP_STRIDE
    half = CMP_BLOCK // 2
    win = half * 2 * LANES
    x = cm.reshape(B, nc, win)
    wk = w_cmp_k.reshape(CMP_BLOCK, HEAD_DIM, HEAD_DIM)
    wv = w_cmp_v.reshape(CMP_BLOCK, HEAD_DIM, HEAD_DIM)
    H = HEAD_DIM

    def build(wk_h, wv_h):
        w = jnp.zeros((half, 2 * LANES, 6 * LANES), f32)
        w = w.at[:, 0:H, H:2 * H].set(wk_h)
        w = w.at[:, H:2 * H, 3 * H:4 * H].set(wk_h)
        w = w.at[:, 2 * H:3 * H, 4 * H:5 * H].set(wv_h)
        w = w.at[:, 2 * H:3 * H, 7 * H:8 * H].set(wv_h)
        w = w.at[:, 3 * H:4 * H, 8 * H:9 * H].set(wv_h)
        w = w.at[:, 3 * H:4 * H, 11 * H:12 * H].set(wv_h)
        return w.reshape(win, 6 * LANES).astype(bf16)

    wa = build(wk[:half], wv[:half])
    wb = build(wk[half:], wv[half:])

    def pe_row(pk, pv):
        return jnp.concatenate([pk, pk, pv, pv], axis=1).reshape(1, win)

    pea = pe_row(pe_k[:half], pe_v[:half])
    peb = pe_row(pe_k[half:], pe_v[half:])
    return pl.pallas_call(
        functools.partial(_compress_kernel, nc=nc),
        out_shape=[jax.ShapeDtypeStruct((B, nc, 2 * LANES), bf16),
                   jax.ShapeDtypeStruct((B, nc, 4 * LANES), bf16)],
        grid=(B,),
        in_specs=[pl.BlockSpec((1, nc, win), lambda b: (b, 0, 0)),
                  _const_spec((1, win)), _const_spec((1, win)),
                  _const_spec((win, 6 * LANES)), _const_spec((win, 6 * LANES))],
        out_specs=[pl.BlockSpec((1, nc, 2 * LANES), lambda b: (b, 0, 0)),
                   pl.BlockSpec((1, nc, 4 * LANES), lambda b: (b, 0, 0))],
        compiler_params=_cparams("parallel"),
        name="compress",
    )(x, pea, peb, wa, wb)


def _cmpsel_kernel(q_ref, kc_ref, vc_ref, pat_ref, ov_ref, ocmp, osel, *, tq, nc, n_sel, top_n, past):
    i = pl.program_id(1)
    kc = kc_ref[0]
    c0 = i * (tq // CMP_STRIDE) - past
    wio = lax.broadcasted_iota(i32, (LANES, nc), 0)
    cio = lax.broadcasted_iota(i32, (LANES, nc), 1)
    shift = jnp.where(cio == wio + c0, 1.0, 0.0).astype(bf16)
    t = i * tq + lax.broadcasted_iota(i32, (tq, nc), 0)
    cend = lax.broadcasted_iota(i32, (tq, nc), 1) * CMP_STRIDE + (CMP_BLOCK - 1)
    valid = cend <= t
    pcs = jnp.zeros((tq, nc), f32)
    outs = []
    for hh in range(NSA_HPG):
        q = q_ref[0, :, hh * LANES:(hh + 1) * LANES]
        pat = pat_ref[hh]
        pat_hi = pat.astype(bf16)
        pat_lo = (pat - pat_hi.astype(f32)).astype(bf16)
        cb = (jnp.dot(pat_hi, shift, preferred_element_type=f32)
              + jnp.dot(pat_lo, shift, preferred_element_type=f32))
        lc = jnp.where(valid, lax.dot_general(q, kc, _NT, preferred_element_type=f32) + cb, NEG)
        m = jnp.max(lc, axis=-1, keepdims=True)
        p = jnp.where(valid, jnp.exp(lc - m), 0.0)
        l = jnp.sum(p, axis=-1, keepdims=True)
        pc = p * jnp.where(l > 0.0, 1.0 / l, 0.0)
        pcs = pcs + pc
        v = vc_ref[0, :, (hh % 2) * LANES:(hh % 2 + 1) * LANES]
        outs.append(jnp.dot(pc.astype(bf16), v, preferred_element_type=f32))
    ocmp[0] = jnp.concatenate([outs[0] + outs[1], outs[2] + outs[3]], axis=1).astype(bf16)

    hi = pcs.astype(bf16)
    lo = (pcs - hi.astype(f32)).astype(bf16)
    ov = ov_ref[...]
    imp = (lax.dot_general(ov, hi, _NT, preferred_element_type=f32)
           + lax.dot_general(ov, lo, _NT, preferred_element_type=f32))
    jio = lax.broadcasted_iota(i32, (n_sel, tq), 0)
    t = i * tq + lax.broadcasted_iota(i32, (n_sel, tq), 1)
    cur = t // SEL_BLOCK
    forced = (jio == 0) | (jio == cur) | (jio == cur - 1)
    score = jnp.where(forced, BIG, jnp.where(jio <= cur, imp, -BIG))
    rank = jnp.zeros((n_sel, tq), f32)
    for jp in range(n_sel):
        r = score[jp:jp + 1, :]
        tie = jnp.where(jio > jp, 1.0, 0.0)
        rank = rank + jnp.where(r > score, 1.0, jnp.where(r == score, tie, 0.0))
    selb = jnp.where(rank < top_n, 0.0, -SEL_MASK)
    padded = jnp.concatenate([selb, jnp.zeros((LANES - n_sel, tq), f32)], axis=0)
    osel[0, 0] = padded.T.astype(bf16)


def _bucket_bounds():
    n = np.arange(0, 4 * REL_MAX_DIST)
    max_exact = REL_BUCKETS // 2
    nf = np.maximum(n, 1).astype(np.float32)
    large = max_exact + (np.log(nf / np.float32(max_exact)) / np.float32(np.log(REL_MAX_DIST / max_exact))
                         * np.float32(REL_BUCKETS - max_exact)).astype(np.int32)
    bucket = np.where(n < max_exact, n, np.minimum(large, REL_BUCKETS - 1))
    return [int(n[bucket > b].min()) for b in range(REL_BUCKETS - 1)]


_BOUNDS = _bucket_bounds()
_CMP_PAST = (_BOUNDS[-1] + CMP_BLOCK - 1 + CMP_STRIDE - 1) // CMP_STRIDE - 1


def _rel_bias_of(d, rb_ref, h):
    far = rb_ref[(REL_BUCKETS - 1) * NSA_HEADS + h]
    v = jnp.zeros(d.shape, f32)
    for b in reversed(range(REL_BUCKETS - 1)):
        v = jnp.where(d < _BOUNDS[b], rb_ref[b * NSA_HEADS + h] - far, v)
    return v


def _biasgen_kernel(rb_ref, oslc, owin, ocmp, *, ts, tw, tc, nk):
    h = pl.program_id(0)

    def tile(t, off, window):
        lo, hi = off - (t - 1), off + (t - 1)
        if hi < 0 or (window is not None and lo >= window):
            return jnp.full((t, t), NEG, f32)
        d = lax.broadcasted_iota(i32, (t, t), 0) - lax.broadcasted_iota(i32, (t, t), 1) + off
        val = _rel_bias_of(d, rb_ref, h) if lo < _BOUNDS[-1] else jnp.zeros((t, t), f32)
        if lo < 0:
            val = jnp.where(d >= 0, val, NEG)
        if window is not None and hi >= window:
            val = jnp.where(d < window, val, NEG)
        return val

    oslc[0, 0, 0] = tile(ts, 0, None)
    oslc[0, 1, 0] = tile(ts, ts, None)
    oslc[0, 2, 0] = jnp.full((ts, ts), NEG, f32)
    for v in range(nk):
        for cc in range(nk):
            owin[0, v, 0, :, cc * tw:(cc + 1) * tw] = tile(tw, (v - cc) * tw, WINDOW)
    rr = lax.broadcasted_iota(i32, (tc, LANES), 0)
    w = lax.broadcasted_iota(i32, (tc, LANES), 1)
    d = rr - CMP_STRIDE * (w - _CMP_PAST) - (CMP_BLOCK - 1)
    ocmp[0] = jnp.where(d >= 0, _rel_bias_of(d, rb_ref, h), 0.0)


def _biasgen(rel_bias, S):
    ts, tw, tc = min(SLC_TILE, S), min(WIN_TILE, S), min(CMP_TILE, S)
    nk = WINDOW // tw + 1
    assert WINDOW % tw == 0 and S >= nk * tw and min(ts, tw) + 1 >= _BOUNDS[-1]
    assert tc // CMP_STRIDE + _CMP_PAST <= LANES
    G = NSA_KV_GROUPS
    return pl.pallas_call(
        functools.partial(_biasgen_kernel, ts=ts, tw=tw, tc=tc, nk=nk),
        out_shape=[jax.ShapeDtypeStruct((G, 3, NSA_HPG, ts, ts), f32),
                   jax.ShapeDtypeStruct((G, nk, NSA_HPG, tw, nk * tw), f32),
                   jax.ShapeDtypeStruct((NSA_HEADS, tc, LANES), f32)],
        grid=(NSA_HEADS,),
        in_specs=[pl.BlockSpec(memory_space=pltpu.SMEM)],
        out_specs=[pl.BlockSpec((1, 3, 1, ts, ts), lambda h: (h // NSA_HPG, 0, h % NSA_HPG, 0, 0)),
                   pl.BlockSpec((1, nk, 1, tw, nk * tw), lambda h: (h // NSA_HPG, 0, h % NSA_HPG, 0, 0)),
                   pl.BlockSpec((1, tc, LANES), lambda h: (h, 0, 0))],
        compiler_params=_cparams("arbitrary"),
        name="biasgen",
    )(rel_bias.reshape(-1))


def _cmpsel(nq_arr, kc, vc, pat, B, S):
    tq = min(CMP_TILE, S)
    nqt = S // tq
    nc = S // CMP_STRIDE
    n_sel = S // SEL_BLOCK
    top_n = min(SEL_TOPN, n_sel)
    G = NSA_KV_GROUPS
    assert n_sel <= HEAD_DIM
    c = np.arange(nc)[None, :]
    j = np.arange(n_sel)[:, None]
    ov = ((c * CMP_STRIDE < j * SEL_BLOCK + SEL_BLOCK) & (c * CMP_STRIDE + CMP_BLOCK > j * SEL_BLOCK)
          & (c < nc - 1)).astype(np.float32)
    q3 = nq_arr.reshape(B, S, NSA_HEADS * LANES)
    return pl.pallas_call(
        functools.partial(_cmpsel_kernel, tq=tq, nc=nc, n_sel=n_sel, top_n=top_n, past=_CMP_PAST),
        out_shape=[jax.ShapeDtypeStruct((B, S, NSA_HEADS * HEAD_DIM), bf16),
                   jax.ShapeDtypeStruct((B, G, S, LANES), bf16)],
        grid=(G, nqt, B),
        in_specs=[pl.BlockSpec((1, tq, NSA_HPG * LANES), lambda g, i, b: (b, i, g)),
                  pl.BlockSpec((1, nc, LANES), lambda g, i, b: (b, 0, g)),
                  pl.BlockSpec((1, nc, 2 * LANES), lambda g, i, b: (b, 0, g)),
                  pl.BlockSpec((NSA_HPG, tq, LANES), lambda g, i, b: (g, 0, 0)),
                  _const_spec((n_sel, nc))],
        out_specs=[pl.BlockSpec((1, tq, 2 * LANES), lambda g, i, b: (b, i, g)),
                   pl.BlockSpec((1, 1, tq, LANES), lambda g, i, b: (b, g, i, 0))],
        compiler_params=_cparams("parallel", "arbitrary", "arbitrary"),
        name="cmpsel",
    )(q3, kc, vc, pat, jnp.asarray(ov, bf16))


def _pair_out(acc_e, acc_o):
    lane = lax.broadcasted_iota(i32, acc_e.shape, 1)
    return jnp.where(lane < HEAD_DIM, acc_e / acc_e[:, HEAD_DIM:HEAD_DIM + 1], acc_o / acc_o[:, 0:1])


def _slc_kernel(q_ref, sb_ref, k_ref, v_ref, bias_ref, o_ref, *, tq):
    i = pl.program_id(2)
    sb = sb_ref[0, 0]
    qs = [q_ref[0, :, h * LANES:(h + 1) * LANES] + sb for h in range(NSA_HPG)]

    def tile(j, carry, bias_idx):
        start = pl.multiple_of(j * tq, tq)
        k = k_ref[0, pl.ds(start, tq), :]
        v = v_ref[0, pl.ds(start, tq), :]
        new = []
        for h in range(NSA_HPG):
            m, acc = carry[h]
            s = lax.dot_general(qs[h], k, _NT, preferred_element_type=f32)
            if bias_idx is not None:
                s = s + bias_ref[0, bias_idx, h]
            m_new = jnp.maximum(m, jnp.max(s, axis=-1, keepdims=True))
            p = jnp.exp(s - m_new).astype(bf16)
            vh = v[:, (h % 2) * LANES:(h % 2 + 1) * LANES]
            acc = jnp.exp(m - m_new) * acc + jnp.dot(p, vh, preferred_element_type=f32)
            new.append((m_new, acc))
        return tuple(new)

    carry = tuple((jnp.full((tq, 1), NEG, f32), jnp.zeros((tq, LANES), f32)) for _ in range(NSA_HPG))
    carry = tile(i, carry, 0)
    carry = tile(jnp.maximum(i - 1, 0), carry, jnp.where(i >= 1, 1, 2))
    carry = lax.fori_loop(0, jnp.maximum(i - 1, 0), lambda j, c: tile(j, c, None), carry)
    o_ref[0] = jnp.concatenate([_pair_out(carry[0][1], carry[1][1]),
                                _pair_out(carry[2][1], carry[3][1])], axis=1).astype(bf16)


def _win_kernel(q_ref, k_ref, v_ref, bias_ref, o_ref, *, tq, nk):
    i = pl.program_id(2)
    start = pl.multiple_of(jnp.maximum(i - (nk - 1), 0) * tq, tq)
    k = k_ref[0, pl.ds(start, nk * tq), :]
    v = v_ref[0, pl.ds(start, nk * tq), :]
    accs = []
    for h in range(NSA_HPG):
        q = q_ref[0, :, h * LANES:(h + 1) * LANES]
        s = lax.dot_general(q, k, _NT, preferred_element_type=f32) + bias_ref[0, 0, h]
        p = jnp.exp(s - jnp.max(s, axis=-1, keepdims=True)).astype(bf16)
        accs.append(jnp.dot(p, v[:, (h % 2) * LANES:(h % 2 + 1) * LANES], preferred_element_type=f32))
    o_ref[0] = jnp.concatenate([_pair_out(accs[0], accs[1]), _pair_out(accs[2], accs[3])], axis=1).astype(bf16)


def _nsa_flash(kind, nq_arr, selb, k_arr, v_arr, bias, B, S):
    tq = bias.shape[3]
    nqt = S // tq
    G = NSA_KV_GROUPS
    q3 = nq_arr.reshape(B, S, NSA_HEADS * LANES)
    k3 = k_arr.reshape(B, S, G * LANES)
    v3 = v_arr.reshape(B, S, G * 2 * LANES)
    q_spec = pl.BlockSpec((1, tq, NSA_HPG * LANES), lambda g, b, i: (b, i, g))
    k_spec = pl.BlockSpec((1, S, LANES), lambda g, b, i: (b, 0, g))
    v_spec = pl.BlockSpec((1, S, 2 * LANES), lambda g, b, i: (b, 0, g))
    if kind == "slc":
        kern = functools.partial(_slc_kernel, tq=tq)
        extra_specs = [pl.BlockSpec((1, 1, tq, LANES), lambda g, b, i: (b, g, i, 0))]
        extra = [selb]
        b_spec = pl.BlockSpec((1, 3, NSA_HPG, tq, tq), lambda g, b, i: (g, 0, 0, 0, 0),
                              pipeline_mode=pl.Buffered(1))
    else:
        nk = bias.shape[1]
        kern = functools.partial(_win_kernel, tq=tq, nk=nk)
        extra_specs, extra = [], []
        b_spec = pl.BlockSpec((1, 1, NSA_HPG, tq, nk * tq), lambda g, b, i: (g, jnp.minimum(i, nk - 1), 0, 0, 0))
    return pl.pallas_call(
        kern,
        out_shape=jax.ShapeDtypeStruct((B, S, NSA_HEADS * HEAD_DIM), bf16),
        grid=(G, B, nqt),
        in_specs=[q_spec] + extra_specs + [k_spec, v_spec, b_spec],
        out_specs=pl.BlockSpec((1, tq, 2 * LANES), lambda g, b, i: (b, i, g)),
        compiler_params=_cparams("parallel", "parallel", "arbitrary"),
        name=kind,
    )(q3, *extra, k3, v3, bias)


def _post_kernel(x_ref, ofox, ocmp, oslc, owin, mg_ref, sm_ref, ga1, sc2, sh2, gpost, gpre,
                 wfp, wnp_, wmo, wr, br, eg, tris,
                 x1_ref, h2_ref, route_ref, cnt_ref, carry_ref, *, tm):
    i = pl.program_id(0)
    W = NSA_WIDTH
    gates = jax.nn.sigmoid(sm_ref[...]).astype(bf16)
    gx = jnp.dot(gates, eg[...], preferred_element_type=f32)
    nsa = (gx[:, :W] * ocmp[...].astype(f32) + gx[:, W:2 * W] * oslc[...].astype(f32)
           + gx[:, 2 * W:] * owin[...].astype(f32))
    y_nsa = jnp.dot(nsa.astype(bf16), wnp_[...], preferred_element_type=f32)
    y_fox = jnp.dot(ofox[...], wfp[...], preferred_element_type=f32)
    mg = mg_ref[...].astype(f32)
    mix = (mg[:, :D_MODEL] * y_fox + mg[:, D_MODEL:] * y_nsa).astype(bf16)
    mixed = jnp.dot(mix, wmo[...], preferred_element_type=f32)
    x1 = x_ref[...] + ga1[0] * _rms(mixed, gpost[...])
    x1_ref[...] = x1
    h2 = _rms(x1, gpre[...]) * (1.0 + sc2[0]) + sh2[0]
    _store_tile_rows(h2_ref, h2)

    lane = lax.broadcasted_iota(i32, (tm, LANES), 1)
    logits = jnp.dot(h2.astype(bf16), wr[...], preferred_element_type=f32) + br[...]
    l = jnp.where(lane < N_EXPERTS, logits, NEG)
    vals, idxs = [], []
    for _ in range(TOP_K):
        m = jnp.max(l, axis=-1, keepdims=True)
        idx = jnp.min(jnp.where(l == m, lane, LANES), axis=-1, keepdims=True)
        vals.append(m)
        idxs.append(idx)
        l = jnp.where(lane == idx, NEG, l)
    es = [jnp.exp(v - vals[0]) for v in vals]
    den = es[0] + es[1] + es[2] + es[3]

    @pl.when(i == 0)
    def _():
        carry_ref[...] = jnp.zeros_like(carry_ref)

    hot = [lane == idx for idx in idxs]
    cnt = sum(jnp.where(h, 1.0, 0.0) for h in hot)
    base = jnp.dot(tris[...], cnt.astype(bf16), preferred_element_type=f32) + carry_ref[0:1, :]
    new_carry = base[tm - 1:tm, :] + cnt[tm - 1:tm, :]
    carry_ref[0:1, :] = new_carry
    cnt_ref[...] = jnp.broadcast_to(new_carry, cnt_ref.shape)
    route = jnp.zeros((tm, LANES), f32)
    for k in range(TOP_K):
        pos = jnp.sum(jnp.where(hot[k], base, 0.0), axis=-1, keepdims=True)
        route = jnp.where(lane == k, idxs[k].astype(f32), route)
        route = jnp.where(lane == TOP_K + k, pos, route)
        route = jnp.where(lane == 2 * TOP_K + k, es[k] / den, route)
    route_ref[...] = route


def _post(x2, ofox, ocmp, oslc, owin, mg, sm, ga1, sc2, sh2, g_post, g_pre2,
          w_fox_proj, w_nsa_proj, w_mix_out, w_router, b_router, S):
    T, D = x2.shape
    tm = min(ROW_TILE, S)
    tiles_per_seq = S // tm
    W = NSA_WIDTH
    eg = np.zeros((LANES, 3 * W), np.float32)
    for h in range(NSA_HEADS):
        for k in range(3):
            eg[FOX_HEADS + 3 * h + k, k * W + h * HEAD_DIM:k * W + (h + 1) * HEAD_DIM] = 1.0
    tris = np.tril(np.ones((tm, tm), np.float32), -1)
    wr = jnp.concatenate([w_router, jnp.zeros((D, LANES - N_EXPERTS), f32)], axis=1).astype(bf16)
    br = jnp.concatenate([b_router, jnp.zeros((LANES - N_EXPERTS,), f32)]).reshape(1, LANES)
    row = lambda n: pl.BlockSpec((tm, n), lambda i: (i, 0))
    mod = pl.BlockSpec((1, 1, D), lambda i: (i // tiles_per_seq, 0, 0))
    consts = [w_fox_proj.astype(bf16), w_nsa_proj.astype(bf16), w_mix_out.astype(bf16), wr, br,
              jnp.asarray(eg, bf16), jnp.asarray(tris, bf16)]
    return pl.pallas_call(
        functools.partial(_post_kernel, tm=tm),
        out_shape=[jax.ShapeDtypeStruct((T, D), f32), jax.ShapeDtypeStruct((T * ROW_TILES, LANES), f32),
                   jax.ShapeDtypeStruct((T, LANES), f32), jax.ShapeDtypeStruct((8, LANES), f32)],
        grid=(T // tm,),
        in_specs=[row(D), row(FOX_WIDTH), row(W), row(W), row(W), row(2 * D), row(LANES),
                  mod, mod, mod, _const_spec((1, D)), _const_spec((1, D))]
        + [_const_spec(c.shape) for c in consts],
        out_specs=[row(D), pl.BlockSpec((tm * ROW_TILES, LANES), lambda i: (i, 0)), row(LANES),
                   pl.BlockSpec((8, LANES), lambda i: (0, 0))],
        scratch_shapes=[pltpu.VMEM((8, LANES), f32)],
        compiler_params=_cparams("arbitrary"),
        name="post",
    )(x2, ofox, ocmp, oslc, owin, mg, sm, ga1, sc2, sh2, g_post.reshape(1, D), g_pre2.reshape(1, D), *consts)


def _moe_kernel(be_ref, nu_ref, tok0_ref, tokn_ref, dstp_ref, h_hbm, wg, bg, wu, bu, wd, bd,
                y_hbm, xbuf, ybuf, wgb, wub, wdb, gsem, ssem, *, bm):
    i = pl.program_id(0)
    nu = nu_ref[0]
    rt = ROW_TILES

    def rows(r):
        return pl.ds(r * rt, rt) if isinstance(r, int) else pl.ds(pl.multiple_of(r * rt, rt), rt)

    def gather_copy(tok_ref, r, s):
        return pltpu.make_async_copy(h_hbm.at[tok_ref[0, 0, r]], xbuf.at[s, rows(r), :], gsem.at[s])

    def scatter_copy(r, s):
        return pltpu.make_async_copy(ybuf.at[s, rows(r), :], y_hbm.at[dstp_ref[0, 0, r]], ssem.at[s])

    def wait_gather(s):
        pltpu.make_async_copy(xbuf.at[s], xbuf.at[s], gsem.at[s]).wait()

    def wait_scatter(s):
        pltpu.make_async_copy(ybuf.at[s], ybuf.at[s], ssem.at[s]).wait()

    @pl.when(i == 0)
    def _():
        ybuf[1] = jnp.zeros((bm * rt, LANES), f32)
        n_real = y_hbm.shape[0] - 2 * bm

        def issue(r, c):
            gather_copy(tok0_ref, r, 0).start()
            for half in range(2):
                pltpu.make_async_copy(ybuf.at[1, rows(r), :], y_hbm.at[n_real + half * bm + r],
                                      ssem.at[1]).start()
            return c

        lax.fori_loop(0, bm, issue, 0)
        for half in range(2):
            wait_scatter(1)

    def step(slot):
        other = 1 - slot
        wait_gather(slot)

        @pl.when(i >= 1)
        def _():
            wait_scatter(slot)

        @pl.when(jnp.logical_or(i == 0, be_ref[i] != be_ref[jnp.maximum(i - 1, 0)]))
        def _():
            wgb[...] = wg[0].astype(bf16)
            wub[...] = wu[0].astype(bf16)
            wdb[...] = wd[0].astype(bf16)

        for r in range(bm):
            gather_copy(tokn_ref, r, other).start(priority=r % 2)
            scatter_copy(r, other).start(priority=(r + 1) % 2)
        x = _load_tile_rows(xbuf.at[slot], bm).astype(bf16)
        g = jnp.dot(x, wgb[...], preferred_element_type=f32) + bg[0]
        u = jnp.dot(x, wub[...], preferred_element_type=f32) + bu[0]
        gt = jnp.minimum(g, SWIGLU_LIMIT)
        up = jnp.clip(u, -SWIGLU_LIMIT, SWIGLU_LIMIT)
        a = (gt * jax.nn.sigmoid(SWIGLU_ALPHA * gt) * (up + 1.0)).astype(bf16)
        _store_tile_rows(ybuf.at[slot], jnp.dot(a, wdb[...], preferred_element_type=f32) + bd[0])

    def drain(slot):
        other = 1 - slot
        wait_gather(slot)
        wait_scatter(slot)

        def issue(r, c):
            scatter_copy(r, other).start()
            return c

        lax.fori_loop(0, bm, issue, 0)
        wait_scatter(other)

    for s in range(2):
        pl.when(jnp.logical_and(i < nu, i % 2 == s))(functools.partial(step, s))
        pl.when(jnp.logical_and(i == nu, i % 2 == s))(functools.partial(drain, s))


def _moe(blk_e, n_used, row_tok, row_dst, h2t, w_gate, b_gate, w_up, b_up, w_down, b_down, n_rows):
    D = D_MODEL
    E, _, F = w_gate.shape
    nbt = row_tok.shape[0]
    bm = MOE_BM
    wsel = lambda i, be, nu: (be[jnp.minimum(i, nu[0] - 1)], 0, 0)
    idx_spec = lambda f: pl.BlockSpec((1, 1, bm), lambda i, be, nu: (f(i), 0, 0), memory_space=pltpu.SMEM)
    return pl.pallas_call(
        functools.partial(_moe_kernel, bm=bm),
        out_shape=jax.ShapeDtypeStruct((n_rows, ROW_TILES, LANES), f32),
        grid_spec=pltpu.PrefetchScalarGridSpec(
            num_scalar_prefetch=2, grid=(nbt,),
            in_specs=[idx_spec(lambda i: 0),
                      idx_spec(lambda i: jnp.minimum(i + 1, nbt - 1)),
                      idx_spec(lambda i: jnp.maximum(i - 1, 0)),
                      pl.BlockSpec(memory_space=pl.ANY),
                      pl.BlockSpec((1, D, F), wsel), pl.BlockSpec((1, 1, F), wsel),
                      pl.BlockSpec((1, D, F), wsel), pl.BlockSpec((1, 1, F), wsel),
                      pl.BlockSpec((1, F, D), wsel), pl.BlockSpec((1, 1, D), wsel)],
            out_specs=pl.BlockSpec(memory_space=pl.ANY),
            scratch_shapes=[pltpu.VMEM((2, bm * ROW_TILES, LANES), f32), pltpu.VMEM((2, bm * ROW_TILES, LANES), f32),
                            pltpu.VMEM((D, F), bf16), pltpu.VMEM((D, F), bf16), pltpu.VMEM((F, D), bf16),
                            pltpu.SemaphoreType.DMA((2,)), pltpu.SemaphoreType.DMA((2,))]),
        compiler_params=_cparams("arbitrary"),
        name="moe",
    )(blk_e, n_used, row_tok, row_tok, row_dst, h2t.reshape(-1, ROW_TILES, LANES),
      w_gate, b_gate.reshape(E, 1, F), w_up, b_up.reshape(E, 1, F), w_down, b_down.reshape(E, 1, D))


def _combine_kernel(y0, y1, y2, y3, route_ref, x1_ref, ga2, gpost, o_ref):
    route = route_ref[...]
    tm = o_ref.shape[0]
    y = jnp.zeros(o_ref.shape, f32)
    for k, yk in enumerate((y0, y1, y2, y3)):
        y = y + route[:, 2 * TOP_K + k:2 * TOP_K + k + 1] * _load_tile_rows(yk, tm)
    o_ref[...] = x1_ref[...] + ga2[0] * _rms(y, gpost[...])


def _combine(y4, route, x1, ga2, g_post2, S):
    T, D = x1.shape
    tm = min(MOE_ROWS, S)
    tiles_per_seq = S // tm
    nt = T // tm
    y4 = y4.reshape(-1, LANES)
    y_spec = lambda k: pl.BlockSpec((tm * ROW_TILES, LANES), lambda i: (k * nt + i, 0))
    return pl.pallas_call(
        _combine_kernel,
        out_shape=jax.ShapeDtypeStruct((T, D), f32),
        grid=(nt,),
        in_specs=[y_spec(k) for k in range(TOP_K)]
        + [pl.BlockSpec((tm, LANES), lambda i: (i, 0)),
           pl.BlockSpec((tm, D), lambda i: (i, 0)),
           pl.BlockSpec((1, 1, D), lambda i: (i // tiles_per_seq, 0, 0)),
           pl.BlockSpec((1, D), lambda i: (0, 0))],
        out_specs=pl.BlockSpec((tm, D), lambda i: (i, 0)),
        compiler_params=_cparams("parallel"),
        name="combine",
    )(y4, y4, y4, y4, route, x1, ga2, g_post2.reshape(1, D))


def kernel(x, c, w_ada, b_ada, g_mix_pre, g_mix_post, w_in, b_forget, pe_k, pe_v, w_cmp_k, w_cmp_v, w_fox_proj, w_nsa_proj, w_mix_out, rel_bias, g_ffn_pre, g_ffn_post, w_router, b_router, w_gate, b_gate, w_up, b_up, w_down, b_down):
    B, S, D = x.shape
    T = B * S
    for l in range(w_ada.shape[0]):
        x2 = x.reshape(T, D)
        ada = _ada(c, w_ada[l], b_ada[l])
        sh1, sc1, ga1, sh2, sc2, ga2 = [a.reshape(B, 1, D) for a in jnp.split(ada, 6, axis=-1)]
        fq, fk, fv, nq, cm, ksl, vsl, kwn, vwn, mg, sm = _inproj(x2, sc1, sh1, g_mix_pre[l], w_in[l], b_forget[l], S)
        o_fox = _fox(fq, fk, fv, B, S)
        kc, vc = _compress(cm, pe_k[l], pe_v[l], w_cmp_k[l], w_cmp_v[l], B, S)
        bias_slc, bias_win, pat_cmp = _biasgen(rel_bias, S)
        o_cmp, selb = _cmpsel(nq, kc, vc, pat_cmp, B, S)
        o_slc = _nsa_flash("slc", nq, selb, ksl, vsl, bias_slc, B, S)
        o_win = _nsa_flash("win", nq, None, kwn, vwn, bias_win, B, S)
        x1, h2, route, cnt = _post(x2, o_fox.reshape(T, -1), o_cmp.reshape(T, -1), o_slc.reshape(T, -1),
                                   o_win.reshape(T, -1), mg, sm, ga1, sc2, sh2, g_mix_post[l], g_ffn_pre[l],
                                   w_fox_proj[l], w_nsa_proj[l], w_mix_out[l], w_router[l], b_router[l], S)
        counts = cnt[0, :N_EXPERTS].astype(i32)
        nblk = (counts + MOE_BM - 1) // MOE_BM
        blk_end = jnp.cumsum(nblk)
        pad_start = (blk_end - nblk) * MOE_BM
        top_i = route[:, :TOP_K].astype(i32)
        A = T * TOP_K
        nbt = -(-A // MOE_BM) + N_EXPERTS + 1
        n_used = blk_end[-1:].astype(i32)
        blk_e = jnp.minimum(jnp.sum(jnp.arange(nbt)[:, None] >= blk_end[None, :], axis=1), N_EXPERTS - 1).astype(i32)
        a_sorted = jnp.sort((top_i * A + jnp.arange(A, dtype=i32).reshape(T, TOP_K)).reshape(-1)) % A
        grp_start = jnp.cumsum(counts) - counts
        j = jnp.arange(MOE_BM, dtype=i32)[None, :]
        b = jnp.arange(nbt, dtype=i32)[:, None]
        r_in_e = b * MOE_BM + j - pad_start[blk_e][:, None]
        valid = (b < n_used[0]) & (r_in_e < counts[blk_e][:, None])
        row_a = a_sorted[jnp.clip(grp_start[blk_e][:, None] + r_in_e, 0, A - 1)]
        row_tok = jnp.where(valid, row_a // TOP_K, 0)
        row_dst = jnp.where(valid, (row_a % TOP_K) * T + row_a // TOP_K, A + (b % 2) * MOE_BM + j)
        y4 = _moe(blk_e, n_used, row_tok.reshape(nbt, 1, MOE_BM), row_dst.reshape(nbt, 1, MOE_BM), h2,
                  w_gate[l], b_gate[l], w_up[l], b_up[l], w_down[l], b_down[l], A + 2 * MOE_BM)
        x = _combine(y4, route, x1, ga2, g_ffn_post[l], S).reshape(B, S, D)
    return x
```

```python
import functools

import numpy as np
import jax
import jax.numpy as jnp
from jax import lax
from jax.experimental import pallas as pl
from jax.experimental.pallas import tpu as pltpu

f32 = jnp.float32
bf16 = jnp.bfloat16
i32 = jnp.int32

D_MODEL = 1024
HEAD_DIM = 64
FOX_HEADS = 8
NSA_HEADS = 8
NSA_KV_GROUPS = 2
NSA_HPG = NSA_HEADS // NSA_KV_GROUPS
FOX_WIDTH = FOX_HEADS * HEAD_DIM
NSA_WIDTH = NSA_HEADS * HEAD_DIM
NSA_KV_WIDTH = NSA_KV_GROUPS * HEAD_DIM
CMP_BLOCK = 32
CMP_STRIDE = 16
SEL_BLOCK = 64
SEL_TOPN = 16
WINDOW = 512
REL_BUCKETS = 32
REL_MAX_DIST = 128
N_EXPERTS = 32
TOP_K = 4
SWIGLU_LIMIT = 7.0
SWIGLU_ALPHA = 1.702
RMS_EPS = 1e-6
NEG = -1e30
BIG = 1e9
IN_SIZES = (FOX_WIDTH, FOX_WIDTH, FOX_WIDTH, FOX_HEADS, NSA_WIDTH,
            NSA_KV_WIDTH, NSA_KV_WIDTH, NSA_KV_WIDTH, NSA_KV_WIDTH, NSA_KV_WIDTH, NSA_KV_WIDTH,
            3 * NSA_HEADS, D_MODEL, D_MODEL)

LANES = 128
VMEM_LIMIT = 56 * 1024 * 1024
FOX_TILE = 512
SLC_TILE = 512
WIN_TILE = 256
CMP_TILE = 256
ROW_TILE = 512
MOE_BM = 512
MOE_ROWS = 256
SEL_MASK = 1e9

_NT = (((1,), (1,)), ((), ()))


def _cparams(*sem):
    return pltpu.CompilerParams(dimension_semantics=sem, vmem_limit_bytes=VMEM_LIMIT)


def _const_spec(shape):
    nd = len(shape)
    return pl.BlockSpec(shape, lambda *_: (0,) * nd, pipeline_mode=pl.Buffered(1))


def _split3(a):
    a1 = a.astype(bf16)
    r1 = a - a1.astype(f32)
    a2 = r1.astype(bf16)
    a3 = (r1 - a2.astype(f32)).astype(bf16)
    return a1, a2, a3


ROW_TILES = D_MODEL // LANES


def _store_tile_rows(ref, val):
    n = val.shape[0]
    for c in range(ROW_TILES):
        ref[pl.ds(c, n, stride=ROW_TILES), :] = val[:, c * LANES:(c + 1) * LANES]


def _load_tile_rows(ref, n):
    return jnp.concatenate([ref[pl.ds(c, n, stride=ROW_TILES), :] for c in range(ROW_TILES)], axis=1)


def _loop_two_tiles(n, body, carry):
    carry = lax.fori_loop(0, n // 2, lambda jj, c: body(2 * jj + 1, body(2 * jj, c)), carry)
    return lax.cond(n % 2 == 1, lambda c: body(n - 1, c), lambda c: c, carry)


def _rms(x, g):
    ms = jnp.mean(x * x, axis=-1, keepdims=True)
    return x * lax.rsqrt(ms + RMS_EPS) * g


def _ada_kernel(c_ref, w_ref, b_ref, o_ref):
    c = c_ref[...]
    s = c * jax.nn.sigmoid(c)
    w = w_ref[...]
    s1, s2, _ = _split3(s)
    w1, w2, _ = _split3(w)
    acc = jnp.dot(s1, w1, preferred_element_type=f32)
    acc += jnp.dot(s1, w2, preferred_element_type=f32)
    acc += jnp.dot(s2, w1, preferred_element_type=f32)
    o_ref[...] = acc + b_ref[...]


def _ada(c, w_ada, b_ada):
    B, D = c.shape
    N = w_ada.shape[1]
    tn = 1024
    return pl.pallas_call(
        _ada_kernel,
        out_shape=jax.ShapeDtypeStruct((B, N), f32),
        grid=(N // tn,),
        in_specs=[pl.BlockSpec((B, D), lambda j: (0, 0)),
                  pl.BlockSpec((D, tn), lambda j: (0, j)),
                  pl.BlockSpec((1, tn), lambda j: (0, j))],
        out_specs=pl.BlockSpec((B, tn), lambda j: (0, j)),
        compiler_params=_cparams("arbitrary"),
        name="ada",
    )(c, w_ada, b_ada.reshape(1, N))


def _inproj_kernel(x_ref, sc_ref, sh_ref, g_ref, bfg_ref, tri_ref, esel_ref,
                   wfq, wfk, wfv, wnq, wcm, wksl, wvsl, wkwn, wvwn, wmg, wsm,
                   bq, bv, bvs,
                   ofq, ofk, ofv, onq, ocm, oksl, ovsl, okwn, ovwn, omg, osm,
                   carry_ref, *, tm, tiles_per_seq):
    i = pl.program_id(0)
    x = x_ref[...]
    h = _rms(x, g_ref[...]) * (1.0 + sc_ref[0]) + sh_ref[0]
    hb = h.astype(bf16)

    def proj(w):
        return jnp.dot(hb, w[...], preferred_element_type=f32)

    ofq[...] = (proj(wfq) + bq[...]).astype(bf16)
    ofv[...] = (proj(wfv) + bv[...]).astype(bf16)
    onq[...] = proj(wnq).astype(bf16)
    ocm[...] = proj(wcm).astype(bf16)
    ovsl[...] = (proj(wvsl) + bvs[...]).astype(bf16)
    okwn[...] = proj(wkwn).astype(bf16)
    ovwn[...] = (proj(wvwn) + bvs[...]).astype(bf16)
    omg[...] = jax.nn.sigmoid(proj(wmg)).astype(bf16)

    row = lax.broadcasted_iota(i32, (tm, 2 * LANES), 0)
    lane = lax.broadcasted_iota(i32, (tm, 2 * LANES), 1)
    blk = ((i % tiles_per_seq) * tm + row) // SEL_BLOCK
    onehot = jnp.where((lane & (LANES - 1)) == blk, 1.0, 0.0)
    oksl[...] = (proj(wksl) + onehot).astype(bf16)

    sm = proj(wsm)
    osm[...] = sm
    z = sm + bfg_ref[...]
    lane1 = lax.broadcasted_iota(i32, (tm, LANES), 1)
    logf = jnp.where(lane1 < FOX_HEADS, jnp.minimum(z, 0.0) - jnp.log(1.0 + jnp.exp(-jnp.abs(z))), 0.0)

    @pl.when(i % tiles_per_seq == 0)
    def _():
        carry_ref[...] = jnp.zeros_like(carry_ref)

    tri = tri_ref[...]
    cum = carry_ref[0:1, :]
    for piece in _split3(logf):
        cum = cum + jnp.dot(tri, piece, preferred_element_type=f32)
    carry_ref[0:1, :] = cum[tm - 1:tm, :]
    ncat = jnp.concatenate(_split3(-cum), axis=1)
    ofk[...] = (proj(wfk) + jnp.dot(ncat, esel_ref[...], preferred_element_type=f32)).astype(bf16)


def _heads_to_lanes(w, lo):
    D = w.shape[0]
    nh = w.shape[1] // HEAD_DIM
    w3 = w.reshape(D, nh, HEAD_DIM)
    z = jnp.zeros_like(w3)
    return jnp.concatenate([w3, z] if lo else [z, w3], axis=2).reshape(D, nh * LANES)


def _heads_even_odd(w):
    D = w.shape[0]
    nh = w.shape[1] // HEAD_DIM
    w4 = w.reshape(D, nh // 2, 2, HEAD_DIM)
    z = jnp.zeros((D, nh // 2, HEAD_DIM), w.dtype)
    even = jnp.concatenate([w4[:, :, 0], z], axis=2)
    odd = jnp.concatenate([z, w4[:, :, 1]], axis=2)
    return jnp.stack([even, odd], axis=2).reshape(D, nh * LANES)


def _group_even_odd(w):
    D = w.shape[0]
    w3 = w.reshape(D, NSA_KV_GROUPS, HEAD_DIM)
    z = jnp.zeros_like(w3)
    return jnp.concatenate([w3, z, z, w3], axis=2).reshape(D, NSA_KV_GROUPS * 2 * LANES)


def _inproj(x2, sc1, sh1, g_pre, w_in, b_forget, S):
    T, D = x2.shape
    tm = min(ROW_TILE, S)
    tiles_per_seq = S // tm
    offs = np.cumsum(IN_SIZES)[:-1].tolist()
    (wfq, wfk, wfv, wff, wnq, wkcm, wvcm, wksl, wvsl, wkwn, wvwn, wng, wmgf, wmgn) = jnp.split(w_in, offs, axis=1)
    scale = HEAD_DIM ** -0.5
    cast = lambda w: w.astype(bf16)
    weights = [
        cast(_heads_to_lanes(wfq * scale, True)),
        cast(_heads_to_lanes(wfk, True)),
        cast(_heads_even_odd(wfv)),
        cast(_heads_to_lanes(wnq * scale, False)),
        cast(jnp.concatenate([wkcm, wvcm], axis=1)),
        cast(_heads_to_lanes(wksl, False)),
        cast(_group_even_odd(wvsl)),
        cast(_heads_to_lanes(wkwn, False)),
        cast(_group_even_odd(wvwn)),
        cast(jnp.concatenate([wmgf, wmgn], axis=1)),
        cast(jnp.concatenate([wff, wng, jnp.zeros((D, LANES - FOX_HEADS - 3 * NSA_HEADS), f32)], axis=1)),
    ]
    bq = np.zeros((1, FOX_HEADS * LANES), np.float32)
    bv = np.zeros((1, FOX_HEADS * LANES), np.float32)
    for h in range(FOX_HEADS):
        bq[0, h * LANES + HEAD_DIM:h * LANES + HEAD_DIM + 3] = 1.0
        bv[0, h * LANES + (HEAD_DIM if h % 2 == 0 else 0)] = 1.0
    bvs = np.zeros((1, NSA_KV_GROUPS * 2 * LANES), np.float32)
    for g in range(NSA_KV_GROUPS):
        bvs[0, g * 2 * LANES + HEAD_DIM] = 1.0
        bvs[0, g * 2 * LANES + LANES] = 1.0
    esel = np.zeros((3 * LANES, FOX_HEADS * LANES), np.float32)
    for j in range(3):
        for h in range(FOX_HEADS):
            esel[j * LANES + h, h * LANES + HEAD_DIM + j] = 1.0
    tri = np.tril(np.ones((tm, tm), np.float32))
    bfg = jnp.concatenate([b_forget, jnp.zeros((LANES - FOX_HEADS,), f32)]).reshape(1, LANES)

    widths = [w.shape[1] for w in weights]
    out_dtypes = [bf16] * 10 + [f32]
    row_spec = lambda n: pl.BlockSpec((tm, n), lambda i: (i, 0))
    mod_spec = pl.BlockSpec((1, 1, D), lambda i: (i // tiles_per_seq, 0, 0))
    consts = [jnp.asarray(tri, bf16), jnp.asarray(esel, bf16)]
    biases = [jnp.asarray(bq), jnp.asarray(bv), jnp.asarray(bvs)]
    order = [0, 1, 2, 3, 4, 5, 6, 7, 8, 9, 10]
    outs = pl.pallas_call(
        functools.partial(_inproj_kernel, tm=tm, tiles_per_seq=tiles_per_seq),
        out_shape=[jax.ShapeDtypeStruct((T, widths[k]), out_dtypes[k]) for k in order],
        grid=(T // tm,),
        in_specs=[row_spec(D), mod_spec, mod_spec, _const_spec((1, D)), _const_spec((1, LANES))]
        + [_const_spec(c.shape) for c in consts]
        + [_const_spec(weights[k].shape) for k in order]
        + [_const_spec(b.shape) for b in biases],
        out_specs=[row_spec(widths[k]) for k in order],
        scratch_shapes=[pltpu.VMEM((8, LANES), f32)],
        compiler_params=_cparams("arbitrary"),
        name="inproj",
    )(x2, sc1, sh1, g_pre.reshape(1, D), bfg, *consts, *[weights[k] for k in order], *biases)
    return outs


def _fox_kernel(q_ref, k_ref, v_ref, o_ref, *, tq):
    i = pl.program_id(2)
    row = lax.broadcasted_iota(i32, (tq, tq), 0)
    col = lax.broadcasted_iota(i32, (tq, tq), 1)
    qs = [q_ref[0, :, hh * LANES:(hh + 1) * LANES] for hh in range(2)]

    def tile(j, carry, diag):
        start = pl.multiple_of(j * tq, tq)
        new = []
        for hh in range(2):
            m, acc = carry[hh]
            k = k_ref[0, pl.ds(start, tq), hh * LANES:(hh + 1) * LANES]
            v = v_ref[0, pl.ds(start, tq), hh * LANES:(hh + 1) * LANES]
            s = lax.dot_general(qs[hh], k, _NT, preferred_element_type=f32)
            if diag:
                s = jnp.where(col <= row, s, NEG)
            m_new = jnp.maximum(m, jnp.max(s, axis=-1, keepdims=True))
            p = jnp.exp(s - m_new).astype(bf16)
            acc = jnp.exp(m - m_new) * acc + jnp.dot(p, v, preferred_element_type=f32)
            new.append((m_new, acc))
        return tuple(new)

    carry = tuple((jnp.full((tq, 1), NEG, f32), jnp.zeros((tq, LANES), f32)) for _ in range(2))
    carry = tile(i, carry, True)
    carry = _loop_two_tiles(i, lambda j, c: tile(j, c, False), carry)
    o_ref[0] = _pair_out(carry[0][1], carry[1][1]).astype(bf16)


def _fox(fq, fk, fv, B, S):
    tq = min(FOX_TILE, S)
    nq = S // tq
    q3 = fq.reshape(B, S, FOX_HEADS * LANES)
    k3 = fk.reshape(B, S, FOX_HEADS * LANES)
    v3 = fv.reshape(B, S, FOX_HEADS * LANES)
    return pl.pallas_call(
        functools.partial(_fox_kernel, tq=tq),
        out_shape=jax.ShapeDtypeStruct((B, S, FOX_WIDTH), bf16),
        grid=(B, FOX_HEADS // 2, nq),
        in_specs=[pl.BlockSpec((1, tq, 2 * LANES), lambda b, hp, i: (b, i, hp)),
                  pl.BlockSpec((1, S, 2 * LANES), lambda b, hp, i: (b, 0, hp)),
                  pl.BlockSpec((1, S, 2 * LANES), lambda b, hp, i: (b, 0, hp))],
        out_specs=pl.BlockSpec((1, tq, LANES), lambda b, hp, i: (b, i, hp)),
        compiler_params=_cparams("parallel", "parallel", "arbitrary"),
        name="fox",
    )(q3, k3, v3)


def _compress_kernel(x_ref, pea_ref, peb_ref, wa_ref, wb_ref, okc, ovc, *, nc):
    x = x_ref[0].astype(f32)
    xa = (x + pea_ref[...]).astype(bf16)
    xb = (x + peb_ref[...]).astype(bf16)
    a = jnp.dot(xa, wa_ref[...], preferred_element_type=f32)
    b = jnp.dot(xb, wb_ref[...], preferred_element_type=f32)
    out = a + pltpu.roll(b, nc - 1, 0)
    okc[0] = out[:, :2 * LANES].astype(bf16)
    ovc[0] = out[:, 2 * LANES:].astype(bf16)


def _compress(cm, pe_k, pe_v, w_cmp_k, w_cmp_v, B, S):
    nc = S // CMP_STRIDE
    half = CMP_BLOCK // 2
    win = half * 2 * LANES
    x = cm.reshape(B, nc, win)
    wk = w_cmp_k.reshape(CMP_BLOCK, HEAD_DIM, HEAD_DIM)
    wv = w_cmp_v.reshape(CMP_BLOCK, HEAD_DIM, HEAD_DIM)
    H = HEAD_DIM

    def build(wk_h, wv_h):
        z = jnp.zeros((half, H, H), f32)

        def rows(cols):
            return jnp.concatenate([cols.get(c, z) for c in range(12)], axis=2)

        w = jnp.concatenate([rows({1: wk_h}),
                             rows({3: wk_h}),
                             rows({4: wv_h, 7: wv_h}),
                             rows({8: wv_h, 11: wv_h})],
                            axis=1)
        return w.reshape(win, 6 * LANES).astype(bf16)

    wa = build(wk[:half], wv[:half])
    wb = build(wk[half:], wv[half:])

    def pe_row(pk, pv):
        return jnp.concatenate([pk, pk, pv, pv], axis=1).reshape(1, win)

    pea = pe_row(pe_k[:half], pe_v[:half])
    peb = pe_row(pe_k[half:], pe_v[half:])
    return pl.pallas_call(
        functools.partial(_compress_kernel, nc=nc),
        out_shape=[jax.ShapeDtypeStruct((B, nc, 2 * LANES), bf16),
                   jax.ShapeDtypeStruct((B, nc, 4 * LANES), bf16)],
        grid=(B,),
        in_specs=[pl.BlockSpec((1, nc, win), lambda b: (b, 0, 0)),
                  _const_spec((1, win)), _const_spec((1, win)),
                  _const_spec((win, 6 * LANES)), _const_spec((win, 6 * LANES))],
        out_specs=[pl.BlockSpec((1, nc, 2 * LANES), lambda b: (b, 0, 0)),
                   pl.BlockSpec((1, nc, 4 * LANES), lambda b: (b, 0, 0))],
        compiler_params=_cparams("parallel"),
        name="compress",
    )(x, pea, peb, wa, wb)


def _cmpsel_kernel(q_ref, kc_ref, vc_ref, pat_ref, ov_ref, ocmp, osel, score_scr, *, tq, nc, n_sel, top_n, past):
    i = pl.program_id(1)
    kc = kc_ref[0]
    c0 = i * (tq // CMP_STRIDE) - past
    wio = lax.broadcasted_iota(i32, (LANES, nc), 0)
    cio = lax.broadcasted_iota(i32, (LANES, nc), 1)
    shift = jnp.where(cio == wio + c0, 1.0, 0.0).astype(bf16)
    t = i * tq + lax.broadcasted_iota(i32, (tq, nc), 0)
    cend = lax.broadcasted_iota(i32, (tq, nc), 1) * CMP_STRIDE + (CMP_BLOCK - 1)
    valid = cend <= t
    pcs = jnp.zeros((tq, nc), f32)
    outs = []
    for hh in range(NSA_HPG):
        q = q_ref[0, :, hh * LANES:(hh + 1) * LANES]
        pat = pat_ref[hh]
        pat_hi = pat.astype(bf16)
        pat_lo = (pat - pat_hi.astype(f32)).astype(bf16)
        cb = (jnp.dot(pat_hi, shift, preferred_element_type=f32)
              + jnp.dot(pat_lo, shift, preferred_element_type=f32))
        lc = jnp.where(valid, lax.dot_general(q, kc, _NT, preferred_element_type=f32) + cb, NEG)
        m = jnp.max(lc, axis=-1, keepdims=True)
        p = jnp.where(valid, jnp.exp(lc - m), 0.0)
        l = jnp.sum(p, axis=-1, keepdims=True)
        pc = p * jnp.where(l > 0.0, 1.0 / l, 0.0)
        pcs = pcs + pc
        v = vc_ref[0, :, (hh % 2) * LANES:(hh % 2 + 1) * LANES]
        outs.append(jnp.dot(pc.astype(bf16), v, preferred_element_type=f32))
    ocmp[0] = jnp.concatenate([outs[0] + outs[1], outs[2] + outs[3]], axis=1).astype(bf16)

    hi = pcs.astype(bf16)
    lo = (pcs - hi.astype(f32)).astype(bf16)
    ov = ov_ref[...]
    imp = (lax.dot_general(ov, hi, _NT, preferred_element_type=f32)
           + lax.dot_general(ov, lo, _NT, preferred_element_type=f32))
    jio = lax.broadcasted_iota(i32, (n_sel, tq), 0)
    t = i * tq + lax.broadcasted_iota(i32, (n_sel, tq), 1)
    cur = t // SEL_BLOCK
    forced = (jio == 0) | (jio == cur) | (jio == cur - 1)
    score = jnp.where(forced, BIG, jnp.where(jio <= cur, imp, -BIG))
    score_scr[...] = score
    blocks_per_tile = tq // SEL_BLOCK

    def visit(g, rank):
        for u in range(blocks_per_tile):
            jp = g * blocks_per_tile + u
            r = score_scr[pl.ds(jp, 1), :]
            tie = jnp.where(jio > jp, 1.0, 0.0)
            rank = rank + jnp.where(r > score, 1.0, jnp.where(r == score, tie, 0.0))
        return rank

    rank = lax.fori_loop(0, i + 1, visit, jnp.zeros((n_sel, tq), f32))
    selb = jnp.where(rank < top_n, 0.0, -SEL_MASK)
    padded = jnp.concatenate([selb, jnp.zeros((LANES - n_sel, tq), f32)], axis=0)
    osel[0, 0] = padded.T.astype(bf16)


def _bucket_bounds():
    n = np.arange(0, 4 * REL_MAX_DIST)
    max_exact = REL_BUCKETS // 2
    nf = np.maximum(n, 1).astype(np.float32)
    large = max_exact + (np.log(nf / np.float32(max_exact)) / np.float32(np.log(REL_MAX_DIST / max_exact))
                         * np.float32(REL_BUCKETS - max_exact)).astype(np.int32)
    bucket = np.where(n < max_exact, n, np.minimum(large, REL_BUCKETS - 1))
    return [int(n[bucket > b].min()) for b in range(REL_BUCKETS - 1)]


_BOUNDS = _bucket_bounds()
_CMP_PAST = (_BOUNDS[-1] + CMP_BLOCK - 1 + CMP_STRIDE - 1) // CMP_STRIDE - 1


def _rel_bias_of(d, rb_ref, h):
    far = rb_ref[(REL_BUCKETS - 1) * NSA_HEADS + h]
    v = jnp.zeros(d.shape, f32)
    for b in reversed(range(REL_BUCKETS - 1)):
        v = jnp.where(d < _BOUNDS[b], rb_ref[b * NSA_HEADS + h] - far, v)
    return v


def _biasgen_kernel(rb_ref, oslc, owin, ocmp, *, ts, tw, tc, nk):
    h = pl.program_id(0)

    def tile(t, off, window):
        lo, hi = off - (t - 1), off + (t - 1)
        if hi < 0 or (window is not None and lo >= window):
            return jnp.full((t, t), NEG, f32)
        d = lax.broadcasted_iota(i32, (t, t), 0) - lax.broadcasted_iota(i32, (t, t), 1) + off
        val = _rel_bias_of(d, rb_ref, h) if lo < _BOUNDS[-1] else jnp.zeros((t, t), f32)
        if lo < 0:
            val = jnp.where(d >= 0, val, NEG)
        if window is not None and hi >= window:
            val = jnp.where(d < window, val, NEG)
        return val

    oslc[0, 0, 0] = tile(ts, 0, None)
    oslc[0, 1, 0] = tile(ts, ts, None)
    oslc[0, 2, 0] = jnp.full((ts, ts), NEG, f32)
    for v in range(nk):
        for cc in range(nk):
            owin[0, v, 0, :, cc * tw:(cc + 1) * tw] = tile(tw, (v - cc) * tw, WINDOW)
    rr = lax.broadcasted_iota(i32, (tc, LANES), 0)
    w = lax.broadcasted_iota(i32, (tc, LANES), 1)
    d = rr - CMP_STRIDE * (w - _CMP_PAST) - (CMP_BLOCK - 1)
    ocmp[0] = jnp.where(d >= 0, _rel_bias_of(d, rb_ref, h), 0.0)


def _biasgen(rel_bias, S):
    ts, tw, tc = min(SLC_TILE, S), min(WIN_TILE, S), min(CMP_TILE, S)
    nk = WINDOW // tw + 1
    assert WINDOW % tw == 0 and S >= nk * tw and min(ts, tw) + 1 >= _BOUNDS[-1]
    assert tc // CMP_STRIDE + _CMP_PAST <= LANES
    G = NSA_KV_GROUPS
    return pl.pallas_call(
        functools.partial(_biasgen_kernel, ts=ts, tw=tw, tc=tc, nk=nk),
        out_shape=[jax.ShapeDtypeStruct((G, 3, NSA_HPG, ts, ts), f32),
                   jax.ShapeDtypeStruct((G, nk, NSA_HPG, tw, nk * tw), f32),
                   jax.ShapeDtypeStruct((NSA_HEADS, tc, LANES), f32)],
        grid=(NSA_HEADS,),
        in_specs=[pl.BlockSpec(memory_space=pltpu.SMEM)],
        out_specs=[pl.BlockSpec((1, 3, 1, ts, ts), lambda h: (h // NSA_HPG, 0, h % NSA_HPG, 0, 0)),
                   pl.BlockSpec((1, nk, 1, tw, nk * tw), lambda h: (h // NSA_HPG, 0, h % NSA_HPG, 0, 0)),
                   pl.BlockSpec((1, tc, LANES), lambda h: (h, 0, 0))],
        compiler_params=_cparams("arbitrary"),
        name="biasgen",
    )(rel_bias.reshape(-1))


def _cmpsel(nq_arr, kc, vc, pat, B, S):
    tq = min(CMP_TILE, S)
    nqt = S // tq
    nc = S // CMP_STRIDE
    n_sel = S // SEL_BLOCK
    top_n = min(SEL_TOPN, n_sel)
    G = NSA_KV_GROUPS
    assert n_sel <= HEAD_DIM
    c = np.arange(nc)[None, :]
    j = np.arange(n_sel)[:, None]
    ov = ((c * CMP_STRIDE < j * SEL_BLOCK + SEL_BLOCK) & (c * CMP_STRIDE + CMP_BLOCK > j * SEL_BLOCK)
          & (c < nc - 1)).astype(np.float32)
    q3 = nq_arr.reshape(B, S, NSA_HEADS * LANES)
    return pl.pallas_call(
        functools.partial(_cmpsel_kernel, tq=tq, nc=nc, n_sel=n_sel, top_n=top_n, past=_CMP_PAST),
        out_shape=[jax.ShapeDtypeStruct((B, S, NSA_HEADS * HEAD_DIM), bf16),
                   jax.ShapeDtypeStruct((B, G, S, LANES), bf16)],
        grid=(G, nqt, B),
        in_specs=[pl.BlockSpec((1, tq, NSA_HPG * LANES), lambda g, i, b: (b, i, g)),
                  pl.BlockSpec((1, nc, LANES), lambda g, i, b: (b, 0, g)),
                  pl.BlockSpec((1, nc, 2 * LANES), lambda g, i, b: (b, 0, g)),
                  pl.BlockSpec((NSA_HPG, tq, LANES), lambda g, i, b: (g, 0, 0)),
                  _const_spec((n_sel, nc))],
        out_specs=[pl.BlockSpec((1, tq, 2 * LANES), lambda g, i, b: (b, i, g)),
                   pl.BlockSpec((1, 1, tq, LANES), lambda g, i, b: (b, g, i, 0))],
        scratch_shapes=[pltpu.VMEM((n_sel, tq), f32)],
        compiler_params=_cparams("parallel", "arbitrary", "arbitrary"),
        name="cmpsel",
    )(q3, kc, vc, pat, jnp.asarray(ov, bf16))


def _pair_out(acc_e, acc_o):
    lane = lax.broadcasted_iota(i32, acc_e.shape, 1)
    return jnp.where(lane < HEAD_DIM, acc_e / acc_e[:, HEAD_DIM:HEAD_DIM + 1], acc_o / acc_o[:, 0:1])


def _slc_kernel(q_ref, sb_ref, k_ref, v_ref, bias_ref, o_ref, *, tq):
    i = pl.program_id(2)
    sb = sb_ref[0, 0]
    qs = [q_ref[0, :, h * LANES:(h + 1) * LANES] + sb for h in range(NSA_HPG)]

    def tile(j, carry, bias_idx):
        start = pl.multiple_of(j * tq, tq)
        k = k_ref[0, pl.ds(start, tq), :]
        v = v_ref[0, pl.ds(start, tq), :]
        new = []
        for h in range(NSA_HPG):
            m, acc = carry[h]
            s = lax.dot_general(qs[h], k, _NT, preferred_element_type=f32)
            if bias_idx is not None:
                s = s + bias_ref[0, bias_idx, h]
            m_new = jnp.maximum(m, jnp.max(s, axis=-1, keepdims=True))
            p = jnp.exp(s - m_new).astype(bf16)
            vh = v[:, (h % 2) * LANES:(h % 2 + 1) * LANES]
            acc = jnp.exp(m - m_new) * acc + jnp.dot(p, vh, preferred_element_type=f32)
            new.append((m_new, acc))
        return tuple(new)

    carry = tuple((jnp.full((tq, 1), NEG, f32), jnp.zeros((tq, LANES), f32)) for _ in range(NSA_HPG))
    carry = tile(i, carry, 0)
    carry = tile(jnp.maximum(i - 1, 0), carry, jnp.where(i >= 1, 1, 2))
    carry = _loop_two_tiles(jnp.maximum(i - 1, 0), lambda j, c: tile(j, c, None), carry)
    o_ref[0] = jnp.concatenate([_pair_out(carry[0][1], carry[1][1]),
                                _pair_out(carry[2][1], carry[3][1])], axis=1).astype(bf16)


def _win_kernel(q_ref, k_ref, v_ref, bias_ref, o_ref, *, tq, nk):
    i = pl.program_id(2)
    start = pl.multiple_of(jnp.maximum(i - (nk - 1), 0) * tq, tq)
    k = k_ref[0, pl.ds(start, nk * tq), :]
    v = v_ref[0, pl.ds(start, nk * tq), :]
    accs = []
    for h in range(NSA_HPG):
        q = q_ref[0, :, h * LANES:(h + 1) * LANES]
        s = lax.dot_general(q, k, _NT, preferred_element_type=f32) + bias_ref[0, 0, h]
        p = jnp.exp(s - jnp.max(s, axis=-1, keepdims=True)).astype(bf16)
        accs.append(jnp.dot(p, v[:, (h % 2) * LANES:(h % 2 + 1) * LANES], preferred_element_type=f32))
    o_ref[0] = jnp.concatenate([_pair_out(accs[0], accs[1]), _pair_out(accs[2], accs[3])], axis=1).astype(bf16)


def _nsa_flash(kind, nq_arr, selb, k_arr, v_arr, bias, B, S):
    tq = bias.shape[3]
    nqt = S // tq
    G = NSA_KV_GROUPS
    q3 = nq_arr.reshape(B, S, NSA_HEADS * LANES)
    k3 = k_arr.reshape(B, S, G * LANES)
    v3 = v_arr.reshape(B, S, G * 2 * LANES)
    q_spec = pl.BlockSpec((1, tq, NSA_HPG * LANES), lambda g, b, i: (b, i, g))
    k_spec = pl.BlockSpec((1, S, LANES), lambda g, b, i: (b, 0, g))
    v_spec = pl.BlockSpec((1, S, 2 * LANES), lambda g, b, i: (b, 0, g))
    if kind == "slc":
        kern = functools.partial(_slc_kernel, tq=tq)
        extra_specs = [pl.BlockSpec((1, 1, tq, LANES), lambda g, b, i: (b, g, i, 0))]
        extra = [selb]
        b_spec = pl.BlockSpec((1, 3, NSA_HPG, tq, tq), lambda g, b, i: (g, 0, 0, 0, 0),
                              pipeline_mode=pl.Buffered(1))
    else:
        nk = bias.shape[1]
        kern = functools.partial(_win_kernel, tq=tq, nk=nk)
        extra_specs, extra = [], []
        b_spec = pl.BlockSpec((1, 1, NSA_HPG, tq, nk * tq), lambda g, b, i: (g, jnp.minimum(i, nk - 1), 0, 0, 0))
    return pl.pallas_call(
        kern,
        out_shape=jax.ShapeDtypeStruct((B, S, NSA_HEADS * HEAD_DIM), bf16),
        grid=(G, B, nqt),
        in_specs=[q_spec] + extra_specs + [k_spec, v_spec, b_spec],
        out_specs=pl.BlockSpec((1, tq, 2 * LANES), lambda g, b, i: (b, i, g)),
        compiler_params=_cparams("parallel", "parallel", "arbitrary"),
        name=kind,
    )(q3, *extra, k3, v3, bias)


def _post_kernel(x_ref, ofox, ocmp, oslc, owin, mg_ref, sm_ref, ga1, sc2, sh2, gpost, gpre,
                 wfp, wnp_, wmo, wr, br, eg,
                 x1_ref, h2_ref, route_ref, cnt_ref, carry_ref, *, tm):
    i = pl.program_id(0)
    W = NSA_WIDTH
    gates = jax.nn.sigmoid(sm_ref[...]).astype(bf16)
    gx = jnp.dot(gates, eg[...], preferred_element_type=f32)
    nsa = (gx[:, :W] * ocmp[...].astype(f32) + gx[:, W:2 * W] * oslc[...].astype(f32)
           + gx[:, 2 * W:] * owin[...].astype(f32))
    y_nsa = jnp.dot(nsa.astype(bf16), wnp_[...], preferred_element_type=f32)
    y_fox = jnp.dot(ofox[...], wfp[...], preferred_element_type=f32)
    mg = mg_ref[...].astype(f32)
    mix = (mg[:, :D_MODEL] * y_fox + mg[:, D_MODEL:] * y_nsa).astype(bf16)
    mixed = jnp.dot(mix, wmo[...], preferred_element_type=f32)
    x1 = x_ref[...] + ga1[0] * _rms(mixed, gpost[...])
    x1_ref[...] = x1
    h2 = _rms(x1, gpre[...]) * (1.0 + sc2[0]) + sh2[0]
    _store_tile_rows(h2_ref, h2)

    lane = lax.broadcasted_iota(i32, (tm, LANES), 1)
    logits = jnp.dot(h2.astype(bf16), wr[...], preferred_element_type=f32) + br[...]
    l = jnp.where(lane < N_EXPERTS, logits, NEG)
    vals, idxs = [], []
    for _ in range(TOP_K):
        m = jnp.max(l, axis=-1, keepdims=True)
        idx = jnp.min(jnp.where(l == m, lane, LANES), axis=-1, keepdims=True)
        vals.append(m)
        idxs.append(idx)
        l = jnp.where(lane == idx, NEG, l)
    es = [jnp.exp(v - vals[0]) for v in vals]
    den = es[0] + es[1] + es[2] + es[3]

    @pl.when(i == 0)
    def _():
        carry_ref[...] = jnp.zeros_like(carry_ref)

    cnt = sum(jnp.where(lane == idx, 1.0, 0.0) for idx in idxs)
    new_carry = carry_ref[0:1, :] + jnp.sum(cnt, axis=0, keepdims=True)
    carry_ref[0:1, :] = new_carry
    cnt_ref[...] = jnp.broadcast_to(new_carry, cnt_ref.shape)
    route = jnp.zeros((tm, LANES), f32)
    for k in range(TOP_K):
        route = jnp.where(lane == k, idxs[k].astype(f32), route)
        route = jnp.where(lane == 2 * TOP_K + k, es[k] / den, route)
    route_ref[...] = route


def _post(x2, ofox, ocmp, oslc, owin, mg, sm, ga1, sc2, sh2, g_post, g_pre2,
          w_fox_proj, w_nsa_proj, w_mix_out, w_router, b_router, S):
    T, D = x2.shape
    tm = min(ROW_TILE, S)
    tiles_per_seq = S // tm
    W = NSA_WIDTH
    eg = np.zeros((LANES, 3 * W), np.float32)
    for h in range(NSA_HEADS):
        for k in range(3):
            eg[FOX_HEADS + 3 * h + k, k * W + h * HEAD_DIM:k * W + (h + 1) * HEAD_DIM] = 1.0
    wr = jnp.concatenate([w_router, jnp.zeros((D, LANES - N_EXPERTS), f32)], axis=1).astype(bf16)
    br = jnp.concatenate([b_router, jnp.zeros((LANES - N_EXPERTS,), f32)]).reshape(1, LANES)
    row = lambda n: pl.BlockSpec((tm, n), lambda i: (i, 0))
    mod = pl.BlockSpec((1, 1, D), lambda i: (i // tiles_per_seq, 0, 0))
    consts = [w_fox_proj.astype(bf16), w_nsa_proj.astype(bf16), w_mix_out.astype(bf16), wr, br,
              jnp.asarray(eg, bf16)]
    return pl.pallas_call(
        functools.partial(_post_kernel, tm=tm),
        out_shape=[jax.ShapeDtypeStruct((T, D), f32), jax.ShapeDtypeStruct((T * ROW_TILES, LANES), f32),
                   jax.ShapeDtypeStruct((T, LANES), f32), jax.ShapeDtypeStruct((8, LANES), f32)],
        grid=(T // tm,),
        in_specs=[row(D), row(FOX_WIDTH), row(W), row(W), row(W), row(2 * D), row(LANES),
                  mod, mod, mod, _const_spec((1, D)), _const_spec((1, D))]
        + [_const_spec(c.shape) for c in consts],
        out_specs=[row(D), pl.BlockSpec((tm * ROW_TILES, LANES), lambda i: (i, 0)), row(LANES),
                   pl.BlockSpec((8, LANES), lambda i: (0, 0))],
        scratch_shapes=[pltpu.VMEM((8, LANES), f32)],
        compiler_params=_cparams("arbitrary"),
        name="post",
    )(x2, ofox, ocmp, oslc, owin, mg, sm, ga1, sc2, sh2, g_post.reshape(1, D), g_pre2.reshape(1, D), *consts)


def _moe_kernel(be_ref, nu_ref, tok0_ref, tokn_ref, dstp_ref, h_hbm, wg, bg, wu, bu, wd, bd,
                y_hbm, xbuf, ybuf, wgb, wub, wdb, gsem, ssem, *, bm):
    i = pl.program_id(0)
    nu = nu_ref[0]
    rt = ROW_TILES

    def rows(r):
        return pl.ds(r * rt, rt) if isinstance(r, int) else pl.ds(pl.multiple_of(r * rt, rt), rt)

    def gather_copy(tok_ref, r, s):
        return pltpu.make_async_copy(h_hbm.at[tok_ref[0, 0, r]], xbuf.at[s, rows(r), :], gsem.at[s])

    def scatter_copy(r, s):
        return pltpu.make_async_copy(ybuf.at[s, rows(r), :], y_hbm.at[dstp_ref[0, 0, r]], ssem.at[s])

    def wait_gather(s):
        pltpu.make_async_copy(xbuf.at[s], xbuf.at[s], gsem.at[s]).wait()

    def wait_scatter(s):
        pltpu.make_async_copy(ybuf.at[s], ybuf.at[s], ssem.at[s]).wait()

    @pl.when(i == 0)
    def _():
        ybuf[1] = jnp.zeros((bm * rt, LANES), f32)
        n_real = y_hbm.shape[0] - 2 * bm

        def issue(r, c):
            gather_copy(tok0_ref, r, 0).start()
            for half in range(2):
                pltpu.make_async_copy(ybuf.at[1, rows(r), :], y_hbm.at[n_real + half * bm + r],
                                      ssem.at[1]).start()
            return c

        lax.fori_loop(0, bm, issue, 0)
        for half in range(2):
            wait_scatter(1)

    def step(slot):
        other = 1 - slot
        wait_gather(slot)

        @pl.when(i >= 1)
        def _():
            wait_scatter(slot)

        @pl.when(jnp.logical_or(i == 0, be_ref[i] != be_ref[jnp.maximum(i - 1, 0)]))
        def _():
            wgb[...] = wg[0].astype(bf16)
            wub[...] = wu[0].astype(bf16)
            wdb[...] = wd[0].astype(bf16)

        for r in range(bm):
            gather_copy(tokn_ref, r, other).start(priority=1)
            scatter_copy(r, other).start(priority=0)
        x = _load_tile_rows(xbuf.at[slot], bm).astype(bf16)
        g = jnp.dot(x, wgb[...], preferred_element_type=f32) + bg[0]
        u = jnp.dot(x, wub[...], preferred_element_type=f32) + bu[0]
        gt = jnp.minimum(g, SWIGLU_LIMIT)
        up = jnp.clip(u, -SWIGLU_LIMIT, SWIGLU_LIMIT)
        a = (gt * jax.nn.sigmoid(SWIGLU_ALPHA * gt) * (up + 1.0)).astype(bf16)
        _store_tile_rows(ybuf.at[slot], jnp.dot(a, wdb[...], preferred_element_type=f32) + bd[0])

    def drain(slot):
        other = 1 - slot
        wait_gather(slot)
        wait_scatter(slot)

        def issue(r, c):
            scatter_copy(r, other).start()
            return c

        lax.fori_loop(0, bm, issue, 0)
        wait_scatter(other)

    for s in range(2):
        pl.when(jnp.logical_and(i < nu, i % 2 == s))(functools.partial(step, s))
        pl.when(jnp.logical_and(i == nu, i % 2 == s))(functools.partial(drain, s))


def _moe(blk_e, n_used, row_tok, row_dst, h2t, w_gate, b_gate, w_up, b_up, w_down, b_down, n_rows):
    D = D_MODEL
    E, _, F = w_gate.shape
    nbt = row_tok.shape[0]
    bm = MOE_BM
    wsel = lambda i, be, nu: (be[jnp.minimum(i, nu[0] - 1)], 0, 0)
    idx_spec = lambda f: pl.BlockSpec((1, 1, bm), lambda i, be, nu: (f(i), 0, 0), memory_space=pltpu.SMEM)
    return pl.pallas_call(
        functools.partial(_moe_kernel, bm=bm),
        out_shape=jax.ShapeDtypeStruct((n_rows, ROW_TILES, LANES), f32),
        grid_spec=pltpu.PrefetchScalarGridSpec(
            num_scalar_prefetch=2, grid=(nbt,),
            in_specs=[idx_spec(lambda i: 0),
                      idx_spec(lambda i: jnp.minimum(i + 1, nbt - 1)),
                      idx_spec(lambda i: jnp.maximum(i - 1, 0)),
                      pl.BlockSpec(memory_space=pl.ANY),
                      pl.BlockSpec((1, D, F), wsel), pl.BlockSpec((1, 1, F), wsel),
                      pl.BlockSpec((1, D, F), wsel), pl.BlockSpec((1, 1, F), wsel),
                      pl.BlockSpec((1, F, D), wsel), pl.BlockSpec((1, 1, D), wsel)],
            out_specs=pl.BlockSpec(memory_space=pl.ANY),
            scratch_shapes=[pltpu.VMEM((2, bm * ROW_TILES, LANES), f32), pltpu.VMEM((2, bm * ROW_TILES, LANES), f32),
                            pltpu.VMEM((D, F), bf16), pltpu.VMEM((D, F), bf16), pltpu.VMEM((F, D), bf16),
                            pltpu.SemaphoreType.DMA((2,)), pltpu.SemaphoreType.DMA((2,))]),
        compiler_params=_cparams("arbitrary"),
        name="moe",
    )(blk_e, n_used, row_tok, row_tok, row_dst, h2t.reshape(-1, ROW_TILES, LANES),
      w_gate, b_gate.reshape(E, 1, F), w_up, b_up.reshape(E, 1, F), w_down, b_down.reshape(E, 1, D))


def _combine_kernel(y0, y1, y2, y3, route_ref, x1_ref, ga2, gpost, o_ref):
    route = route_ref[...]
    tm = o_ref.shape[0]
    y = jnp.zeros(o_ref.shape, f32)
    for k, yk in enumerate((y0, y1, y2, y3)):
        y = y + route[:, 2 * TOP_K + k:2 * TOP_K + k + 1] * _load_tile_rows(yk, tm)
    o_ref[...] = x1_ref[...] + ga2[0] * _rms(y, gpost[...])


def _combine(y4, route, x1, ga2, g_post2, S):
    T, D = x1.shape
    tm = min(MOE_ROWS, S)
    tiles_per_seq = S // tm
    nt = T // tm
    y4 = y4.reshape(-1, LANES)
    y_spec = lambda k: pl.BlockSpec((tm * ROW_TILES, LANES), lambda i: (k * nt + i, 0))
    return pl.pallas_call(
        _combine_kernel,
        out_shape=jax.ShapeDtypeStruct((T, D), f32),
        grid=(nt,),
        in_specs=[y_spec(k) for k in range(TOP_K)]
        + [pl.BlockSpec((tm, LANES), lambda i: (i, 0)),
           pl.BlockSpec((tm, D), lambda i: (i, 0)),
           pl.BlockSpec((1, 1, D), lambda i: (i // tiles_per_seq, 0, 0)),
           pl.BlockSpec((1, D), lambda i: (0, 0))],
        out_specs=pl.BlockSpec((tm, D), lambda i: (i, 0)),
        compiler_params=_cparams("parallel"),
        name="combine",
    )(y4, y4, y4, y4, route, x1, ga2, g_post2.reshape(1, D))


def kernel(x, c, w_ada, b_ada, g_mix_pre, g_mix_post, w_in, b_forget, pe_k, pe_v, w_cmp_k, w_cmp_v, w_fox_proj, w_nsa_proj, w_mix_out, rel_bias, g_ffn_pre, g_ffn_post, w_router, b_router, w_gate, b_gate, w_up, b_up, w_down, b_down):
    B, S, D = x.shape
    T = B * S
    for l in range(w_ada.shape[0]):
        x2 = x.reshape(T, D)
        ada = _ada(c, w_ada[l], b_ada[l])
        sh1, sc1, ga1, sh2, sc2, ga2 = [a.reshape(B, 1, D) for a in jnp.split(ada, 6, axis=-1)]
        fq, fk, fv, nq, cm, ksl, vsl, kwn, vwn, mg, sm = _inproj(x2, sc1, sh1, g_mix_pre[l], w_in[l], b_forget[l], S)
        o_fox = _fox(fq, fk, fv, B, S)
        kc, vc = _compress(cm, pe_k[l], pe_v[l], w_cmp_k[l], w_cmp_v[l], B, S)
        bias_slc, bias_win, pat_cmp = _biasgen(rel_bias, S)
        o_cmp, selb = _cmpsel(nq, kc, vc, pat_cmp, B, S)
        o_slc = _nsa_flash("slc", nq, selb, ksl, vsl, bias_slc, B, S)
        o_win = _nsa_flash("win", nq, None, kwn, vwn, bias_win, B, S)
        x1, h2, route, cnt = _post(x2, o_fox.reshape(T, -1), o_cmp.reshape(T, -1), o_slc.reshape(T, -1),
                                   o_win.reshape(T, -1), mg, sm, ga1, sc2, sh2, g_mix_post[l], g_ffn_pre[l],
                                   w_fox_proj[l], w_nsa_proj[l], w_mix_out[l], w_router[l], b_router[l], S)
        counts = cnt[0, :N_EXPERTS].astype(i32)
        nblk = (counts + MOE_BM - 1) // MOE_BM
        blk_end = jnp.cumsum(nblk)
        pad_start = (blk_end - nblk) * MOE_BM
        top_i = route[:, :TOP_K].astype(i32)
        A = T * TOP_K
        nbt = -(-A // MOE_BM) + N_EXPERTS + 1
        n_used = blk_end[-1:].astype(i32)
        blk_e = jnp.minimum(jnp.sum(jnp.arange(nbt)[:, None] >= blk_end[None, :], axis=1), N_EXPERTS - 1).astype(i32)
        a_sorted = jnp.sort((top_i * A + jnp.arange(A, dtype=i32).reshape(T, TOP_K)).reshape(-1)) % A
        grp_start = jnp.cumsum(counts) - counts
        j = jnp.arange(MOE_BM, dtype=i32)[None, :]
        b = jnp.arange(nbt, dtype=i32)[:, None]
        r_in_e = b * MOE_BM + j - pad_start[blk_e][:, None]
        valid = (b < n_used[0]) & (r_in_e < counts[blk_e][:, None])
        row_a = a_sorted[jnp.clip(grp_start[blk_e][:, None] + r_in_e, 0, A - 1)]
        row_tok = jnp.where(valid, row_a // TOP_K, 0)
        row_dst = jnp.where(valid, (row_a % TOP_K) * T + row_a // TOP_K, A + (b % 2) * MOE_BM + j)
        y4 = _moe(blk_e, n_used, row_tok.reshape(nbt, 1, MOE_BM), row_dst.reshape(nbt, 1, MOE_BM), h2,
                  w_gate[l], b_gate[l], w_up[l], b_up[l], w_down[l], b_down[l], A + 2 * MOE_BM)
        x = _combine(y4, route, x1, ga2, g_ffn_post[l], S).reshape(B, S, D)
    return x
```

```python
import functools

import numpy as np
import jax
import jax.numpy as jnp
from jax import lax
from jax.experimental import pallas as pl
from jax.experimental.pallas import tpu as pltpu

f32 = jnp.float32
bf16 = jnp.bfloat16
i32 = jnp.int32

D_MODEL = 1024
HEAD_DIM = 64
FOX_HEADS = 8
NSA_HEADS = 8
NSA_KV_GROUPS = 2
NSA_HPG = NSA_HEADS // NSA_KV_GROUPS
FOX_WIDTH = FOX_HEADS * HEAD_DIM
NSA_WIDTH = NSA_HEADS * HEAD_DIM
NSA_KV_WIDTH = NSA_KV_GROUPS * HEAD_DIM
CMP_BLOCK = 32
CMP_STRIDE = 16
SEL_BLOCK = 64
SEL_TOPN = 16
WINDOW = 512
REL_BUCKETS = 32
REL_MAX_DIST = 128
N_EXPERTS = 32
TOP_K = 4
SWIGLU_LIMIT = 7.0
SWIGLU_ALPHA = 1.702
RMS_EPS = 1e-6
NEG = -1e30
BIG = 1e9
IN_SIZES = (FOX_WIDTH, FOX_WIDTH, FOX_WIDTH, FOX_HEADS, NSA_WIDTH,
            NSA_KV_WIDTH, NSA_KV_WIDTH, NSA_KV_WIDTH, NSA_KV_WIDTH, NSA_KV_WIDTH, NSA_KV_WIDTH,
            3 * NSA_HEADS, D_MODEL, D_MODEL)

LANES = 128
VMEM_LIMIT = 56 * 1024 * 1024
FOX_TILE = 512
SLC_TILE = 512
WIN_TILE = 512
CMP_TILE = 512
ROW_TILE = 512
MOE_BM = 512
MOE_ROWS = 256
SEL_MASK = 1e9

_NT = (((1,), (1,)), ((), ()))


def _cparams(*sem):
    return pltpu.CompilerParams(dimension_semantics=sem, vmem_limit_bytes=VMEM_LIMIT)


def _const_spec(shape):
    nd = len(shape)
    return pl.BlockSpec(shape, lambda *_: (0,) * nd, pipeline_mode=pl.Buffered(1))


def _split3(a):
    a1 = a.astype(bf16)
    r1 = a - a1.astype(f32)
    a2 = r1.astype(bf16)
    a3 = (r1 - a2.astype(f32)).astype(bf16)
    return a1, a2, a3


ROW_TILES = D_MODEL // LANES


def _store_tile_rows(ref, val):
    n = val.shape[0]
    for c in range(ROW_TILES):
        ref[pl.ds(c, n, stride=ROW_TILES), :] = val[:, c * LANES:(c + 1) * LANES]


def _load_tile_rows(ref, n):
    return jnp.concatenate([ref[pl.ds(c, n, stride=ROW_TILES), :] for c in range(ROW_TILES)], axis=1)


def _loop_two_tiles(n, body, carry):
    carry = lax.fori_loop(0, n // 2, lambda jj, c: body(2 * jj + 1, body(2 * jj, c)), carry)
    return lax.cond(n % 2 == 1, lambda c: body(n - 1, c), lambda c: c, carry)


def _rms(x, g):
    ms = jnp.mean(x * x, axis=-1, keepdims=True)
    return x * lax.rsqrt(ms + RMS_EPS) * g


def _ada_kernel(c_ref, w_ref, b_ref, o_ref):
    c = c_ref[...]
    s = c * jax.nn.sigmoid(c)
    w = w_ref[...]
    s1, s2, _ = _split3(s)
    w1, w2, _ = _split3(w)
    acc = jnp.dot(s1, w1, preferred_element_type=f32)
    acc += jnp.dot(s1, w2, preferred_element_type=f32)
    acc += jnp.dot(s2, w1, preferred_element_type=f32)
    o_ref[...] = acc + b_ref[...]


def _ada(c, w_ada, b_ada):
    B, D = c.shape
    N = w_ada.shape[1]
    tn = 1024
    return pl.pallas_call(
        _ada_kernel,
        out_shape=jax.ShapeDtypeStruct((B, N), f32),
        grid=(N // tn,),
        in_specs=[pl.BlockSpec((B, D), lambda j: (0, 0)),
                  pl.BlockSpec((D, tn), lambda j: (0, j)),
                  pl.BlockSpec((1, tn), lambda j: (0, j))],
        out_specs=pl.BlockSpec((B, tn), lambda j: (0, j)),
        compiler_params=_cparams("arbitrary"),
        name="ada",
    )(c, w_ada, b_ada.reshape(1, N))


def _inproj_kernel(x_ref, sc_ref, sh_ref, g_ref, bfg_ref, tri_ref, esel_ref,
                   wfq, wfk, wfv, wnq, wcm, wksl, wvsl, wkwn, wvwn, wmg, wsm,
                   bq, bv, bvs,
                   ofq, ofk, ofv, onq, ocm, oksl, ovsl, okwn, ovwn, omg, osm,
                   carry_ref, *, tm, tiles_per_seq):
    i = pl.program_id(0)
    x = x_ref[...]
    h = _rms(x, g_ref[...]) * (1.0 + sc_ref[0]) + sh_ref[0]
    hb = h.astype(bf16)

    def proj(w):
        return jnp.dot(hb, w[...], preferred_element_type=f32)

    ofq[...] = (proj(wfq) + bq[...]).astype(bf16)
    ofv[...] = (proj(wfv) + bv[...]).astype(bf16)
    onq[...] = proj(wnq).astype(bf16)
    ocm[...] = proj(wcm).astype(bf16)
    ovsl[...] = (proj(wvsl) + bvs[...]).astype(bf16)
    okwn[...] = proj(wkwn).astype(bf16)
    ovwn[...] = (proj(wvwn) + bvs[...]).astype(bf16)
    omg[...] = jax.nn.sigmoid(proj(wmg)).astype(bf16)

    row = lax.broadcasted_iota(i32, (tm, 2 * LANES), 0)
    lane = lax.broadcasted_iota(i32, (tm, 2 * LANES), 1)
    blk = ((i % tiles_per_seq) * tm + row) // SEL_BLOCK
    onehot = jnp.where((lane & (LANES - 1)) == blk, 1.0, 0.0)
    oksl[...] = (proj(wksl) + onehot).astype(bf16)

    sm = proj(wsm)
    osm[...] = sm
    z = sm + bfg_ref[...]
    lane1 = lax.broadcasted_iota(i32, (tm, LANES), 1)
    logf = jnp.where(lane1 < FOX_HEADS, jnp.minimum(z, 0.0) - jnp.log(1.0 + jnp.exp(-jnp.abs(z))), 0.0)

    @pl.when(i % tiles_per_seq == 0)
    def _():
        carry_ref[...] = jnp.zeros_like(carry_ref)

    tri = tri_ref[...]
    cum = carry_ref[0:1, :]
    for piece in _split3(logf):
        cum = cum + jnp.dot(tri, piece, preferred_element_type=f32)
    carry_ref[0:1, :] = cum[tm - 1:tm, :]
    ncat = jnp.concatenate(_split3(-cum), axis=1)
    ofk[...] = (proj(wfk) + jnp.dot(ncat, esel_ref[...], preferred_element_type=f32)).astype(bf16)


def _heads_to_lanes(w, lo):
    D = w.shape[0]
    nh = w.shape[1] // HEAD_DIM
    w3 = w.reshape(D, nh, HEAD_DIM)
    z = jnp.zeros_like(w3)
    return jnp.concatenate([w3, z] if lo else [z, w3], axis=2).reshape(D, nh * LANES)


def _heads_even_odd(w):
    D = w.shape[0]
    nh = w.shape[1] // HEAD_DIM
    w4 = w.reshape(D, nh // 2, 2, HEAD_DIM)
    z = jnp.zeros((D, nh // 2, HEAD_DIM), w.dtype)
    even = jnp.concatenate([w4[:, :, 0], z], axis=2)
    odd = jnp.concatenate([z, w4[:, :, 1]], axis=2)
    return jnp.stack([even, odd], axis=2).reshape(D, nh * LANES)


def _group_even_odd(w):
    D = w.shape[0]
    w3 = w.reshape(D, NSA_KV_GROUPS, HEAD_DIM)
    z = jnp.zeros_like(w3)
    return jnp.concatenate([w3, z, z, w3], axis=2).reshape(D, NSA_KV_GROUPS * 2 * LANES)


def _inproj(x2, sc1, sh1, g_pre, w_in, b_forget, S):
    T, D = x2.shape
    tm = min(ROW_TILE, S)
    tiles_per_seq = S // tm
    offs = np.cumsum(IN_SIZES)[:-1].tolist()
    (wfq, wfk, wfv, wff, wnq, wkcm, wvcm, wksl, wvsl, wkwn, wvwn, wng, wmgf, wmgn) = jnp.split(w_in, offs, axis=1)
    scale = HEAD_DIM ** -0.5
    cast = lambda w: w.astype(bf16)
    weights = [
        cast(_heads_to_lanes(wfq * scale, True)),
        cast(_heads_to_lanes(wfk, True)),
        cast(_heads_even_odd(wfv)),
        cast(_heads_to_lanes(wnq * scale, False)),
        cast(jnp.concatenate([wkcm, wvcm], axis=1)),
        cast(_heads_to_lanes(wksl, False)),
        cast(_group_even_odd(wvsl)),
        cast(_heads_to_lanes(wkwn, False)),
        cast(_group_even_odd(wvwn)),
        cast(jnp.concatenate([wmgf, wmgn], axis=1)),
        cast(jnp.concatenate([wff, wng, jnp.zeros((D, LANES - FOX_HEADS - 3 * NSA_HEADS), f32)], axis=1)),
    ]
    bq = np.zeros((1, FOX_HEADS * LANES), np.float32)
    bv = np.zeros((1, FOX_HEADS * LANES), np.float32)
    for h in range(FOX_HEADS):
        bq[0, h * LANES + HEAD_DIM:h * LANES + HEAD_DIM + 3] = 1.0
        bv[0, h * LANES + (HEAD_DIM if h % 2 == 0 else 0)] = 1.0
    bvs = np.zeros((1, NSA_KV_GROUPS * 2 * LANES), np.float32)
    for g in range(NSA_KV_GROUPS):
        bvs[0, g * 2 * LANES + HEAD_DIM] = 1.0
        bvs[0, g * 2 * LANES + LANES] = 1.0
    esel = np.zeros((3 * LANES, FOX_HEADS * LANES), np.float32)
    for j in range(3):
        for h in range(FOX_HEADS):
            esel[j * LANES + h, h * LANES + HEAD_DIM + j] = 1.0
    tri = np.tril(np.ones((tm, tm), np.float32))
    bfg = jnp.concatenate([b_forget, jnp.zeros((LANES - FOX_HEADS,), f32)]).reshape(1, LANES)

    widths = [w.shape[1] for w in weights]
    out_dtypes = [bf16] * 10 + [f32]
    row_spec = lambda n: pl.BlockSpec((tm, n), lambda i: (i, 0))
    mod_spec = pl.BlockSpec((1, 1, D), lambda i: (i // tiles_per_seq, 0, 0))
    consts = [jnp.asarray(tri, bf16), jnp.asarray(esel, bf16)]
    biases = [jnp.asarray(bq), jnp.asarray(bv), jnp.asarray(bvs)]
    order = [0, 1, 2, 3, 4, 5, 6, 7, 8, 9, 10]
    outs = pl.pallas_call(
        functools.partial(_inproj_kernel, tm=tm, tiles_per_seq=tiles_per_seq),
        out_shape=[jax.ShapeDtypeStruct((T, widths[k]), out_dtypes[k]) for k in order],
        grid=(T // tm,),
        in_specs=[row_spec(D), mod_spec, mod_spec, _const_spec((1, D)), _const_spec((1, LANES))]
        + [_const_spec(c.shape) for c in consts]
        + [_const_spec(weights[k].shape) for k in order]
        + [_const_spec(b.shape) for b in biases],
        out_specs=[row_spec(widths[k]) for k in order],
        scratch_shapes=[pltpu.VMEM((8, LANES), f32)],
        compiler_params=_cparams("arbitrary"),
        name="inproj",
    )(x2, sc1, sh1, g_pre.reshape(1, D), bfg, *consts, *[weights[k] for k in order], *biases)
    return outs


def _fox_kernel(q_ref, k_ref, v_ref, o_ref, *, tq):
    i = pl.program_id(2)
    row = lax.broadcasted_iota(i32, (tq, tq), 0)
    col = lax.broadcasted_iota(i32, (tq, tq), 1)
    qs = [q_ref[0, :, hh * LANES:(hh + 1) * LANES] for hh in range(2)]

    def tile(j, carry, diag):
        start = pl.multiple_of(j * tq, tq)
        new = []
        for hh in range(2):
            m, acc = carry[hh]
            k = k_ref[0, pl.ds(start, tq), hh * LANES:(hh + 1) * LANES]
            v = v_ref[0, pl.ds(start, tq), hh * LANES:(hh + 1) * LANES]
            s = lax.dot_general(qs[hh], k, _NT, preferred_element_type=f32)
            if diag:
                s = jnp.where(col <= row, s, NEG)
            m_new = jnp.maximum(m, jnp.max(s, axis=-1, keepdims=True))
            p = jnp.exp(s - m_new).astype(bf16)
            acc = jnp.exp(m - m_new) * acc + jnp.dot(p, v, preferred_element_type=f32)
            new.append((m_new, acc))
        return tuple(new)

    carry = tuple((jnp.full((tq, 1), NEG, f32), jnp.zeros((tq, LANES), f32)) for _ in range(2))
    carry = tile(i, carry, True)
    carry = _loop_two_tiles(i, lambda j, c: tile(j, c, False), carry)
    o_ref[0] = _pair_out(carry[0][1], carry[1][1]).astype(bf16)


def _fox(fq, fk, fv, B, S):
    tq = min(FOX_TILE, S)
    nq = S // tq
    q3 = fq.reshape(B, S, FOX_HEADS * LANES)
    k3 = fk.reshape(B, S, FOX_HEADS * LANES)
    v3 = fv.reshape(B, S, FOX_HEADS * LANES)
    return pl.pallas_call(
        functools.partial(_fox_kernel, tq=tq),
        out_shape=jax.ShapeDtypeStruct((B, S, FOX_WIDTH), bf16),
        grid=(B, FOX_HEADS // 2, nq),
        in_specs=[pl.BlockSpec((1, tq, 2 * LANES), lambda b, hp, i: (b, i, hp)),
                  pl.BlockSpec((1, S, 2 * LANES), lambda b, hp, i: (b, 0, hp)),
                  pl.BlockSpec((1, S, 2 * LANES), lambda b, hp, i: (b, 0, hp))],
        out_specs=pl.BlockSpec((1, tq, LANES), lambda b, hp, i: (b, i, hp)),
        compiler_params=_cparams("parallel", "parallel", "arbitrary"),
        name="fox",
    )(q3, k3, v3)


def _compress_kernel(x_ref, pea_ref, peb_ref, wa_ref, wb_ref, okc, ovc, *, nc):
    x = x_ref[0].astype(f32)
    xa = (x + pea_ref[...]).astype(bf16)
    xb = (x + peb_ref[...]).astype(bf16)
    a = jnp.dot(xa, wa_ref[...], preferred_element_type=f32)
    b = jnp.dot(xb, wb_ref[...], preferred_element_type=f32)
    out = a + pltpu.roll(b, nc - 1, 0)
    okc[0] = out[:, :2 * LANES].astype(bf16)
    ovc[0] = out[:, 2 * LANES:].astype(bf16)


def _compress(cm, pe_k, pe_v, w_cmp_k, w_cmp_v, B, S):
    nc = S // CMP_STRIDE
    half = CMP_BLOCK // 2
    win = half * 2 * LANES
    x = cm.reshape(B, nc, win)
    wk = w_cmp_k.reshape(CMP_BLOCK, HEAD_DIM, HEAD_DIM)
    wv = w_cmp_v.reshape(CMP_BLOCK, HEAD_DIM, HEAD_DIM)
    H = HEAD_DIM

    def build(wk_h, wv_h):
        z = jnp.zeros((half, H, H), f32)

        def rows(cols):
            return jnp.concatenate([cols.get(c, z) for c in range(12)], axis=2)

        w = jnp.concatenate([rows({1: wk_h}),
                             rows({3: wk_h}),
                             rows({4: wv_h, 7: wv_h}),
                             rows({8: wv_h, 11: wv_h})],
                            axis=1)
        return w.reshape(win, 6 * LANES).astype(bf16)

    wa = build(wk[:half], wv[:half])
    wb = build(wk[half:], wv[half:])

    def pe_row(pk, pv):
        return jnp.concatenate([pk, pk, pv, pv], axis=1).reshape(1, win)

    pea = pe_row(pe_k[:half], pe_v[:half])
    peb = pe_row(pe_k[half:], pe_v[half:])
    return pl.pallas_call(
        functools.partial(_compress_kernel, nc=nc),
        out_shape=[jax.ShapeDtypeStruct((B, nc, 2 * LANES), bf16),
                   jax.ShapeDtypeStruct((B, nc, 4 * LANES), bf16)],
        grid=(B,),
        in_specs=[pl.BlockSpec((1, nc, win), lambda b: (b, 0, 0)),
                  _const_spec((1, win)), _const_spec((1, win)),
                  _const_spec((win, 6 * LANES)), _const_spec((win, 6 * LANES))],
        out_specs=[pl.BlockSpec((1, nc, 2 * LANES), lambda b: (b, 0, 0)),
                   pl.BlockSpec((1, nc, 4 * LANES), lambda b: (b, 0, 0))],
        compiler_params=_cparams("parallel"),
        name="compress",
    )(x, pea, peb, wa, wb)


def _cmpsel_kernel(q_ref, kc_ref, vc_ref, pat_ref, ov_ref, ocmp, osel, score_scr, *, tq, nc, n_sel, top_n, past):
    i = pl.program_id(1)
    kc = kc_ref[0]
    c0 = i * (tq // CMP_STRIDE) - past
    wio = lax.broadcasted_iota(i32, (LANES, nc), 0)
    cio = lax.broadcasted_iota(i32, (LANES, nc), 1)
    shift = jnp.where(cio == wio + c0, 1.0, 0.0).astype(bf16)
    t = i * tq + lax.broadcasted_iota(i32, (tq, nc), 0)
    cend = lax.broadcasted_iota(i32, (tq, nc), 1) * CMP_STRIDE + (CMP_BLOCK - 1)
    valid = cend <= t
    pcs = jnp.zeros((tq, nc), f32)
    outs = []
    for hh in range(NSA_HPG):
        q = q_ref[0, :, hh * LANES:(hh + 1) * LANES]
        pat = pat_ref[hh]
        pat_hi = pat.astype(bf16)
        pat_lo = (pat - pat_hi.astype(f32)).astype(bf16)
        cb = (jnp.dot(pat_hi, shift, preferred_element_type=f32)
              + jnp.dot(pat_lo, shift, preferred_element_type=f32))
        lc = jnp.where(valid, lax.dot_general(q, kc, _NT, preferred_element_type=f32) + cb, NEG)
        m = jnp.max(lc, axis=-1, keepdims=True)
        p = jnp.where(valid, jnp.exp(lc - m), 0.0)
        l = jnp.sum(p, axis=-1, keepdims=True)
        pc = p * jnp.where(l > 0.0, 1.0 / l, 0.0)
        pcs = pcs + pc
        v = vc_ref[0, :, (hh % 2) * LANES:(hh % 2 + 1) * LANES]
        outs.append(jnp.dot(pc.astype(bf16), v, preferred_element_type=f32))
    ocmp[0] = jnp.concatenate([outs[0] + outs[1], outs[2] + outs[3]], axis=1).astype(bf16)

    hi = pcs.astype(bf16)
    lo = (pcs - hi.astype(f32)).astype(bf16)
    ov = ov_ref[...]
    imp = (lax.dot_general(ov, hi, _NT, preferred_element_type=f32)
           + lax.dot_general(ov, lo, _NT, preferred_element_type=f32))
    jio = lax.broadcasted_iota(i32, (n_sel, tq), 0)
    t = i * tq + lax.broadcasted_iota(i32, (n_sel, tq), 1)
    cur = t // SEL_BLOCK
    forced = (jio == 0) | (jio == cur) | (jio == cur - 1)
    score = jnp.where(forced, BIG, jnp.where(jio <= cur, imp, -BIG))
    score_scr[...] = score
    blocks_per_tile = tq // SEL_BLOCK

    def visit(g, rank):
        for u in range(blocks_per_tile):
            jp = g * blocks_per_tile + u
            r = score_scr[pl.ds(jp, 1), :]
            tie = jnp.where(jio > jp, 1.0, 0.0)
            rank = rank + jnp.where(r > score, 1.0, jnp.where(r == score, tie, 0.0))
        return rank

    rank = lax.fori_loop(0, i + 1, visit, jnp.zeros((n_sel, tq), f32))
    selb = jnp.where(rank < top_n, 0.0, -SEL_MASK)
    padded = jnp.concatenate([selb, jnp.zeros((LANES - n_sel, tq), f32)], axis=0)
    osel[0, 0] = padded.T.astype(bf16)


def _bucket_bounds():
    n = np.arange(0, 4 * REL_MAX_DIST)
    max_exact = REL_BUCKETS // 2
    nf = np.maximum(n, 1).astype(np.float32)
    large = max_exact + (np.log(nf / np.float32(max_exact)) / np.float32(np.log(REL_MAX_DIST / max_exact))
                         * np.float32(REL_BUCKETS - max_exact)).astype(np.int32)
    bucket = np.where(n < max_exact, n, np.minimum(large, REL_BUCKETS - 1))
    return [int(n[bucket > b].min()) for b in range(REL_BUCKETS - 1)]


_BOUNDS = _bucket_bounds()
_CMP_PAST = (_BOUNDS[-1] + CMP_BLOCK - 1 + CMP_STRIDE - 1) // CMP_STRIDE - 1


def _rel_bias_of(d, rb_ref, h):
    far = rb_ref[(REL_BUCKETS - 1) * NSA_HEADS + h]
    v = jnp.zeros(d.shape, f32)
    for b in reversed(range(REL_BUCKETS - 1)):
        v = jnp.where(d < _BOUNDS[b], rb_ref[b * NSA_HEADS + h] - far, v)
    return v


def _biasgen_kernel(rb_ref, oslc, owin, ocmp, *, ts, tw, tc, nk):
    h = pl.program_id(0)

    def tile(t, off, window):
        lo, hi = off - (t - 1), off + (t - 1)
        if hi < 0 or (window is not None and lo >= window):
            return jnp.full((t, t), NEG, f32)
        d = lax.broadcasted_iota(i32, (t, t), 0) - lax.broadcasted_iota(i32, (t, t), 1) + off
        val = _rel_bias_of(d, rb_ref, h) if lo < _BOUNDS[-1] else jnp.zeros((t, t), f32)
        if lo < 0:
            val = jnp.where(d >= 0, val, NEG)
        if window is not None and hi >= window:
            val = jnp.where(d < window, val, NEG)
        return val

    oslc[0, 0, 0] = tile(ts, 0, None)
    oslc[0, 1, 0] = tile(ts, ts, None)
    oslc[0, 2, 0] = jnp.full((ts, ts), NEG, f32)
    for v in range(nk):
        for cc in range(nk):
            owin[0, v, 0, :, cc * tw:(cc + 1) * tw] = tile(tw, (v - cc) * tw, WINDOW)
    rr = lax.broadcasted_iota(i32, (tc, LANES), 0)
    w = lax.broadcasted_iota(i32, (tc, LANES), 1)
    d = rr - CMP_STRIDE * (w - _CMP_PAST) - (CMP_BLOCK - 1)
    ocmp[0] = jnp.where(d >= 0, _rel_bias_of(d, rb_ref, h), 0.0)


def _biasgen(rel_bias, S):
    ts, tw, tc = min(SLC_TILE, S), min(WIN_TILE, S), min(CMP_TILE, S)
    nk = WINDOW // tw + 1
    assert WINDOW % tw == 0 and S >= nk * tw and min(ts, tw) + 1 >= _BOUNDS[-1]
    assert tc // CMP_STRIDE + _CMP_PAST <= LANES
    G = NSA_KV_GROUPS
    return pl.pallas_call(
        functools.partial(_biasgen_kernel, ts=ts, tw=tw, tc=tc, nk=nk),
        out_shape=[jax.ShapeDtypeStruct((G, 3, NSA_HPG, ts, ts), f32),
                   jax.ShapeDtypeStruct((G, nk, NSA_HPG, tw, nk * tw), f32),
                   jax.ShapeDtypeStruct((NSA_HEADS, tc, LANES), f32)],
        grid=(NSA_HEADS,),
        in_specs=[pl.BlockSpec(memory_space=pltpu.SMEM)],
        out_specs=[pl.BlockSpec((1, 3, 1, ts, ts), lambda h: (h // NSA_HPG, 0, h % NSA_HPG, 0, 0)),
                   pl.BlockSpec((1, nk, 1, tw, nk * tw), lambda h: (h // NSA_HPG, 0, h % NSA_HPG, 0, 0)),
                   pl.BlockSpec((1, tc, LANES), lambda h: (h, 0, 0))],
        compiler_params=_cparams("arbitrary"),
        name="biasgen",
    )(rel_bias.reshape(-1))


def _cmpsel(nq_arr, kc, vc, pat, B, S):
    tq = min(CMP_TILE, S)
    nqt = S // tq
    nc = S // CMP_STRIDE
    n_sel = S // SEL_BLOCK
    top_n = min(SEL_TOPN, n_sel)
    G = NSA_KV_GROUPS
    assert n_sel <= HEAD_DIM
    c = np.arange(nc)[None, :]
    j = np.arange(n_sel)[:, None]
    ov = ((c * CMP_STRIDE < j * SEL_BLOCK + SEL_BLOCK) & (c * CMP_STRIDE + CMP_BLOCK > j * SEL_BLOCK)
          & (c < nc - 1)).astype(np.float32)
    q3 = nq_arr.reshape(B, S, NSA_HEADS * LANES)
    return pl.pallas_call(
        functools.partial(_cmpsel_kernel, tq=tq, nc=nc, n_sel=n_sel, top_n=top_n, past=_CMP_PAST),
        out_shape=[jax.ShapeDtypeStruct((B, S, NSA_HEADS * HEAD_DIM), bf16),
                   jax.ShapeDtypeStruct((B, G, S, LANES), bf16)],
        grid=(G, nqt, B),
        in_specs=[pl.BlockSpec((1, tq, NSA_HPG * LANES), lambda g, i, b: (b, i, g)),
                  pl.BlockSpec((1, nc, LANES), lambda g, i, b: (b, 0, g)),
                  pl.BlockSpec((1, nc, 2 * LANES), lambda g, i, b: (b, 0, g)),
                  pl.BlockSpec((NSA_HPG, tq, LANES), lambda g, i, b: (g, 0, 0)),
                  _const_spec((n_sel, nc))],
        out_specs=[pl.BlockSpec((1, tq, 2 * LANES), lambda g, i, b: (b, i, g)),
                   pl.BlockSpec((1, 1, tq, LANES), lambda g, i, b: (b, g, i, 0))],
        scratch_shapes=[pltpu.VMEM((n_sel, tq), f32)],
        compiler_params=_cparams("parallel", "arbitrary", "arbitrary"),
        name="cmpsel",
    )(q3, kc, vc, pat, jnp.asarray(ov, bf16))


def _pair_out(acc_e, acc_o):
    lane = lax.broadcasted_iota(i32, acc_e.shape, 1)
    return jnp.where(lane < HEAD_DIM, acc_e / acc_e[:, HEAD_DIM:HEAD_DIM + 1], acc_o / acc_o[:, 0:1])


def _slc_kernel(q_ref, sb_ref, k_ref, v_ref, bias_ref, o_ref, *, tq):
    i = pl.program_id(2)
    sb = sb_ref[0, 0]
    qs = [q_ref[0, :, h * LANES:(h + 1) * LANES] + sb for h in range(NSA_HPG)]

    def tile(j, carry, bias_idx):
        start = pl.multiple_of(j * tq, tq)
        k = k_ref[0, pl.ds(start, tq), :]
        v = v_ref[0, pl.ds(start, tq), :]
        new = []
        for h in range(NSA_HPG):
            m, acc = carry[h]
            s = lax.dot_general(qs[h], k, _NT, preferred_element_type=f32)
            if bias_idx is not None:
                s = s + bias_ref[0, bias_idx, h]
            m_new = jnp.maximum(m, jnp.max(s, axis=-1, keepdims=True))
            p = jnp.exp(s - m_new).astype(bf16)
            vh = v[:, (h % 2) * LANES:(h % 2 + 1) * LANES]
            acc = jnp.exp(m - m_new) * acc + jnp.dot(p, vh, preferred_element_type=f32)
            new.append((m_new, acc))
        return tuple(new)

    carry = tuple((jnp.full((tq, 1), NEG, f32), jnp.zeros((tq, LANES), f32)) for _ in range(NSA_HPG))
    carry = tile(i, carry, 0)
    carry = tile(jnp.maximum(i - 1, 0), carry, jnp.where(i >= 1, 1, 2))
    carry = _loop_two_tiles(jnp.maximum(i - 1, 0), lambda j, c: tile(j, c, None), carry)
    o_ref[0] = jnp.concatenate([_pair_out(carry[0][1], carry[1][1]),
                                _pair_out(carry[2][1], carry[3][1])], axis=1).astype(bf16)


def _win_kernel(q_ref, k_ref, v_ref, bias_ref, o_ref, *, tq, nk):
    i = pl.program_id(2)
    start = pl.multiple_of(jnp.maximum(i - (nk - 1), 0) * tq, tq)
    k = k_ref[0, pl.ds(start, nk * tq), :]
    v = v_ref[0, pl.ds(start, nk * tq), :]
    accs = []
    for h in range(NSA_HPG):
        q = q_ref[0, :, h * LANES:(h + 1) * LANES]
        s = lax.dot_general(q, k, _NT, preferred_element_type=f32) + bias_ref[0, 0, h]
        p = jnp.exp(s - jnp.max(s, axis=-1, keepdims=True)).astype(bf16)
        accs.append(jnp.dot(p, v[:, (h % 2) * LANES:(h % 2 + 1) * LANES], preferred_element_type=f32))
    o_ref[0] = jnp.concatenate([_pair_out(accs[0], accs[1]), _pair_out(accs[2], accs[3])], axis=1).astype(bf16)


def _nsa_flash(kind, nq_arr, selb, k_arr, v_arr, bias, B, S):
    tq = bias.shape[3]
    nqt = S // tq
    G = NSA_KV_GROUPS
    q3 = nq_arr.reshape(B, S, NSA_HEADS * LANES)
    k3 = k_arr.reshape(B, S, G * LANES)
    v3 = v_arr.reshape(B, S, G * 2 * LANES)
    q_spec = pl.BlockSpec((1, tq, NSA_HPG * LANES), lambda g, b, i: (b, i, g))
    k_spec = pl.BlockSpec((1, S, LANES), lambda g, b, i: (b, 0, g))
    v_spec = pl.BlockSpec((1, S, 2 * LANES), lambda g, b, i: (b, 0, g))
    if kind == "slc":
        kern = functools.partial(_slc_kernel, tq=tq)
        extra_specs = [pl.BlockSpec((1, 1, tq, LANES), lambda g, b, i: (b, g, i, 0))]
        extra = [selb]
        b_spec = pl.BlockSpec((1, 3, NSA_HPG, tq, tq), lambda g, b, i: (g, 0, 0, 0, 0),
                              pipeline_mode=pl.Buffered(1))
    else:
        nk = bias.shape[1]
        kern = functools.partial(_win_kernel, tq=tq, nk=nk)
        extra_specs, extra = [], []
        b_spec = pl.BlockSpec((1, 1, NSA_HPG, tq, nk * tq), lambda g, b, i: (g, jnp.minimum(i, nk - 1), 0, 0, 0))
    return pl.pallas_call(
        kern,
        out_shape=jax.ShapeDtypeStruct((B, S, NSA_HEADS * HEAD_DIM), bf16),
        grid=(G, B, nqt),
        in_specs=[q_spec] + extra_specs + [k_spec, v_spec, b_spec],
        out_specs=pl.BlockSpec((1, tq, 2 * LANES), lambda g, b, i: (b, i, g)),
        compiler_params=_cparams("parallel", "parallel", "arbitrary"),
        name=kind,
    )(q3, *extra, k3, v3, bias)


def _post_kernel(x_ref, ofox, ocmp, oslc, owin, mg_ref, sm_ref, ga1, sc2, sh2, gpost, gpre,
                 wfp, wnp_, wmo, wr, br, eg,
                 x1_ref, h2_ref, route_ref, cnt_ref, carry_ref, *, tm):
    i = pl.program_id(0)
    W = NSA_WIDTH
    gates = jax.nn.sigmoid(sm_ref[...]).astype(bf16)
    gx = jnp.dot(gates, eg[...], preferred_element_type=f32)
    nsa = (gx[:, :W] * ocmp[...].astype(f32) + gx[:, W:2 * W] * oslc[...].astype(f32)
           + gx[:, 2 * W:] * owin[...].astype(f32))
    y_nsa = jnp.dot(nsa.astype(bf16), wnp_[...], preferred_element_type=f32)
    y_fox = jnp.dot(ofox[...], wfp[...], preferred_element_type=f32)
    mg = mg_ref[...].astype(f32)
    mix = (mg[:, :D_MODEL] * y_fox + mg[:, D_MODEL:] * y_nsa).astype(bf16)
    mixed = jnp.dot(mix, wmo[...], preferred_element_type=f32)
    x1 = x_ref[...] + ga1[0] * _rms(mixed, gpost[...])
    x1_ref[...] = x1
    h2 = _rms(x1, gpre[...]) * (1.0 + sc2[0]) + sh2[0]
    _store_tile_rows(h2_ref, h2)

    lane = lax.broadcasted_iota(i32, (tm, LANES), 1)
    logits = jnp.dot(h2.astype(bf16), wr[...], preferred_element_type=f32) + br[...]
    l = jnp.where(lane < N_EXPERTS, logits, NEG)
    vals, idxs = [], []
    for _ in range(TOP_K):
        m = jnp.max(l, axis=-1, keepdims=True)
        idx = jnp.min(jnp.where(l == m, lane, LANES), axis=-1, keepdims=True)
        vals.append(m)
        idxs.append(idx)
        l = jnp.where(lane == idx, NEG, l)
    es = [jnp.exp(v - vals[0]) for v in vals]
    den = es[0] + es[1] + es[2] + es[3]

    @pl.when(i == 0)
    def _():
        carry_ref[...] = jnp.zeros_like(carry_ref)

    cnt = sum(jnp.where(lane == idx, 1.0, 0.0) for idx in idxs)
    new_carry = carry_ref[0:1, :] + jnp.sum(cnt, axis=0, keepdims=True)
    carry_ref[0:1, :] = new_carry
    cnt_ref[...] = jnp.broadcast_to(new_carry, cnt_ref.shape)
    route = jnp.zeros((tm, LANES), f32)
    for k in range(TOP_K):
        route = jnp.where(lane == k, idxs[k].astype(f32), route)
        route = jnp.where(lane == 2 * TOP_K + k, es[k] / den, route)
    route_ref[...] = route


def _post(x2, ofox, ocmp, oslc, owin, mg, sm, ga1, sc2, sh2, g_post, g_pre2,
          w_fox_proj, w_nsa_proj, w_mix_out, w_router, b_router, S):
    T, D = x2.shape
    tm = min(ROW_TILE, S)
    tiles_per_seq = S // tm
    W = NSA_WIDTH
    eg = np.zeros((LANES, 3 * W), np.float32)
    for h in range(NSA_HEADS):
        for k in range(3):
            eg[FOX_HEADS + 3 * h + k, k * W + h * HEAD_DIM:k * W + (h + 1) * HEAD_DIM] = 1.0
    wr = jnp.concatenate([w_router, jnp.zeros((D, LANES - N_EXPERTS), f32)], axis=1).astype(bf16)
    br = jnp.concatenate([b_router, jnp.zeros((LANES - N_EXPERTS,), f32)]).reshape(1, LANES)
    row = lambda n: pl.BlockSpec((tm, n), lambda i: (i, 0))
    mod = pl.BlockSpec((1, 1, D), lambda i: (i // tiles_per_seq, 0, 0))
    consts = [w_fox_proj.astype(bf16), w_nsa_proj.astype(bf16), w_mix_out.astype(bf16), wr, br,
              jnp.asarray(eg, bf16)]
    return pl.pallas_call(
        functools.partial(_post_kernel, tm=tm),
        out_shape=[jax.ShapeDtypeStruct((T, D), f32), jax.ShapeDtypeStruct((T * ROW_TILES, LANES), f32),
                   jax.ShapeDtypeStruct((T, LANES), f32), jax.ShapeDtypeStruct((8, LANES), f32)],
        grid=(T // tm,),
        in_specs=[row(D), row(FOX_WIDTH), row(W), row(W), row(W), row(2 * D), row(LANES),
                  mod, mod, mod, _const_spec((1, D)), _const_spec((1, D))]
        + [_const_spec(c.shape) for c in consts],
        out_specs=[row(D), pl.BlockSpec((tm * ROW_TILES, LANES), lambda i: (i, 0)), row(LANES),
                   pl.BlockSpec((8, LANES), lambda i: (0, 0))],
        scratch_shapes=[pltpu.VMEM((8, LANES), f32)],
        compiler_params=_cparams("arbitrary"),
        name="post",
    )(x2, ofox, ocmp, oslc, owin, mg, sm, ga1, sc2, sh2, g_post.reshape(1, D), g_pre2.reshape(1, D), *consts)


def _moe_kernel(be_ref, nu_ref, tok0_ref, tokn_ref, dstp_ref, h_hbm, wg, bg, wu, bu, wd, bd,
                y_hbm, xbuf, ybuf, wgb, wub, wdb, gsem, ssem, *, bm):
    i = pl.program_id(0)
    nu = nu_ref[0]
    rt = ROW_TILES

    def rows(r):
        return pl.ds(r * rt, rt) if isinstance(r, int) else pl.ds(pl.multiple_of(r * rt, rt), rt)

    def gather_copy(tok_ref, r, s):
        return pltpu.make_async_copy(h_hbm.at[tok_ref[0, 0, r]], xbuf.at[s, rows(r), :], gsem.at[s])

    def scatter_copy(r, s):
        return pltpu.make_async_copy(ybuf.at[s, rows(r), :], y_hbm.at[dstp_ref[0, 0, r]], ssem.at[s])

    def wait_gather(s):
        pltpu.make_async_copy(xbuf.at[s], xbuf.at[s], gsem.at[s]).wait()

    def wait_scatter(s):
        pltpu.make_async_copy(ybuf.at[s], ybuf.at[s], ssem.at[s]).wait()

    @pl.when(i == 0)
    def _():
        ybuf[1] = jnp.zeros((bm * rt, LANES), f32)
        n_real = y_hbm.shape[0] - 2 * bm

        def issue(r, c):
            gather_copy(tok0_ref, r, 0).start()
            for half in range(2):
                pltpu.make_async_copy(ybuf.at[1, rows(r), :], y_hbm.at[n_real + half * bm + r],
                                      ssem.at[1]).start()
            return c

        lax.fori_loop(0, bm, issue, 0)
        for half in range(2):
            wait_scatter(1)

    def step(slot):
        other = 1 - slot
        wait_gather(slot)

        @pl.when(i >= 1)
        def _():
            wait_scatter(slot)

        @pl.when(jnp.logical_or(i == 0, be_ref[i] != be_ref[jnp.maximum(i - 1, 0)]))
        def _():
            wgb[...] = wg[0].astype(bf16)
            wub[...] = wu[0].astype(bf16)
            wdb[...] = wd[0].astype(bf16)

        for r in range(bm):
            gather_copy(tokn_ref, r, other).start()
            scatter_copy(r, other).start()
        x = _load_tile_rows(xbuf.at[slot], bm).astype(bf16)
        g = jnp.dot(x, wgb[...], preferred_element_type=f32) + bg[0]
        u = jnp.dot(x, wub[...], preferred_element_type=f32) + bu[0]
        gt = jnp.minimum(g, SWIGLU_LIMIT)
        up = jnp.clip(u, -SWIGLU_LIMIT, SWIGLU_LIMIT)
        a = (gt * jax.nn.sigmoid(SWIGLU_ALPHA * gt) * (up + 1.0)).astype(bf16)
        _store_tile_rows(ybuf.at[slot], jnp.dot(a, wdb[...], preferred_element_type=f32) + bd[0])

    def drain(slot):
        other = 1 - slot
        wait_gather(slot)
        wait_scatter(slot)

        def issue(r, c):
            scatter_copy(r, other).start()
            return c

        lax.fori_loop(0, bm, issue, 0)
        wait_scatter(other)

    for s in range(2):
        pl.when(jnp.logical_and(i < nu, i % 2 == s))(functools.partial(step, s))
        pl.when(jnp.logical_and(i == nu, i % 2 == s))(functools.partial(drain, s))


def _moe(blk_e, n_used, row_tok, row_dst, h2t, w_gate, b_gate, w_up, b_up, w_down, b_down, n_rows):
    D = D_MODEL
    E, _, F = w_gate.shape
    nbt = row_tok.shape[0]
    bm = MOE_BM
    wsel = lambda i, be, nu: (be[jnp.minimum(i, nu[0] - 1)], 0, 0)
    idx_spec = lambda f: pl.BlockSpec((1, 1, bm), lambda i, be, nu: (f(i), 0, 0), memory_space=pltpu.SMEM)
    return pl.pallas_call(
        functools.partial(_moe_kernel, bm=bm),
        out_shape=jax.ShapeDtypeStruct((n_rows, ROW_TILES, LANES), f32),
        grid_spec=pltpu.PrefetchScalarGridSpec(
            num_scalar_prefetch=2, grid=(nbt,),
            in_specs=[idx_spec(lambda i: 0),
                      idx_spec(lambda i: jnp.minimum(i + 1, nbt - 1)),
                      idx_spec(lambda i: jnp.maximum(i - 1, 0)),
                      pl.BlockSpec(memory_space=pl.ANY),
                      pl.BlockSpec((1, D, F), wsel), pl.BlockSpec((1, 1, F), wsel),
                      pl.BlockSpec((1, D, F), wsel), pl.BlockSpec((1, 1, F), wsel),
                      pl.BlockSpec((1, F, D), wsel), pl.BlockSpec((1, 1, D), wsel)],
            out_specs=pl.BlockSpec(memory_space=pl.ANY),
            scratch_shapes=[pltpu.VMEM((2, bm * ROW_TILES, LANES), f32), pltpu.VMEM((2, bm * ROW_TILES, LANES), f32),
                            pltpu.VMEM((D, F), bf16), pltpu.VMEM((D, F), bf16), pltpu.VMEM((F, D), bf16),
                            pltpu.SemaphoreType.DMA((2,)), pltpu.SemaphoreType.DMA((2,))]),
        compiler_params=_cparams("arbitrary"),
        name="moe",
    )(blk_e, n_used, row_tok, row_tok, row_dst, h2t.reshape(-1, ROW_TILES, LANES),
      w_gate, b_gate.reshape(E, 1, F), w_up, b_up.reshape(E, 1, F), w_down, b_down.reshape(E, 1, D))


def _combine_kernel(y0, y1, y2, y3, route_ref, x1_ref, ga2, gpost, o_ref):
    route = route_ref[...]
    tm = o_ref.shape[0]
    y = jnp.zeros(o_ref.shape, f32)
    for k, yk in enumerate((y0, y1, y2, y3)):
        y = y + route[:, 2 * TOP_K + k:2 * TOP_K + k + 1] * _load_tile_rows(yk, tm)
    o_ref[...] = x1_ref[...] + ga2[0] * _rms(y, gpost[...])


def _combine(y4, route, x1, ga2, g_post2, S):
    T, D = x1.shape
    tm = min(MOE_ROWS, S)
    tiles_per_seq = S // tm
    nt = T // tm
    y4 = y4.reshape(-1, LANES)
    y_spec = lambda k: pl.BlockSpec((tm * ROW_TILES, LANES), lambda i: (k * nt + i, 0))
    return pl.pallas_call(
        _combine_kernel,
        out_shape=jax.ShapeDtypeStruct((T, D), f32),
        grid=(nt,),
        in_specs=[y_spec(k) for k in range(TOP_K)]
        + [pl.BlockSpec((tm, LANES), lambda i: (i, 0)),
           pl.BlockSpec((tm, D), lambda i: (i, 0)),
           pl.BlockSpec((1, 1, D), lambda i: (i // tiles_per_seq, 0, 0)),
           pl.BlockSpec((1, D), lambda i: (0, 0))],
        out_specs=pl.BlockSpec((tm, D), lambda i: (i, 0)),
        compiler_params=_cparams("parallel"),
        name="combine",
    )(y4, y4, y4, y4, route, x1, ga2, g_post2.reshape(1, D))


def kernel(x, c, w_ada, b_ada, g_mix_pre, g_mix_post, w_in, b_forget, pe_k, pe_v, w_cmp_k, w_cmp_v, w_fox_proj, w_nsa_proj, w_mix_out, rel_bias, g_ffn_pre, g_ffn_post, w_router, b_router, w_gate, b_gate, w_up, b_up, w_down, b_down):
    B, S, D = x.shape
    T = B * S
    for l in range(w_ada.shape[0]):
        x2 = x.reshape(T, D)
        ada = _ada(c, w_ada[l], b_ada[l])
        sh1, sc1, ga1, sh2, sc2, ga2 = [a.reshape(B, 1, D) for a in jnp.split(ada, 6, axis=-1)]
        fq, fk, fv, nq, cm, ksl, vsl, kwn, vwn, mg, sm = _inproj(x2, sc1, sh1, g_mix_pre[l], w_in[l], b_forget[l], S)
        o_fox = _fox(fq, fk, fv, B, S)
        kc, vc = _compress(cm, pe_k[l], pe_v[l], w_cmp_k[l], w_cmp_v[l], B, S)
        bias_slc, bias_win, pat_cmp = _biasgen(rel_bias, S)
        o_cmp, selb = _cmpsel(nq, kc, vc, pat_cmp, B, S)
        o_slc = _nsa_flash("slc", nq, selb, ksl, vsl, bias_slc, B, S)
        o_win = _nsa_flash("win", nq, None, kwn, vwn, bias_win, B, S)
        x1, h2, route, cnt = _post(x2, o_fox.reshape(T, -1), o_cmp.reshape(T, -1), o_slc.reshape(T, -1),
                                   o_win.reshape(T, -1), mg, sm, ga1, sc2, sh2, g_mix_post[l], g_ffn_pre[l],
                                   w_fox_proj[l], w_nsa_proj[l], w_mix_out[l], w_router[l], b_router[l], S)
        counts = cnt[0, :N_EXPERTS].astype(i32)
        nblk = (counts + MOE_BM - 1) // MOE_BM
        blk_end = jnp.cumsum(nblk)
        pad_start = (blk_end - nblk) * MOE_BM
        top_i = route[:, :TOP_K].astype(i32)
        A = T * TOP_K
        nbt = -(-A // MOE_BM) + N_EXPERTS + 1
        n_used = blk_end[-1:].astype(i32)
        blk_e = jnp.minimum(jnp.sum(jnp.arange(nbt)[:, None] >= blk_end[None, :], axis=1), N_EXPERTS - 1).astype(i32)
        a_sorted = jnp.sort((top_i * A + jnp.arange(A, dtype=i32).reshape(T, TOP_K)).reshape(-1)) % A
        grp_start = jnp.cumsum(counts) - counts
        j = jnp.arange(MOE_BM, dtype=i32)[None, :]
        b = jnp.arange(nbt, dtype=i32)[:, None]
        r_in_e = b * MOE_BM + j - pad_start[blk_e][:, None]
        valid = (b < n_used[0]) & (r_in_e < counts[blk_e][:, None])
        row_a = a_sorted[jnp.clip(grp_start[blk_e][:, None] + r_in_e, 0, A - 1)]
        row_tok = jnp.where(valid, row_a // TOP_K, 0)
        row_dst = jnp.where(valid, (row_a % TOP_K) * T + row_a // TOP_K, A + (b % 2) * MOE_BM + j)
        y4 = _moe(blk_e, n_used, row_tok.reshape(nbt, 1, MOE_BM), row_dst.reshape(nbt, 1, MOE_BM), h2,
                  w_gate[l], b_gate[l], w_up[l], b_up[l], w_down[l], b_down[l], A + 2 * MOE_BM)
        x = _combine(y4, route, x1, ga2, g_ffn_post[l], S).reshape(B, S, D)
    return x
```

```python
import functools

import numpy as np
import jax
import jax.numpy as jnp
from jax import lax
from jax.experimental import pallas as pl
from jax.experimental.pallas import tpu as pltpu

f32 = jnp.float32
bf16 = jnp.bfloat16
i32 = jnp.int32

D_MODEL = 1024
HEAD_DIM = 64
FOX_HEADS = 8
NSA_HEADS = 8
NSA_KV_GROUPS = 2
NSA_HPG = NSA_HEADS // NSA_KV_GROUPS
FOX_WIDTH = FOX_HEADS * HEAD_DIM
NSA_WIDTH = NSA_HEADS * HEAD_DIM
NSA_KV_WIDTH = NSA_KV_GROUPS * HEAD_DIM
CMP_BLOCK = 32
CMP_STRIDE = 16
SEL_BLOCK = 64
SEL_TOPN = 16
WINDOW = 512
REL_BUCKETS = 32
REL_MAX_DIST = 128
N_EXPERTS = 32
TOP_K = 4
SWIGLU_LIMIT = 7.0
SWIGLU_ALPHA = 1.702
RMS_EPS = 1e-6
NEG = -1e30
BIG = 1e9
IN_SIZES = (FOX_WIDTH, FOX_WIDTH, FOX_WIDTH, FOX_HEADS, NSA_WIDTH,
            NSA_KV_WIDTH, NSA_KV_WIDTH, NSA_KV_WIDTH, NSA_KV_WIDTH, NSA_KV_WIDTH, NSA_KV_WIDTH,
            3 * NSA_HEADS, D_MODEL, D_MODEL)

LANES = 128
VMEM_LIMIT = 56 * 1024 * 1024
FOX_TILE = 512
SLC_TILE = 512
WIN_TILE = 512
CMP_TILE = 512
ROW_TILE = 512
MOE_BM = 512
MOE_ROWS = 256
SEL_MASK = 1e9

_NT = (((1,), (1,)), ((), ()))


def _cparams(*sem):
    return pltpu.CompilerParams(dimension_semantics=sem, vmem_limit_bytes=VMEM_LIMIT)


def _const_spec(shape):
    nd = len(shape)
    return pl.BlockSpec(shape, lambda *_: (0,) * nd, pipeline_mode=pl.Buffered(1))


def _split3(a):
    a1 = a.astype(bf16)
    r1 = a - a1.astype(f32)
    a2 = r1.astype(bf16)
    a3 = (r1 - a2.astype(f32)).astype(bf16)
    return a1, a2, a3


ROW_TILES = D_MODEL // LANES


def _store_tile_rows(ref, val):
    n = val.shape[0]
    for c in range(ROW_TILES):
        ref[pl.ds(c, n, stride=ROW_TILES), :] = val[:, c * LANES:(c + 1) * LANES]


def _load_tile_rows(ref, n):
    return jnp.concatenate([ref[pl.ds(c, n, stride=ROW_TILES), :] for c in range(ROW_TILES)], axis=1)


def _loop_two_tiles(n, body, carry):
    carry = lax.fori_loop(0, n // 2, lambda jj, c: body(2 * jj + 1, body(2 * jj, c)), carry)
    return lax.cond(n % 2 == 1, lambda c: body(n - 1, c), lambda c: c, carry)


def _rms(x, g):
    ms = jnp.mean(x * x, axis=-1, keepdims=True)
    return x * lax.rsqrt(ms + RMS_EPS) * g


def _ada_kernel(c_ref, w_ref, b_ref, o_ref):
    c = c_ref[...]
    s = c * jax.nn.sigmoid(c)
    w = w_ref[...]
    s1, s2, _ = _split3(s)
    w1, w2, _ = _split3(w)
    acc = jnp.dot(s1, w1, preferred_element_type=f32)
    acc += jnp.dot(s1, w2, preferred_element_type=f32)
    acc += jnp.dot(s2, w1, preferred_element_type=f32)
    o_ref[...] = acc + b_ref[...]


def _ada(c, w_ada, b_ada):
    B, D = c.shape
    N = w_ada.shape[1]
    tn = 1024
    return pl.pallas_call(
        _ada_kernel,
        out_shape=jax.ShapeDtypeStruct((B, N), f32),
        grid=(N // tn,),
        in_specs=[pl.BlockSpec((B, D), lambda j: (0, 0)),
                  pl.BlockSpec((D, tn), lambda j: (0, j)),
                  pl.BlockSpec((1, tn), lambda j: (0, j))],
        out_specs=pl.BlockSpec((B, tn), lambda j: (0, j)),
        compiler_params=_cparams("arbitrary"),
        name="ada",
    )(c, w_ada, b_ada.reshape(1, N))


def _inproj_kernel(x_ref, sc_ref, sh_ref, g_ref, bfg_ref, tri_ref, esel_ref,
                   wfq, wfk, wfv, wnq, wcm, wkv, wmg, wsm,
                   bq, bv, bvs,
                   ofq, ofk, ofv, onq, ocm, oksl, ovsl, okwn, ovwn, omg, osm,
                   carry_ref, *, tm, tiles_per_seq):
    i = pl.program_id(0)
    x = x_ref[...]
    h = _rms(x, g_ref[...]) * (1.0 + sc_ref[0]) + sh_ref[0]
    hb = h.astype(bf16)

    def proj(w):
        return jnp.dot(hb, w[...], preferred_element_type=f32)

    low_half = lax.broadcasted_iota(i32, (tm, LANES), 1) < HEAD_DIM

    def place(tile, src_hi, dst_hi):
        t = tile if src_hi == dst_hi else pltpu.roll(tile, HEAD_DIM, 1)
        return jnp.where(low_half, 0.0, t) if dst_hi else jnp.where(low_half, t, 0.0)

    def tiles(c):
        return [c[:, j * LANES:(j + 1) * LANES] for j in range(c.shape[1] // LANES)]

    def heads(c, even_hi, odd_hi):
        out = []
        for t in tiles(c):
            out += [place(t, False, even_hi), place(t, True, odd_hi)]
        return jnp.concatenate(out, axis=1)

    def group_both(t):
        return jnp.concatenate([place(t, False, False), place(t, False, True),
                                place(t, True, False), place(t, True, True)], axis=1)

    ofq[...] = (heads(proj(wfq), False, False) + bq[...]).astype(bf16)
    ofv[...] = (heads(proj(wfv), False, True) + bv[...]).astype(bf16)
    onq[...] = heads(proj(wnq), True, True).astype(bf16)
    ocm[...] = proj(wcm).astype(bf16)
    ksl_c, vsl_c, kwn_c, vwn_c = tiles(proj(wkv))
    ovsl[...] = (group_both(vsl_c) + bvs[...]).astype(bf16)
    okwn[...] = heads(kwn_c, True, True).astype(bf16)
    ovwn[...] = (group_both(vwn_c) + bvs[...]).astype(bf16)
    omg[...] = jax.nn.sigmoid(proj(wmg)).astype(bf16)

    row = lax.broadcasted_iota(i32, (tm, 2 * LANES), 0)
    lane = lax.broadcasted_iota(i32, (tm, 2 * LANES), 1)
    blk = ((i % tiles_per_seq) * tm + row) // SEL_BLOCK
    onehot = jnp.where((lane & (LANES - 1)) == blk, 1.0, 0.0)
    oksl[...] = (heads(ksl_c, True, True) + onehot).astype(bf16)

    sm = proj(wsm)
    osm[...] = sm
    z = sm + bfg_ref[...]
    lane1 = lax.broadcasted_iota(i32, (tm, LANES), 1)
    logf = jnp.where(lane1 < FOX_HEADS, jnp.minimum(z, 0.0) - jnp.log(1.0 + jnp.exp(-jnp.abs(z))), 0.0)

    @pl.when(i % tiles_per_seq == 0)
    def _():
        carry_ref[...] = jnp.zeros_like(carry_ref)

    tri = tri_ref[...]
    cum = carry_ref[0:1, :]
    for piece in _split3(logf):
        cum = cum + jnp.dot(tri, piece, preferred_element_type=f32)
    carry_ref[0:1, :] = cum[tm - 1:tm, :]
    ncat = jnp.concatenate(_split3(-cum), axis=1)
    ofk[...] = (heads(proj(wfk), False, False)
                + jnp.dot(ncat, esel_ref[...], preferred_element_type=f32)).astype(bf16)


def _inproj(x2, sc1, sh1, g_pre, w_in, b_forget, S):
    T, D = x2.shape
    tm = min(ROW_TILE, S)
    tiles_per_seq = S // tm
    offs = np.cumsum(IN_SIZES)[:-1].tolist()
    (wfq, wfk, wfv, wff, wnq, wkcm, wvcm, wksl, wvsl, wkwn, wvwn, wng, wmgf, wmgn) = jnp.split(w_in, offs, axis=1)
    scale = HEAD_DIM ** -0.5
    cast = lambda w: w.astype(bf16)
    weights = [
        cast(wfq * scale),
        cast(wfk),
        cast(wfv),
        cast(wnq * scale),
        cast(jnp.concatenate([wkcm, wvcm], axis=1)),
        cast(jnp.concatenate([wksl, wvsl, wkwn, wvwn], axis=1)),
        cast(jnp.concatenate([wmgf, wmgn], axis=1)),
        cast(jnp.concatenate([wff, wng, jnp.zeros((D, LANES - FOX_HEADS - 3 * NSA_HEADS), f32)], axis=1)),
    ]
    G2 = NSA_KV_GROUPS * LANES
    widths = [FOX_HEADS * LANES] * 3 + [NSA_HEADS * LANES, 2 * NSA_KV_WIDTH, G2, 2 * G2, G2, 2 * G2, 2 * D, LANES]
    bq = np.zeros((1, FOX_HEADS * LANES), np.float32)
    bv = np.zeros((1, FOX_HEADS * LANES), np.float32)
    for h in range(FOX_HEADS):
        bq[0, h * LANES + HEAD_DIM:h * LANES + HEAD_DIM + 3] = 1.0
        bv[0, h * LANES + (HEAD_DIM if h % 2 == 0 else 0)] = 1.0
    bvs = np.zeros((1, NSA_KV_GROUPS * 2 * LANES), np.float32)
    for g in range(NSA_KV_GROUPS):
        bvs[0, g * 2 * LANES + HEAD_DIM] = 1.0
        bvs[0, g * 2 * LANES + LANES] = 1.0
    esel = np.zeros((3 * LANES, FOX_HEADS * LANES), np.float32)
    for j in range(3):
        for h in range(FOX_HEADS):
            esel[j * LANES + h, h * LANES + HEAD_DIM + j] = 1.0
    tri = np.tril(np.ones((tm, tm), np.float32))
    bfg = jnp.concatenate([b_forget, jnp.zeros((LANES - FOX_HEADS,), f32)]).reshape(1, LANES)

    out_dtypes = [bf16] * 10 + [f32]
    row_spec = lambda n: pl.BlockSpec((tm, n), lambda i: (i, 0))
    mod_spec = pl.BlockSpec((1, 1, D), lambda i: (i // tiles_per_seq, 0, 0))
    consts = [jnp.asarray(tri, bf16), jnp.asarray(esel, bf16)]
    biases = [jnp.asarray(bq), jnp.asarray(bv), jnp.asarray(bvs)]
    outs = pl.pallas_call(
        functools.partial(_inproj_kernel, tm=tm, tiles_per_seq=tiles_per_seq),
        out_shape=[jax.ShapeDtypeStruct((T, n), dt) for n, dt in zip(widths, out_dtypes)],
        grid=(T // tm,),
        in_specs=[row_spec(D), mod_spec, mod_spec, _const_spec((1, D)), _const_spec((1, LANES))]
        + [_const_spec(c.shape) for c in consts]
        + [_const_spec(w.shape) for w in weights]
        + [_const_spec(b.shape) for b in biases],
        out_specs=[row_spec(n) for n in widths],
        scratch_shapes=[pltpu.VMEM((8, LANES), f32)],
        compiler_params=_cparams("arbitrary"),
        name="inproj",
    )(x2, sc1, sh1, g_pre.reshape(1, D), bfg, *consts, *weights, *biases)
    return outs


def _fox_kernel(q_ref, k_ref, v_ref, o_ref, *, tq):
    i = pl.program_id(2)
    row = lax.broadcasted_iota(i32, (tq, tq), 0)
    col = lax.broadcasted_iota(i32, (tq, tq), 1)
    qs = [q_ref[0, :, hh * LANES:(hh + 1) * LANES] for hh in range(2)]

    def tile(j, carry, diag):
        start = pl.multiple_of(j * tq, tq)
        new = []
        for hh in range(2):
            m, acc = carry[hh]
            k = k_ref[0, pl.ds(start, tq), hh * LANES:(hh + 1) * LANES]
            v = v_ref[0, pl.ds(start, tq), hh * LANES:(hh + 1) * LANES]
            s = lax.dot_general(qs[hh], k, _NT, preferred_element_type=f32)
            if diag:
                s = jnp.where(col <= row, s, NEG)
            m_new = jnp.maximum(m, jnp.max(s, axis=-1, keepdims=True))
            p = jnp.exp(s - m_new).astype(bf16)
            acc = jnp.exp(m - m_new) * acc + jnp.dot(p, v, preferred_element_type=f32)
            new.append((m_new, acc))
        return tuple(new)

    carry = tuple((jnp.full((tq, 1), NEG, f32), jnp.zeros((tq, LANES), f32)) for _ in range(2))
    carry = tile(i, carry, True)
    carry = _loop_two_tiles(i, lambda j, c: tile(j, c, False), carry)
    o_ref[0] = _pair_out(carry[0][1], carry[1][1]).astype(bf16)


def _fox(fq, fk, fv, B, S):
    tq = min(FOX_TILE, S)
    nq = S // tq
    q3 = fq.reshape(B, S, FOX_HEADS * LANES)
    k3 = fk.reshape(B, S, FOX_HEADS * LANES)
    v3 = fv.reshape(B, S, FOX_HEADS * LANES)
    return pl.pallas_call(
        functools.partial(_fox_kernel, tq=tq),
        out_shape=jax.ShapeDtypeStruct((B, S, FOX_WIDTH), bf16),
        grid=(B, FOX_HEADS // 2, nq),
        in_specs=[pl.BlockSpec((1, tq, 2 * LANES), lambda b, hp, i: (b, i, hp)),
                  pl.BlockSpec((1, S, 2 * LANES), lambda b, hp, i: (b, 0, hp)),
                  pl.BlockSpec((1, S, 2 * LANES), lambda b, hp, i: (b, 0, hp))],
        out_specs=pl.BlockSpec((1, tq, LANES), lambda b, hp, i: (b, i, hp)),
        compiler_params=_cparams("parallel", "parallel", "arbitrary"),
        name="fox",
    )(q3, k3, v3)


def _compress_kernel(x_ref, pea_ref, peb_ref, wa_ref, wb_ref, okc, ovc, *, nc):
    x = x_ref[0].astype(f32)
    xa = (x + pea_ref[...]).astype(bf16)
    xb = (x + peb_ref[...]).astype(bf16)
    a = jnp.dot(xa, wa_ref[...], preferred_element_type=f32)
    b = jnp.dot(xb, wb_ref[...], preferred_element_type=f32)
    out = a + pltpu.roll(b, nc - 1, 0)
    okc[0] = out[:, :2 * LANES].astype(bf16)
    ovc[0] = out[:, 2 * LANES:].astype(bf16)


def _compress(cm, pe_k, pe_v, w_cmp_k, w_cmp_v, B, S):
    nc = S // CMP_STRIDE
    half = CMP_BLOCK // 2
    win = half * 2 * LANES
    x = cm.reshape(B, nc, win)
    wk = w_cmp_k.reshape(CMP_BLOCK, HEAD_DIM, HEAD_DIM)
    wv = w_cmp_v.reshape(CMP_BLOCK, HEAD_DIM, HEAD_DIM)
    H = HEAD_DIM

    def build(wk_h, wv_h):
        z = jnp.zeros((half, H, H), f32)

        def rows(cols):
            return jnp.concatenate([cols.get(c, z) for c in range(12)], axis=2)

        w = jnp.concatenate([rows({1: wk_h}),
                             rows({3: wk_h}),
                             rows({4: wv_h, 7: wv_h}),
                             rows({8: wv_h, 11: wv_h})],
                            axis=1)
        return w.reshape(win, 6 * LANES).astype(bf16)

    wa = build(wk[:half], wv[:half])
    wb = build(wk[half:], wv[half:])

    def pe_row(pk, pv):
        return jnp.concatenate([pk, pk, pv, pv], axis=1).reshape(1, win)

    pea = pe_row(pe_k[:half], pe_v[:half])
    peb = pe_row(pe_k[half:], pe_v[half:])
    return pl.pallas_call(
        functools.partial(_compress_kernel, nc=nc),
        out_shape=[jax.ShapeDtypeStruct((B, nc, 2 * LANES), bf16),
                   jax.ShapeDtypeStruct((B, nc, 4 * LANES), bf16)],
        grid=(B,),
        in_specs=[pl.BlockSpec((1, nc, win), lambda b: (b, 0, 0)),
                  _const_spec((1, win)), _const_spec((1, win)),
                  _const_spec((win, 6 * LANES)), _const_spec((win, 6 * LANES))],
        out_specs=[pl.BlockSpec((1, nc, 2 * LANES), lambda b: (b, 0, 0)),
                   pl.BlockSpec((1, nc, 4 * LANES), lambda b: (b, 0, 0))],
        compiler_params=_cparams("parallel"),
        name="compress",
    )(x, pea, peb, wa, wb)


def _cmpsel_kernel(q_ref, kc_ref, vc_ref, pat_ref, ov_ref, ocmp, osel, score_scr, *, tq, nc, n_sel, top_n, past):
    i = pl.program_id(1)
    kc = kc_ref[0]
    c0 = i * (tq // CMP_STRIDE) - past
    wio = lax.broadcasted_iota(i32, (LANES, nc), 0)
    cio = lax.broadcasted_iota(i32, (LANES, nc), 1)
    shift = jnp.where(cio == wio + c0, 1.0, 0.0).astype(bf16)
    t = i * tq + lax.broadcasted_iota(i32, (tq, nc), 0)
    cend = lax.broadcasted_iota(i32, (tq, nc), 1) * CMP_STRIDE + (CMP_BLOCK - 1)
    valid = cend <= t
    pcs = jnp.zeros((tq, nc), f32)
    outs = []
    for hh in range(NSA_HPG):
        q = q_ref[0, :, hh * LANES:(hh + 1) * LANES]
        pat = pat_ref[hh]
        pat_hi = pat.astype(bf16)
        pat_lo = (pat - pat_hi.astype(f32)).astype(bf16)
        cb = (jnp.dot(pat_hi, shift, preferred_element_type=f32)
              + jnp.dot(pat_lo, shift, preferred_element_type=f32))
        lc = jnp.where(valid, lax.dot_general(q, kc, _NT, preferred_element_type=f32) + cb, NEG)
        m = jnp.max(lc, axis=-1, keepdims=True)
        p = jnp.where(valid, jnp.exp(lc - m), 0.0)
        l = jnp.sum(p, axis=-1, keepdims=True)
        pc = p * jnp.where(l > 0.0, 1.0 / l, 0.0)
        pcs = pcs + pc
        v = vc_ref[0, :, (hh % 2) * LANES:(hh % 2 + 1) * LANES]
        outs.append(jnp.dot(pc.astype(bf16), v, preferred_element_type=f32))
    ocmp[0] = jnp.concatenate([outs[0] + outs[1], outs[2] + outs[3]], axis=1).astype(bf16)

    hi = pcs.astype(bf16)
    lo = (pcs - hi.astype(f32)).astype(bf16)
    ov = ov_ref[...]
    imp = (lax.dot_general(ov, hi, _NT, preferred_element_type=f32)
           + lax.dot_general(ov, lo, _NT, preferred_element_type=f32))
    jio = lax.broadcasted_iota(i32, (n_sel, tq), 0)
    t = i * tq + lax.broadcasted_iota(i32, (n_sel, tq), 1)
    cur = t // SEL_BLOCK
    forced = (jio == 0) | (jio == cur) | (jio == cur - 1)
    score = jnp.where(forced, BIG, jnp.where(jio <= cur, imp, -BIG))
    score_scr[...] = score
    blocks_per_tile = tq // SEL_BLOCK

    def visit(g, rank):
        for u in range(blocks_per_tile):
            jp = g * blocks_per_tile + u
            r = score_scr[pl.ds(jp, 1), :]
            tie = jnp.where(jio > jp, 1.0, 0.0)
            rank = rank + jnp.where(r > score, 1.0, jnp.where(r == score, tie, 0.0))
        return rank

    rank = lax.fori_loop(0, i + 1, visit, jnp.zeros((n_sel, tq), f32))
    selb = jnp.where(rank < top_n, 0.0, -SEL_MASK)
    padded = jnp.concatenate([selb, jnp.zeros((LANES - n_sel, tq), f32)], axis=0)
    osel[0, 0] = padded.T.astype(bf16)


def _bucket_bounds():
    n = np.arange(0, 4 * REL_MAX_DIST)
    max_exact = REL_BUCKETS // 2
    nf = np.maximum(n, 1).astype(np.float32)
    large = max_exact + (np.log(nf / np.float32(max_exact)) / np.float32(np.log(REL_MAX_DIST / max_exact))
                         * np.float32(REL_BUCKETS - max_exact)).astype(np.int32)
    bucket = np.where(n < max_exact, n, np.minimum(large, REL_BUCKETS - 1))
    return [int(n[bucket > b].min()) for b in range(REL_BUCKETS - 1)]


_BOUNDS = _bucket_bounds()
_CMP_PAST = (_BOUNDS[-1] + CMP_BLOCK - 1 + CMP_STRIDE - 1) // CMP_STRIDE - 1


def _rel_bias_of(d, rb_ref, h):
    far = rb_ref[(REL_BUCKETS - 1) * NSA_HEADS + h]
    v = jnp.zeros(d.shape, f32)
    for b in reversed(range(REL_BUCKETS - 1)):
        v = jnp.where(d < _BOUNDS[b], rb_ref[b * NSA_HEADS + h] - far, v)
    return v


def _biasgen_kernel(rb_ref, oslc, owin, ocmp, *, ts, tw, tc, nk):
    h = pl.program_id(0)

    def tile(t, off, window):
        lo, hi = off - (t - 1), off + (t - 1)
        if hi < 0 or (window is not None and lo >= window):
            return jnp.full((t, t), NEG, f32)
        d = lax.broadcasted_iota(i32, (t, t), 0) - lax.broadcasted_iota(i32, (t, t), 1) + off
        val = _rel_bias_of(d, rb_ref, h) if lo < _BOUNDS[-1] else jnp.zeros((t, t), f32)
        if lo < 0:
            val = jnp.where(d >= 0, val, NEG)
        if window is not None and hi >= window:
            val = jnp.where(d < window, val, NEG)
        return val

    oslc[0, 0, 0] = tile(ts, 0, None)
    oslc[0, 1, 0] = tile(ts, ts, None)
    oslc[0, 2, 0] = jnp.full((ts, ts), NEG, f32)
    for v in range(nk):
        for cc in range(nk):
            owin[0, v, 0, :, cc * tw:(cc + 1) * tw] = tile(tw, (v - cc) * tw, WINDOW)
    rr = lax.broadcasted_iota(i32, (tc, LANES), 0)
    w = lax.broadcasted_iota(i32, (tc, LANES), 1)
    d = rr - CMP_STRIDE * (w - _CMP_PAST) - (CMP_BLOCK - 1)
    ocmp[0] = jnp.where(d >= 0, _rel_bias_of(d, rb_ref, h), 0.0)


def _biasgen(rel_bias, S):
    ts, tw, tc = min(SLC_TILE, S), min(WIN_TILE, S), min(CMP_TILE, S)
    nk = WINDOW // tw + 1
    assert WINDOW % tw == 0 and S >= nk * tw and min(ts, tw) + 1 >= _BOUNDS[-1]
    assert tc // CMP_STRIDE + _CMP_PAST <= LANES
    G = NSA_KV_GROUPS
    return pl.pallas_call(
        functools.partial(_biasgen_kernel, ts=ts, tw=tw, tc=tc, nk=nk),
        out_shape=[jax.ShapeDtypeStruct((G, 3, NSA_HPG, ts, ts), f32),
                   jax.ShapeDtypeStruct((G, nk, NSA_HPG, tw, nk * tw), f32),
                   jax.ShapeDtypeStruct((NSA_HEADS, tc, LANES), f32)],
        grid=(NSA_HEADS,),
        in_specs=[pl.BlockSpec(memory_space=pltpu.SMEM)],
        out_specs=[pl.BlockSpec((1, 3, 1, ts, ts), lambda h: (h // NSA_HPG, 0, h % NSA_HPG, 0, 0)),
                   pl.BlockSpec((1, nk, 1, tw, nk * tw), lambda h: (h // NSA_HPG, 0, h % NSA_HPG, 0, 0)),
                   pl.BlockSpec((1, tc, LANES), lambda h: (h, 0, 0))],
        compiler_params=_cparams("arbitrary"),
        name="biasgen",
    )(rel_bias.reshape(-1))


def _cmpsel(nq_arr, kc, vc, pat, B, S):
    tq = min(CMP_TILE, S)
    nqt = S // tq
    nc = S // CMP_STRIDE
    n_sel = S // SEL_BLOCK
    top_n = min(SEL_TOPN, n_sel)
    G = NSA_KV_GROUPS
    assert n_sel <= HEAD_DIM
    c = np.arange(nc)[None, :]
    j = np.arange(n_sel)[:, None]
    ov = ((c * CMP_STRIDE < j * SEL_BLOCK + SEL_BLOCK) & (c * CMP_STRIDE + CMP_BLOCK > j * SEL_BLOCK)
          & (c < nc - 1)).astype(np.float32)
    q3 = nq_arr.reshape(B, S, NSA_HEADS * LANES)
    return pl.pallas_call(
        functools.partial(_cmpsel_kernel, tq=tq, nc=nc, n_sel=n_sel, top_n=top_n, past=_CMP_PAST),
        out_shape=[jax.ShapeDtypeStruct((B, S, NSA_HEADS * HEAD_DIM), bf16),
                   jax.ShapeDtypeStruct((B, G, S, LANES), bf16)],
        grid=(G, nqt, B),
        in_specs=[pl.BlockSpec((1, tq, NSA_HPG * LANES), lambda g, i, b: (b, i, g)),
                  pl.BlockSpec((1, nc, LANES), lambda g, i, b: (b, 0, g)),
                  pl.BlockSpec((1, nc, 2 * LANES), lambda g, i, b: (b, 0, g)),
                  pl.BlockSpec((NSA_HPG, tq, LANES), lambda g, i, b: (g, 0, 0)),
                  _const_spec((n_sel, nc))],
        out_specs=[pl.BlockSpec((1, tq, 2 * LANES), lambda g, i, b: (b, i, g)),
                   pl.BlockSpec((1, 1, tq, LANES), lambda g, i, b: (b, g, i, 0))],
        scratch_shapes=[pltpu.VMEM((n_sel, tq), f32)],
        compiler_params=_cparams("parallel", "arbitrary", "arbitrary"),
        name="cmpsel",
    )(q3, kc, vc, pat, jnp.asarray(ov, bf16))


def _pair_out(acc_e, acc_o):
    lane = lax.broadcasted_iota(i32, acc_e.shape, 1)
    return jnp.where(lane < HEAD_DIM, acc_e / acc_e[:, HEAD_DIM:HEAD_DIM + 1], acc_o / acc_o[:, 0:1])


def _slc_kernel(q_ref, sb_ref, k_ref, v_ref, bias_ref, o_ref, *, tq):
    i = pl.program_id(2)
    sb = sb_ref[0, 0]
    qs = [q_ref[0, :, h * LANES:(h + 1) * LANES] + sb for h in range(NSA_HPG)]

    def tile(j, carry, bias_idx):
        start = pl.multiple_of(j * tq, tq)
        k = k_ref[0, pl.ds(start, tq), :]
        v = v_ref[0, pl.ds(start, tq), :]
        new = []
        for h in range(NSA_HPG):
            m, acc = carry[h]
            s = lax.dot_general(qs[h], k, _NT, preferred_element_type=f32)
            if bias_idx is not None:
                s = s + bias_ref[0, bias_idx, h]
            m_new = jnp.maximum(m, jnp.max(s, axis=-1, keepdims=True))
            p = jnp.exp(s - m_new).astype(bf16)
            vh = v[:, (h % 2) * LANES:(h % 2 + 1) * LANES]
            acc = jnp.exp(m - m_new) * acc + jnp.dot(p, vh, preferred_element_type=f32)
            new.append((m_new, acc))
        return tuple(new)

    carry = tuple((jnp.full((tq, 1), NEG, f32), jnp.zeros((tq, LANES), f32)) for _ in range(NSA_HPG))
    carry = tile(i, carry, 0)
    carry = tile(jnp.maximum(i - 1, 0), carry, jnp.where(i >= 1, 1, 2))
    carry = _loop_two_tiles(jnp.maximum(i - 1, 0), lambda j, c: tile(j, c, None), carry)
    o_ref[0] = jnp.concatenate([_pair_out(carry[0][1], carry[1][1]),
                                _pair_out(carry[2][1], carry[3][1])], axis=1).astype(bf16)


def _win_kernel(q_ref, k_ref, v_ref, bias_ref, o_ref, *, tq, nk):
    i = pl.program_id(2)
    start = pl.multiple_of(jnp.maximum(i - (nk - 1), 0) * tq, tq)
    k = k_ref[0, pl.ds(start, nk * tq), :]
    v = v_ref[0, pl.ds(start, nk * tq), :]
    accs = []
    for h in range(NSA_HPG):
        q = q_ref[0, :, h * LANES:(h + 1) * LANES]
        s = lax.dot_general(q, k, _NT, preferred_element_type=f32) + bias_ref[0, 0, h]
        p = jnp.exp(s - jnp.max(s, axis=-1, keepdims=True)).astype(bf16)
        accs.append(jnp.dot(p, v[:, (h % 2) * LANES:(h % 2 + 1) * LANES], preferred_element_type=f32))
    o_ref[0] = jnp.concatenate([_pair_out(accs[0], accs[1]), _pair_out(accs[2], accs[3])], axis=1).astype(bf16)


def _nsa_flash(kind, nq_arr, selb, k_arr, v_arr, bias, B, S):
    tq = bias.shape[3]
    nqt = S // tq
    G = NSA_KV_GROUPS
    q3 = nq_arr.reshape(B, S, NSA_HEADS * LANES)
    k3 = k_arr.reshape(B, S, G * LANES)
    v3 = v_arr.reshape(B, S, G * 2 * LANES)
    q_spec = pl.BlockSpec((1, tq, NSA_HPG * LANES), lambda g, b, i: (b, i, g))
    k_spec = pl.BlockSpec((1, S, LANES), lambda g, b, i: (b, 0, g))
    v_spec = pl.BlockSpec((1, S, 2 * LANES), lambda g, b, i: (b, 0, g))
    if kind == "slc":
        kern = functools.partial(_slc_kernel, tq=tq)
        extra_specs = [pl.BlockSpec((1, 1, tq, LANES), lambda g, b, i: (b, g, i, 0))]
        extra = [selb]
        b_spec = pl.BlockSpec((1, 3, NSA_HPG, tq, tq), lambda g, b, i: (g, 0, 0, 0, 0),
                              pipeline_mode=pl.Buffered(1))
    else:
        nk = bias.shape[1]
        kern = functools.partial(_win_kernel, tq=tq, nk=nk)
        extra_specs, extra = [], []
        b_spec = pl.BlockSpec((1, 1, NSA_HPG, tq, nk * tq), lambda g, b, i: (g, jnp.minimum(i, nk - 1), 0, 0, 0))
    return pl.pallas_call(
        kern,
        out_shape=jax.ShapeDtypeStruct((B, S, NSA_HEADS * HEAD_DIM), bf16),
        grid=(G, B, nqt),
        in_specs=[q_spec] + extra_specs + [k_spec, v_spec, b_spec],
        out_specs=pl.BlockSpec((1, tq, 2 * LANES), lambda g, b, i: (b, i, g)),
        compiler_params=_cparams("parallel", "parallel", "arbitrary"),
        name=kind,
    )(q3, *extra, k3, v3, bias)


def _post_kernel(x_ref, ofox, ocmp, oslc, owin, mg_ref, sm_ref, ga1, sc2, sh2, gpost, gpre,
                 wfp, wnp_, wmo, wr, br, eg,
                 x1_ref, h2_ref, route_ref, cnt_ref, carry_ref, *, tm):
    i = pl.program_id(0)
    W = NSA_WIDTH
    gates = jax.nn.sigmoid(sm_ref[...]).astype(bf16)
    gx = jnp.dot(gates, eg[...], preferred_element_type=f32)
    nsa = (gx[:, :W] * ocmp[...].astype(f32) + gx[:, W:2 * W] * oslc[...].astype(f32)
           + gx[:, 2 * W:] * owin[...].astype(f32))
    y_nsa = jnp.dot(nsa.astype(bf16), wnp_[...], preferred_element_type=f32)
    y_fox = jnp.dot(ofox[...], wfp[...], preferred_element_type=f32)
    mg = mg_ref[...].astype(f32)
    mix = (mg[:, :D_MODEL] * y_fox + mg[:, D_MODEL:] * y_nsa).astype(bf16)
    mixed = jnp.dot(mix, wmo[...], preferred_element_type=f32)
    x1 = x_ref[...] + ga1[0] * _rms(mixed, gpost[...])
    x1_ref[...] = x1
    h2 = _rms(x1, gpre[...]) * (1.0 + sc2[0]) + sh2[0]
    _store_tile_rows(h2_ref, h2)

    lane = lax.broadcasted_iota(i32, (tm, LANES), 1)
    logits = jnp.dot(h2.astype(bf16), wr[...], preferred_element_type=f32) + br[...]
    l = jnp.where(lane < N_EXPERTS, logits, NEG)
    vals, idxs = [], []
    for _ in range(TOP_K):
        m = jnp.max(l, axis=-1, keepdims=True)
        idx = jnp.min(jnp.where(l == m, lane, LANES), axis=-1, keepdims=True)
        vals.append(m)
        idxs.append(idx)
        l = jnp.where(lane == idx, NEG, l)
    es = [jnp.exp(v - vals[0]) for v in vals]
    den = es[0] + es[1] + es[2] + es[3]

    @pl.when(i == 0)
    def _():
        carry_ref[...] = jnp.zeros_like(carry_ref)

    cnt = sum(jnp.where(lane == idx, 1.0, 0.0) for idx in idxs)
    new_carry = carry_ref[0:1, :] + jnp.sum(cnt, axis=0, keepdims=True)
    carry_ref[0:1, :] = new_carry
    cnt_ref[...] = jnp.broadcast_to(new_carry, cnt_ref.shape)
    route = jnp.zeros((tm, LANES), f32)
    for k in range(TOP_K):
        route = jnp.where(lane == k, idxs[k].astype(f32), route)
        route = jnp.where(lane == 2 * TOP_K + k, es[k] / den, route)
    route_ref[...] = route


def _post(x2, ofox, ocmp, oslc, owin, mg, sm, ga1, sc2, sh2, g_post, g_pre2,
          w_fox_proj, w_nsa_proj, w_mix_out, w_router, b_router, S):
    T, D = x2.shape
    tm = min(ROW_TILE, S)
    tiles_per_seq = S // tm
    W = NSA_WIDTH
    eg = np.zeros((LANES, 3 * W), np.float32)
    for h in range(NSA_HEADS):
        for k in range(3):
            eg[FOX_HEADS + 3 * h + k, k * W + h * HEAD_DIM:k * W + (h + 1) * HEAD_DIM] = 1.0
    wr = jnp.concatenate([w_router, jnp.zeros((D, LANES - N_EXPERTS), f32)], axis=1).astype(bf16)
    br = jnp.concatenate([b_router, jnp.zeros((LANES - N_EXPERTS,), f32)]).reshape(1, LANES)
    row = lambda n: pl.BlockSpec((tm, n), lambda i: (i, 0))
    mod = pl.BlockSpec((1, 1, D), lambda i: (i // tiles_per_seq, 0, 0))
    consts = [w_fox_proj.astype(bf16), w_nsa_proj.astype(bf16), w_mix_out.astype(bf16), wr, br,
              jnp.asarray(eg, bf16)]
    return pl.pallas_call(
        functools.partial(_post_kernel, tm=tm),
        out_shape=[jax.ShapeDtypeStruct((T, D), f32), jax.ShapeDtypeStruct((T * ROW_TILES, LANES), f32),
                   jax.ShapeDtypeStruct((T, LANES), f32), jax.ShapeDtypeStruct((8, LANES), f32)],
        grid=(T // tm,),
        in_specs=[row(D), row(FOX_WIDTH), row(W), row(W), row(W), row(2 * D), row(LANES),
                  mod, mod, mod, _const_spec((1, D)), _const_spec((1, D))]
        + [_const_spec(c.shape) for c in consts],
        out_specs=[row(D), pl.BlockSpec((tm * ROW_TILES, LANES), lambda i: (i, 0)), row(LANES),
                   pl.BlockSpec((8, LANES), lambda i: (0, 0))],
        scratch_shapes=[pltpu.VMEM((8, LANES), f32)],
        compiler_params=_cparams("arbitrary"),
        name="post",
    )(x2, ofox, ocmp, oslc, owin, mg, sm, ga1, sc2, sh2, g_post.reshape(1, D), g_pre2.reshape(1, D), *consts)


def _moe_kernel(be_ref, nu_ref, tok0_ref, tokn_ref, dstp_ref, h_hbm, wg, bg, wu, bu, wd, bd,
                y_hbm, xbuf, ybuf, wgb, wub, wdb, gsem, ssem, *, bm):
    i = pl.program_id(0)
    nu = nu_ref[0]
    rt = ROW_TILES

    def rows(r):
        return pl.ds(r * rt, rt) if isinstance(r, int) else pl.ds(pl.multiple_of(r * rt, rt), rt)

    def gather_copy(tok_ref, r, s):
        return pltpu.make_async_copy(h_hbm.at[tok_ref[0, 0, r]], xbuf.at[s, rows(r), :], gsem.at[s])

    def scatter_copy(r, s):
        return pltpu.make_async_copy(ybuf.at[s, rows(r), :], y_hbm.at[dstp_ref[0, 0, r]], ssem.at[s])

    def wait_gather(s):
        pltpu.make_async_copy(xbuf.at[s], xbuf.at[s], gsem.at[s]).wait()

    def wait_scatter(s):
        pltpu.make_async_copy(ybuf.at[s], ybuf.at[s], ssem.at[s]).wait()

    @pl.when(i == 0)
    def _():
        ybuf[1] = jnp.zeros((bm * rt, LANES), f32)
        n_real = y_hbm.shape[0] - 2 * bm

        def issue(r, c):
            gather_copy(tok0_ref, r, 0).start()
            for half in range(2):
                pltpu.make_async_copy(ybuf.at[1, rows(r), :], y_hbm.at[n_real + half * bm + r],
                                      ssem.at[1]).start()
            return c

        lax.fori_loop(0, bm, issue, 0)
        for half in range(2):
            wait_scatter(1)

    def step(slot):
        other = 1 - slot
        wait_gather(slot)

        @pl.when(i >= 1)
        def _():
            wait_scatter(slot)

        @pl.when(jnp.logical_or(i == 0, be_ref[i] != be_ref[jnp.maximum(i - 1, 0)]))
        def _():
            wgb[...] = wg[0].astype(bf16)
            wub[...] = wu[0].astype(bf16)
            wdb[...] = wd[0].astype(bf16)

        for r in range(bm):
            gather_copy(tokn_ref, r, other).start(priority=r % 2)
            scatter_copy(r, other).start(priority=(r + 1) % 2)
        x = _load_tile_rows(xbuf.at[slot], bm).astype(bf16)
        g = jnp.dot(x, wgb[...], preferred_element_type=f32) + bg[0]
        u = jnp.dot(x, wub[...], preferred_element_type=f32) + bu[0]
        gt = jnp.minimum(g, SWIGLU_LIMIT)
        up = jnp.clip(u, -SWIGLU_LIMIT, SWIGLU_LIMIT)
        a = (gt * jax.nn.sigmoid(SWIGLU_ALPHA * gt) * (up + 1.0)).astype(bf16)
        _store_tile_rows(ybuf.at[slot], jnp.dot(a, wdb[...], preferred_element_type=f32) + bd[0])

    def drain(slot):
        other = 1 - slot
        wait_gather(slot)
        wait_scatter(slot)

        def issue(r, c):
            scatter_copy(r, other).start()
            return c

        lax.fori_loop(0, bm, issue, 0)
        wait_scatter(other)

    for s in range(2):
        pl.when(jnp.logical_and(i < nu, i % 2 == s))(functools.partial(step, s))
        pl.when(jnp.logical_and(i == nu, i % 2 == s))(functools.partial(drain, s))


def _moe(blk_e, n_used, row_tok, row_dst, h2t, w_gate, b_gate, w_up, b_up, w_down, b_down, n_rows):
    D = D_MODEL
    E, _, F = w_gate.shape
    nbt = row_tok.shape[0]
    bm = MOE_BM
    wsel = lambda i, be, nu: (be[jnp.minimum(i, nu[0] - 1)], 0, 0)
    idx_spec = lambda f: pl.BlockSpec((1, 1, bm), lambda i, be, nu: (f(i), 0, 0), memory_space=pltpu.SMEM)
    return pl.pallas_call(
        functools.partial(_moe_kernel, bm=bm),
        out_shape=jax.ShapeDtypeStruct((n_rows, ROW_TILES, LANES), f32),
        grid_spec=pltpu.PrefetchScalarGridSpec(
            num_scalar_prefetch=2, grid=(nbt,),
            in_specs=[idx_spec(lambda i: 0),
                      idx_spec(lambda i: jnp.minimum(i + 1, nbt - 1)),
                      idx_spec(lambda i: jnp.maximum(i - 1, 0)),
                      pl.BlockSpec(memory_space=pl.ANY),
                      pl.BlockSpec((1, D, F), wsel), pl.BlockSpec((1, 1, F), wsel),
                      pl.BlockSpec((1, D, F), wsel), pl.BlockSpec((1, 1, F), wsel),
                      pl.BlockSpec((1, F, D), wsel), pl.BlockSpec((1, 1, D), wsel)],
            out_specs=pl.BlockSpec(memory_space=pl.ANY),
            scratch_shapes=[pltpu.VMEM((2, bm * ROW_TILES, LANES), f32), pltpu.VMEM((2, bm * ROW_TILES, LANES), f32),
                            pltpu.VMEM((D, F), bf16), pltpu.VMEM((D, F), bf16), pltpu.VMEM((F, D), bf16),
                            pltpu.SemaphoreType.DMA((2,)), pltpu.SemaphoreType.DMA((2,))]),
        compiler_params=_cparams("arbitrary"),
        name="moe",
    )(blk_e, n_used, row_tok, row_tok, row_dst, h2t.reshape(-1, ROW_TILES, LANES),
      w_gate, b_gate.reshape(E, 1, F), w_up, b_up.reshape(E, 1, F), w_down, b_down.reshape(E, 1, D))


def _combine_kernel(y0, y1, y2, y3, route_ref, x1_ref, ga2, gpost, o_ref):
    route = route_ref[...]
    tm = o_ref.shape[0]
    y = jnp.zeros(o_ref.shape, f32)
    for k, yk in enumerate((y0, y1, y2, y3)):
        y = y + route[:, 2 * TOP_K + k:2 * TOP_K + k + 1] * _load_tile_rows(yk, tm)
    o_ref[...] = x1_ref[...] + ga2[0] * _rms(y, gpost[...])


def _combine(y4, route, x1, ga2, g_post2, S):
    T, D = x1.shape
    tm = min(MOE_ROWS, S)
    tiles_per_seq = S // tm
    nt = T // tm
    y4 = y4.reshape(-1, LANES)
    y_spec = lambda k: pl.BlockSpec((tm * ROW_TILES, LANES), lambda i: (k * nt + i, 0))
    return pl.pallas_call(
        _combine_kernel,
        out_shape=jax.ShapeDtypeStruct((T, D), f32),
        grid=(nt,),
        in_specs=[y_spec(k) for k in range(TOP_K)]
        + [pl.BlockSpec((tm, LANES), lambda i: (i, 0)),
           pl.BlockSpec((tm, D), lambda i: (i, 0)),
           pl.BlockSpec((1, 1, D), lambda i: (i // tiles_per_seq, 0, 0)),
           pl.BlockSpec((1, D), lambda i: (0, 0))],
        out_specs=pl.BlockSpec((tm, D), lambda i: (i, 0)),
        compiler_params=_cparams("parallel"),
        name="combine",
    )(y4, y4, y4, y4, route, x1, ga2, g_post2.reshape(1, D))


def kernel(x, c, w_ada, b_ada, g_mix_pre, g_mix_post, w_in, b_forget, pe_k, pe_v, w_cmp_k, w_cmp_v, w_fox_proj, w_nsa_proj, w_mix_out, rel_bias, g_ffn_pre, g_ffn_post, w_router, b_router, w_gate, b_gate, w_up, b_up, w_down, b_down):
    B, S, D = x.shape
    T = B * S
    for l in range(w_ada.shape[0]):
        x2 = x.reshape(T, D)
        ada = _ada(c, w_ada[l], b_ada[l])
        sh1, sc1, ga1, sh2, sc2, ga2 = [a.reshape(B, 1, D) for a in jnp.split(ada, 6, axis=-1)]
        fq, fk, fv, nq, cm, ksl, vsl, kwn, vwn, mg, sm = _inproj(x2, sc1, sh1, g_mix_pre[l], w_in[l], b_forget[l], S)
        o_fox = _fox(fq, fk, fv, B, S)
        kc, vc = _compress(cm, pe_k[l], pe_v[l], w_cmp_k[l], w_cmp_v[l], B, S)
        bias_slc, bias_win, pat_cmp = _biasgen(rel_bias, S)
        o_cmp, selb = _cmpsel(nq, kc, vc, pat_cmp, B, S)
        o_slc = _nsa_flash("slc", nq, selb, ksl, vsl, bias_slc, B, S)
        o_win = _nsa_flash("win", nq, None, kwn, vwn, bias_win, B, S)
        x1, h2, route, cnt = _post(x2, o_fox.reshape(T, -1), o_cmp.reshape(T, -1), o_slc.reshape(T, -1),
                                   o_win.reshape(T, -1), mg, sm, ga1, sc2, sh2, g_mix_post[l], g_ffn_pre[l],
                                   w_fox_proj[l], w_nsa_proj[l], w_mix_out[l], w_router[l], b_router[l], S)
        counts = cnt[0, :N_EXPERTS].astype(i32)
        nblk = (counts + MOE_BM - 1) // MOE_BM
        blk_end = jnp.cumsum(nblk)
        pad_start = (blk_end - nblk) * MOE_BM
        top_i = route[:, :TOP_K].astype(i32)
        A = T * TOP_K
        nbt = -(-A // MOE_BM) + N_EXPERTS + 1
        n_used = blk_end[-1:].astype(i32)
        blk_e = jnp.minimum(jnp.sum(jnp.arange(nbt)[:, None] >= blk_end[None, :], axis=1), N_EXPERTS - 1).astype(i32)
        a_sorted = jnp.sort((top_i * A + jnp.arange(A, dtype=i32).reshape(T, TOP_K)).reshape(-1)) % A
        grp_start = jnp.cumsum(counts) - counts
        j = jnp.arange(MOE_BM, dtype=i32)[None, :]
        b = jnp.arange(nbt, dtype=i32)[:, None]
        r_in_e = b * MOE_BM + j - pad_start[blk_e][:, None]
        valid = (b < n_used[0]) & (r_in_e < counts[blk_e][:, None])
        row_a = a_sorted[jnp.clip(grp_start[blk_e][:, None] + r_in_e, 0, A - 1)]
        row_tok = jnp.where(valid, row_a // TOP_K, 0)
        row_dst = jnp.where(valid, (row_a % TOP_K) * T + row_a // TOP_K, A + (b % 2) * MOE_BM + j)
        y4 = _moe(blk_e, n_used, row_tok.reshape(nbt, 1, MOE_BM), row_dst.reshape(nbt, 1, MOE_BM), h2,
                  w_gate[l], b_gate[l], w_up[l], b_up[l], w_down[l], b_down[l], A + 2 * MOE_BM)
        x = _combine(y4, route, x1, ga2, g_ffn_post[l], S).reshape(B, S, D)
    return x
```

```python
import functools

import numpy as np
import jax
import jax.numpy as jnp
from jax import lax
from jax.experimental import pallas as pl
from jax.experimental.pallas import tpu as pltpu

f32 = jnp.float32
bf16 = jnp.bfloat16
i32 = jnp.int32

D_MODEL = 1024
HEAD_DIM = 64
FOX_HEADS = 8
NSA_HEADS = 8
NSA_KV_GROUPS = 2
NSA_HPG = NSA_HEADS // NSA_KV_GROUPS
FOX_WIDTH = FOX_HEADS * HEAD_DIM
NSA_WIDTH = NSA_HEADS * HEAD_DIM
NSA_KV_WIDTH = NSA_KV_GROUPS * HEAD_DIM
CMP_BLOCK = 32
CMP_STRIDE = 16
SEL_BLOCK = 64
SEL_TOPN = 16
WINDOW = 512
REL_BUCKETS = 32
REL_MAX_DIST = 128
N_EXPERTS = 32
TOP_K = 4
SWIGLU_LIMIT = 7.0
SWIGLU_ALPHA = 1.702
RMS_EPS = 1e-6
NEG = -1e30
BIG = 1e9
IN_SIZES = (FOX_WIDTH, FOX_WIDTH, FOX_WIDTH, FOX_HEADS, NSA_WIDTH,
            NSA_KV_WIDTH, NSA_KV_WIDTH, NSA_KV_WIDTH, NSA_KV_WIDTH, NSA_KV_WIDTH, NSA_KV_WIDTH,
            3 * NSA_HEADS, D_MODEL, D_MODEL)

LANES = 128
VMEM_LIMIT = 56 * 1024 * 1024
FOX_TILE = 512
FOX_STEP_HEADS = 8
SLC_TILE = 512
WIN_TILE = 512
CMP_TILE = 512
ROW_TILE = 512
MOE_BM = 512
MOE_ROWS = 256
SEL_MASK = 1e9

_NT = (((1,), (1,)), ((), ()))


def _cparams(*sem):
    return pltpu.CompilerParams(dimension_semantics=sem, vmem_limit_bytes=VMEM_LIMIT)


def _const_spec(shape):
    nd = len(shape)
    return pl.BlockSpec(shape, lambda *_: (0,) * nd, pipeline_mode=pl.Buffered(1))


def _split3(a):
    a1 = a.astype(bf16)
    r1 = a - a1.astype(f32)
    a2 = r1.astype(bf16)
    a3 = (r1 - a2.astype(f32)).astype(bf16)
    return a1, a2, a3


ROW_TILES = D_MODEL // LANES


def _store_tile_rows(ref, val):
    n = val.shape[0]
    for c in range(ROW_TILES):
        ref[pl.ds(c, n, stride=ROW_TILES), :] = val[:, c * LANES:(c + 1) * LANES]


def _load_tile_rows(ref, n):
    return jnp.concatenate([ref[pl.ds(c, n, stride=ROW_TILES), :] for c in range(ROW_TILES)], axis=1)


def _loop_two_tiles(n, body, carry):
    carry = lax.fori_loop(0, n // 2, lambda jj, c: body(2 * jj + 1, body(2 * jj, c)), carry)
    return lax.cond(n % 2 == 1, lambda c: body(n - 1, c), lambda c: c, carry)


def _rms(x, g):
    ms = jnp.mean(x * x, axis=-1, keepdims=True)
    return x * lax.rsqrt(ms + RMS_EPS) * g


def _ada_kernel(c_ref, w_ref, b_ref, o_ref):
    c = c_ref[...]
    s = c * jax.nn.sigmoid(c)
    w = w_ref[...]
    s1, s2, _ = _split3(s)
    w1, w2, _ = _split3(w)
    acc = jnp.dot(s1, w1, preferred_element_type=f32)
    acc += jnp.dot(s1, w2, preferred_element_type=f32)
    acc += jnp.dot(s2, w1, preferred_element_type=f32)
    o_ref[...] = acc + b_ref[...]


def _ada(c, w_ada, b_ada):
    B, D = c.shape
    N = w_ada.shape[1]
    tn = 1024
    return pl.pallas_call(
        _ada_kernel,
        out_shape=jax.ShapeDtypeStruct((B, N), f32),
        grid=(N // tn,),
        in_specs=[pl.BlockSpec((B, D), lambda j: (0, 0)),
                  pl.BlockSpec((D, tn), lambda j: (0, j)),
                  pl.BlockSpec((1, tn), lambda j: (0, j))],
        out_specs=pl.BlockSpec((B, tn), lambda j: (0, j)),
        compiler_params=_cparams("arbitrary"),
        name="ada",
    )(c, w_ada, b_ada.reshape(1, N))


def _inproj_kernel(x_ref, sc_ref, sh_ref, g_ref, bfg_ref, tri_ref, esel_ref,
                   wfq, wfk, wfv, wnq, wcm, wkv, wmg, wsm,
                   bq, bv, bvs,
                   ofq, ofk, ofv, onq, ocm, oksl, ovsl, okwn, ovwn, omg, osm,
                   carry_ref, *, tm, tiles_per_seq):
    i = pl.program_id(0)
    x = x_ref[...]
    h = _rms(x, g_ref[...]) * (1.0 + sc_ref[0]) + sh_ref[0]
    hb = h.astype(bf16)

    def proj(w):
        return jnp.dot(hb, w[...], preferred_element_type=f32)

    low_half = lax.broadcasted_iota(i32, (tm, LANES), 1) < HEAD_DIM

    def place(tile, src_hi, dst_hi):
        t = tile if src_hi == dst_hi else pltpu.roll(tile, HEAD_DIM, 1)
        return jnp.where(low_half, 0.0, t) if dst_hi else jnp.where(low_half, t, 0.0)

    def tiles(c):
        return [c[:, j * LANES:(j + 1) * LANES] for j in range(c.shape[1] // LANES)]

    def heads(c, even_hi, odd_hi):
        out = []
        for t in tiles(c):
            out += [place(t, False, even_hi), place(t, True, odd_hi)]
        return jnp.concatenate(out, axis=1)

    def group_both(t):
        return jnp.concatenate([place(t, False, False), place(t, False, True),
                                place(t, True, False), place(t, True, True)], axis=1)

    ofq[...] = (heads(proj(wfq), False, False) + bq[...]).astype(bf16)
    ofv[...] = (heads(proj(wfv), False, True) + bv[...]).astype(bf16)
    onq[...] = heads(proj(wnq), True, True).astype(bf16)
    ocm[...] = proj(wcm).astype(bf16)
    ksl_c, vsl_c, kwn_c, vwn_c = tiles(proj(wkv))
    ovsl[...] = (group_both(vsl_c) + bvs[...]).astype(bf16)
    okwn[...] = heads(kwn_c, True, True).astype(bf16)
    ovwn[...] = (group_both(vwn_c) + bvs[...]).astype(bf16)
    omg[...] = jax.nn.sigmoid(proj(wmg)).astype(bf16)

    row = lax.broadcasted_iota(i32, (tm, 2 * LANES), 0)
    lane = lax.broadcasted_iota(i32, (tm, 2 * LANES), 1)
    blk = ((i % tiles_per_seq) * tm + row) // SEL_BLOCK
    onehot = jnp.where((lane & (LANES - 1)) == blk, 1.0, 0.0)
    oksl[...] = (heads(ksl_c, True, True) + onehot).astype(bf16)

    sm = proj(wsm)
    osm[...] = sm
    z = sm + bfg_ref[...]
    lane1 = lax.broadcasted_iota(i32, (tm, LANES), 1)
    logf = jnp.where(lane1 < FOX_HEADS, jnp.minimum(z, 0.0) - jnp.log(1.0 + jnp.exp(-jnp.abs(z))), 0.0)

    @pl.when(i % tiles_per_seq == 0)
    def _():
        carry_ref[...] = jnp.zeros_like(carry_ref)

    tri = tri_ref[...]
    cum = carry_ref[0:1, :]
    for piece in _split3(logf):
        cum = cum + jnp.dot(tri, piece, preferred_element_type=f32)
    carry_ref[0:1, :] = cum[tm - 1:tm, :]
    ncat = jnp.concatenate(_split3(-cum), axis=1)
    ofk[...] = (heads(proj(wfk), False, False)
                + jnp.dot(ncat, esel_ref[...], preferred_element_type=f32)).astype(bf16)


def _inproj(x2, sc1, sh1, g_pre, w_in, b_forget, S):
    T, D = x2.shape
    tm = min(ROW_TILE, S)
    tiles_per_seq = S // tm
    offs = np.cumsum(IN_SIZES)[:-1].tolist()
    (wfq, wfk, wfv, wff, wnq, wkcm, wvcm, wksl, wvsl, wkwn, wvwn, wng, wmgf, wmgn) = jnp.split(w_in, offs, axis=1)
    scale = HEAD_DIM ** -0.5
    cast = lambda w: w.astype(bf16)
    weights = [
        cast(wfq * scale),
        cast(wfk),
        cast(wfv),
        cast(wnq * scale),
        cast(jnp.concatenate([wkcm, wvcm], axis=1)),
        cast(jnp.concatenate([wksl, wvsl, wkwn, wvwn], axis=1)),
        cast(jnp.concatenate([wmgf, wmgn], axis=1)),
        cast(jnp.concatenate([wff, wng, jnp.zeros((D, LANES - FOX_HEADS - 3 * NSA_HEADS), f32)], axis=1)),
    ]
    G2 = NSA_KV_GROUPS * LANES
    widths = [FOX_HEADS * LANES] * 3 + [NSA_HEADS * LANES, 2 * NSA_KV_WIDTH, G2, 2 * G2, G2, 2 * G2, 2 * D, LANES]
    bq = np.zeros((1, FOX_HEADS * LANES), np.float32)
    bv = np.zeros((1, FOX_HEADS * LANES), np.float32)
    for h in range(FOX_HEADS):
        bq[0, h * LANES + HEAD_DIM:h * LANES + HEAD_DIM + 3] = 1.0
        bv[0, h * LANES + (HEAD_DIM if h % 2 == 0 else 0)] = 1.0
    bvs = np.zeros((1, NSA_KV_GROUPS * 2 * LANES), np.float32)
    for g in range(NSA_KV_GROUPS):
        bvs[0, g * 2 * LANES + HEAD_DIM] = 1.0
        bvs[0, g * 2 * LANES + LANES] = 1.0
    esel = np.zeros((3 * LANES, FOX_HEADS * LANES), np.float32)
    for j in range(3):
        for h in range(FOX_HEADS):
            esel[j * LANES + h, h * LANES + HEAD_DIM + j] = 1.0
    tri = np.tril(np.ones((tm, tm), np.float32))
    bfg = jnp.concatenate([b_forget, jnp.zeros((LANES - FOX_HEADS,), f32)]).reshape(1, LANES)

    out_dtypes = [bf16] * 10 + [f32]
    row_spec = lambda n: pl.BlockSpec((tm, n), lambda i: (i, 0))
    mod_spec = pl.BlockSpec((1, 1, D), lambda i: (i // tiles_per_seq, 0, 0))
    consts = [jnp.asarray(tri, bf16), jnp.asarray(esel, bf16)]
    biases = [jnp.asarray(bq), jnp.asarray(bv), jnp.asarray(bvs)]
    outs = pl.pallas_call(
        functools.partial(_inproj_kernel, tm=tm, tiles_per_seq=tiles_per_seq),
        out_shape=[jax.ShapeDtypeStruct((T, n), dt) for n, dt in zip(widths, out_dtypes)],
        grid=(T // tm,),
        in_specs=[row_spec(D), mod_spec, mod_spec, _const_spec((1, D)), _const_spec((1, LANES))]
        + [_const_spec(c.shape) for c in consts]
        + [_const_spec(w.shape) for w in weights]
        + [_const_spec(b.shape) for b in biases],
        out_specs=[row_spec(n) for n in widths],
        scratch_shapes=[pltpu.VMEM((8, LANES), f32)],
        compiler_params=_cparams("arbitrary"),
        name="inproj",
    )(x2, sc1, sh1, g_pre.reshape(1, D), bfg, *consts, *weights, *biases)
    return outs


def _fox_kernel(q_ref, k_ref, v_ref, o_ref, *, tq, nh):
    i = pl.program_id(2)
    row = lax.broadcasted_iota(i32, (tq, tq), 0)
    col = lax.broadcasted_iota(i32, (tq, tq), 1)
    qs = [q_ref[0, :, hh * LANES:(hh + 1) * LANES] for hh in range(nh)]

    def tile(j, carry, diag):
        start = pl.multiple_of(j * tq, tq)
        new = []
        for hh in range(nh):
            m, acc = carry[hh]
            k = k_ref[0, pl.ds(start, tq), hh * LANES:(hh + 1) * LANES]
            v = v_ref[0, pl.ds(start, tq), hh * LANES:(hh + 1) * LANES]
            s = lax.dot_general(qs[hh], k, _NT, preferred_element_type=f32)
            if diag:
                s = jnp.where(col <= row, s, NEG)
            m_new = jnp.maximum(m, jnp.max(s, axis=-1, keepdims=True))
            p = jnp.exp(s - m_new).astype(bf16)
            acc = jnp.exp(m - m_new) * acc + jnp.dot(p, v, preferred_element_type=f32)
            new.append((m_new, acc))
        return tuple(new)

    carry = tuple((jnp.full((tq, 1), NEG, f32), jnp.zeros((tq, LANES), f32)) for _ in range(nh))
    carry = tile(i, carry, True)
    carry = _loop_two_tiles(i, lambda j, c: tile(j, c, False), carry)
    o_ref[0] = jnp.concatenate([_pair_out(carry[2 * pr][1], carry[2 * pr + 1][1]) for pr in range(nh // 2)],
                               axis=1).astype(bf16)


def _fox(fq, fk, fv, B, S):
    tq = min(FOX_TILE, S)
    nq = S // tq
    nh = FOX_STEP_HEADS
    q3 = fq.reshape(B, S, FOX_HEADS * LANES)
    k3 = fk.reshape(B, S, FOX_HEADS * LANES)
    v3 = fv.reshape(B, S, FOX_HEADS * LANES)
    return pl.pallas_call(
        functools.partial(_fox_kernel, tq=tq, nh=nh),
        out_shape=jax.ShapeDtypeStruct((B, S, FOX_WIDTH), bf16),
        grid=(B, FOX_HEADS // nh, nq),
        in_specs=[pl.BlockSpec((1, tq, nh * LANES), lambda b, hp, i: (b, i, hp)),
                  pl.BlockSpec((1, S, nh * LANES), lambda b, hp, i: (b, 0, hp)),
                  pl.BlockSpec((1, S, nh * LANES), lambda b, hp, i: (b, 0, hp))],
        out_specs=pl.BlockSpec((1, tq, nh // 2 * LANES), lambda b, hp, i: (b, i, hp)),
        compiler_params=_cparams("parallel", "parallel", "arbitrary"),
        name="fox",
    )(q3, k3, v3)


def _compress_kernel(x_ref, pea_ref, peb_ref, wa_ref, wb_ref, okc, ovc, *, nc):
    x = x_ref[0].astype(f32)
    xa = (x + pea_ref[...]).astype(bf16)
    xb = (x + peb_ref[...]).astype(bf16)
    a = jnp.dot(xa, wa_ref[...], preferred_element_type=f32)
    b = jnp.dot(xb, wb_ref[...], preferred_element_type=f32)
    out = a + pltpu.roll(b, nc - 1, 0)
    okc[0] = out[:, :2 * LANES].astype(bf16)
    ovc[0] = out[:, 2 * LANES:].astype(bf16)


def _compress(cm, pe_k, pe_v, w_cmp_k, w_cmp_v, B, S):
    nc = S // CMP_STRIDE
    half = CMP_BLOCK // 2
    win = half * 2 * LANES
    x = cm.reshape(B, nc, win)
    wk = w_cmp_k.reshape(CMP_BLOCK, HEAD_DIM, HEAD_DIM)
    wv = w_cmp_v.reshape(CMP_BLOCK, HEAD_DIM, HEAD_DIM)
    H = HEAD_DIM

    def build(wk_h, wv_h):
        z = jnp.zeros((half, H, H), f32)

        def rows(cols):
            return jnp.concatenate([cols.get(c, z) for c in range(12)], axis=2)

        w = jnp.concatenate([rows({1: wk_h}),
                             rows({3: wk_h}),
                             rows({4: wv_h, 7: wv_h}),
                             rows({8: wv_h, 11: wv_h})],
                            axis=1)
        return w.reshape(win, 6 * LANES).astype(bf16)

    wa = build(wk[:half], wv[:half])
    wb = build(wk[half:], wv[half:])

    def pe_row(pk, pv):
        return jnp.concatenate([pk, pk, pv, pv], axis=1).reshape(1, win)

    pea = pe_row(pe_k[:half], pe_v[:half])
    peb = pe_row(pe_k[half:], pe_v[half:])
    return pl.pallas_call(
        functools.partial(_compress_kernel, nc=nc),
        out_shape=[jax.ShapeDtypeStruct((B, nc, 2 * LANES), bf16),
                   jax.ShapeDtypeStruct((B, nc, 4 * LANES), bf16)],
        grid=(B,),
        in_specs=[pl.BlockSpec((1, nc, win), lambda b: (b, 0, 0)),
                  _const_spec((1, win)), _const_spec((1, win)),
                  _const_spec((win, 6 * LANES)), _const_spec((win, 6 * LANES))],
        out_specs=[pl.BlockSpec((1, nc, 2 * LANES), lambda b: (b, 0, 0)),
                   pl.BlockSpec((1, nc, 4 * LANES), lambda b: (b, 0, 0))],
        compiler_params=_cparams("parallel"),
        name="compress",
    )(x, pea, peb, wa, wb)


def _cmpsel_kernel(q_ref, kc_ref, vc_ref, pat_ref, ov_ref, ocmp, osel, score_scr, *, tq, nc, n_sel, top_n, past):
    i = pl.program_id(1)
    kc = kc_ref[0]
    c0 = i * (tq // CMP_STRIDE) - past
    wio = lax.broadcasted_iota(i32, (LANES, nc), 0)
    cio = lax.broadcasted_iota(i32, (LANES, nc), 1)
    shift = jnp.where(cio == wio + c0, 1.0, 0.0).astype(bf16)
    t = i * tq + lax.broadcasted_iota(i32, (tq, nc), 0)
    cend = lax.broadcasted_iota(i32, (tq, nc), 1) * CMP_STRIDE + (CMP_BLOCK - 1)
    valid = cend <= t
    pcs = jnp.zeros((tq, nc), f32)
    outs = []
    for hh in range(NSA_HPG):
        q = q_ref[0, :, hh * LANES:(hh + 1) * LANES]
        pat = pat_ref[hh]
        pat_hi = pat.astype(bf16)
        pat_lo = (pat - pat_hi.astype(f32)).astype(bf16)
        cb = (jnp.dot(pat_hi, shift, preferred_element_type=f32)
              + jnp.dot(pat_lo, shift, preferred_element_type=f32))
        lc = jnp.where(valid, lax.dot_general(q, kc, _NT, preferred_element_type=f32) + cb, NEG)
        m = jnp.max(lc, axis=-1, keepdims=True)
        p = jnp.where(valid, jnp.exp(lc - m), 0.0)
        l = jnp.sum(p, axis=-1, keepdims=True)
        pc = p * jnp.where(l > 0.0, 1.0 / l, 0.0)
        pcs = pcs + pc
        v = vc_ref[0, :, (hh % 2) * LANES:(hh % 2 + 1) * LANES]
        outs.append(jnp.dot(pc.astype(bf16), v, preferred_element_type=f32))
    ocmp[0] = jnp.concatenate([outs[0] + outs[1], outs[2] + outs[3]], axis=1).astype(bf16)

    hi = pcs.astype(bf16)
    lo = (pcs - hi.astype(f32)).astype(bf16)
    ov = ov_ref[...]
    imp = (lax.dot_general(ov, hi, _NT, preferred_element_type=f32)
           + lax.dot_general(ov, lo, _NT, preferred_element_type=f32))
    jio = lax.broadcasted_iota(i32, (n_sel, tq), 0)
    t = i * tq + lax.broadcasted_iota(i32, (n_sel, tq), 1)
    cur = t // SEL_BLOCK
    forced = (jio == 0) | (jio == cur) | (jio == cur - 1)
    score = jnp.where(forced, BIG, jnp.where(jio <= cur, imp, -BIG))
    score_scr[...] = score
    blocks_per_tile = tq // SEL_BLOCK

    def visit(g, rank):
        for u in range(blocks_per_tile):
            jp = g * blocks_per_tile + u
            r = score_scr[pl.ds(jp, 1), :]
            tie = jnp.where(jio > jp, 1.0, 0.0)
            rank = rank + jnp.where(r > score, 1.0, jnp.where(r == score, tie, 0.0))
        return rank

    rank = lax.fori_loop(0, i + 1, visit, jnp.zeros((n_sel, tq), f32))
    selb = jnp.where(rank < top_n, 0.0, -SEL_MASK)
    padded = jnp.concatenate([selb, jnp.zeros((LANES - n_sel, tq), f32)], axis=0)
    osel[0, 0] = padded.T.astype(bf16)


def _bucket_bounds():
    n = np.arange(0, 4 * REL_MAX_DIST)
    max_exact = REL_BUCKETS // 2
    nf = np.maximum(n, 1).astype(np.float32)
    large = max_exact + (np.log(nf / np.float32(max_exact)) / np.float32(np.log(REL_MAX_DIST / max_exact))
                         * np.float32(REL_BUCKETS - max_exact)).astype(np.int32)
    bucket = np.where(n < max_exact, n, np.minimum(large, REL_BUCKETS - 1))
    return [int(n[bucket > b].min()) for b in range(REL_BUCKETS - 1)]


_BOUNDS = _bucket_bounds()
_CMP_PAST = (_BOUNDS[-1] + CMP_BLOCK - 1 + CMP_STRIDE - 1) // CMP_STRIDE - 1


def _rel_bias_of(d, rb_ref, h):
    far = rb_ref[(REL_BUCKETS - 1) * NSA_HEADS + h]
    v = jnp.zeros(d.shape, f32)
    for b in reversed(range(REL_BUCKETS - 1)):
        v = jnp.where(d < _BOUNDS[b], rb_ref[b * NSA_HEADS + h] - far, v)
    return v


def _biasgen_kernel(rb_ref, oslc, owin, ocmp, *, ts, tw, tc, nk):
    h = pl.program_id(0)

    def tile(t, off, window):
        lo, hi = off - (t - 1), off + (t - 1)
        if hi < 0 or (window is not None and lo >= window):
            return jnp.full((t, t), NEG, f32)
        d = lax.broadcasted_iota(i32, (t, t), 0) - lax.broadcasted_iota(i32, (t, t), 1) + off
        val = _rel_bias_of(d, rb_ref, h) if lo < _BOUNDS[-1] else jnp.zeros((t, t), f32)
        if lo < 0:
            val = jnp.where(d >= 0, val, NEG)
        if window is not None and hi >= window:
            val = jnp.where(d < window, val, NEG)
        return val

    oslc[0, 0, 0] = tile(ts, 0, None)
    oslc[0, 1, 0] = tile(ts, ts, None)
    oslc[0, 2, 0] = jnp.full((ts, ts), NEG, f32)
    for v in range(nk):
        for cc in range(nk):
            owin[0, v, 0, :, cc * tw:(cc + 1) * tw] = tile(tw, (v - cc) * tw, WINDOW)
    rr = lax.broadcasted_iota(i32, (tc, LANES), 0)
    w = lax.broadcasted_iota(i32, (tc, LANES), 1)
    d = rr - CMP_STRIDE * (w - _CMP_PAST) - (CMP_BLOCK - 1)
    ocmp[0] = jnp.where(d >= 0, _rel_bias_of(d, rb_ref, h), 0.0)


def _biasgen(rel_bias, S):
    ts, tw, tc = min(SLC_TILE, S), min(WIN_TILE, S), min(CMP_TILE, S)
    nk = WINDOW // tw + 1
    assert WINDOW % tw == 0 and S >= nk * tw and min(ts, tw) + 1 >= _BOUNDS[-1]
    assert tc // CMP_STRIDE + _CMP_PAST <= LANES
    G = NSA_KV_GROUPS
    return pl.pallas_call(
        functools.partial(_biasgen_kernel, ts=ts, tw=tw, tc=tc, nk=nk),
        out_shape=[jax.ShapeDtypeStruct((G, 3, NSA_HPG, ts, ts), f32),
                   jax.ShapeDtypeStruct((G, nk, NSA_HPG, tw, nk * tw), f32),
                   jax.ShapeDtypeStruct((NSA_HEADS, tc, LANES), f32)],
        grid=(NSA_HEADS,),
        in_specs=[pl.BlockSpec(memory_space=pltpu.SMEM)],
        out_specs=[pl.BlockSpec((1, 3, 1, ts, ts), lambda h: (h // NSA_HPG, 0, h % NSA_HPG, 0, 0)),
                   pl.BlockSpec((1, nk, 1, tw, nk * tw), lambda h: (h // NSA_HPG, 0, h % NSA_HPG, 0, 0)),
                   pl.BlockSpec((1, tc, LANES), lambda h: (h, 0, 0))],
        compiler_params=_cparams("arbitrary"),
        name="biasgen",
    )(rel_bias.reshape(-1))


def _cmpsel(nq_arr, kc, vc, pat, B, S):
    tq = min(CMP_TILE, S)
    nqt = S // tq
    nc = S // CMP_STRIDE
    n_sel = S // SEL_BLOCK
    top_n = min(SEL_TOPN, n_sel)
    G = NSA_KV_GROUPS
    assert n_sel <= HEAD_DIM
    c = np.arange(nc)[None, :]
    j = np.arange(n_sel)[:, None]
    ov = ((c * CMP_STRIDE < j * SEL_BLOCK + SEL_BLOCK) & (c * CMP_STRIDE + CMP_BLOCK > j * SEL_BLOCK)
          & (c < nc - 1)).astype(np.float32)
    q3 = nq_arr.reshape(B, S, NSA_HEADS * LANES)
    return pl.pallas_call(
        functools.partial(_cmpsel_kernel, tq=tq, nc=nc, n_sel=n_sel, top_n=top_n, past=_CMP_PAST),
        out_shape=[jax.ShapeDtypeStruct((B, S, NSA_HEADS * HEAD_DIM), bf16),
                   jax.ShapeDtypeStruct((B, G, S, LANES), bf16)],
        grid=(G, nqt, B),
        in_specs=[pl.BlockSpec((1, tq, NSA_HPG * LANES), lambda g, i, b: (b, i, g)),
                  pl.BlockSpec((1, nc, LANES), lambda g, i, b: (b, 0, g)),
                  pl.BlockSpec((1, nc, 2 * LANES), lambda g, i, b: (b, 0, g)),
                  pl.BlockSpec((NSA_HPG, tq, LANES), lambda g, i, b: (g, 0, 0)),
                  _const_spec((n_sel, nc))],
        out_specs=[pl.BlockSpec((1, tq, 2 * LANES), lambda g, i, b: (b, i, g)),
                   pl.BlockSpec((1, 1, tq, LANES), lambda g, i, b: (b, g, i, 0))],
        scratch_shapes=[pltpu.VMEM((n_sel, tq), f32)],
        compiler_params=_cparams("parallel", "arbitrary", "arbitrary"),
        name="cmpsel",
    )(q3, kc, vc, pat, jnp.asarray(ov, bf16))


def _pair_out(acc_e, acc_o):
    lane = lax.broadcasted_iota(i32, acc_e.shape, 1)
    return jnp.where(lane < HEAD_DIM, acc_e / acc_e[:, HEAD_DIM:HEAD_DIM + 1], acc_o / acc_o[:, 0:1])


def _slc_kernel(q_ref, sb_ref, k_ref, v_ref, bias_ref, o_ref, *, tq):
    i = pl.program_id(2)
    sb = sb_ref[0, 0]
    qs = [q_ref[0, :, h * LANES:(h + 1) * LANES] + sb for h in range(NSA_HPG)]

    def tile(j, carry, bias_idx):
        start = pl.multiple_of(j * tq, tq)
        k = k_ref[0, pl.ds(start, tq), :]
        v = v_ref[0, pl.ds(start, tq), :]
        new = []
        for h in range(NSA_HPG):
            m, acc = carry[h]
            s = lax.dot_general(qs[h], k, _NT, preferred_element_type=f32)
            if bias_idx is not None:
                s = s + bias_ref[0, bias_idx, h]
            m_new = jnp.maximum(m, jnp.max(s, axis=-1, keepdims=True))
            p = jnp.exp(s - m_new).astype(bf16)
            vh = v[:, (h % 2) * LANES:(h % 2 + 1) * LANES]
            acc = jnp.exp(m - m_new) * acc + jnp.dot(p, vh, preferred_element_type=f32)
            new.append((m_new, acc))
        return tuple(new)

    carry = tuple((jnp.full((tq, 1), NEG, f32), jnp.zeros((tq, LANES), f32)) for _ in range(NSA_HPG))
    carry = tile(i, carry, 0)
    carry = tile(jnp.maximum(i - 1, 0), carry, jnp.where(i >= 1, 1, 2))
    carry = _loop_two_tiles(jnp.maximum(i - 1, 0), lambda j, c: tile(j, c, None), carry)
    o_ref[0] = jnp.concatenate([_pair_out(carry[0][1], carry[1][1]),
                                _pair_out(carry[2][1], carry[3][1])], axis=1).astype(bf16)


def _win_kernel(q_ref, k_ref, v_ref, bias_ref, o_ref, *, tq, nk):
    i = pl.program_id(2)
    start = pl.multiple_of(jnp.maximum(i - (nk - 1), 0) * tq, tq)
    k = k_ref[0, pl.ds(start, nk * tq), :]
    v = v_ref[0, pl.ds(start, nk * tq), :]
    accs = []
    for h in range(NSA_HPG):
        q = q_ref[0, :, h * LANES:(h + 1) * LANES]
        s = lax.dot_general(q, k, _NT, preferred_element_type=f32) + bias_ref[0, 0, h]
        p = jnp.exp(s - jnp.max(s, axis=-1, keepdims=True)).astype(bf16)
        accs.append(jnp.dot(p, v[:, (h % 2) * LANES:(h % 2 + 1) * LANES], preferred_element_type=f32))
    o_ref[0] = jnp.concatenate([_pair_out(accs[0], accs[1]), _pair_out(accs[2], accs[3])], axis=1).astype(bf16)


def _nsa_flash(kind, nq_arr, selb, k_arr, v_arr, bias, B, S):
    tq = bias.shape[3]
    nqt = S // tq
    G = NSA_KV_GROUPS
    q3 = nq_arr.reshape(B, S, NSA_HEADS * LANES)
    k3 = k_arr.reshape(B, S, G * LANES)
    v3 = v_arr.reshape(B, S, G * 2 * LANES)
    q_spec = pl.BlockSpec((1, tq, NSA_HPG * LANES), lambda g, b, i: (b, i, g))
    k_spec = pl.BlockSpec((1, S, LANES), lambda g, b, i: (b, 0, g))
    v_spec = pl.BlockSpec((1, S, 2 * LANES), lambda g, b, i: (b, 0, g))
    if kind == "slc":
        kern = functools.partial(_slc_kernel, tq=tq)
        extra_specs = [pl.BlockSpec((1, 1, tq, LANES), lambda g, b, i: (b, g, i, 0))]
        extra = [selb]
        b_spec = pl.BlockSpec((1, 3, NSA_HPG, tq, tq), lambda g, b, i: (g, 0, 0, 0, 0),
                              pipeline_mode=pl.Buffered(1))
    else:
        nk = bias.shape[1]
        kern = functools.partial(_win_kernel, tq=tq, nk=nk)
        extra_specs, extra = [], []
        b_spec = pl.BlockSpec((1, 1, NSA_HPG, tq, nk * tq), lambda g, b, i: (g, jnp.minimum(i, nk - 1), 0, 0, 0))
    return pl.pallas_call(
        kern,
        out_shape=jax.ShapeDtypeStruct((B, S, NSA_HEADS * HEAD_DIM), bf16),
        grid=(G, B, nqt),
        in_specs=[q_spec] + extra_specs + [k_spec, v_spec, b_spec],
        out_specs=pl.BlockSpec((1, tq, 2 * LANES), lambda g, b, i: (b, i, g)),
        compiler_params=_cparams("parallel", "parallel", "arbitrary"),
        name=kind,
    )(q3, *extra, k3, v3, bias)


def _post_kernel(x_ref, ofox, ocmp, oslc, owin, mg_ref, sm_ref, ga1, sc2, sh2, gpost, gpre,
                 wfp, wnp_, wmo, wr, br, eg,
                 x1_ref, h2_ref, route_ref, cnt_ref, carry_ref, *, tm):
    i = pl.program_id(0)
    W = NSA_WIDTH
    n = tm

    def slab(r0):
        rs = pl.ds(r0, n)
        gates = jax.nn.sigmoid(sm_ref[rs, :]).astype(bf16)
        gx = jnp.dot(gates, eg[...], preferred_element_type=f32)
        nsa = (gx[:, :W] * ocmp[rs, :].astype(f32) + gx[:, W:2 * W] * oslc[rs, :].astype(f32)
               + gx[:, 2 * W:] * owin[rs, :].astype(f32))
        y_nsa = jnp.dot(nsa.astype(bf16), wnp_[...], preferred_element_type=f32)
        y_fox = jnp.dot(ofox[rs, :], wfp[...], preferred_element_type=f32)
        mg = mg_ref[rs, :].astype(f32)
        mix = (mg[:, :D_MODEL] * y_fox + mg[:, D_MODEL:] * y_nsa).astype(bf16)
        mixed = jnp.dot(mix, wmo[...], preferred_element_type=f32)
        x1 = x_ref[rs, :] + ga1[0] * _rms(mixed, gpost[...])
        x1_ref[rs, :] = x1
        h2 = _rms(x1, gpre[...]) * (1.0 + sc2[0]) + sh2[0]
        _store_tile_rows(h2_ref.at[pl.ds(r0 * ROW_TILES, n * ROW_TILES), :], h2)

        lane = lax.broadcasted_iota(i32, (n, LANES), 1)
        logits = jnp.dot(h2.astype(bf16), wr[...], preferred_element_type=f32) + br[...]
        l = jnp.where(lane < N_EXPERTS, logits, NEG)
        vals, idxs = [], []
        for _ in range(TOP_K):
            m = jnp.max(l, axis=-1, keepdims=True)
            idx = jnp.min(jnp.where(l == m, lane, LANES), axis=-1, keepdims=True)
            vals.append(m)
            idxs.append(idx)
            l = jnp.where(lane == idx, NEG, l)
        es = [jnp.exp(v - vals[0]) for v in vals]
        den = es[0] + es[1] + es[2] + es[3]
        route = jnp.zeros((n, LANES), f32)
        for k in range(TOP_K):
            route = jnp.where(lane == k, idxs[k].astype(f32), route)
            route = jnp.where(lane == 2 * TOP_K + k, es[k] / den, route)
        route_ref[rs, :] = route
        cnt = sum(jnp.where(lane == idx, 1.0, 0.0) for idx in idxs)
        return jnp.sum(cnt, axis=0, keepdims=True)

    @pl.when(i == 0)
    def _():
        carry_ref[...] = jnp.zeros_like(carry_ref)

    new_carry = carry_ref[0:1, :] + slab(0)
    carry_ref[0:1, :] = new_carry
    cnt_ref[...] = jnp.broadcast_to(new_carry, cnt_ref.shape)


def _post(x2, ofox, ocmp, oslc, owin, mg, sm, ga1, sc2, sh2, g_post, g_pre2,
          w_fox_proj, w_nsa_proj, w_mix_out, w_router, b_router, S):
    T, D = x2.shape
    tm = min(ROW_TILE, S)
    tiles_per_seq = S // tm
    W = NSA_WIDTH
    eg = np.zeros((LANES, 3 * W), np.float32)
    for h in range(NSA_HEADS):
        for k in range(3):
            eg[FOX_HEADS + 3 * h + k, k * W + h * HEAD_DIM:k * W + (h + 1) * HEAD_DIM] = 1.0
    wr = jnp.concatenate([w_router, jnp.zeros((D, LANES - N_EXPERTS), f32)], axis=1).astype(bf16)
    br = jnp.concatenate([b_router, jnp.zeros((LANES - N_EXPERTS,), f32)]).reshape(1, LANES)
    row = lambda n: pl.BlockSpec((tm, n), lambda i: (i, 0))
    mod = pl.BlockSpec((1, 1, D), lambda i: (i // tiles_per_seq, 0, 0))
    consts = [w_fox_proj.astype(bf16), w_nsa_proj.astype(bf16), w_mix_out.astype(bf16), wr, br,
              jnp.asarray(eg, bf16)]
    return pl.pallas_call(
        functools.partial(_post_kernel, tm=tm),
        out_shape=[jax.ShapeDtypeStruct((T, D), f32), jax.ShapeDtypeStruct((T * ROW_TILES, LANES), f32),
                   jax.ShapeDtypeStruct((T, LANES), f32), jax.ShapeDtypeStruct((8, LANES), f32)],
        grid=(T // tm,),
        in_specs=[row(D), row(FOX_WIDTH), row(W), row(W), row(W), row(2 * D), row(LANES),
                  mod, mod, mod, _const_spec((1, D)), _const_spec((1, D))]
        + [_const_spec(c.shape) for c in consts],
        out_specs=[row(D), pl.BlockSpec((tm * ROW_TILES, LANES), lambda i: (i, 0)), row(LANES),
                   pl.BlockSpec((8, LANES), lambda i: (0, 0))],
        scratch_shapes=[pltpu.VMEM((8, LANES), f32)],
        compiler_params=_cparams("arbitrary"),
        name="post",
    )(x2, ofox, ocmp, oslc, owin, mg, sm, ga1, sc2, sh2, g_post.reshape(1, D), g_pre2.reshape(1, D), *consts)


def _moe_kernel(be_ref, nu_ref, tok0_ref, tokn_ref, dstp_ref, h_hbm, wg, bg, wu, bu, wd, bd,
                y_hbm, xbuf, ybuf, wgb, wub, wdb, gsem, ssem, *, bm):
    i = pl.program_id(0)
    nu = nu_ref[0]
    rt = ROW_TILES

    def rows(r):
        return pl.ds(r * rt, rt) if isinstance(r, int) else pl.ds(pl.multiple_of(r * rt, rt), rt)

    def gather_copy(tok_ref, r, s):
        return pltpu.make_async_copy(h_hbm.at[tok_ref[0, 0, r]], xbuf.at[s, rows(r), :], gsem.at[s])

    def scatter_copy(r, s):
        return pltpu.make_async_copy(ybuf.at[s, rows(r), :], y_hbm.at[dstp_ref[0, 0, r]], ssem.at[s])

    def wait_gather(s):
        pltpu.make_async_copy(xbuf.at[s], xbuf.at[s], gsem.at[s]).wait()

    def wait_scatter(s):
        pltpu.make_async_copy(ybuf.at[s], ybuf.at[s], ssem.at[s]).wait()

    @pl.when(i == 0)
    def _():
        ybuf[1] = jnp.zeros((bm * rt, LANES), f32)
        n_real = y_hbm.shape[0] - 2 * bm

        def issue(r, c):
            gather_copy(tok0_ref, r, 0).start()
            for half in range(2):
                pltpu.make_async_copy(ybuf.at[1, rows(r), :], y_hbm.at[n_real + half * bm + r],
                                      ssem.at[1]).start()
            return c

        lax.fori_loop(0, bm, issue, 0)
        for half in range(2):
            wait_scatter(1)

    def step(slot):
        other = 1 - slot
        wait_gather(slot)

        @pl.when(i >= 1)
        def _():
            wait_scatter(slot)

        @pl.when(jnp.logical_or(i == 0, be_ref[i] != be_ref[jnp.maximum(i - 1, 0)]))
        def _():
            wgb[...] = wg[0].astype(bf16)
            wub[...] = wu[0].astype(bf16)
            wdb[...] = wd[0].astype(bf16)

        for r in range(bm):
            gather_copy(tokn_ref, r, other).start(priority=r % 2)
            scatter_copy(r, other).start(priority=(r + 1) % 2)
        x = _load_tile_rows(xbuf.at[slot], bm).astype(bf16)
        g = jnp.dot(x, wgb[...], preferred_element_type=f32) + bg[0]
        u = jnp.dot(x, wub[...], preferred_element_type=f32) + bu[0]
        gt = jnp.minimum(g, SWIGLU_LIMIT)
        up = jnp.clip(u, -SWIGLU_LIMIT, SWIGLU_LIMIT)
        a = (gt * jax.nn.sigmoid(SWIGLU_ALPHA * gt) * (up + 1.0)).astype(bf16)
        _store_tile_rows(ybuf.at[slot], jnp.dot(a, wdb[...], preferred_element_type=f32) + bd[0])

    def drain(slot):
        other = 1 - slot
        wait_gather(slot)
        wait_scatter(slot)

        def issue(r, c):
            scatter_copy(r, other).start()
            return c

        lax.fori_loop(0, bm, issue, 0)
        wait_scatter(other)

    for s in range(2):
        pl.when(jnp.logical_and(i < nu, i % 2 == s))(functools.partial(step, s))
        pl.when(jnp.logical_and(i == nu, i % 2 == s))(functools.partial(drain, s))


def _moe(blk_e, n_used, row_tok, row_dst, h2t, w_gate, b_gate, w_up, b_up, w_down, b_down, n_rows):
    D = D_MODEL
    E, _, F = w_gate.shape
    nbt = row_tok.shape[0]
    bm = MOE_BM
    wsel = lambda i, be, nu: (be[jnp.minimum(i, nu[0] - 1)], 0, 0)
    idx_spec = lambda f: pl.BlockSpec((1, 1, bm), lambda i, be, nu: (f(i), 0, 0), memory_space=pltpu.SMEM)
    return pl.pallas_call(
        functools.partial(_moe_kernel, bm=bm),
        out_shape=jax.ShapeDtypeStruct((n_rows, ROW_TILES, LANES), f32),
        grid_spec=pltpu.PrefetchScalarGridSpec(
            num_scalar_prefetch=2, grid=(nbt,),
            in_specs=[idx_spec(lambda i: 0),
                      idx_spec(lambda i: jnp.minimum(i + 1, nbt - 1)),
                      idx_spec(lambda i: jnp.maximum(i - 1, 0)),
                      pl.BlockSpec(memory_space=pl.ANY),
                      pl.BlockSpec((1, D, F), wsel), pl.BlockSpec((1, 1, F), wsel),
                      pl.BlockSpec((1, D, F), wsel), pl.BlockSpec((1, 1, F), wsel),
                      pl.BlockSpec((1, F, D), wsel), pl.BlockSpec((1, 1, D), wsel)],
            out_specs=pl.BlockSpec(memory_space=pl.ANY),
            scratch_shapes=[pltpu.VMEM((2, bm * ROW_TILES, LANES), f32), pltpu.VMEM((2, bm * ROW_TILES, LANES), f32),
                            pltpu.VMEM((D, F), bf16), pltpu.VMEM((D, F), bf16), pltpu.VMEM((F, D), bf16),
                            pltpu.SemaphoreType.DMA((2,)), pltpu.SemaphoreType.DMA((2,))]),
        compiler_params=_cparams("arbitrary"),
        name="moe",
    )(blk_e, n_used, row_tok, row_tok, row_dst, h2t.reshape(-1, ROW_TILES, LANES),
      w_gate, b_gate.reshape(E, 1, F), w_up, b_up.reshape(E, 1, F), w_down, b_down.reshape(E, 1, D))


def _combine_kernel(y0, y1, y2, y3, route_ref, x1_ref, ga2, gpost, o_ref):
    route = route_ref[...]
    tm = o_ref.shape[0]
    y = jnp.zeros(o_ref.shape, f32)
    for k, yk in enumerate((y0, y1, y2, y3)):
        y = y + route[:, 2 * TOP_K + k:2 * TOP_K + k + 1] * _load_tile_rows(yk, tm)
    o_ref[...] = x1_ref[...] + ga2[0] * _rms(y, gpost[...])


def _combine(y4, route, x1, ga2, g_post2, S):
    T, D = x1.shape
    tm = min(MOE_ROWS, S)
    tiles_per_seq = S // tm
    nt = T // tm
    y4 = y4.reshape(-1, LANES)
    y_spec = lambda k: pl.BlockSpec((tm * ROW_TILES, LANES), lambda i: (k * nt + i, 0))
    return pl.pallas_call(
        _combine_kernel,
        out_shape=jax.ShapeDtypeStruct((T, D), f32),
        grid=(nt,),
        in_specs=[y_spec(k) for k in range(TOP_K)]
        + [pl.BlockSpec((tm, LANES), lambda i: (i, 0)),
           pl.BlockSpec((tm, D), lambda i: (i, 0)),
           pl.BlockSpec((1, 1, D), lambda i: (i // tiles_per_seq, 0, 0)),
           pl.BlockSpec((1, D), lambda i: (0, 0))],
        out_specs=pl.BlockSpec((tm, D), lambda i: (i, 0)),
        compiler_params=_cparams("parallel"),
        name="combine",
    )(y4, y4, y4, y4, route, x1, ga2, g_post2.reshape(1, D))


def kernel(x, c, w_ada, b_ada, g_mix_pre, g_mix_post, w_in, b_forget, pe_k, pe_v, w_cmp_k, w_cmp_v, w_fox_proj, w_nsa_proj, w_mix_out, rel_bias, g_ffn_pre, g_ffn_post, w_router, b_router, w_gate, b_gate, w_up, b_up, w_down, b_down):
    B, S, D = x.shape
    T = B * S
    for l in range(w_ada.shape[0]):
        x2 = x.reshape(T, D)
        ada = _ada(c, w_ada[l], b_ada[l])
        sh1, sc1, ga1, sh2, sc2, ga2 = [a.reshape(B, 1, D) for a in jnp.split(ada, 6, axis=-1)]
        fq, fk, fv, nq, cm, ksl, vsl, kwn, vwn, mg, sm = _inproj(x2, sc1, sh1, g_mix_pre[l], w_in[l], b_forget[l], S)
        o_fox = _fox(fq, fk, fv, B, S)
        kc, vc = _compress(cm, pe_k[l], pe_v[l], w_cmp_k[l], w_cmp_v[l], B, S)
        bias_slc, bias_win, pat_cmp = _biasgen(rel_bias, S)
        o_cmp, selb = _cmpsel(nq, kc, vc, pat_cmp, B, S)
        o_slc = _nsa_flash("slc", nq, selb, ksl, vsl, bias_slc, B, S)
        o_win = _nsa_flash("win", nq, None, kwn, vwn, bias_win, B, S)
        x1, h2, route, cnt = _post(x2, o_fox.reshape(T, -1), o_cmp.reshape(T, -1), o_slc.reshape(T, -1),
                                   o_win.reshape(T, -1), mg, sm, ga1, sc2, sh2, g_mix_post[l], g_ffn_pre[l],
                                   w_fox_proj[l], w_nsa_proj[l], w_mix_out[l], w_router[l], b_router[l], S)
        counts = cnt[0, :N_EXPERTS].astype(i32)
        nblk = (counts + MOE_BM - 1) // MOE_BM
        blk_end = jnp.cumsum(nblk)
        pad_start = (blk_end - nblk) * MOE_BM
        top_i = route[:, :TOP_K].astype(i32)
        A = T * TOP_K
        nbt = -(-A // MOE_BM) + N_EXPERTS + 1
        n_used = blk_end[-1:].astype(i32)
        blk_e = jnp.minimum(jnp.sum(jnp.arange(nbt)[:, None] >= blk_end[None, :], axis=1), N_EXPERTS - 1).astype(i32)
        a_sorted = jnp.sort((top_i * A + jnp.arange(A, dtype=i32).reshape(T, TOP_K)).reshape(-1)) % A
        grp_start = jnp.cumsum(counts) - counts
        j = jnp.arange(MOE_BM, dtype=i32)[None, :]
        b = jnp.arange(nbt, dtype=i32)[:, None]
        r_in_e = b * MOE_BM + j - pad_start[blk_e][:, None]
        valid = (b < n_used[0]) & (r_in_e < counts[blk_e][:, None])
        row_a = a_sorted[jnp.clip(grp_start[blk_e][:, None] + r_in_e, 0, A - 1)]
        row_tok = jnp.where(valid, row_a // TOP_K, 0)
        row_dst = jnp.where(valid, (row_a % TOP_K) * T + row_a // TOP_K, A + (b % 2) * MOE_BM + j)
        y4 = _moe(blk_e, n_used, row_tok.reshape(nbt, 1, MOE_BM), row_dst.reshape(nbt, 1, MOE_BM), h2,
                  w_gate[l], b_gate[l], w_up[l], b_up[l], w_down[l], b_down[l], A + 2 * MOE_BM)
        x = _combine(y4, route, x1, ga2, g_ffn_post[l], S).reshape(B, S, D)
    return x
```

```python
import functools

import numpy as np
import jax
import jax.numpy as jnp
from jax import lax
from jax.experimental import pallas as pl
from jax.experimental.pallas import tpu as pltpu

f32 = jnp.float32
bf16 = jnp.bfloat16
i32 = jnp.int32

D_MODEL = 1024
HEAD_DIM = 64
FOX_HEADS = 8
NSA_HEADS = 8
NSA_KV_GROUPS = 2
NSA_HPG = NSA_HEADS // NSA_KV_GROUPS
FOX_WIDTH = FOX_HEADS * HEAD_DIM
NSA_WIDTH = NSA_HEADS * HEAD_DIM
NSA_KV_WIDTH = NSA_KV_GROUPS * HEAD_DIM
CMP_BLOCK = 32
CMP_STRIDE = 16
SEL_BLOCK = 64
SEL_TOPN = 16
WINDOW = 512
REL_BUCKETS = 32
REL_MAX_DIST = 128
N_EXPERTS = 32
TOP_K = 4
SWIGLU_LIMIT = 7.0
SWIGLU_ALPHA = 1.702
RMS_EPS = 1e-6
NEG = -1e30
BIG = 1e9
IN_SIZES = (FOX_WIDTH, FOX_WIDTH, FOX_WIDTH, FOX_HEADS, NSA_WIDTH,
            NSA_KV_WIDTH, NSA_KV_WIDTH, NSA_KV_WIDTH, NSA_KV_WIDTH, NSA_KV_WIDTH, NSA_KV_WIDTH,
            3 * NSA_HEADS, D_MODEL, D_MODEL)

LANES = 128
VMEM_LIMIT = 56 * 1024 * 1024
FOX_TILE = 512
FOX_STEP_HEADS = 8
SLC_TILE = 512
WIN_TILE = 512
CMP_TILE = 1024
ROW_TILE = 512
MOE_BM = 512
MOE_ROWS = 256
SEL_MASK = 1e9

_NT = (((1,), (1,)), ((), ()))


def _cparams(*sem):
    return pltpu.CompilerParams(dimension_semantics=sem, vmem_limit_bytes=VMEM_LIMIT)


def _const_spec(shape):
    nd = len(shape)
    return pl.BlockSpec(shape, lambda *_: (0,) * nd, pipeline_mode=pl.Buffered(1))


def _split3(a):
    a1 = a.astype(bf16)
    r1 = a - a1.astype(f32)
    a2 = r1.astype(bf16)
    a3 = (r1 - a2.astype(f32)).astype(bf16)
    return a1, a2, a3


ROW_TILES = D_MODEL // LANES


def _store_tile_rows(ref, val):
    n = val.shape[0]
    for c in range(ROW_TILES):
        ref[pl.ds(c, n, stride=ROW_TILES), :] = val[:, c * LANES:(c + 1) * LANES]


def _load_tile_rows(ref, n):
    return jnp.concatenate([ref[pl.ds(c, n, stride=ROW_TILES), :] for c in range(ROW_TILES)], axis=1)


def _loop_two_tiles(n, body, carry):
    carry = lax.fori_loop(0, n // 2, lambda jj, c: body(2 * jj + 1, body(2 * jj, c)), carry)
    return lax.cond(n % 2 == 1, lambda c: body(n - 1, c), lambda c: c, carry)


def _rms(x, g):
    ms = jnp.mean(x * x, axis=-1, keepdims=True)
    return x * lax.rsqrt(ms + RMS_EPS) * g


def _ada_kernel(c_ref, w_ref, b_ref, o_ref):
    c = c_ref[...]
    s = c * jax.nn.sigmoid(c)
    w = w_ref[...]
    s1, s2, _ = _split3(s)
    w1, w2, _ = _split3(w)
    acc = jnp.dot(s1, w1, preferred_element_type=f32)
    acc += jnp.dot(s1, w2, preferred_element_type=f32)
    acc += jnp.dot(s2, w1, preferred_element_type=f32)
    o_ref[...] = acc + b_ref[...]


def _ada(c, w_ada, b_ada):
    B, D = c.shape
    N = w_ada.shape[1]
    tn = 1024
    return pl.pallas_call(
        _ada_kernel,
        out_shape=jax.ShapeDtypeStruct((B, N), f32),
        grid=(N // tn,),
        in_specs=[pl.BlockSpec((B, D), lambda j: (0, 0)),
                  pl.BlockSpec((D, tn), lambda j: (0, j)),
                  pl.BlockSpec((1, tn), lambda j: (0, j))],
        out_specs=pl.BlockSpec((B, tn), lambda j: (0, j)),
        compiler_params=_cparams("arbitrary"),
        name="ada",
    )(c, w_ada, b_ada.reshape(1, N))


def _inproj_kernel(x_ref, sc_ref, sh_ref, g_ref, bfg_ref, tri_ref, esel_ref,
                   wfq, wfk, wfv, wnq, wcm, wkv, wmg, wsm,
                   bq, bv, bvs,
                   ofq, ofk, ofv, onq, ocm, oksl, ovsl, okwn, ovwn, omg, osm,
                   carry_ref, *, tm, tiles_per_seq):
    i = pl.program_id(0)
    x = x_ref[...]
    h = _rms(x, g_ref[...]) * (1.0 + sc_ref[0]) + sh_ref[0]
    hb = h.astype(bf16)

    def proj(w):
        return jnp.dot(hb, w[...], preferred_element_type=f32)

    low_half = lax.broadcasted_iota(i32, (tm, LANES), 1) < HEAD_DIM

    def place(tile, src_hi, dst_hi):
        t = tile if src_hi == dst_hi else pltpu.roll(tile, HEAD_DIM, 1)
        return jnp.where(low_half, 0.0, t) if dst_hi else jnp.where(low_half, t, 0.0)

    def tiles(c):
        return [c[:, j * LANES:(j + 1) * LANES] for j in range(c.shape[1] // LANES)]

    def heads(c, even_hi, odd_hi):
        out = []
        for t in tiles(c):
            out += [place(t, False, even_hi), place(t, True, odd_hi)]
        return jnp.concatenate(out, axis=1)

    def group_both(t):
        return jnp.concatenate([place(t, False, False), place(t, False, True),
                                place(t, True, False), place(t, True, True)], axis=1)

    ofq[...] = (heads(proj(wfq), False, False) + bq[...]).astype(bf16)
    ofv[...] = (heads(proj(wfv), False, True) + bv[...]).astype(bf16)
    onq[...] = heads(proj(wnq), True, True).astype(bf16)
    ocm[...] = proj(wcm).astype(bf16)
    ksl_c, vsl_c, kwn_c, vwn_c = tiles(proj(wkv))
    ovsl[...] = (group_both(vsl_c) + bvs[...]).astype(bf16)
    okwn[...] = heads(kwn_c, True, True).astype(bf16)
    ovwn[...] = (group_both(vwn_c) + bvs[...]).astype(bf16)
    omg[...] = jax.nn.sigmoid(proj(wmg)).astype(bf16)

    row = lax.broadcasted_iota(i32, (tm, 2 * LANES), 0)
    lane = lax.broadcasted_iota(i32, (tm, 2 * LANES), 1)
    blk = ((i % tiles_per_seq) * tm + row) // SEL_BLOCK
    onehot = jnp.where((lane & (LANES - 1)) == blk, 1.0, 0.0)
    oksl[...] = (heads(ksl_c, True, True) + onehot).astype(bf16)

    sm = proj(wsm)
    osm[...] = sm
    z = sm + bfg_ref[...]
    lane1 = lax.broadcasted_iota(i32, (tm, LANES), 1)
    logf = jnp.where(lane1 < FOX_HEADS, jnp.minimum(z, 0.0) - jnp.log(1.0 + jnp.exp(-jnp.abs(z))), 0.0)

    @pl.when(i % tiles_per_seq == 0)
    def _():
        carry_ref[...] = jnp.zeros_like(carry_ref)

    tri = tri_ref[...]
    cum = carry_ref[0:1, :]
    for piece in _split3(logf):
        cum = cum + jnp.dot(tri, piece, preferred_element_type=f32)
    carry_ref[0:1, :] = cum[tm - 1:tm, :]
    ncat = jnp.concatenate(_split3(-cum), axis=1)
    ofk[...] = (heads(proj(wfk), False, False)
                + jnp.dot(ncat, esel_ref[...], preferred_element_type=f32)).astype(bf16)


def _inproj(x2, sc1, sh1, g_pre, w_in, b_forget, S):
    T, D = x2.shape
    tm = min(ROW_TILE, S)
    tiles_per_seq = S // tm
    offs = np.cumsum(IN_SIZES)[:-1].tolist()
    (wfq, wfk, wfv, wff, wnq, wkcm, wvcm, wksl, wvsl, wkwn, wvwn, wng, wmgf, wmgn) = jnp.split(w_in, offs, axis=1)
    scale = HEAD_DIM ** -0.5
    cast = lambda w: w.astype(bf16)
    weights = [
        cast(wfq * scale),
        cast(wfk),
        cast(wfv),
        cast(wnq * scale),
        cast(jnp.concatenate([wkcm, wvcm], axis=1)),
        cast(jnp.concatenate([wksl, wvsl, wkwn, wvwn], axis=1)),
        cast(jnp.concatenate([wmgf, wmgn], axis=1)),
        cast(jnp.concatenate([wff, wng, jnp.zeros((D, LANES - FOX_HEADS - 3 * NSA_HEADS), f32)], axis=1)),
    ]
    G2 = NSA_KV_GROUPS * LANES
    widths = [FOX_HEADS * LANES] * 3 + [NSA_HEADS * LANES, 2 * NSA_KV_WIDTH, G2, 2 * G2, G2, 2 * G2, 2 * D, LANES]
    bq = np.zeros((1, FOX_HEADS * LANES), np.float32)
    bv = np.zeros((1, FOX_HEADS * LANES), np.float32)
    for h in range(FOX_HEADS):
        bq[0, h * LANES + HEAD_DIM:h * LANES + HEAD_DIM + 3] = 1.0
        bv[0, h * LANES + (HEAD_DIM if h % 2 == 0 else 0)] = 1.0
    bvs = np.zeros((1, NSA_KV_GROUPS * 2 * LANES), np.float32)
    for g in range(NSA_KV_GROUPS):
        bvs[0, g * 2 * LANES + HEAD_DIM] = 1.0
        bvs[0, g * 2 * LANES + LANES] = 1.0
    esel = np.zeros((3 * LANES, FOX_HEADS * LANES), np.float32)
    for j in range(3):
        for h in range(FOX_HEADS):
            esel[j * LANES + h, h * LANES + HEAD_DIM + j] = 1.0
    tri = np.tril(np.ones((tm, tm), np.float32))
    bfg = jnp.concatenate([b_forget, jnp.zeros((LANES - FOX_HEADS,), f32)]).reshape(1, LANES)

    out_dtypes = [bf16] * 10 + [f32]
    row_spec = lambda n: pl.BlockSpec((tm, n), lambda i: (i, 0))
    mod_spec = pl.BlockSpec((1, 1, D), lambda i: (i // tiles_per_seq, 0, 0))
    consts = [jnp.asarray(tri, bf16), jnp.asarray(esel, bf16)]
    biases = [jnp.asarray(bq), jnp.asarray(bv), jnp.asarray(bvs)]
    outs = pl.pallas_call(
        functools.partial(_inproj_kernel, tm=tm, tiles_per_seq=tiles_per_seq),
        out_shape=[jax.ShapeDtypeStruct((T, n), dt) for n, dt in zip(widths, out_dtypes)],
        grid=(T // tm,),
        in_specs=[row_spec(D), mod_spec, mod_spec, _const_spec((1, D)), _const_spec((1, LANES))]
        + [_const_spec(c.shape) for c in consts]
        + [_const_spec(w.shape) for w in weights]
        + [_const_spec(b.shape) for b in biases],
        out_specs=[row_spec(n) for n in widths],
        scratch_shapes=[pltpu.VMEM((8, LANES), f32)],
        compiler_params=_cparams("arbitrary"),
        name="inproj",
    )(x2, sc1, sh1, g_pre.reshape(1, D), bfg, *consts, *weights, *biases)
    return outs


def _fox_kernel(q_ref, k_ref, v_ref, o_ref, *, tq, nh):
    i = pl.program_id(2)
    row = lax.broadcasted_iota(i32, (tq, tq), 0)
    col = lax.broadcasted_iota(i32, (tq, tq), 1)
    qs = [q_ref[0, :, hh * LANES:(hh + 1) * LANES] for hh in range(nh)]

    def tile(j, carry, diag):
        start = pl.multiple_of(j * tq, tq)
        new = []
        for hh in range(nh):
            m, acc = carry[hh]
            k = k_ref[0, pl.ds(start, tq), hh * LANES:(hh + 1) * LANES]
            v = v_ref[0, pl.ds(start, tq), hh * LANES:(hh + 1) * LANES]
            s = lax.dot_general(qs[hh], k, _NT, preferred_element_type=f32)
            if diag:
                s = jnp.where(col <= row, s, NEG)
            m_new = jnp.maximum(m, jnp.max(s, axis=-1, keepdims=True))
            p = jnp.exp(s - m_new).astype(bf16)
            acc = jnp.exp(m - m_new) * acc + jnp.dot(p, v, preferred_element_type=f32)
            new.append((m_new, acc))
        return tuple(new)

    carry = tuple((jnp.full((tq, 1), NEG, f32), jnp.zeros((tq, LANES), f32)) for _ in range(nh))
    carry = tile(i, carry, True)
    carry = _loop_two_tiles(i, lambda j, c: tile(j, c, False), carry)
    o_ref[0] = jnp.concatenate([_pair_out(carry[2 * pr][1], carry[2 * pr + 1][1]) for pr in range(nh // 2)],
                               axis=1).astype(bf16)


def _fox(fq, fk, fv, B, S):
    tq = min(FOX_TILE, S)
    nq = S // tq
    nh = FOX_STEP_HEADS
    q3 = fq.reshape(B, S, FOX_HEADS * LANES)
    k3 = fk.reshape(B, S, FOX_HEADS * LANES)
    v3 = fv.reshape(B, S, FOX_HEADS * LANES)
    return pl.pallas_call(
        functools.partial(_fox_kernel, tq=tq, nh=nh),
        out_shape=jax.ShapeDtypeStruct((B, S, FOX_WIDTH), bf16),
        grid=(B, FOX_HEADS // nh, nq),
        in_specs=[pl.BlockSpec((1, tq, nh * LANES), lambda b, hp, i: (b, i, hp)),
                  pl.BlockSpec((1, S, nh * LANES), lambda b, hp, i: (b, 0, hp)),
                  pl.BlockSpec((1, S, nh * LANES), lambda b, hp, i: (b, 0, hp))],
        out_specs=pl.BlockSpec((1, tq, nh // 2 * LANES), lambda b, hp, i: (b, i, hp)),
        compiler_params=_cparams("parallel", "parallel", "arbitrary"),
        name="fox",
    )(q3, k3, v3)


def _compress_kernel(x_ref, pea_ref, peb_ref, wa_ref, wb_ref, okc, ovc, *, nc):
    x = x_ref[0].astype(f32)
    xa = (x + pea_ref[...]).astype(bf16)
    xb = (x + peb_ref[...]).astype(bf16)
    a = jnp.dot(xa, wa_ref[...], preferred_element_type=f32)
    b = jnp.dot(xb, wb_ref[...], preferred_element_type=f32)
    out = a + pltpu.roll(b, nc - 1, 0)
    okc[0] = out[:, :2 * LANES].astype(bf16)
    ovc[0] = out[:, 2 * LANES:].astype(bf16)


def _compress(cm, pe_k, pe_v, w_cmp_k, w_cmp_v, B, S):
    nc = S // CMP_STRIDE
    half = CMP_BLOCK // 2
    win = half * 2 * LANES
    x = cm.reshape(B, nc, win)
    wk = w_cmp_k.reshape(CMP_BLOCK, HEAD_DIM, HEAD_DIM)
    wv = w_cmp_v.reshape(CMP_BLOCK, HEAD_DIM, HEAD_DIM)
    H = HEAD_DIM

    def build(wk_h, wv_h):
        z = jnp.zeros((half, H, H), f32)

        def rows(cols):
            return jnp.concatenate([cols.get(c, z) for c in range(12)], axis=2)

        w = jnp.concatenate([rows({1: wk_h}),
                             rows({3: wk_h}),
                             rows({4: wv_h, 7: wv_h}),
                             rows({8: wv_h, 11: wv_h})],
                            axis=1)
        return w.reshape(win, 6 * LANES).astype(bf16)

    wa = build(wk[:half], wv[:half])
    wb = build(wk[half:], wv[half:])

    def pe_row(pk, pv):
        return jnp.concatenate([pk, pk, pv, pv], axis=1).reshape(1, win)

    pea = pe_row(pe_k[:half], pe_v[:half])
    peb = pe_row(pe_k[half:], pe_v[half:])
    return pl.pallas_call(
        functools.partial(_compress_kernel, nc=nc),
        out_shape=[jax.ShapeDtypeStruct((B, nc, 2 * LANES), bf16),
                   jax.ShapeDtypeStruct((B, nc, 4 * LANES), bf16)],
        grid=(B,),
        in_specs=[pl.BlockSpec((1, nc, win), lambda b: (b, 0, 0)),
                  _const_spec((1, win)), _const_spec((1, win)),
                  _const_spec((win, 6 * LANES)), _const_spec((win, 6 * LANES))],
        out_specs=[pl.BlockSpec((1, nc, 2 * LANES), lambda b: (b, 0, 0)),
                   pl.BlockSpec((1, nc, 4 * LANES), lambda b: (b, 0, 0))],
        compiler_params=_cparams("parallel"),
        name="compress",
    )(x, pea, peb, wa, wb)


def _cmpsel_kernel(q_ref, kc_ref, vc_ref, pat_ref, ov_ref, ocmp, osel, *, tq, nc, n_sel, top_n, past):
    i = pl.program_id(1)
    kc = kc_ref[0]
    c0 = i * (tq // CMP_STRIDE) - past
    wio = lax.broadcasted_iota(i32, (LANES, nc), 0)
    cio = lax.broadcasted_iota(i32, (LANES, nc), 1)
    shift = jnp.where(cio == wio + c0, 1.0, 0.0).astype(bf16)
    t = i * tq + lax.broadcasted_iota(i32, (tq, nc), 0)
    cend = lax.broadcasted_iota(i32, (tq, nc), 1) * CMP_STRIDE + (CMP_BLOCK - 1)
    valid = cend <= t
    pcs = jnp.zeros((tq, nc), f32)
    outs = []
    for hh in range(NSA_HPG):
        q = q_ref[0, :, hh * LANES:(hh + 1) * LANES]
        pat = pat_ref[hh]
        pat_hi = pat.astype(bf16)
        pat_lo = (pat - pat_hi.astype(f32)).astype(bf16)
        cb = (jnp.dot(pat_hi, shift, preferred_element_type=f32)
              + jnp.dot(pat_lo, shift, preferred_element_type=f32))
        lc = jnp.where(valid, lax.dot_general(q, kc, _NT, preferred_element_type=f32) + cb, NEG)
        m = jnp.max(lc, axis=-1, keepdims=True)
        p = jnp.where(valid, jnp.exp(lc - m), 0.0)
        l = jnp.sum(p, axis=-1, keepdims=True)
        pc = p * jnp.where(l > 0.0, 1.0 / l, 0.0)
        pcs = pcs + pc
        v = vc_ref[0, :, (hh % 2) * LANES:(hh % 2 + 1) * LANES]
        outs.append(jnp.dot(pc.astype(bf16), v, preferred_element_type=f32))
    ocmp[0] = jnp.concatenate([outs[0] + outs[1], outs[2] + outs[3]], axis=1).astype(bf16)

    hi = pcs.astype(bf16)
    lo = (pcs - hi.astype(f32)).astype(bf16)
    ov = ov_ref[...]
    imp = (lax.dot_general(ov, hi, _NT, preferred_element_type=f32)
           + lax.dot_general(ov, lo, _NT, preferred_element_type=f32))
    jio = lax.broadcasted_iota(i32, (n_sel, tq), 0)
    t = i * tq + lax.broadcasted_iota(i32, (n_sel, tq), 1)
    cur = t // SEL_BLOCK
    forced = (jio == 0) | (jio == cur) | (jio == cur - 1)
    score = jnp.where(forced, BIG, jnp.where(jio <= cur, imp, -BIG))
    picked = jnp.zeros((n_sel, tq), f32)
    jf = jio.astype(f32)
    for _ in range(top_n):
        best = jnp.max(score, axis=0, keepdims=True)
        jbest = jnp.min(jnp.where(score == best, jf, float(n_sel)), axis=0, keepdims=True)
        hit = jf == jbest
        picked = jnp.where(hit, 1.0, picked)
        score = jnp.where(hit, NEG, score)
    selb = jnp.where(picked > 0.0, 0.0, -SEL_MASK)
    padded = jnp.concatenate([selb, jnp.zeros((LANES - n_sel, tq), f32)], axis=0)
    osel[0, 0] = padded.T.astype(bf16)


def _bucket_bounds():
    n = np.arange(0, 4 * REL_MAX_DIST)
    max_exact = REL_BUCKETS // 2
    nf = np.maximum(n, 1).astype(np.float32)
    large = max_exact + (np.log(nf / np.float32(max_exact)) / np.float32(np.log(REL_MAX_DIST / max_exact))
                         * np.float32(REL_BUCKETS - max_exact)).astype(np.int32)
    bucket = np.where(n < max_exact, n, np.minimum(large, REL_BUCKETS - 1))
    return [int(n[bucket > b].min()) for b in range(REL_BUCKETS - 1)]


_BOUNDS = _bucket_bounds()
_CMP_PAST = (_BOUNDS[-1] + CMP_BLOCK - 1 + CMP_STRIDE - 1) // CMP_STRIDE - 1


def _rel_bias_of(d, rb_ref, h):
    far = rb_ref[(REL_BUCKETS - 1) * NSA_HEADS + h]
    v = jnp.zeros(d.shape, f32)
    for b in reversed(range(REL_BUCKETS - 1)):
        v = jnp.where(d < _BOUNDS[b], rb_ref[b * NSA_HEADS + h] - far, v)
    return v


def _biasgen_kernel(rb_ref, oslc, owin, ocmp, *, ts, tw, tc, nk):
    h = pl.program_id(0)

    def tile(t, off, window):
        lo, hi = off - (t - 1), off + (t - 1)
        if hi < 0 or (window is not None and lo >= window):
            return jnp.full((t, t), NEG, f32)
        d = lax.broadcasted_iota(i32, (t, t), 0) - lax.broadcasted_iota(i32, (t, t), 1) + off
        val = _rel_bias_of(d, rb_ref, h) if lo < _BOUNDS[-1] else jnp.zeros((t, t), f32)
        if lo < 0:
            val = jnp.where(d >= 0, val, NEG)
        if window is not None and hi >= window:
            val = jnp.where(d < window, val, NEG)
        return val

    oslc[0, 0, 0] = tile(ts, 0, None)
    oslc[0, 1, 0] = tile(ts, ts, None)
    oslc[0, 2, 0] = jnp.full((ts, ts), NEG, f32)
    for v in range(nk):
        for cc in range(nk):
            owin[0, v, 0, :, cc * tw:(cc + 1) * tw] = tile(tw, (v - cc) * tw, WINDOW)
    rr = lax.broadcasted_iota(i32, (tc, LANES), 0)
    w = lax.broadcasted_iota(i32, (tc, LANES), 1)
    d = rr - CMP_STRIDE * (w - _CMP_PAST) - (CMP_BLOCK - 1)
    ocmp[0] = jnp.where(d >= 0, _rel_bias_of(d, rb_ref, h), 0.0)


def _biasgen(rel_bias, S):
    ts, tw, tc = min(SLC_TILE, S), min(WIN_TILE, S), min(CMP_TILE, S)
    nk = WINDOW // tw + 1
    assert WINDOW % tw == 0 and S >= nk * tw and min(ts, tw) + 1 >= _BOUNDS[-1]
    assert tc // CMP_STRIDE + _CMP_PAST <= LANES
    G = NSA_KV_GROUPS
    return pl.pallas_call(
        functools.partial(_biasgen_kernel, ts=ts, tw=tw, tc=tc, nk=nk),
        out_shape=[jax.ShapeDtypeStruct((G, 3, NSA_HPG, ts, ts), f32),
                   jax.ShapeDtypeStruct((G, nk, NSA_HPG, tw, nk * tw), f32),
                   jax.ShapeDtypeStruct((NSA_HEADS, tc, LANES), f32)],
        grid=(NSA_HEADS,),
        in_specs=[pl.BlockSpec(memory_space=pltpu.SMEM)],
        out_specs=[pl.BlockSpec((1, 3, 1, ts, ts), lambda h: (h // NSA_HPG, 0, h % NSA_HPG, 0, 0)),
                   pl.BlockSpec((1, nk, 1, tw, nk * tw), lambda h: (h // NSA_HPG, 0, h % NSA_HPG, 0, 0)),
                   pl.BlockSpec((1, tc, LANES), lambda h: (h, 0, 0))],
        compiler_params=_cparams("arbitrary"),
        name="biasgen",
    )(rel_bias.reshape(-1))


def _cmpsel(nq_arr, kc, vc, pat, B, S):
    tq = min(CMP_TILE, S)
    nqt = S // tq
    nc = S // CMP_STRIDE
    n_sel = S // SEL_BLOCK
    top_n = min(SEL_TOPN, n_sel)
    G = NSA_KV_GROUPS
    assert n_sel <= HEAD_DIM
    c = np.arange(nc)[None, :]
    j = np.arange(n_sel)[:, None]
    ov = ((c * CMP_STRIDE < j * SEL_BLOCK + SEL_BLOCK) & (c * CMP_STRIDE + CMP_BLOCK > j * SEL_BLOCK)
          & (c < nc - 1)).astype(np.float32)
    q3 = nq_arr.reshape(B, S, NSA_HEADS * LANES)
    return pl.pallas_call(
        functools.partial(_cmpsel_kernel, tq=tq, nc=nc, n_sel=n_sel, top_n=top_n, past=_CMP_PAST),
        out_shape=[jax.ShapeDtypeStruct((B, S, NSA_HEADS * HEAD_DIM), bf16),
                   jax.ShapeDtypeStruct((B, G, S, LANES), bf16)],
        grid=(G, nqt, B),
        in_specs=[pl.BlockSpec((1, tq, NSA_HPG * LANES), lambda g, i, b: (b, i, g)),
                  pl.BlockSpec((1, nc, LANES), lambda g, i, b: (b, 0, g)),
                  pl.BlockSpec((1, nc, 2 * LANES), lambda g, i, b: (b, 0, g)),
                  pl.BlockSpec((NSA_HPG, tq, LANES), lambda g, i, b: (g, 0, 0)),
                  _const_spec((n_sel, nc))],
        out_specs=[pl.BlockSpec((1, tq, 2 * LANES), lambda g, i, b: (b, i, g)),
                   pl.BlockSpec((1, 1, tq, LANES), lambda g, i, b: (b, g, i, 0))],
        compiler_params=_cparams("parallel", "arbitrary", "arbitrary"),
        name="cmpsel",
    )(q3, kc, vc, pat, jnp.asarray(ov, bf16))


def _pair_out(acc_e, acc_o):
    lane = lax.broadcasted_iota(i32, acc_e.shape, 1)
    return jnp.where(lane < HEAD_DIM, acc_e / acc_e[:, HEAD_DIM:HEAD_DIM + 1], acc_o / acc_o[:, 0:1])


def _slc_kernel(q_ref, sb_ref, k_ref, v_ref, bias_ref, o_ref, *, tq):
    i = pl.program_id(2)
    sb = sb_ref[0, 0]
    qs = [q_ref[0, :, h * LANES:(h + 1) * LANES] + sb for h in range(NSA_HPG)]

    def tile(j, carry, bias_idx):
        start = pl.multiple_of(j * tq, tq)
        k = k_ref[0, pl.ds(start, tq), :]
        v = v_ref[0, pl.ds(start, tq), :]
        new = []
        for h in range(NSA_HPG):
            m, acc = carry[h]
            s = lax.dot_general(qs[h], k, _NT, preferred_element_type=f32)
            if bias_idx is not None:
                s = s + bias_ref[0, bias_idx, h]
            m_new = jnp.maximum(m, jnp.max(s, axis=-1, keepdims=True))
            p = jnp.exp(s - m_new).astype(bf16)
            vh = v[:, (h % 2) * LANES:(h % 2 + 1) * LANES]
            acc = jnp.exp(m - m_new) * acc + jnp.dot(p, vh, preferred_element_type=f32)
            new.append((m_new, acc))
        return tuple(new)

    carry = tuple((jnp.full((tq, 1), NEG, f32), jnp.zeros((tq, LANES), f32)) for _ in range(NSA_HPG))
    carry = tile(i, carry, 0)
    carry = tile(jnp.maximum(i - 1, 0), carry, jnp.where(i >= 1, 1, 2))
    carry = _loop_two_tiles(jnp.maximum(i - 1, 0), lambda j, c: tile(j, c, None), carry)
    o_ref[0] = jnp.concatenate([_pair_out(carry[0][1], carry[1][1]),
                                _pair_out(carry[2][1], carry[3][1])], axis=1).astype(bf16)


def _win_kernel(q_ref, k_ref, v_ref, bias_ref, o_ref, *, tq, nk):
    i = pl.program_id(2)
    start = pl.multiple_of(jnp.maximum(i - (nk - 1), 0) * tq, tq)
    k = k_ref[0, pl.ds(start, nk * tq), :]
    v = v_ref[0, pl.ds(start, nk * tq), :]
    accs = []
    for h in range(NSA_HPG):
        q = q_ref[0, :, h * LANES:(h + 1) * LANES]
        s = lax.dot_general(q, k, _NT, preferred_element_type=f32) + bias_ref[0, 0, h]
        p = jnp.exp(s - jnp.max(s, axis=-1, keepdims=True)).astype(bf16)
        accs.append(jnp.dot(p, v[:, (h % 2) * LANES:(h % 2 + 1) * LANES], preferred_element_type=f32))
    o_ref[0] = jnp.concatenate([_pair_out(accs[0], accs[1]), _pair_out(accs[2], accs[3])], axis=1).astype(bf16)


def _nsa_flash(kind, nq_arr, selb, k_arr, v_arr, bias, B, S):
    tq = bias.shape[3]
    nqt = S // tq
    G = NSA_KV_GROUPS
    q3 = nq_arr.reshape(B, S, NSA_HEADS * LANES)
    k3 = k_arr.reshape(B, S, G * LANES)
    v3 = v_arr.reshape(B, S, G * 2 * LANES)
    q_spec = pl.BlockSpec((1, tq, NSA_HPG * LANES), lambda g, b, i: (b, i, g))
    k_spec = pl.BlockSpec((1, S, LANES), lambda g, b, i: (b, 0, g))
    v_spec = pl.BlockSpec((1, S, 2 * LANES), lambda g, b, i: (b, 0, g))
    if kind == "slc":
        kern = functools.partial(_slc_kernel, tq=tq)
        extra_specs = [pl.BlockSpec((1, 1, tq, LANES), lambda g, b, i: (b, g, i, 0))]
        extra = [selb]
        b_spec = pl.BlockSpec((1, 3, NSA_HPG, tq, tq), lambda g, b, i: (g, 0, 0, 0, 0),
                              pipeline_mode=pl.Buffered(1))
    else:
        nk = bias.shape[1]
        kern = functools.partial(_win_kernel, tq=tq, nk=nk)
        extra_specs, extra = [], []
        b_spec = pl.BlockSpec((1, 1, NSA_HPG, tq, nk * tq), lambda g, b, i: (g, jnp.minimum(i, nk - 1), 0, 0, 0))
    return pl.pallas_call(
        kern,
        out_shape=jax.ShapeDtypeStruct((B, S, NSA_HEADS * HEAD_DIM), bf16),
        grid=(G, B, nqt),
        in_specs=[q_spec] + extra_specs + [k_spec, v_spec, b_spec],
        out_specs=pl.BlockSpec((1, tq, 2 * LANES), lambda g, b, i: (b, i, g)),
        compiler_params=_cparams("parallel", "parallel", "arbitrary"),
        name=kind,
    )(q3, *extra, k3, v3, bias)


def _post_kernel(x_ref, ofox, ocmp, oslc, owin, mg_ref, sm_ref, ga1, sc2, sh2, gpost, gpre,
                 wfp, wnp_, wmo, wr, br, eg,
                 x1_ref, h2_ref, route_ref, cnt_ref, carry_ref, *, tm):
    i = pl.program_id(0)
    W = NSA_WIDTH
    n = tm

    def slab(r0):
        rs = pl.ds(r0, n)
        gates = jax.nn.sigmoid(sm_ref[rs, :]).astype(bf16)
        gx = jnp.dot(gates, eg[...], preferred_element_type=f32)
        nsa = (gx[:, :W] * ocmp[rs, :].astype(f32) + gx[:, W:2 * W] * oslc[rs, :].astype(f32)
               + gx[:, 2 * W:] * owin[rs, :].astype(f32))
        y_nsa = jnp.dot(nsa.astype(bf16), wnp_[...], preferred_element_type=f32)
        y_fox = jnp.dot(ofox[rs, :], wfp[...], preferred_element_type=f32)
        mg = mg_ref[rs, :].astype(f32)
        mix = (mg[:, :D_MODEL] * y_fox + mg[:, D_MODEL:] * y_nsa).astype(bf16)
        mixed = jnp.dot(mix, wmo[...], preferred_element_type=f32)
        x1 = x_ref[rs, :] + ga1[0] * _rms(mixed, gpost[...])
        x1_ref[rs, :] = x1
        h2 = _rms(x1, gpre[...]) * (1.0 + sc2[0]) + sh2[0]
        _store_tile_rows(h2_ref.at[pl.ds(r0 * ROW_TILES, n * ROW_TILES), :], h2)

        lane = lax.broadcasted_iota(i32, (n, LANES), 1)
        logits = jnp.dot(h2.astype(bf16), wr[...], preferred_element_type=f32) + br[...]
        l = jnp.where(lane < N_EXPERTS, logits, NEG)
        vals, idxs = [], []
        for _ in range(TOP_K):
            m = jnp.max(l, axis=-1, keepdims=True)
            idx = jnp.min(jnp.where(l == m, lane, LANES), axis=-1, keepdims=True)
            vals.append(m)
            idxs.append(idx)
            l = jnp.where(lane == idx, NEG, l)
        es = [jnp.exp(v - vals[0]) for v in vals]
        den = es[0] + es[1] + es[2] + es[3]
        route = jnp.zeros((n, LANES), f32)
        for k in range(TOP_K):
            route = jnp.where(lane == k, idxs[k].astype(f32), route)
            route = jnp.where(lane == 2 * TOP_K + k, es[k] / den, route)
        route_ref[rs, :] = route
        cnt = sum(jnp.where(lane == idx, 1.0, 0.0) for idx in idxs)
        return jnp.sum(cnt, axis=0, keepdims=True)

    @pl.when(i == 0)
    def _():
        carry_ref[...] = jnp.zeros_like(carry_ref)

    new_carry = carry_ref[0:1, :] + slab(0)
    carry_ref[0:1, :] = new_carry
    cnt_ref[...] = jnp.broadcast_to(new_carry, cnt_ref.shape)


def _post(x2, ofox, ocmp, oslc, owin, mg, sm, ga1, sc2, sh2, g_post, g_pre2,
          w_fox_proj, w_nsa_proj, w_mix_out, w_router, b_router, S):
    T, D = x2.shape
    tm = min(ROW_TILE, S)
    tiles_per_seq = S // tm
    W = NSA_WIDTH
    eg = np.zeros((LANES, 3 * W), np.float32)
    for h in range(NSA_HEADS):
        for k in range(3):
            eg[FOX_HEADS + 3 * h + k, k * W + h * HEAD_DIM:k * W + (h + 1) * HEAD_DIM] = 1.0
    wr = jnp.concatenate([w_router, jnp.zeros((D, LANES - N_EXPERTS), f32)], axis=1).astype(bf16)
    br = jnp.concatenate([b_router, jnp.zeros((LANES - N_EXPERTS,), f32)]).reshape(1, LANES)
    row = lambda n: pl.BlockSpec((tm, n), lambda i: (i, 0))
    mod = pl.BlockSpec((1, 1, D), lambda i: (i // tiles_per_seq, 0, 0))
    consts = [w_fox_proj.astype(bf16), w_nsa_proj.astype(bf16), w_mix_out.astype(bf16), wr, br,
              jnp.asarray(eg, bf16)]
    return pl.pallas_call(
        functools.partial(_post_kernel, tm=tm),
        out_shape=[jax.ShapeDtypeStruct((T, D), f32), jax.ShapeDtypeStruct((T * ROW_TILES, LANES), f32),
                   jax.ShapeDtypeStruct((T, LANES), f32), jax.ShapeDtypeStruct((8, LANES), f32)],
        grid=(T // tm,),
        in_specs=[row(D), row(FOX_WIDTH), row(W), row(W), row(W), row(2 * D), row(LANES),
                  mod, mod, mod, _const_spec((1, D)), _const_spec((1, D))]
        + [_const_spec(c.shape) for c in consts],
        out_specs=[row(D), pl.BlockSpec((tm * ROW_TILES, LANES), lambda i: (i, 0)), row(LANES),
                   pl.BlockSpec((8, LANES), lambda i: (0, 0))],
        scratch_shapes=[pltpu.VMEM((8, LANES), f32)],
        compiler_params=_cparams("arbitrary"),
        name="post",
    )(x2, ofox, ocmp, oslc, owin, mg, sm, ga1, sc2, sh2, g_post.reshape(1, D), g_pre2.reshape(1, D), *consts)


def _moe_kernel(be_ref, nu_ref, tok0_ref, tokn_ref, dstp_ref, h_hbm, wg, bg, wu, bu, wd, bd,
                y_hbm, xbuf, ybuf, wgb, wub, wdb, gsem, ssem, *, bm):
    i = pl.program_id(0)
    nu = nu_ref[0]
    rt = ROW_TILES

    def rows(r):
        return pl.ds(r * rt, rt) if isinstance(r, int) else pl.ds(pl.multiple_of(r * rt, rt), rt)

    def gather_copy(tok_ref, r, s):
        return pltpu.make_async_copy(h_hbm.at[tok_ref[0, 0, r]], xbuf.at[s, rows(r), :], gsem.at[s])

    def scatter_copy(r, s):
        return pltpu.make_async_copy(ybuf.at[s, rows(r), :], y_hbm.at[dstp_ref[0, 0, r]], ssem.at[s])

    def wait_gather(s):
        pltpu.make_async_copy(xbuf.at[s], xbuf.at[s], gsem.at[s]).wait()

    def wait_scatter(s):
        pltpu.make_async_copy(ybuf.at[s], ybuf.at[s], ssem.at[s]).wait()

    @pl.when(i == 0)
    def _():
        ybuf[1] = jnp.zeros((bm * rt, LANES), f32)
        n_real = y_hbm.shape[0] - 2 * bm

        def issue(r, c):
            gather_copy(tok0_ref, r, 0).start()
            for half in range(2):
                pltpu.make_async_copy(ybuf.at[1, rows(r), :], y_hbm.at[n_real + half * bm + r],
                                      ssem.at[1]).start()
            return c

        lax.fori_loop(0, bm, issue, 0)
        for half in range(2):
            wait_scatter(1)

    def step(slot):
        other = 1 - slot
        wait_gather(slot)

        @pl.when(i >= 1)
        def _():
            wait_scatter(slot)

        @pl.when(jnp.logical_or(i == 0, be_ref[i] != be_ref[jnp.maximum(i - 1, 0)]))
        def _():
            wgb[...] = wg[0].astype(bf16)
            wub[...] = wu[0].astype(bf16)
            wdb[...] = wd[0].astype(bf16)

        for r in range(bm):
            gather_copy(tokn_ref, r, other).start(priority=r % 2)
            scatter_copy(r, other).start(priority=(r + 1) % 2)
        x = _load_tile_rows(xbuf.at[slot], bm).astype(bf16)
        g = jnp.dot(x, wgb[...], preferred_element_type=f32) + bg[0]
        u = jnp.dot(x, wub[...], preferred_element_type=f32) + bu[0]
        gt = jnp.minimum(g, SWIGLU_LIMIT)
        up = jnp.clip(u, -SWIGLU_LIMIT, SWIGLU_LIMIT)
        a = (gt * jax.nn.sigmoid(SWIGLU_ALPHA * gt) * (up + 1.0)).astype(bf16)
        _store_tile_rows(ybuf.at[slot], jnp.dot(a, wdb[...], preferred_element_type=f32) + bd[0])

    def drain(slot):
        other = 1 - slot
        wait_gather(slot)
        wait_scatter(slot)

        def issue(r, c):
            scatter_copy(r, other).start()
            return c

        lax.fori_loop(0, bm, issue, 0)
        wait_scatter(other)

    for s in range(2):
        pl.when(jnp.logical_and(i < nu, i % 2 == s))(functools.partial(step, s))
        pl.when(jnp.logical_and(i == nu, i % 2 == s))(functools.partial(drain, s))


def _moe(blk_e, n_used, row_tok, row_dst, h2t, w_gate, b_gate, w_up, b_up, w_down, b_down, n_rows):
    D = D_MODEL
    E, _, F = w_gate.shape
    nbt = row_tok.shape[0]
    bm = MOE_BM
    wsel = lambda i, be, nu: (be[jnp.minimum(i, nu[0] - 1)], 0, 0)
    idx_spec = lambda f: pl.BlockSpec((1, 1, bm), lambda i, be, nu: (f(i), 0, 0), memory_space=pltpu.SMEM)
    return pl.pallas_call(
        functools.partial(_moe_kernel, bm=bm),
        out_shape=jax.ShapeDtypeStruct((n_rows, ROW_TILES, LANES), f32),
        grid_spec=pltpu.PrefetchScalarGridSpec(
            num_scalar_prefetch=2, grid=(nbt,),
            in_specs=[idx_spec(lambda i: 0),
                      idx_spec(lambda i: jnp.minimum(i + 1, nbt - 1)),
                      idx_spec(lambda i: jnp.maximum(i - 1, 0)),
                      pl.BlockSpec(memory_space=pl.ANY),
                      pl.BlockSpec((1, D, F), wsel), pl.BlockSpec((1, 1, F), wsel),
                      pl.BlockSpec((1, D, F), wsel), pl.BlockSpec((1, 1, F), wsel),
                      pl.BlockSpec((1, F, D), wsel), pl.BlockSpec((1, 1, D), wsel)],
            out_specs=pl.BlockSpec(memory_space=pl.ANY),
            scratch_shapes=[pltpu.VMEM((2, bm * ROW_TILES, LANES), f32), pltpu.VMEM((2, bm * ROW_TILES, LANES), f32),
                            pltpu.VMEM((D, F), bf16), pltpu.VMEM((D, F), bf16), pltpu.VMEM((F, D), bf16),
                            pltpu.SemaphoreType.DMA((2,)), pltpu.SemaphoreType.DMA((2,))]),
        compiler_params=_cparams("arbitrary"),
        name="moe",
    )(blk_e, n_used, row_tok, row_tok, row_dst, h2t.reshape(-1, ROW_TILES, LANES),
      w_gate, b_gate.reshape(E, 1, F), w_up, b_up.reshape(E, 1, F), w_down, b_down.reshape(E, 1, D))


def _combine_kernel(y0, y1, y2, y3, route_ref, x1_ref, ga2, gpost, o_ref):
    route = route_ref[...]
    tm = o_ref.shape[0]
    y = jnp.zeros(o_ref.shape, f32)
    for k, yk in enumerate((y0, y1, y2, y3)):
        y = y + route[:, 2 * TOP_K + k:2 * TOP_K + k + 1] * _load_tile_rows(yk, tm)
    o_ref[...] = x1_ref[...] + ga2[0] * _rms(y, gpost[...])


def _combine(y4, route, x1, ga2, g_post2, S):
    T, D = x1.shape
    tm = min(MOE_ROWS, S)
    tiles_per_seq = S // tm
    nt = T // tm
    y4 = y4.reshape(-1, LANES)
    y_spec = lambda k: pl.BlockSpec((tm * ROW_TILES, LANES), lambda i: (k * nt + i, 0))
    return pl.pallas_call(
        _combine_kernel,
        out_shape=jax.ShapeDtypeStruct((T, D), f32),
        grid=(nt,),
        in_specs=[y_spec(k) for k in range(TOP_K)]
        + [pl.BlockSpec((tm, LANES), lambda i: (i, 0)),
           pl.BlockSpec((tm, D), lambda i: (i, 0)),
           pl.BlockSpec((1, 1, D), lambda i: (i // tiles_per_seq, 0, 0)),
           pl.BlockSpec((1, D), lambda i: (0, 0))],
        out_specs=pl.BlockSpec((tm, D), lambda i: (i, 0)),
        compiler_params=_cparams("parallel"),
        name="combine",
    )(y4, y4, y4, y4, route, x1, ga2, g_post2.reshape(1, D))


def kernel(x, c, w_ada, b_ada, g_mix_pre, g_mix_post, w_in, b_forget, pe_k, pe_v, w_cmp_k, w_cmp_v, w_fox_proj, w_nsa_proj, w_mix_out, rel_bias, g_ffn_pre, g_ffn_post, w_router, b_router, w_gate, b_gate, w_up, b_up, w_down, b_down):
    B, S, D = x.shape
    T = B * S
    for l in range(w_ada.shape[0]):
        x2 = x.reshape(T, D)
        ada = _ada(c, w_ada[l], b_ada[l])
        sh1, sc1, ga1, sh2, sc2, ga2 = [a.reshape(B, 1, D) for a in jnp.split(ada, 6, axis=-1)]
        fq, fk, fv, nq, cm, ksl, vsl, kwn, vwn, mg, sm = _inproj(x2, sc1, sh1, g_mix_pre[l], w_in[l], b_forget[l], S)
        o_fox = _fox(fq, fk, fv, B, S)
        kc, vc = _compress(cm, pe_k[l], pe_v[l], w_cmp_k[l], w_cmp_v[l], B, S)
        bias_slc, bias_win, pat_cmp = _biasgen(rel_bias, S)
        o_cmp, selb = _cmpsel(nq, kc, vc, pat_cmp, B, S)
        o_slc = _nsa_flash("slc", nq, selb, ksl, vsl, bias_slc, B, S)
        o_win = _nsa_flash("win", nq, None, kwn, vwn, bias_win, B, S)
        x1, h2, route, cnt = _post(x2, o_fox.reshape(T, -1), o_cmp.reshape(T, -1), o_slc.reshape(T, -1),
                                   o_win.reshape(T, -1), mg, sm, ga1, sc2, sh2, g_mix_post[l], g_ffn_pre[l],
                                   w_fox_proj[l], w_nsa_proj[l], w_mix_out[l], w_router[l], b_router[l], S)
        counts = cnt[0, :N_EXPERTS].astype(i32)
        nblk = (counts + MOE_BM - 1) // MOE_BM
        blk_end = jnp.cumsum(nblk)
        pad_start = (blk_end - nblk) * MOE_BM
        top_i = route[:, :TOP_K].astype(i32)
        A = T * TOP_K
        nbt = -(-A // MOE_BM) + N_EXPERTS + 1
        n_used = blk_end[-1:].astype(i32)
        blk_e = jnp.minimum(jnp.sum(jnp.arange(nbt)[:, None] >= blk_end[None, :], axis=1), N_EXPERTS - 1).astype(i32)
        a_sorted = jnp.sort((top_i * A + jnp.arange(A, dtype=i32).reshape(T, TOP_K)).reshape(-1)) % A
        grp_start = jnp.cumsum(counts) - counts
        j = jnp.arange(MOE_BM, dtype=i32)[None, :]
        b = jnp.arange(nbt, dtype=i32)[:, None]
        r_in_e = b * MOE_BM + j - pad_start[blk_e][:, None]
        valid = (b < n_used[0]) & (r_in_e < counts[blk_e][:, None])
        row_a = a_sorted[jnp.clip(grp_start[blk_e][:, None] + r_in_e, 0, A - 1)]
        row_tok = jnp.where(valid, row_a // TOP_K, 0)
        row_dst = jnp.where(valid, (row_a % TOP_K) * T + row_a // TOP_K, A + (b % 2) * MOE_BM + j)
        y4 = _moe(blk_e, n_used, row_tok.reshape(nbt, 1, MOE_BM), row_dst.reshape(nbt, 1, MOE_BM), h2,
                  w_gate[l], b_gate[l], w_up[l], b_up[l], w_down[l], b_down[l], A + 2 * MOE_BM)
        x = _combine(y4, route, x1, ga2, g_ffn_post[l], S).reshape(B, S, D)
    return x
```

```python
import functools

import numpy as np
import jax
import jax.numpy as jnp
from jax import lax
from jax.experimental import pallas as pl
from jax.experimental.pallas import tpu as pltpu

f32 = jnp.float32
bf16 = jnp.bfloat16
i32 = jnp.int32

D_MODEL = 1024
HEAD_DIM = 64
FOX_HEADS = 8
NSA_HEADS = 8
NSA_KV_GROUPS = 2
NSA_HPG = NSA_HEADS // NSA_KV_GROUPS
FOX_WIDTH = FOX_HEADS * HEAD_DIM
NSA_WIDTH = NSA_HEADS * HEAD_DIM
NSA_KV_WIDTH = NSA_KV_GROUPS * HEAD_DIM
CMP_BLOCK = 32
CMP_STRIDE = 16
SEL_BLOCK = 64
SEL_TOPN = 16
WINDOW = 512
REL_BUCKETS = 32
REL_MAX_DIST = 128
N_EXPERTS = 32
TOP_K = 4
SWIGLU_LIMIT = 7.0
SWIGLU_ALPHA = 1.702
RMS_EPS = 1e-6
NEG = -1e30
BIG = 1e9
IN_SIZES = (FOX_WIDTH, FOX_WIDTH, FOX_WIDTH, FOX_HEADS, NSA_WIDTH,
            NSA_KV_WIDTH, NSA_KV_WIDTH, NSA_KV_WIDTH, NSA_KV_WIDTH, NSA_KV_WIDTH, NSA_KV_WIDTH,
            3 * NSA_HEADS, D_MODEL, D_MODEL)

LANES = 128
VMEM_LIMIT = 56 * 1024 * 1024
FOX_TILE = 512
FOX_STEP_HEADS = 8
SLC_TILE = 512
TILES_PER_TRIP = 3
WIN_TILE = 512
CMP_TILE = 1024
ROW_TILE = 512
MOE_BM = 512
MOE_ROWS = 512
SEL_MASK = 1e9

_NT = (((1,), (1,)), ((), ()))


def _cparams(*sem):
    return pltpu.CompilerParams(dimension_semantics=sem, vmem_limit_bytes=VMEM_LIMIT)


def _const_spec(shape):
    nd = len(shape)
    return pl.BlockSpec(shape, lambda *_: (0,) * nd, pipeline_mode=pl.Buffered(1))


def _split3(a):
    a1 = a.astype(bf16)
    r1 = a - a1.astype(f32)
    a2 = r1.astype(bf16)
    a3 = (r1 - a2.astype(f32)).astype(bf16)
    return a1, a2, a3


ROW_TILES = D_MODEL // LANES


def _store_tile_rows(ref, val):
    n = val.shape[0]
    for c in range(ROW_TILES):
        ref[pl.ds(c, n, stride=ROW_TILES), :] = val[:, c * LANES:(c + 1) * LANES]


def _load_tile_rows(ref, n):
    return jnp.concatenate([ref[pl.ds(c, n, stride=ROW_TILES), :] for c in range(ROW_TILES)], axis=1)


def _loop_tiles(n, body, carry):
    u = TILES_PER_TRIP

    def trip(jj, c):
        for t in range(u):
            c = body(u * jj + t, c)
        return c

    carry = lax.fori_loop(0, n // u, trip, carry)
    return lax.fori_loop((n // u) * u, n, body, carry)


def _rms(x, g):
    ms = jnp.mean(x * x, axis=-1, keepdims=True)
    return x * lax.rsqrt(ms + RMS_EPS) * g


def _ada_kernel(c_ref, w_ref, b_ref, o_ref):
    c = c_ref[...]
    s = c * jax.nn.sigmoid(c)
    w = w_ref[...]
    s1, s2, _ = _split3(s)
    w1, w2, _ = _split3(w)
    acc = jnp.dot(s1, w1, preferred_element_type=f32)
    acc += jnp.dot(s1, w2, preferred_element_type=f32)
    acc += jnp.dot(s2, w1, preferred_element_type=f32)
    o_ref[...] = acc + b_ref[...]


def _ada(c, w_ada, b_ada):
    B, D = c.shape
    N = w_ada.shape[1]
    tn = 1024
    return pl.pallas_call(
        _ada_kernel,
        out_shape=jax.ShapeDtypeStruct((B, N), f32),
        grid=(N // tn,),
        in_specs=[pl.BlockSpec((B, D), lambda j: (0, 0)),
                  pl.BlockSpec((D, tn), lambda j: (0, j)),
                  pl.BlockSpec((1, tn), lambda j: (0, j))],
        out_specs=pl.BlockSpec((B, tn), lambda j: (0, j)),
        compiler_params=_cparams("arbitrary"),
        name="ada",
    )(c, w_ada, b_ada.reshape(1, N))


def _inproj_kernel(x_ref, sc_ref, sh_ref, g_ref, bfg_ref, tri_ref, esel_ref,
                   wfq, wfk, wfv, wnq, wcm, wkv, wmg, wsm,
                   bq, bv, bvs,
                   ofq, ofk, ofv, onq, ocm, oksl, ovsl, okwn, ovwn, omg, osm,
                   carry_ref, *, tm, tiles_per_seq):
    i = pl.program_id(0)
    x = x_ref[...]
    h = _rms(x, g_ref[...]) * (1.0 + sc_ref[0]) + sh_ref[0]
    hb = h.astype(bf16)

    def proj(w):
        return jnp.dot(hb, w[...], preferred_element_type=f32)

    low_half = lax.broadcasted_iota(i32, (tm, LANES), 1) < HEAD_DIM

    def place(tile, src_hi, dst_hi):
        t = tile if src_hi == dst_hi else pltpu.roll(tile, HEAD_DIM, 1)
        return jnp.where(low_half, 0.0, t) if dst_hi else jnp.where(low_half, t, 0.0)

    def tiles(c):
        return [c[:, j * LANES:(j + 1) * LANES] for j in range(c.shape[1] // LANES)]

    def heads(c, even_hi, odd_hi):
        out = []
        for t in tiles(c):
            out += [place(t, False, even_hi), place(t, True, odd_hi)]
        return jnp.concatenate(out, axis=1)

    def group_both(t):
        return jnp.concatenate([place(t, False, False), place(t, False, True),
                                place(t, True, False), place(t, True, True)], axis=1)

    ofq[...] = (heads(proj(wfq), False, False) + bq[...]).astype(bf16)
    ofv[...] = (heads(proj(wfv), False, True) + bv[...]).astype(bf16)
    onq[...] = heads(proj(wnq), True, True).astype(bf16)
    ocm[...] = proj(wcm).astype(bf16)
    ksl_c, vsl_c, kwn_c, vwn_c = tiles(proj(wkv))
    ovsl[...] = (group_both(vsl_c) + bvs[...]).astype(bf16)
    okwn[...] = heads(kwn_c, True, True).astype(bf16)
    ovwn[...] = (group_both(vwn_c) + bvs[...]).astype(bf16)
    omg[...] = jax.nn.sigmoid(proj(wmg)).astype(bf16)

    row = lax.broadcasted_iota(i32, (tm, 2 * LANES), 0)
    lane = lax.broadcasted_iota(i32, (tm, 2 * LANES), 1)
    blk = ((i % tiles_per_seq) * tm + row) // SEL_BLOCK
    onehot = jnp.where((lane & (LANES - 1)) == blk, 1.0, 0.0)
    oksl[...] = (heads(ksl_c, True, True) + onehot).astype(bf16)

    sm = proj(wsm)
    osm[...] = sm
    z = sm + bfg_ref[...]
    lane1 = lax.broadcasted_iota(i32, (tm, LANES), 1)
    logf = jnp.where(lane1 < FOX_HEADS, jnp.minimum(z, 0.0) - jnp.log(1.0 + jnp.exp(-jnp.abs(z))), 0.0)

    @pl.when(i % tiles_per_seq == 0)
    def _():
        carry_ref[...] = jnp.zeros_like(carry_ref)

    tri = tri_ref[...]
    cum = carry_ref[0:1, :]
    for piece in _split3(logf):
        cum = cum + jnp.dot(tri, piece, preferred_element_type=f32)
    carry_ref[0:1, :] = cum[tm - 1:tm, :]
    ncat = jnp.concatenate(_split3(-cum), axis=1)
    ofk[...] = (heads(proj(wfk), False, False)
                + jnp.dot(ncat, esel_ref[...], preferred_element_type=f32)).astype(bf16)


def _inproj(x2, sc1, sh1, g_pre, w_in, b_forget, S):
    T, D = x2.shape
    tm = min(ROW_TILE, S)
    tiles_per_seq = S // tm
    offs = np.cumsum(IN_SIZES)[:-1].tolist()
    (wfq, wfk, wfv, wff, wnq, wkcm, wvcm, wksl, wvsl, wkwn, wvwn, wng, wmgf, wmgn) = jnp.split(w_in, offs, axis=1)
    scale = HEAD_DIM ** -0.5
    cast = lambda w: w.astype(bf16)
    weights = [
        cast(wfq * scale),
        cast(wfk),
        cast(wfv),
        cast(wnq * scale),
        cast(jnp.concatenate([wkcm, wvcm], axis=1)),
        cast(jnp.concatenate([wksl, wvsl, wkwn, wvwn], axis=1)),
        cast(jnp.concatenate([wmgf, wmgn], axis=1)),
        cast(jnp.concatenate([wff, wng, jnp.zeros((D, LANES - FOX_HEADS - 3 * NSA_HEADS), f32)], axis=1)),
    ]
    G2 = NSA_KV_GROUPS * LANES
    widths = [FOX_HEADS * LANES] * 3 + [NSA_HEADS * LANES, 2 * NSA_KV_WIDTH, G2, 2 * G2, G2, 2 * G2, 2 * D, LANES]
    bq = np.zeros((1, FOX_HEADS * LANES), np.float32)
    bv = np.zeros((1, FOX_HEADS * LANES), np.float32)
    for h in range(FOX_HEADS):
        bq[0, h * LANES + HEAD_DIM:h * LANES + HEAD_DIM + 3] = 1.0
        bv[0, h * LANES + (HEAD_DIM if h % 2 == 0 else 0)] = 1.0
    bvs = np.zeros((1, NSA_KV_GROUPS * 2 * LANES), np.float32)
    for g in range(NSA_KV_GROUPS):
        bvs[0, g * 2 * LANES + HEAD_DIM] = 1.0
        bvs[0, g * 2 * LANES + LANES] = 1.0
    esel = np.zeros((3 * LANES, FOX_HEADS * LANES), np.float32)
    for j in range(3):
        for h in range(FOX_HEADS):
            esel[j * LANES + h, h * LANES + HEAD_DIM + j] = 1.0
    tri = np.tril(np.ones((tm, tm), np.float32))
    bfg = jnp.concatenate([b_forget, jnp.zeros((LANES - FOX_HEADS,), f32)]).reshape(1, LANES)

    out_dtypes = [bf16] * 10 + [f32]
    row_spec = lambda n: pl.BlockSpec((tm, n), lambda i: (i, 0))
    mod_spec = pl.BlockSpec((1, 1, D), lambda i: (i // tiles_per_seq, 0, 0))
    consts = [jnp.asarray(tri, bf16), jnp.asarray(esel, bf16)]
    biases = [jnp.asarray(bq), jnp.asarray(bv), jnp.asarray(bvs)]
    outs = pl.pallas_call(
        functools.partial(_inproj_kernel, tm=tm, tiles_per_seq=tiles_per_seq),
        out_shape=[jax.ShapeDtypeStruct((T, n), dt) for n, dt in zip(widths, out_dtypes)],
        grid=(T // tm,),
        in_specs=[row_spec(D), mod_spec, mod_spec, _const_spec((1, D)), _const_spec((1, LANES))]
        + [_const_spec(c.shape) for c in consts]
        + [_const_spec(w.shape) for w in weights]
        + [_const_spec(b.shape) for b in biases],
        out_specs=[row_spec(n) for n in widths],
        scratch_shapes=[pltpu.VMEM((8, LANES), f32)],
        compiler_params=_cparams("arbitrary"),
        name="inproj",
    )(x2, sc1, sh1, g_pre.reshape(1, D), bfg, *consts, *weights, *biases)
    return outs


def _fox_kernel(q_ref, k_ref, v_ref, o_ref, *, tq, nh):
    i = pl.program_id(2)
    row = lax.broadcasted_iota(i32, (tq, tq), 0)
    col = lax.broadcasted_iota(i32, (tq, tq), 1)
    qs = [q_ref[0, :, hh * LANES:(hh + 1) * LANES] for hh in range(nh)]

    def tile(j, carry, diag):
        start = pl.multiple_of(j * tq, tq)
        new = []
        for hh in range(nh):
            m, acc = carry[hh]
            k = k_ref[0, pl.ds(start, tq), hh * LANES:(hh + 1) * LANES]
            v = v_ref[0, pl.ds(start, tq), hh * LANES:(hh + 1) * LANES]
            s = lax.dot_general(qs[hh], k, _NT, preferred_element_type=f32)
            if diag:
                s = jnp.where(col <= row, s, NEG)
            m_new = jnp.maximum(m, jnp.max(s, axis=-1, keepdims=True))
            p = jnp.exp(s - m_new).astype(bf16)
            acc = jnp.exp(m - m_new) * acc + jnp.dot(p, v, preferred_element_type=f32)
            new.append((m_new, acc))
        return tuple(new)

    carry = tuple((jnp.full((tq, 1), NEG, f32), jnp.zeros((tq, LANES), f32)) for _ in range(nh))
    carry = tile(i, carry, True)
    carry = _loop_tiles(i, lambda j, c: tile(j, c, False), carry)
    o_ref[0] = jnp.concatenate([_pair_out(carry[2 * pr][1], carry[2 * pr + 1][1]) for pr in range(nh // 2)],
                               axis=1).astype(bf16)


def _fox(fq, fk, fv, B, S):
    tq = min(FOX_TILE, S)
    nq = S // tq
    nh = FOX_STEP_HEADS
    q3 = fq.reshape(B, S, FOX_HEADS * LANES)
    k3 = fk.reshape(B, S, FOX_HEADS * LANES)
    v3 = fv.reshape(B, S, FOX_HEADS * LANES)
    return pl.pallas_call(
        functools.partial(_fox_kernel, tq=tq, nh=nh),
        out_shape=jax.ShapeDtypeStruct((B, S, FOX_WIDTH), bf16),
        grid=(B, FOX_HEADS // nh, nq),
        in_specs=[pl.BlockSpec((1, tq, nh * LANES), lambda b, hp, i: (b, i, hp)),
                  pl.BlockSpec((1, S, nh * LANES), lambda b, hp, i: (b, 0, hp)),
                  pl.BlockSpec((1, S, nh * LANES), lambda b, hp, i: (b, 0, hp))],
        out_specs=pl.BlockSpec((1, tq, nh // 2 * LANES), lambda b, hp, i: (b, i, hp)),
        compiler_params=_cparams("parallel", "parallel", "arbitrary"),
        name="fox",
    )(q3, k3, v3)


def _compress_kernel(x_ref, pea_ref, peb_ref, wa_ref, wb_ref, okc, ovc, *, nc):
    x = x_ref[0].astype(f32)
    xa = (x + pea_ref[...]).astype(bf16)
    xb = (x + peb_ref[...]).astype(bf16)
    a = jnp.dot(xa, wa_ref[...], preferred_element_type=f32)
    b = jnp.dot(xb, wb_ref[...], preferred_element_type=f32)
    out = a + pltpu.roll(b, nc - 1, 0)
    okc[0] = out[:, :2 * LANES].astype(bf16)
    ovc[0] = out[:, 2 * LANES:].astype(bf16)


def _compress(cm, pe_k, pe_v, w_cmp_k, w_cmp_v, B, S):
    nc = S // CMP_STRIDE
    half = CMP_BLOCK // 2
    win = half * 2 * LANES
    x = cm.reshape(B, nc, win)
    wk = w_cmp_k.reshape(CMP_BLOCK, HEAD_DIM, HEAD_DIM)
    wv = w_cmp_v.reshape(CMP_BLOCK, HEAD_DIM, HEAD_DIM)
    H = HEAD_DIM

    def build(wk_h, wv_h):
        z = jnp.zeros((half, H, H), f32)

        def rows(cols):
            return jnp.concatenate([cols.get(c, z) for c in range(12)], axis=2)

        w = jnp.concatenate([rows({1: wk_h}),
                             rows({3: wk_h}),
                             rows({4: wv_h, 7: wv_h}),
                             rows({8: wv_h, 11: wv_h})],
                            axis=1)
        return w.reshape(win, 6 * LANES).astype(bf16)

    wa = build(wk[:half], wv[:half])
    wb = build(wk[half:], wv[half:])

    def pe_row(pk, pv):
        return jnp.concatenate([pk, pk, pv, pv], axis=1).reshape(1, win)

    pea = pe_row(pe_k[:half], pe_v[:half])
    peb = pe_row(pe_k[half:], pe_v[half:])
    return pl.pallas_call(
        functools.partial(_compress_kernel, nc=nc),
        out_shape=[jax.ShapeDtypeStruct((B, nc, 2 * LANES), bf16),
                   jax.ShapeDtypeStruct((B, nc, 4 * LANES), bf16)],
        grid=(B,),
        in_specs=[pl.BlockSpec((1, nc, win), lambda b: (b, 0, 0)),
                  _const_spec((1, win)), _const_spec((1, win)),
                  _const_spec((win, 6 * LANES)), _const_spec((win, 6 * LANES))],
        out_specs=[pl.BlockSpec((1, nc, 2 * LANES), lambda b: (b, 0, 0)),
                   pl.BlockSpec((1, nc, 4 * LANES), lambda b: (b, 0, 0))],
        compiler_params=_cparams("parallel"),
        name="compress",
    )(x, pea, peb, wa, wb)


def _cmpsel_kernel(q_ref, kc_ref, vc_ref, pat_ref, ov_ref, ocmp, osel, *, tq, nc, n_sel, top_n, past):
    i = pl.program_id(1)
    kc = kc_ref[0]
    c0 = i * (tq // CMP_STRIDE) - past
    wio = lax.broadcasted_iota(i32, (LANES, nc), 0)
    cio = lax.broadcasted_iota(i32, (LANES, nc), 1)
    shift = jnp.where(cio == wio + c0, 1.0, 0.0).astype(bf16)
    t = i * tq + lax.broadcasted_iota(i32, (tq, nc), 0)
    cend = lax.broadcasted_iota(i32, (tq, nc), 1) * CMP_STRIDE + (CMP_BLOCK - 1)
    valid = cend <= t
    pcs = jnp.zeros((tq, nc), f32)
    outs = []
    for hh in range(NSA_HPG):
        q = q_ref[0, :, hh * LANES:(hh + 1) * LANES]
        pat = pat_ref[hh]
        pat_hi = pat.astype(bf16)
        pat_lo = (pat - pat_hi.astype(f32)).astype(bf16)
        cb = (jnp.dot(pat_hi, shift, preferred_element_type=f32)
              + jnp.dot(pat_lo, shift, preferred_element_type=f32))
        lc = jnp.where(valid, lax.dot_general(q, kc, _NT, preferred_element_type=f32) + cb, NEG)
        m = jnp.max(lc, axis=-1, keepdims=True)
        p = jnp.where(valid, jnp.exp(lc - m), 0.0)
        l = jnp.sum(p, axis=-1, keepdims=True)
        pc = p * jnp.where(l > 0.0, 1.0 / l, 0.0)
        pcs = pcs + pc
        v = vc_ref[0, :, (hh % 2) * LANES:(hh % 2 + 1) * LANES]
        outs.append(jnp.dot(pc.astype(bf16), v, preferred_element_type=f32))
    ocmp[0] = jnp.concatenate([outs[0] + outs[1], outs[2] + outs[3]], axis=1).astype(bf16)

    hi = pcs.astype(bf16)
    lo = (pcs - hi.astype(f32)).astype(bf16)
    ov = ov_ref[...]
    imp = (lax.dot_general(ov, hi, _NT, preferred_element_type=f32)
           + lax.dot_general(ov, lo, _NT, preferred_element_type=f32))
    jio = lax.broadcasted_iota(i32, (n_sel, tq), 0)
    t = i * tq + lax.broadcasted_iota(i32, (n_sel, tq), 1)
    cur = t // SEL_BLOCK
    forced = (jio == 0) | (jio == cur) | (jio == cur - 1)
    score = jnp.where(forced, BIG, jnp.where(jio <= cur, imp, -BIG))
    picked = jnp.zeros((n_sel, tq), f32)
    jf = jio.astype(f32)
    for _ in range(top_n):
        best = jnp.max(score, axis=0, keepdims=True)
        jbest = jnp.min(jnp.where(score == best, jf, float(n_sel)), axis=0, keepdims=True)
        hit = jf == jbest
        picked = jnp.where(hit, 1.0, picked)
        score = jnp.where(hit, NEG, score)
    selb = jnp.where(picked > 0.0, 0.0, -SEL_MASK)
    padded = jnp.concatenate([selb, jnp.zeros((LANES - n_sel, tq), f32)], axis=0)
    osel[0, 0] = padded.T.astype(bf16)


def _bucket_bounds():
    n = np.arange(0, 4 * REL_MAX_DIST)
    max_exact = REL_BUCKETS // 2
    nf = np.maximum(n, 1).astype(np.float32)
    large = max_exact + (np.log(nf / np.float32(max_exact)) / np.float32(np.log(REL_MAX_DIST / max_exact))
                         * np.float32(REL_BUCKETS - max_exact)).astype(np.int32)
    bucket = np.where(n < max_exact, n, np.minimum(large, REL_BUCKETS - 1))
    return [int(n[bucket > b].min()) for b in range(REL_BUCKETS - 1)]


_BOUNDS = _bucket_bounds()
_CMP_PAST = (_BOUNDS[-1] + CMP_BLOCK - 1 + CMP_STRIDE - 1) // CMP_STRIDE - 1


def _rel_bias_of(d, rb_ref, h):
    far = rb_ref[(REL_BUCKETS - 1) * NSA_HEADS + h]
    v = jnp.zeros(d.shape, f32)
    for b in reversed(range(REL_BUCKETS - 1)):
        v = jnp.where(d < _BOUNDS[b], rb_ref[b * NSA_HEADS + h] - far, v)
    return v


def _biasgen_kernel(rb_ref, oslc, owin, ocmp, *, ts, tw, tc, nk):
    h = pl.program_id(0)

    def tile(t, off, window):
        lo, hi = off - (t - 1), off + (t - 1)
        if hi < 0 or (window is not None and lo >= window):
            return jnp.full((t, t), NEG, f32)
        d = lax.broadcasted_iota(i32, (t, t), 0) - lax.broadcasted_iota(i32, (t, t), 1) + off
        val = _rel_bias_of(d, rb_ref, h) if lo < _BOUNDS[-1] else jnp.zeros((t, t), f32)
        if lo < 0:
            val = jnp.where(d >= 0, val, NEG)
        if window is not None and hi >= window:
            val = jnp.where(d < window, val, NEG)
        return val

    oslc[0, 0, 0] = tile(ts, 0, None)
    oslc[0, 1, 0] = tile(ts, ts, None)
    oslc[0, 2, 0] = jnp.full((ts, ts), NEG, f32)
    for v in range(nk):
        for cc in range(nk):
            owin[0, v, 0, :, cc * tw:(cc + 1) * tw] = tile(tw, (v - cc) * tw, WINDOW)
    rr = lax.broadcasted_iota(i32, (tc, LANES), 0)
    w = lax.broadcasted_iota(i32, (tc, LANES), 1)
    d = rr - CMP_STRIDE * (w - _CMP_PAST) - (CMP_BLOCK - 1)
    ocmp[0] = jnp.where(d >= 0, _rel_bias_of(d, rb_ref, h), 0.0)


def _biasgen(rel_bias, S):
    ts, tw, tc = min(SLC_TILE, S), min(WIN_TILE, S), min(CMP_TILE, S)
    nk = WINDOW // tw + 1
    assert WINDOW % tw == 0 and S >= nk * tw and min(ts, tw) + 1 >= _BOUNDS[-1]
    assert tc // CMP_STRIDE + _CMP_PAST <= LANES
    G = NSA_KV_GROUPS
    return pl.pallas_call(
        functools.partial(_biasgen_kernel, ts=ts, tw=tw, tc=tc, nk=nk),
        out_shape=[jax.ShapeDtypeStruct((G, 3, NSA_HPG, ts, ts), f32),
                   jax.ShapeDtypeStruct((G, nk, NSA_HPG, tw, nk * tw), f32),
                   jax.ShapeDtypeStruct((NSA_HEADS, tc, LANES), f32)],
        grid=(NSA_HEADS,),
        in_specs=[pl.BlockSpec(memory_space=pltpu.SMEM)],
        out_specs=[pl.BlockSpec((1, 3, 1, ts, ts), lambda h: (h // NSA_HPG, 0, h % NSA_HPG, 0, 0)),
                   pl.BlockSpec((1, nk, 1, tw, nk * tw), lambda h: (h // NSA_HPG, 0, h % NSA_HPG, 0, 0)),
                   pl.BlockSpec((1, tc, LANES), lambda h: (h, 0, 0))],
        compiler_params=_cparams("arbitrary"),
        name="biasgen",
    )(rel_bias.reshape(-1))


def _cmpsel(nq_arr, kc, vc, pat, B, S):
    tq = min(CMP_TILE, S)
    nqt = S // tq
    nc = S // CMP_STRIDE
    n_sel = S // SEL_BLOCK
    top_n = min(SEL_TOPN, n_sel)
    G = NSA_KV_GROUPS
    assert n_sel <= HEAD_DIM
    c = np.arange(nc)[None, :]
    j = np.arange(n_sel)[:, None]
    ov = ((c * CMP_STRIDE < j * SEL_BLOCK + SEL_BLOCK) & (c * CMP_STRIDE + CMP_BLOCK > j * SEL_BLOCK)
          & (c < nc - 1)).astype(np.float32)
    q3 = nq_arr.reshape(B, S, NSA_HEADS * LANES)
    return pl.pallas_call(
        functools.partial(_cmpsel_kernel, tq=tq, nc=nc, n_sel=n_sel, top_n=top_n, past=_CMP_PAST),
        out_shape=[jax.ShapeDtypeStruct((B, S, NSA_HEADS * HEAD_DIM), bf16),
                   jax.ShapeDtypeStruct((B, G, S, LANES), bf16)],
        grid=(G, nqt, B),
        in_specs=[pl.BlockSpec((1, tq, NSA_HPG * LANES), lambda g, i, b: (b, i, g)),
                  pl.BlockSpec((1, nc, LANES), lambda g, i, b: (b, 0, g)),
                  pl.BlockSpec((1, nc, 2 * LANES), lambda g, i, b: (b, 0, g)),
                  pl.BlockSpec((NSA_HPG, tq, LANES), lambda g, i, b: (g, 0, 0)),
                  _const_spec((n_sel, nc))],
        out_specs=[pl.BlockSpec((1, tq, 2 * LANES), lambda g, i, b: (b, i, g)),
                   pl.BlockSpec((1, 1, tq, LANES), lambda g, i, b: (b, g, i, 0))],
        compiler_params=_cparams("parallel", "arbitrary", "arbitrary"),
        name="cmpsel",
    )(q3, kc, vc, pat, jnp.asarray(ov, bf16))


def _pair_out(acc_e, acc_o):
    lane = lax.broadcasted_iota(i32, acc_e.shape, 1)
    return jnp.where(lane < HEAD_DIM, acc_e / acc_e[:, HEAD_DIM:HEAD_DIM + 1], acc_o / acc_o[:, 0:1])


def _slc_kernel(q_ref, sb_ref, k_ref, v_ref, bias_ref, o_ref, *, tq):
    i = pl.program_id(2)
    sb = sb_ref[0, 0]
    qs = [q_ref[0, :, h * LANES:(h + 1) * LANES] + sb for h in range(NSA_HPG)]

    def tile(j, carry, bias_idx):
        start = pl.multiple_of(j * tq, tq)
        k = k_ref[0, pl.ds(start, tq), :]
        v = v_ref[0, pl.ds(start, tq), :]
        new = []
        for h in range(NSA_HPG):
            m, acc = carry[h]
            s = lax.dot_general(qs[h], k, _NT, preferred_element_type=f32)
            if bias_idx is not None:
                s = s + bias_ref[0, bias_idx, h]
            m_new = jnp.maximum(m, jnp.max(s, axis=-1, keepdims=True))
            p = jnp.exp(s - m_new).astype(bf16)
            vh = v[:, (h % 2) * LANES:(h % 2 + 1) * LANES]
            acc = jnp.exp(m - m_new) * acc + jnp.dot(p, vh, preferred_element_type=f32)
            new.append((m_new, acc))
        return tuple(new)

    carry = tuple((jnp.full((tq, 1), NEG, f32), jnp.zeros((tq, LANES), f32)) for _ in range(NSA_HPG))
    carry = tile(i, carry, 0)
    carry = tile(jnp.maximum(i - 1, 0), carry, jnp.where(i >= 1, 1, 2))
    carry = _loop_tiles(jnp.maximum(i - 1, 0), lambda j, c: tile(j, c, None), carry)
    o_ref[0] = jnp.concatenate([_pair_out(carry[0][1], carry[1][1]),
                                _pair_out(carry[2][1], carry[3][1])], axis=1).astype(bf16)


def _win_kernel(q_ref, k_ref, v_ref, bias_ref, o_ref, *, tq, nk):
    i = pl.program_id(2)
    start = pl.multiple_of(jnp.maximum(i - (nk - 1), 0) * tq, tq)
    k = k_ref[0, pl.ds(start, nk * tq), :]
    v = v_ref[0, pl.ds(start, nk * tq), :]
    accs = []
    for h in range(NSA_HPG):
        q = q_ref[0, :, h * LANES:(h + 1) * LANES]
        s = lax.dot_general(q, k, _NT, preferred_element_type=f32) + bias_ref[0, 0, h]
        p = jnp.exp(s - jnp.max(s, axis=-1, keepdims=True)).astype(bf16)
        accs.append(jnp.dot(p, v[:, (h % 2) * LANES:(h % 2 + 1) * LANES], preferred_element_type=f32))
    o_ref[0] = jnp.concatenate([_pair_out(accs[0], accs[1]), _pair_out(accs[2], accs[3])], axis=1).astype(bf16)


def _nsa_flash(kind, nq_arr, selb, k_arr, v_arr, bias, B, S):
    tq = bias.shape[3]
    nqt = S // tq
    G = NSA_KV_GROUPS
    q3 = nq_arr.reshape(B, S, NSA_HEADS * LANES)
    k3 = k_arr.reshape(B, S, G * LANES)
    v3 = v_arr.reshape(B, S, G * 2 * LANES)
    q_spec = pl.BlockSpec((1, tq, NSA_HPG * LANES), lambda g, b, i: (b, i, g))
    k_spec = pl.BlockSpec((1, S, LANES), lambda g, b, i: (b, 0, g))
    v_spec = pl.BlockSpec((1, S, 2 * LANES), lambda g, b, i: (b, 0, g))
    if kind == "slc":
        kern = functools.partial(_slc_kernel, tq=tq)
        extra_specs = [pl.BlockSpec((1, 1, tq, LANES), lambda g, b, i: (b, g, i, 0))]
        extra = [selb]
        b_spec = pl.BlockSpec((1, 3, NSA_HPG, tq, tq), lambda g, b, i: (g, 0, 0, 0, 0),
                              pipeline_mode=pl.Buffered(1))
    else:
        nk = bias.shape[1]
        kern = functools.partial(_win_kernel, tq=tq, nk=nk)
        extra_specs, extra = [], []
        b_spec = pl.BlockSpec((1, 1, NSA_HPG, tq, nk * tq), lambda g, b, i: (g, jnp.minimum(i, nk - 1), 0, 0, 0))
    return pl.pallas_call(
        kern,
        out_shape=jax.ShapeDtypeStruct((B, S, NSA_HEADS * HEAD_DIM), bf16),
        grid=(G, B, nqt),
        in_specs=[q_spec] + extra_specs + [k_spec, v_spec, b_spec],
        out_specs=pl.BlockSpec((1, tq, 2 * LANES), lambda g, b, i: (b, i, g)),
        compiler_params=_cparams("parallel", "parallel", "arbitrary"),
        name=kind,
    )(q3, *extra, k3, v3, bias)


def _post_kernel(x_ref, ofox, ocmp, oslc, owin, mg_ref, sm_ref, ga1, sc2, sh2, gpost, gpre,
                 wfp, wnp_, wmo, wr, br, eg,
                 x1_ref, h2_ref, route_ref, cnt_ref, carry_ref, *, tm):
    i = pl.program_id(0)
    W = NSA_WIDTH
    n = tm

    def slab(r0):
        rs = pl.ds(r0, n)
        gates = jax.nn.sigmoid(sm_ref[rs, :]).astype(bf16)
        gx = jnp.dot(gates, eg[...], preferred_element_type=f32)
        nsa = (gx[:, :W] * ocmp[rs, :].astype(f32) + gx[:, W:2 * W] * oslc[rs, :].astype(f32)
               + gx[:, 2 * W:] * owin[rs, :].astype(f32))
        y_nsa = jnp.dot(nsa.astype(bf16), wnp_[...], preferred_element_type=f32)
        y_fox = jnp.dot(ofox[rs, :], wfp[...], preferred_element_type=f32)
        mg = mg_ref[rs, :].astype(f32)
        mix = (mg[:, :D_MODEL] * y_fox + mg[:, D_MODEL:] * y_nsa).astype(bf16)
        mixed = jnp.dot(mix, wmo[...], preferred_element_type=f32)
        x1 = x_ref[rs, :] + ga1[0] * _rms(mixed, gpost[...])
        x1_ref[rs, :] = x1
        h2 = _rms(x1, gpre[...]) * (1.0 + sc2[0]) + sh2[0]
        _store_tile_rows(h2_ref.at[pl.ds(r0 * ROW_TILES, n * ROW_TILES), :], h2)

        lane = lax.broadcasted_iota(i32, (n, LANES), 1)
        logits = jnp.dot(h2.astype(bf16), wr[...], preferred_element_type=f32) + br[...]
        l = jnp.where(lane < N_EXPERTS, logits, NEG)
        vals, idxs = [], []
        for _ in range(TOP_K):
            m = jnp.max(l, axis=-1, keepdims=True)
            idx = jnp.min(jnp.where(l == m, lane, LANES), axis=-1, keepdims=True)
            vals.append(m)
            idxs.append(idx)
            l = jnp.where(lane == idx, NEG, l)
        es = [jnp.exp(v - vals[0]) for v in vals]
        den = es[0] + es[1] + es[2] + es[3]
        route = jnp.zeros((n, LANES), f32)
        for k in range(TOP_K):
            route = jnp.where(lane == k, idxs[k].astype(f32), route)
            route = jnp.where(lane == 2 * TOP_K + k, es[k] / den, route)
        route_ref[rs, :] = route
        cnt = sum(jnp.where(lane == idx, 1.0, 0.0) for idx in idxs)
        return jnp.sum(cnt, axis=0, keepdims=True)

    @pl.when(i == 0)
    def _():
        carry_ref[...] = jnp.zeros_like(carry_ref)

    new_carry = carry_ref[0:1, :] + slab(0)
    carry_ref[0:1, :] = new_carry
    cnt_ref[...] = jnp.broadcast_to(new_carry, cnt_ref.shape)


def _post(x2, ofox, ocmp, oslc, owin, mg, sm, ga1, sc2, sh2, g_post, g_pre2,
          w_fox_proj, w_nsa_proj, w_mix_out, w_router, b_router, S):
    T, D = x2.shape
    tm = min(ROW_TILE, S)
    tiles_per_seq = S // tm
    W = NSA_WIDTH
    eg = np.zeros((LANES, 3 * W), np.float32)
    for h in range(NSA_HEADS):
        for k in range(3):
            eg[FOX_HEADS + 3 * h + k, k * W + h * HEAD_DIM:k * W + (h + 1) * HEAD_DIM] = 1.0
    wr = jnp.concatenate([w_router, jnp.zeros((D, LANES - N_EXPERTS), f32)], axis=1).astype(bf16)
    br = jnp.concatenate([b_router, jnp.zeros((LANES - N_EXPERTS,), f32)]).reshape(1, LANES)
    row = lambda n: pl.BlockSpec((tm, n), lambda i: (i, 0))
    mod = pl.BlockSpec((1, 1, D), lambda i: (i // tiles_per_seq, 0, 0))
    consts = [w_fox_proj.astype(bf16), w_nsa_proj.astype(bf16), w_mix_out.astype(bf16), wr, br,
              jnp.asarray(eg, bf16)]
    return pl.pallas_call(
        functools.partial(_post_kernel, tm=tm),
        out_shape=[jax.ShapeDtypeStruct((T, D), f32), jax.ShapeDtypeStruct((T * ROW_TILES, LANES), f32),
                   jax.ShapeDtypeStruct((T, LANES), f32), jax.ShapeDtypeStruct((8, LANES), f32)],
        grid=(T // tm,),
        in_specs=[row(D), row(FOX_WIDTH), row(W), row(W), row(W), row(2 * D), row(LANES),
                  mod, mod, mod, _const_spec((1, D)), _const_spec((1, D))]
        + [_const_spec(c.shape) for c in consts],
        out_specs=[row(D), pl.BlockSpec((tm * ROW_TILES, LANES), lambda i: (i, 0)), row(LANES),
                   pl.BlockSpec((8, LANES), lambda i: (0, 0))],
        scratch_shapes=[pltpu.VMEM((8, LANES), f32)],
        compiler_params=_cparams("arbitrary"),
        name="post",
    )(x2, ofox, ocmp, oslc, owin, mg, sm, ga1, sc2, sh2, g_post.reshape(1, D), g_pre2.reshape(1, D), *consts)


def _moe_kernel(be_ref, nu_ref, tok0_ref, tokn_ref, dstp_ref, h_hbm, wg, bg, wu, bu, wd, bd,
                y_hbm, xbuf, ybuf, wgb, wub, wdb, gsem, ssem, *, bm):
    i = pl.program_id(0)
    nu = nu_ref[0]
    rt = ROW_TILES

    def rows(r):
        return pl.ds(r * rt, rt) if isinstance(r, int) else pl.ds(pl.multiple_of(r * rt, rt), rt)

    def gather_copy(tok_ref, r, s):
        return pltpu.make_async_copy(h_hbm.at[tok_ref[0, 0, r]], xbuf.at[s, rows(r), :], gsem.at[s])

    def scatter_copy(r, s):
        return pltpu.make_async_copy(ybuf.at[s, rows(r), :], y_hbm.at[dstp_ref[0, 0, r]], ssem.at[s])

    def wait_gather(s):
        pltpu.make_async_copy(xbuf.at[s], xbuf.at[s], gsem.at[s]).wait()

    def wait_scatter(s):
        pltpu.make_async_copy(ybuf.at[s], ybuf.at[s], ssem.at[s]).wait()

    @pl.when(i == 0)
    def _():
        ybuf[1] = jnp.zeros((bm * rt, LANES), f32)
        n_real = y_hbm.shape[0] - 2 * bm

        def issue(r, c):
            gather_copy(tok0_ref, r, 0).start()
            for half in range(2):
                pltpu.make_async_copy(ybuf.at[1, rows(r), :], y_hbm.at[n_real + half * bm + r],
                                      ssem.at[1]).start()
            return c

        lax.fori_loop(0, bm, issue, 0)
        for half in range(2):
            wait_scatter(1)

    def step(slot):
        other = 1 - slot
        wait_gather(slot)

        @pl.when(i >= 1)
        def _():
            wait_scatter(slot)

        @pl.when(jnp.logical_or(i == 0, be_ref[i] != be_ref[jnp.maximum(i - 1, 0)]))
        def _():
            wgb[...] = wg[0].astype(bf16)
            wub[...] = wu[0].astype(bf16)
            wdb[...] = wd[0].astype(bf16)

        for r in range(bm):
            gather_copy(tokn_ref, r, other).start(priority=r % 2)
            scatter_copy(r, other).start(priority=(r + 1) % 2)
        x = _load_tile_rows(xbuf.at[slot], bm).astype(bf16)
        g = jnp.dot(x, wgb[...], preferred_element_type=f32) + bg[0]
        u = jnp.dot(x, wub[...], preferred_element_type=f32) + bu[0]
        gt = jnp.minimum(g, SWIGLU_LIMIT)
        up = jnp.clip(u, -SWIGLU_LIMIT, SWIGLU_LIMIT)
        a = (gt * jax.nn.sigmoid(SWIGLU_ALPHA * gt) * (up + 1.0)).astype(bf16)
        _store_tile_rows(ybuf.at[slot], jnp.dot(a, wdb[...], preferred_element_type=f32) + bd[0])

    def drain(slot):
        other = 1 - slot
        wait_gather(slot)
        wait_scatter(slot)

        def issue(r, c):
            scatter_copy(r, other).start()
            return c

        lax.fori_loop(0, bm, issue, 0)
        wait_scatter(other)

    for s in range(2):
        pl.when(jnp.logical_and(i < nu, i % 2 == s))(functools.partial(step, s))
        pl.when(jnp.logical_and(i == nu, i % 2 == s))(functools.partial(drain, s))


def _moe(blk_e, n_used, row_tok, row_dst, h2t, w_gate, b_gate, w_up, b_up, w_down, b_down, n_rows):
    D = D_MODEL
    E, _, F = w_gate.shape
    nbt = row_tok.shape[0]
    bm = MOE_BM
    wsel = lambda i, be, nu: (be[jnp.minimum(i, nu[0] - 1)], 0, 0)
    idx_spec = lambda f: pl.BlockSpec((1, 1, bm), lambda i, be, nu: (f(i), 0, 0), memory_space=pltpu.SMEM)
    return pl.pallas_call(
        functools.partial(_moe_kernel, bm=bm),
        out_shape=jax.ShapeDtypeStruct((n_rows, ROW_TILES, LANES), f32),
        grid_spec=pltpu.PrefetchScalarGridSpec(
            num_scalar_prefetch=2, grid=(nbt,),
            in_specs=[idx_spec(lambda i: 0),
                      idx_spec(lambda i: jnp.minimum(i + 1, nbt - 1)),
                      idx_spec(lambda i: jnp.maximum(i - 1, 0)),
                      pl.BlockSpec(memory_space=pl.ANY),
                      pl.BlockSpec((1, D, F), wsel), pl.BlockSpec((1, 1, F), wsel),
                      pl.BlockSpec((1, D, F), wsel), pl.BlockSpec((1, 1, F), wsel),
                      pl.BlockSpec((1, F, D), wsel), pl.BlockSpec((1, 1, D), wsel)],
            out_specs=pl.BlockSpec(memory_space=pl.ANY),
            scratch_shapes=[pltpu.VMEM((2, bm * ROW_TILES, LANES), f32), pltpu.VMEM((2, bm * ROW_TILES, LANES), f32),
                            pltpu.VMEM((D, F), bf16), pltpu.VMEM((D, F), bf16), pltpu.VMEM((F, D), bf16),
                            pltpu.SemaphoreType.DMA((2,)), pltpu.SemaphoreType.DMA((2,))]),
        compiler_params=_cparams("arbitrary"),
        name="moe",
    )(blk_e, n_used, row_tok, row_tok, row_dst, h2t.reshape(-1, ROW_TILES, LANES),
      w_gate, b_gate.reshape(E, 1, F), w_up, b_up.reshape(E, 1, F), w_down, b_down.reshape(E, 1, D))


def _combine_kernel(y0, y1, y2, y3, route_ref, x1_ref, ga2, gpost, o_ref):
    route = route_ref[...]
    tm = o_ref.shape[0]
    y = jnp.zeros(o_ref.shape, f32)
    for k, yk in enumerate((y0, y1, y2, y3)):
        y = y + route[:, 2 * TOP_K + k:2 * TOP_K + k + 1] * _load_tile_rows(yk, tm)
    o_ref[...] = x1_ref[...] + ga2[0] * _rms(y, gpost[...])


def _combine(y4, route, x1, ga2, g_post2, S):
    T, D = x1.shape
    tm = min(MOE_ROWS, S)
    tiles_per_seq = S // tm
    nt = T // tm
    y4 = y4.reshape(-1, LANES)
    y_spec = lambda k: pl.BlockSpec((tm * ROW_TILES, LANES), lambda i: (k * nt + i, 0))
    return pl.pallas_call(
        _combine_kernel,
        out_shape=jax.ShapeDtypeStruct((T, D), f32),
        grid=(nt,),
        in_specs=[y_spec(k) for k in range(TOP_K)]
        + [pl.BlockSpec((tm, LANES), lambda i: (i, 0)),
           pl.BlockSpec((tm, D), lambda i: (i, 0)),
           pl.BlockSpec((1, 1, D), lambda i: (i // tiles_per_seq, 0, 0)),
           pl.BlockSpec((1, D), lambda i: (0, 0))],
        out_specs=pl.BlockSpec((tm, D), lambda i: (i, 0)),
        compiler_params=_cparams("parallel"),
        name="combine",
    )(y4, y4, y4, y4, route, x1, ga2, g_post2.reshape(1, D))


def kernel(x, c, w_ada, b_ada, g_mix_pre, g_mix_post, w_in, b_forget, pe_k, pe_v, w_cmp_k, w_cmp_v, w_fox_proj, w_nsa_proj, w_mix_out, rel_bias, g_ffn_pre, g_ffn_post, w_router, b_router, w_gate, b_gate, w_up, b_up, w_down, b_down):
    B, S, D = x.shape
    T = B * S
    for l in range(w_ada.shape[0]):
        x2 = x.reshape(T, D)
        ada = _ada(c, w_ada[l], b_ada[l])
        sh1, sc1, ga1, sh2, sc2, ga2 = [a.reshape(B, 1, D) for a in jnp.split(ada, 6, axis=-1)]
        fq, fk, fv, nq, cm, ksl, vsl, kwn, vwn, mg, sm = _inproj(x2, sc1, sh1, g_mix_pre[l], w_in[l], b_forget[l], S)
        o_fox = _fox(fq, fk, fv, B, S)
        kc, vc = _compress(cm, pe_k[l], pe_v[l], w_cmp_k[l], w_cmp_v[l], B, S)
        bias_slc, bias_win, pat_cmp = _biasgen(rel_bias, S)
        o_cmp, selb = _cmpsel(nq, kc, vc, pat_cmp, B, S)
        o_slc = _nsa_flash("slc", nq, selb, ksl, vsl, bias_slc, B, S)
        o_win = _nsa_flash("win", nq, None, kwn, vwn, bias_win, B, S)
        x1, h2, route, cnt = _post(x2, o_fox.reshape(T, -1), o_cmp.reshape(T, -1), o_slc.reshape(T, -1),
                                   o_win.reshape(T, -1), mg, sm, ga1, sc2, sh2, g_mix_post[l], g_ffn_pre[l],
                                   w_fox_proj[l], w_nsa_proj[l], w_mix_out[l], w_router[l], b_router[l], S)
        counts = cnt[0, :N_EXPERTS].astype(i32)
        nblk = (counts + MOE_BM - 1) // MOE_BM
        blk_end = jnp.cumsum(nblk)
        pad_start = (blk_end - nblk) * MOE_BM
        top_i = route[:, :TOP_K].astype(i32)
        A = T * TOP_K
        nbt = -(-A // MOE_BM) + N_EXPERTS + 1
        n_used = blk_end[-1:].astype(i32)
        blk_e = jnp.minimum(jnp.sum(jnp.arange(nbt)[:, None] >= blk_end[None, :], axis=1), N_EXPERTS - 1).astype(i32)
        a_sorted = jnp.sort((top_i * A + jnp.arange(A, dtype=i32).reshape(T, TOP_K)).reshape(-1)) % A
        grp_start = jnp.cumsum(counts) - counts
        j = jnp.arange(MOE_BM, dtype=i32)[None, :]
        b = jnp.arange(nbt, dtype=i32)[:, None]
        r_in_e = b * MOE_BM + j - pad_start[blk_e][:, None]
        valid = (b < n_used[0]) & (r_in_e < counts[blk_e][:, None])
        row_a = a_sorted[jnp.clip(grp_start[blk_e][:, None] + r_in_e, 0, A - 1)]
        row_tok = jnp.where(valid, row_a // TOP_K, 0)
        row_dst = jnp.where(valid, (row_a % TOP_K) * T + row_a // TOP_K, A + (b % 2) * MOE_BM + j)
        y4 = _moe(blk_e, n_used, row_tok.reshape(nbt, 1, MOE_BM), row_dst.reshape(nbt, 1, MOE_BM), h2,
                  w_gate[l], b_gate[l], w_up[l], b_up[l], w_down[l], b_down[l], A + 2 * MOE_BM)
        x = _combine(y4, route, x1, ga2, g_ffn_post[l], S).reshape(B, S, D)
    return x
```

```python
import functools

import numpy as np
import jax
import jax.numpy as jnp
from jax import lax
from jax.experimental import pallas as pl
from jax.experimental.pallas import tpu as pltpu

f32 = jnp.float32
bf16 = jnp.bfloat16
i32 = jnp.int32

D_MODEL = 1024
HEAD_DIM = 64
FOX_HEADS = 8
NSA_HEADS = 8
NSA_KV_GROUPS = 2
NSA_HPG = NSA_HEADS // NSA_KV_GROUPS
FOX_WIDTH = FOX_HEADS * HEAD_DIM
NSA_WIDTH = NSA_HEADS * HEAD_DIM
NSA_KV_WIDTH = NSA_KV_GROUPS * HEAD_DIM
CMP_BLOCK = 32
CMP_STRIDE = 16
SEL_BLOCK = 64
SEL_TOPN = 16
WINDOW = 512
REL_BUCKETS = 32
REL_MAX_DIST = 128
N_EXPERTS = 32
TOP_K = 4
SWIGLU_LIMIT = 7.0
SWIGLU_ALPHA = 1.702
RMS_EPS = 1e-6
NEG = -1e30
BIG = 1e9
IN_SIZES = (FOX_WIDTH, FOX_WIDTH, FOX_WIDTH, FOX_HEADS, NSA_WIDTH,
            NSA_KV_WIDTH, NSA_KV_WIDTH, NSA_KV_WIDTH, NSA_KV_WIDTH, NSA_KV_WIDTH, NSA_KV_WIDTH,
            3 * NSA_HEADS, D_MODEL, D_MODEL)

LANES = 128
VMEM_LIMIT = 56 * 1024 * 1024
FOX_TILE = 512
FOX_STEP_HEADS = 8
SLC_TILE = 512
TILES_PER_TRIP = 3
WIN_TILE = 512
CMP_TILE = 1024
ROW_TILE = 512
MOE_BM = 512
MOE_ROWS = 512
SEL_MASK = 1e9

_NT = (((1,), (1,)), ((), ()))


def _cparams(*sem):
    return pltpu.CompilerParams(dimension_semantics=sem, vmem_limit_bytes=VMEM_LIMIT)


def _const_spec(shape):
    nd = len(shape)
    return pl.BlockSpec(shape, lambda *_: (0,) * nd, pipeline_mode=pl.Buffered(1))


def _split3(a):
    a1 = a.astype(bf16)
    r1 = a - a1.astype(f32)
    a2 = r1.astype(bf16)
    a3 = (r1 - a2.astype(f32)).astype(bf16)
    return a1, a2, a3


ROW_TILES = D_MODEL // LANES


def _store_tile_rows(ref, val):
    n = val.shape[0]
    for c in range(ROW_TILES):
        ref[pl.ds(c, n, stride=ROW_TILES), :] = val[:, c * LANES:(c + 1) * LANES]


def _load_tile_rows(ref, n):
    return jnp.concatenate([ref[pl.ds(c, n, stride=ROW_TILES), :] for c in range(ROW_TILES)], axis=1)


def _loop_tiles(n, body, carry):
    u = TILES_PER_TRIP

    def trip(jj, c):
        for t in range(u):
            c = body(u * jj + t, c)
        return c

    carry = lax.fori_loop(0, n // u, trip, carry)
    return lax.fori_loop((n // u) * u, n, body, carry)


def _rms(x, g):
    ms = jnp.mean(x * x, axis=-1, keepdims=True)
    return x * lax.rsqrt(ms + RMS_EPS) * g


def _ada_kernel(c_ref, w_ref, b_ref, o_ref):
    c = c_ref[...]
    s = c * jax.nn.sigmoid(c)
    w = w_ref[...]
    s1, s2, _ = _split3(s)
    w1, w2, _ = _split3(w)
    acc = jnp.dot(s1, w1, preferred_element_type=f32)
    acc += jnp.dot(s1, w2, preferred_element_type=f32)
    acc += jnp.dot(s2, w1, preferred_element_type=f32)
    o_ref[...] = acc + b_ref[...]


def _ada(c, w_ada, b_ada):
    B, D = c.shape
    N = w_ada.shape[1]
    tn = 1024
    return pl.pallas_call(
        _ada_kernel,
        out_shape=jax.ShapeDtypeStruct((B, N), f32),
        grid=(N // tn,),
        in_specs=[pl.BlockSpec((B, D), lambda j: (0, 0)),
                  pl.BlockSpec((D, tn), lambda j: (0, j)),
                  pl.BlockSpec((1, tn), lambda j: (0, j))],
        out_specs=pl.BlockSpec((B, tn), lambda j: (0, j)),
        compiler_params=_cparams("arbitrary"),
        name="ada",
    )(c, w_ada, b_ada.reshape(1, N))


def _inproj_kernel(x_ref, sc_ref, sh_ref, g_ref, bfg_ref, tri_ref, esel_ref,
                   wfq, wfk, wfv, wnq, wcm, wkv, wmg, wsm,
                   bq, bv, bvs,
                   ofq, ofk, ofv, onq, ocm, oksl, ovsl, okwn, ovwn, omg, osm,
                   carry_ref, *, tm, tiles_per_seq):
    i = pl.program_id(0)
    x = x_ref[...]
    h = _rms(x, g_ref[...]) * (1.0 + sc_ref[0]) + sh_ref[0]
    hb = h.astype(bf16)

    def proj(w):
        return jnp.dot(hb, w[...], preferred_element_type=f32)

    low_half = lax.broadcasted_iota(i32, (tm, LANES), 1) < HEAD_DIM

    def place(tile, src_hi, dst_hi):
        t = tile if src_hi == dst_hi else pltpu.roll(tile, HEAD_DIM, 1)
        return jnp.where(low_half, 0.0, t) if dst_hi else jnp.where(low_half, t, 0.0)

    def tiles(c):
        return [c[:, j * LANES:(j + 1) * LANES] for j in range(c.shape[1] // LANES)]

    def heads(c, even_hi, odd_hi):
        out = []
        for t in tiles(c):
            out += [place(t, False, even_hi), place(t, True, odd_hi)]
        return jnp.concatenate(out, axis=1)

    def group_both(t):
        return jnp.concatenate([place(t, False, False), place(t, False, True),
                                place(t, True, False), place(t, True, True)], axis=1)

    ofq[...] = (heads(proj(wfq), False, False) + bq[...]).astype(bf16)
    ofv[...] = (heads(proj(wfv), False, True) + bv[...]).astype(bf16)
    onq[...] = heads(proj(wnq), True, True).astype(bf16)
    ocm[...] = proj(wcm).astype(bf16)
    ksl_c, vsl_c, kwn_c, vwn_c = tiles(proj(wkv))
    ovsl[...] = (group_both(vsl_c) + bvs[...]).astype(bf16)
    okwn[...] = heads(kwn_c, True, True).astype(bf16)
    ovwn[...] = (group_both(vwn_c) + bvs[...]).astype(bf16)
    omg[...] = jax.nn.sigmoid(proj(wmg)).astype(bf16)

    row = lax.broadcasted_iota(i32, (tm, 2 * LANES), 0)
    lane = lax.broadcasted_iota(i32, (tm, 2 * LANES), 1)
    blk = ((i % tiles_per_seq) * tm + row) // SEL_BLOCK
    onehot = jnp.where((lane & (LANES - 1)) == blk, 1.0, 0.0)
    oksl[...] = (heads(ksl_c, True, True) + onehot).astype(bf16)

    sm = proj(wsm)
    osm[...] = sm
    z = sm + bfg_ref[...]
    lane1 = lax.broadcasted_iota(i32, (tm, LANES), 1)
    logf = jnp.where(lane1 < FOX_HEADS, jnp.minimum(z, 0.0) - jnp.log(1.0 + jnp.exp(-jnp.abs(z))), 0.0)

    @pl.when(i % tiles_per_seq == 0)
    def _():
        carry_ref[...] = jnp.zeros_like(carry_ref)

    tri = tri_ref[...]
    cum = carry_ref[0:1, :]
    for piece in _split3(logf):
        cum = cum + jnp.dot(tri, piece, preferred_element_type=f32)
    carry_ref[0:1, :] = cum[tm - 1:tm, :]
    ncat = jnp.concatenate(_split3(-cum), axis=1)
    ofk[...] = (heads(proj(wfk), False, False)
                + jnp.dot(ncat, esel_ref[...], preferred_element_type=f32)).astype(bf16)


def _inproj(x2, sc1, sh1, g_pre, w_in, b_forget, S):
    T, D = x2.shape
    tm = min(ROW_TILE, S)
    tiles_per_seq = S // tm
    offs = np.cumsum(IN_SIZES)[:-1].tolist()
    (wfq, wfk, wfv, wff, wnq, wkcm, wvcm, wksl, wvsl, wkwn, wvwn, wng, wmgf, wmgn) = jnp.split(w_in, offs, axis=1)
    scale = HEAD_DIM ** -0.5
    cast = lambda w: w.astype(bf16)
    weights = [
        cast(wfq * scale),
        cast(wfk),
        cast(wfv),
        cast(wnq * scale),
        cast(jnp.concatenate([wkcm, wvcm], axis=1)),
        cast(jnp.concatenate([wksl, wvsl, wkwn, wvwn], axis=1)),
        cast(jnp.concatenate([wmgf, wmgn], axis=1)),
        cast(jnp.concatenate([wff, wng, jnp.zeros((D, LANES - FOX_HEADS - 3 * NSA_HEADS), f32)], axis=1)),
    ]
    G2 = NSA_KV_GROUPS * LANES
    widths = [FOX_HEADS * LANES] * 3 + [NSA_HEADS * LANES, 2 * NSA_KV_WIDTH, G2, 2 * G2, G2, 2 * G2, 2 * D, LANES]
    bq = np.zeros((1, FOX_HEADS * LANES), np.float32)
    bv = np.zeros((1, FOX_HEADS * LANES), np.float32)
    for h in range(FOX_HEADS):
        bq[0, h * LANES + HEAD_DIM:h * LANES + HEAD_DIM + 3] = 1.0
        bv[0, h * LANES + (HEAD_DIM if h % 2 == 0 else 0)] = 1.0
    bvs = np.zeros((1, NSA_KV_GROUPS * 2 * LANES), np.float32)
    for g in range(NSA_KV_GROUPS):
        bvs[0, g * 2 * LANES + HEAD_DIM] = 1.0
        bvs[0, g * 2 * LANES + LANES] = 1.0
    esel = np.zeros((3 * LANES, FOX_HEADS * LANES), np.float32)
    for j in range(3):
        for h in range(FOX_HEADS):
            esel[j * LANES + h, h * LANES + HEAD_DIM + j] = 1.0
    tri = np.tril(np.ones((tm, tm), np.float32))
    bfg = jnp.concatenate([b_forget, jnp.zeros((LANES - FOX_HEADS,), f32)]).reshape(1, LANES)

    out_dtypes = [bf16] * 10 + [f32]
    row_spec = lambda n: pl.BlockSpec((tm, n), lambda i: (i, 0))
    mod_spec = pl.BlockSpec((1, 1, D), lambda i: (i // tiles_per_seq, 0, 0))
    consts = [jnp.asarray(tri, bf16), jnp.asarray(esel, bf16)]
    biases = [jnp.asarray(bq), jnp.asarray(bv), jnp.asarray(bvs)]
    outs = pl.pallas_call(
        functools.partial(_inproj_kernel, tm=tm, tiles_per_seq=tiles_per_seq),
        out_shape=[jax.ShapeDtypeStruct((T, n), dt) for n, dt in zip(widths, out_dtypes)],
        grid=(T // tm,),
        in_specs=[row_spec(D), mod_spec, mod_spec, _const_spec((1, D)), _const_spec((1, LANES))]
        + [_const_spec(c.shape) for c in consts]
        + [_const_spec(w.shape) for w in weights]
        + [_const_spec(b.shape) for b in biases],
        out_specs=[row_spec(n) for n in widths],
        scratch_shapes=[pltpu.VMEM((8, LANES), f32)],
        compiler_params=_cparams("arbitrary"),
        name="inproj",
    )(x2, sc1, sh1, g_pre.reshape(1, D), bfg, *consts, *weights, *biases)
    return outs


def _fox_kernel(q_ref, k_ref, v_ref, o_ref, *, tq, nh):
    i = pl.program_id(2)
    row = lax.broadcasted_iota(i32, (tq, tq), 0)
    col = lax.broadcasted_iota(i32, (tq, tq), 1)
    qs = [q_ref[0, :, hh * LANES:(hh + 1) * LANES] for hh in range(nh)]

    def tile(j, carry, diag):
        start = pl.multiple_of(j * tq, tq)
        new = []
        for hh in range(nh):
            m, acc = carry[hh]
            k = k_ref[0, pl.ds(start, tq), hh * LANES:(hh + 1) * LANES]
            v = v_ref[0, pl.ds(start, tq), hh * LANES:(hh + 1) * LANES]
            s = lax.dot_general(qs[hh], k, _NT, preferred_element_type=f32)
            if diag:
                s = jnp.where(col <= row, s, NEG)
            m_new = jnp.maximum(m, jnp.max(s, axis=-1, keepdims=True))
            p = jnp.exp((s - m_new).astype(bf16))
            acc = jnp.exp(m - m_new) * acc + jnp.dot(p, v, preferred_element_type=f32)
            new.append((m_new, acc))
        return tuple(new)

    carry = tuple((jnp.full((tq, 1), NEG, f32), jnp.zeros((tq, LANES), f32)) for _ in range(nh))
    carry = tile(i, carry, True)
    carry = _loop_tiles(i, lambda j, c: tile(j, c, False), carry)
    o_ref[0] = jnp.concatenate([_pair_out(carry[2 * pr][1], carry[2 * pr + 1][1]) for pr in range(nh // 2)],
                               axis=1).astype(bf16)


def _fox(fq, fk, fv, B, S):
    tq = min(FOX_TILE, S)
    nq = S // tq
    nh = FOX_STEP_HEADS
    q3 = fq.reshape(B, S, FOX_HEADS * LANES)
    k3 = fk.reshape(B, S, FOX_HEADS * LANES)
    v3 = fv.reshape(B, S, FOX_HEADS * LANES)
    return pl.pallas_call(
        functools.partial(_fox_kernel, tq=tq, nh=nh),
        out_shape=jax.ShapeDtypeStruct((B, S, FOX_WIDTH), bf16),
        grid=(B, FOX_HEADS // nh, nq),
        in_specs=[pl.BlockSpec((1, tq, nh * LANES), lambda b, hp, i: (b, i, hp)),
                  pl.BlockSpec((1, S, nh * LANES), lambda b, hp, i: (b, 0, hp)),
                  pl.BlockSpec((1, S, nh * LANES), lambda b, hp, i: (b, 0, hp))],
        out_specs=pl.BlockSpec((1, tq, nh // 2 * LANES), lambda b, hp, i: (b, i, hp)),
        compiler_params=_cparams("parallel", "parallel", "arbitrary"),
        name="fox",
    )(q3, k3, v3)


def _compress_kernel(x_ref, pea_ref, peb_ref, wa_ref, wb_ref, okc, ovc, *, nc):
    x = x_ref[0].astype(f32)
    xa = (x + pea_ref[...]).astype(bf16)
    xb = (x + peb_ref[...]).astype(bf16)
    a = jnp.dot(xa, wa_ref[...], preferred_element_type=f32)
    b = jnp.dot(xb, wb_ref[...], preferred_element_type=f32)
    out = a + pltpu.roll(b, nc - 1, 0)
    okc[0] = out[:, :2 * LANES].astype(bf16)
    ovc[0] = out[:, 2 * LANES:].astype(bf16)


def _compress(cm, pe_k, pe_v, w_cmp_k, w_cmp_v, B, S):
    nc = S // CMP_STRIDE
    half = CMP_BLOCK // 2
    win = half * 2 * LANES
    x = cm.reshape(B, nc, win)
    wk = w_cmp_k.reshape(CMP_BLOCK, HEAD_DIM, HEAD_DIM)
    wv = w_cmp_v.reshape(CMP_BLOCK, HEAD_DIM, HEAD_DIM)
    H = HEAD_DIM

    def build(wk_h, wv_h):
        z = jnp.zeros((half, H, H), f32)

        def rows(cols):
            return jnp.concatenate([cols.get(c, z) for c in range(12)], axis=2)

        w = jnp.concatenate([rows({1: wk_h}),
                             rows({3: wk_h}),
                             rows({4: wv_h, 7: wv_h}),
                             rows({8: wv_h, 11: wv_h})],
                            axis=1)
        return w.reshape(win, 6 * LANES).astype(bf16)

    wa = build(wk[:half], wv[:half])
    wb = build(wk[half:], wv[half:])

    def pe_row(pk, pv):
        return jnp.concatenate([pk, pk, pv, pv], axis=1).reshape(1, win)

    pea = pe_row(pe_k[:half], pe_v[:half])
    peb = pe_row(pe_k[half:], pe_v[half:])
    return pl.pallas_call(
        functools.partial(_compress_kernel, nc=nc),
        out_shape=[jax.ShapeDtypeStruct((B, nc, 2 * LANES), bf16),
                   jax.ShapeDtypeStruct((B, nc, 4 * LANES), bf16)],
        grid=(B,),
        in_specs=[pl.BlockSpec((1, nc, win), lambda b: (b, 0, 0)),
                  _const_spec((1, win)), _const_spec((1, win)),
                  _const_spec((win, 6 * LANES)), _const_spec((win, 6 * LANES))],
        out_specs=[pl.BlockSpec((1, nc, 2 * LANES), lambda b: (b, 0, 0)),
                   pl.BlockSpec((1, nc, 4 * LANES), lambda b: (b, 0, 0))],
        compiler_params=_cparams("parallel"),
        name="compress",
    )(x, pea, peb, wa, wb)


def _cmpsel_kernel(q_ref, kc_ref, vc_ref, pat_ref, ov_ref, ocmp, osel, *, tq, nc, n_sel, top_n, past):
    i = pl.program_id(1)
    kc = kc_ref[0]
    c0 = i * (tq // CMP_STRIDE) - past
    wio = lax.broadcasted_iota(i32, (LANES, nc), 0)
    cio = lax.broadcasted_iota(i32, (LANES, nc), 1)
    shift = jnp.where(cio == wio + c0, 1.0, 0.0).astype(bf16)
    t = i * tq + lax.broadcasted_iota(i32, (tq, nc), 0)
    cend = lax.broadcasted_iota(i32, (tq, nc), 1) * CMP_STRIDE + (CMP_BLOCK - 1)
    valid = cend <= t
    pcs = jnp.zeros((tq, nc), f32)
    outs = []
    for hh in range(NSA_HPG):
        q = q_ref[0, :, hh * LANES:(hh + 1) * LANES]
        pat = pat_ref[hh]
        pat_hi = pat.astype(bf16)
        pat_lo = (pat - pat_hi.astype(f32)).astype(bf16)
        cb = (jnp.dot(pat_hi, shift, preferred_element_type=f32)
              + jnp.dot(pat_lo, shift, preferred_element_type=f32))
        lc = jnp.where(valid, lax.dot_general(q, kc, _NT, preferred_element_type=f32) + cb, NEG)
        m = jnp.max(lc, axis=-1, keepdims=True)
        p = jnp.where(valid, jnp.exp(lc - m), 0.0)
        l = jnp.sum(p, axis=-1, keepdims=True)
        pc = p * jnp.where(l > 0.0, 1.0 / l, 0.0)
        pcs = pcs + pc
        v = vc_ref[0, :, (hh % 2) * LANES:(hh % 2 + 1) * LANES]
        outs.append(jnp.dot(pc.astype(bf16), v, preferred_element_type=f32))
    ocmp[0] = jnp.concatenate([outs[0] + outs[1], outs[2] + outs[3]], axis=1).astype(bf16)

    hi = pcs.astype(bf16)
    lo = (pcs - hi.astype(f32)).astype(bf16)
    ov = ov_ref[...]
    imp = (lax.dot_general(ov, hi, _NT, preferred_element_type=f32)
           + lax.dot_general(ov, lo, _NT, preferred_element_type=f32))
    jio = lax.broadcasted_iota(i32, (n_sel, tq), 0)
    t = i * tq + lax.broadcasted_iota(i32, (n_sel, tq), 1)
    cur = t // SEL_BLOCK
    forced = (jio == 0) | (jio == cur) | (jio == cur - 1)
    score = jnp.where(forced, BIG, jnp.where(jio <= cur, imp, -BIG))
    picked = jnp.zeros((n_sel, tq), f32)
    jf = jio.astype(f32)
    for _ in range(top_n):
        best = jnp.max(score, axis=0, keepdims=True)
        jbest = jnp.min(jnp.where(score == best, jf, float(n_sel)), axis=0, keepdims=True)
        hit = jf == jbest
        picked = jnp.where(hit, 1.0, picked)
        score = jnp.where(hit, NEG, score)
    selb = jnp.where(picked > 0.0, 0.0, -SEL_MASK)
    padded = jnp.concatenate([selb, jnp.zeros((LANES - n_sel, tq), f32)], axis=0)
    osel[0, 0] = padded.T.astype(bf16)


def _bucket_bounds():
    n = np.arange(0, 4 * REL_MAX_DIST)
    max_exact = REL_BUCKETS // 2
    nf = np.maximum(n, 1).astype(np.float32)
    large = max_exact + (np.log(nf / np.float32(max_exact)) / np.float32(np.log(REL_MAX_DIST / max_exact))
                         * np.float32(REL_BUCKETS - max_exact)).astype(np.int32)
    bucket = np.where(n < max_exact, n, np.minimum(large, REL_BUCKETS - 1))
    return [int(n[bucket > b].min()) for b in range(REL_BUCKETS - 1)]


_BOUNDS = _bucket_bounds()
_CMP_PAST = (_BOUNDS[-1] + CMP_BLOCK - 1 + CMP_STRIDE - 1) // CMP_STRIDE - 1


def _rel_bias_of(d, rb_ref, h):
    far = rb_ref[(REL_BUCKETS - 1) * NSA_HEADS + h]
    v = jnp.zeros(d.shape, f32)
    for b in reversed(range(REL_BUCKETS - 1)):
        v = jnp.where(d < _BOUNDS[b], rb_ref[b * NSA_HEADS + h] - far, v)
    return v


def _biasgen_kernel(rb_ref, oslc, owin, ocmp, *, ts, tw, tc, nk):
    h = pl.program_id(0)

    def tile(t, off, window):
        lo, hi = off - (t - 1), off + (t - 1)
        if hi < 0 or (window is not None and lo >= window):
            return jnp.full((t, t), NEG, f32)
        d = lax.broadcasted_iota(i32, (t, t), 0) - lax.broadcasted_iota(i32, (t, t), 1) + off
        val = _rel_bias_of(d, rb_ref, h) if lo < _BOUNDS[-1] else jnp.zeros((t, t), f32)
        if lo < 0:
            val = jnp.where(d >= 0, val, NEG)
        if window is not None and hi >= window:
            val = jnp.where(d < window, val, NEG)
        return val

    oslc[0, 0, 0] = tile(ts, 0, None)
    oslc[0, 1, 0] = tile(ts, ts, None)
    oslc[0, 2, 0] = jnp.full((ts, ts), NEG, f32)
    for v in range(nk):
        for cc in range(nk):
            owin[0, v, 0, :, cc * tw:(cc + 1) * tw] = tile(tw, (v - cc) * tw, WINDOW)
    rr = lax.broadcasted_iota(i32, (tc, LANES), 0)
    w = lax.broadcasted_iota(i32, (tc, LANES), 1)
    d = rr - CMP_STRIDE * (w - _CMP_PAST) - (CMP_BLOCK - 1)
    ocmp[0] = jnp.where(d >= 0, _rel_bias_of(d, rb_ref, h), 0.0)


def _biasgen(rel_bias, S):
    ts, tw, tc = min(SLC_TILE, S), min(WIN_TILE, S), min(CMP_TILE, S)
    nk = WINDOW // tw + 1
    assert WINDOW % tw == 0 and S >= nk * tw and min(ts, tw) + 1 >= _BOUNDS[-1]
    assert tc // CMP_STRIDE + _CMP_PAST <= LANES
    G = NSA_KV_GROUPS
    return pl.pallas_call(
        functools.partial(_biasgen_kernel, ts=ts, tw=tw, tc=tc, nk=nk),
        out_shape=[jax.ShapeDtypeStruct((G, 3, NSA_HPG, ts, ts), f32),
                   jax.ShapeDtypeStruct((G, nk, NSA_HPG, tw, nk * tw), f32),
                   jax.ShapeDtypeStruct((NSA_HEADS, tc, LANES), f32)],
        grid=(NSA_HEADS,),
        in_specs=[pl.BlockSpec(memory_space=pltpu.SMEM)],
        out_specs=[pl.BlockSpec((1, 3, 1, ts, ts), lambda h: (h // NSA_HPG, 0, h % NSA_HPG, 0, 0)),
                   pl.BlockSpec((1, nk, 1, tw, nk * tw), lambda h: (h // NSA_HPG, 0, h % NSA_HPG, 0, 0)),
                   pl.BlockSpec((1, tc, LANES), lambda h: (h, 0, 0))],
        compiler_params=_cparams("arbitrary"),
        name="biasgen",
    )(rel_bias.reshape(-1))


def _cmpsel(nq_arr, kc, vc, pat, B, S):
    tq = min(CMP_TILE, S)
    nqt = S // tq
    nc = S // CMP_STRIDE
    n_sel = S // SEL_BLOCK
    top_n = min(SEL_TOPN, n_sel)
    G = NSA_KV_GROUPS
    assert n_sel <= HEAD_DIM
    c = np.arange(nc)[None, :]
    j = np.arange(n_sel)[:, None]
    ov = ((c * CMP_STRIDE < j * SEL_BLOCK + SEL_BLOCK) & (c * CMP_STRIDE + CMP_BLOCK > j * SEL_BLOCK)
          & (c < nc - 1)).astype(np.float32)
    q3 = nq_arr.reshape(B, S, NSA_HEADS * LANES)
    return pl.pallas_call(
        functools.partial(_cmpsel_kernel, tq=tq, nc=nc, n_sel=n_sel, top_n=top_n, past=_CMP_PAST),
        out_shape=[jax.ShapeDtypeStruct((B, S, NSA_HEADS * HEAD_DIM), bf16),
                   jax.ShapeDtypeStruct((B, G, S, LANES), bf16)],
        grid=(G, nqt, B),
        in_specs=[pl.BlockSpec((1, tq, NSA_HPG * LANES), lambda g, i, b: (b, i, g)),
                  pl.BlockSpec((1, nc, LANES), lambda g, i, b: (b, 0, g)),
                  pl.BlockSpec((1, nc, 2 * LANES), lambda g, i, b: (b, 0, g)),
                  pl.BlockSpec((NSA_HPG, tq, LANES), lambda g, i, b: (g, 0, 0)),
                  _const_spec((n_sel, nc))],
        out_specs=[pl.BlockSpec((1, tq, 2 * LANES), lambda g, i, b: (b, i, g)),
                   pl.BlockSpec((1, 1, tq, LANES), lambda g, i, b: (b, g, i, 0))],
        compiler_params=_cparams("parallel", "arbitrary", "arbitrary"),
        name="cmpsel",
    )(q3, kc, vc, pat, jnp.asarray(ov, bf16))


def _pair_out(acc_e, acc_o):
    lane = lax.broadcasted_iota(i32, acc_e.shape, 1)
    return jnp.where(lane < HEAD_DIM, acc_e / acc_e[:, HEAD_DIM:HEAD_DIM + 1], acc_o / acc_o[:, 0:1])


def _slc_kernel(q_ref, sb_ref, k_ref, v_ref, bias_ref, o_ref, *, tq):
    i = pl.program_id(2)
    sb = sb_ref[0, 0]
    qs = [q_ref[0, :, h * LANES:(h + 1) * LANES] + sb for h in range(NSA_HPG)]

    def tile(j, carry, bias_idx):
        start = pl.multiple_of(j * tq, tq)
        k = k_ref[0, pl.ds(start, tq), :]
        v = v_ref[0, pl.ds(start, tq), :]
        new = []
        for h in range(NSA_HPG):
            m, acc = carry[h]
            s = lax.dot_general(qs[h], k, _NT, preferred_element_type=f32)
            if bias_idx is not None:
                s = s + bias_ref[0, bias_idx, h]
            m_new = jnp.maximum(m, jnp.max(s, axis=-1, keepdims=True))
            p = jnp.exp((s - m_new).astype(bf16))
            vh = v[:, (h % 2) * LANES:(h % 2 + 1) * LANES]
            acc = jnp.exp(m - m_new) * acc + jnp.dot(p, vh, preferred_element_type=f32)
            new.append((m_new, acc))
        return tuple(new)

    carry = tuple((jnp.full((tq, 1), NEG, f32), jnp.zeros((tq, LANES), f32)) for _ in range(NSA_HPG))
    carry = tile(i, carry, 0)
    carry = tile(jnp.maximum(i - 1, 0), carry, jnp.where(i >= 1, 1, 2))
    carry = _loop_tiles(jnp.maximum(i - 1, 0), lambda j, c: tile(j, c, None), carry)
    o_ref[0] = jnp.concatenate([_pair_out(carry[0][1], carry[1][1]),
                                _pair_out(carry[2][1], carry[3][1])], axis=1).astype(bf16)


def _win_kernel(q_ref, k_ref, v_ref, bias_ref, o_ref, *, tq, nk):
    i = pl.program_id(2)
    start = pl.multiple_of(jnp.maximum(i - (nk - 1), 0) * tq, tq)
    k = k_ref[0, pl.ds(start, nk * tq), :]
    v = v_ref[0, pl.ds(start, nk * tq), :]
    accs = []
    for h in range(NSA_HPG):
        q = q_ref[0, :, h * LANES:(h + 1) * LANES]
        s = lax.dot_general(q, k, _NT, preferred_element_type=f32) + bias_ref[0, 0, h]
        p = jnp.exp((s - jnp.max(s, axis=-1, keepdims=True)).astype(bf16))
        accs.append(jnp.dot(p, v[:, (h % 2) * LANES:(h % 2 + 1) * LANES], preferred_element_type=f32))
    o_ref[0] = jnp.concatenate([_pair_out(accs[0], accs[1]), _pair_out(accs[2], accs[3])], axis=1).astype(bf16)


def _nsa_flash(kind, nq_arr, selb, k_arr, v_arr, bias, B, S):
    tq = bias.shape[3]
    nqt = S // tq
    G = NSA_KV_GROUPS
    q3 = nq_arr.reshape(B, S, NSA_HEADS * LANES)
    k3 = k_arr.reshape(B, S, G * LANES)
    v3 = v_arr.reshape(B, S, G * 2 * LANES)
    q_spec = pl.BlockSpec((1, tq, NSA_HPG * LANES), lambda g, b, i: (b, i, g))
    k_spec = pl.BlockSpec((1, S, LANES), lambda g, b, i: (b, 0, g))
    v_spec = pl.BlockSpec((1, S, 2 * LANES), lambda g, b, i: (b, 0, g))
    if kind == "slc":
        kern = functools.partial(_slc_kernel, tq=tq)
        extra_specs = [pl.BlockSpec((1, 1, tq, LANES), lambda g, b, i: (b, g, i, 0))]
        extra = [selb]
        b_spec = pl.BlockSpec((1, 3, NSA_HPG, tq, tq), lambda g, b, i: (g, 0, 0, 0, 0),
                              pipeline_mode=pl.Buffered(1))
    else:
        nk = bias.shape[1]
        kern = functools.partial(_win_kernel, tq=tq, nk=nk)
        extra_specs, extra = [], []
        b_spec = pl.BlockSpec((1, 1, NSA_HPG, tq, nk * tq), lambda g, b, i: (g, jnp.minimum(i, nk - 1), 0, 0, 0))
    return pl.pallas_call(
        kern,
        out_shape=jax.ShapeDtypeStruct((B, S, NSA_HEADS * HEAD_DIM), bf16),
        grid=(G, B, nqt),
        in_specs=[q_spec] + extra_specs + [k_spec, v_spec, b_spec],
        out_specs=pl.BlockSpec((1, tq, 2 * LANES), lambda g, b, i: (b, i, g)),
        compiler_params=_cparams("parallel", "parallel", "arbitrary"),
        name=kind,
    )(q3, *extra, k3, v3, bias)


def _post_kernel(x_ref, ofox, ocmp, oslc, owin, mg_ref, sm_ref, ga1, sc2, sh2, gpost, gpre,
                 wfp, wnp_, wmo, wr, br, eg,
                 x1_ref, h2_ref, route_ref, cnt_ref, carry_ref, *, tm):
    i = pl.program_id(0)
    W = NSA_WIDTH
    n = tm

    def slab(r0):
        rs = pl.ds(r0, n)
        gates = jax.nn.sigmoid(sm_ref[rs, :]).astype(bf16)
        gx = jnp.dot(gates, eg[...], preferred_element_type=f32)
        nsa = (gx[:, :W] * ocmp[rs, :].astype(f32) + gx[:, W:2 * W] * oslc[rs, :].astype(f32)
               + gx[:, 2 * W:] * owin[rs, :].astype(f32))
        y_nsa = jnp.dot(nsa.astype(bf16), wnp_[...], preferred_element_type=f32)
        y_fox = jnp.dot(ofox[rs, :], wfp[...], preferred_element_type=f32)
        mg = mg_ref[rs, :].astype(f32)
        mix = (mg[:, :D_MODEL] * y_fox + mg[:, D_MODEL:] * y_nsa).astype(bf16)
        mixed = jnp.dot(mix, wmo[...], preferred_element_type=f32)
        x1 = x_ref[rs, :] + ga1[0] * _rms(mixed, gpost[...])
        x1_ref[rs, :] = x1
        h2 = _rms(x1, gpre[...]) * (1.0 + sc2[0]) + sh2[0]
        _store_tile_rows(h2_ref.at[pl.ds(r0 * ROW_TILES, n * ROW_TILES), :], h2)

        lane = lax.broadcasted_iota(i32, (n, LANES), 1)
        logits = jnp.dot(h2.astype(bf16), wr[...], preferred_element_type=f32) + br[...]
        l = jnp.where(lane < N_EXPERTS, logits, NEG)
        vals, idxs = [], []
        for _ in range(TOP_K):
            m = jnp.max(l, axis=-1, keepdims=True)
            idx = jnp.min(jnp.where(l == m, lane, LANES), axis=-1, keepdims=True)
            vals.append(m)
            idxs.append(idx)
            l = jnp.where(lane == idx, NEG, l)
        es = [jnp.exp(v - vals[0]) for v in vals]
        den = es[0] + es[1] + es[2] + es[3]
        route = jnp.zeros((n, LANES), f32)
        for k in range(TOP_K):
            route = jnp.where(lane == k, idxs[k].astype(f32), route)
            route = jnp.where(lane == 2 * TOP_K + k, es[k] / den, route)
        route_ref[rs, :] = route
        cnt = sum(jnp.where(lane == idx, 1.0, 0.0) for idx in idxs)
        return jnp.sum(cnt, axis=0, keepdims=True)

    @pl.when(i == 0)
    def _():
        carry_ref[...] = jnp.zeros_like(carry_ref)

    new_carry = carry_ref[0:1, :] + slab(0)
    carry_ref[0:1, :] = new_carry
    cnt_ref[...] = jnp.broadcast_to(new_carry, cnt_ref.shape)


def _post(x2, ofox, ocmp, oslc, owin, mg, sm, ga1, sc2, sh2, g_post, g_pre2,
          w_fox_proj, w_nsa_proj, w_mix_out, w_router, b_router, S):
    T, D = x2.shape
    tm = min(ROW_TILE, S)
    tiles_per_seq = S // tm
    W = NSA_WIDTH
    eg = np.zeros((LANES, 3 * W), np.float32)
    for h in range(NSA_HEADS):
        for k in range(3):
            eg[FOX_HEADS + 3 * h + k, k * W + h * HEAD_DIM:k * W + (h + 1) * HEAD_DIM] = 1.0
    wr = jnp.concatenate([w_router, jnp.zeros((D, LANES - N_EXPERTS), f32)], axis=1).astype(bf16)
    br = jnp.concatenate([b_router, jnp.zeros((LANES - N_EXPERTS,), f32)]).reshape(1, LANES)
    row = lambda n: pl.BlockSpec((tm, n), lambda i: (i, 0))
    mod = pl.BlockSpec((1, 1, D), lambda i: (i // tiles_per_seq, 0, 0))
    consts = [w_fox_proj.astype(bf16), w_nsa_proj.astype(bf16), w_mix_out.astype(bf16), wr, br,
              jnp.asarray(eg, bf16)]
    return pl.pallas_call(
        functools.partial(_post_kernel, tm=tm),
        out_shape=[jax.ShapeDtypeStruct((T, D), f32), jax.ShapeDtypeStruct((T * ROW_TILES, LANES), f32),
                   jax.ShapeDtypeStruct((T, LANES), f32), jax.ShapeDtypeStruct((8, LANES), f32)],
        grid=(T // tm,),
        in_specs=[row(D), row(FOX_WIDTH), row(W), row(W), row(W), row(2 * D), row(LANES),
                  mod, mod, mod, _const_spec((1, D)), _const_spec((1, D))]
        + [_const_spec(c.shape) for c in consts],
        out_specs=[row(D), pl.BlockSpec((tm * ROW_TILES, LANES), lambda i: (i, 0)), row(LANES),
                   pl.BlockSpec((8, LANES), lambda i: (0, 0))],
        scratch_shapes=[pltpu.VMEM((8, LANES), f32)],
        compiler_params=_cparams("arbitrary"),
        name="post",
    )(x2, ofox, ocmp, oslc, owin, mg, sm, ga1, sc2, sh2, g_post.reshape(1, D), g_pre2.reshape(1, D), *consts)


def _moe_kernel(be_ref, nu_ref, tok0_ref, tokn_ref, dstp_ref, h_hbm, wg, bg, wu, bu, wd, bd,
                y_hbm, xbuf, ybuf, wgb, wub, wdb, gsem, ssem, *, bm):
    i = pl.program_id(0)
    nu = nu_ref[0]
    rt = ROW_TILES

    def rows(r):
        return pl.ds(r * rt, rt) if isinstance(r, int) else pl.ds(pl.multiple_of(r * rt, rt), rt)

    def gather_copy(tok_ref, r, s):
        return pltpu.make_async_copy(h_hbm.at[tok_ref[0, 0, r]], xbuf.at[s, rows(r), :], gsem.at[s])

    def scatter_copy(r, s):
        return pltpu.make_async_copy(ybuf.at[s, rows(r), :], y_hbm.at[dstp_ref[0, 0, r]], ssem.at[s])

    def wait_gather(s):
        pltpu.make_async_copy(xbuf.at[s], xbuf.at[s], gsem.at[s]).wait()

    def wait_scatter(s):
        pltpu.make_async_copy(ybuf.at[s], ybuf.at[s], ssem.at[s]).wait()

    @pl.when(i == 0)
    def _():
        ybuf[1] = jnp.zeros((bm * rt, LANES), f32)
        n_real = y_hbm.shape[0] - 2 * bm

        def issue(r, c):
            gather_copy(tok0_ref, r, 0).start()
            for half in range(2):
                pltpu.make_async_copy(ybuf.at[1, rows(r), :], y_hbm.at[n_real + half * bm + r],
                                      ssem.at[1]).start()
            return c

        lax.fori_loop(0, bm, issue, 0)
        for half in range(2):
            wait_scatter(1)

    def step(slot):
        other = 1 - slot
        wait_gather(slot)

        @pl.when(i >= 1)
        def _():
            wait_scatter(slot)

        @pl.when(jnp.logical_or(i == 0, be_ref[i] != be_ref[jnp.maximum(i - 1, 0)]))
        def _():
            wgb[...] = wg[0].astype(bf16)
            wub[...] = wu[0].astype(bf16)
            wdb[...] = wd[0].astype(bf16)

        for r in range(bm):
            gather_copy(tokn_ref, r, other).start(priority=r % 2)
            scatter_copy(r, other).start(priority=(r + 1) % 2)
        x = _load_tile_rows(xbuf.at[slot], bm).astype(bf16)
        g = jnp.dot(x, wgb[...], preferred_element_type=f32) + bg[0]
        u = jnp.dot(x, wub[...], preferred_element_type=f32) + bu[0]
        gt = jnp.minimum(g, SWIGLU_LIMIT)
        up = jnp.clip(u, -SWIGLU_LIMIT, SWIGLU_LIMIT)
        a = (gt * jax.nn.sigmoid(SWIGLU_ALPHA * gt) * (up + 1.0)).astype(bf16)
        _store_tile_rows(ybuf.at[slot], jnp.dot(a, wdb[...], preferred_element_type=f32) + bd[0])

    def drain(slot):
        other = 1 - slot
        wait_gather(slot)
        wait_scatter(slot)

        def issue(r, c):
            scatter_copy(r, other).start()
            return c

        lax.fori_loop(0, bm, issue, 0)
        wait_scatter(other)

    for s in range(2):
        pl.when(jnp.logical_and(i < nu, i % 2 == s))(functools.partial(step, s))
        pl.when(jnp.logical_and(i == nu, i % 2 == s))(functools.partial(drain, s))


def _moe(blk_e, n_used, row_tok, row_dst, h2t, w_gate, b_gate, w_up, b_up, w_down, b_down, n_rows):
    D = D_MODEL
    E, _, F = w_gate.shape
    nbt = row_tok.shape[0]
    bm = MOE_BM
    wsel = lambda i, be, nu: (be[jnp.minimum(i, nu[0] - 1)], 0, 0)
    idx_spec = lambda f: pl.BlockSpec((1, 1, bm), lambda i, be, nu: (f(i), 0, 0), memory_space=pltpu.SMEM)
    return pl.pallas_call(
        functools.partial(_moe_kernel, bm=bm),
        out_shape=jax.ShapeDtypeStruct((n_rows, ROW_TILES, LANES), f32),
        grid_spec=pltpu.PrefetchScalarGridSpec(
            num_scalar_prefetch=2, grid=(nbt,),
            in_specs=[idx_spec(lambda i: 0),
                      idx_spec(lambda i: jnp.minimum(i + 1, nbt - 1)),
                      idx_spec(lambda i: jnp.maximum(i - 1, 0)),
                      pl.BlockSpec(memory_space=pl.ANY),
                      pl.BlockSpec((1, D, F), wsel), pl.BlockSpec((1, 1, F), wsel),
                      pl.BlockSpec((1, D, F), wsel), pl.BlockSpec((1, 1, F), wsel),
                      pl.BlockSpec((1, F, D), wsel), pl.BlockSpec((1, 1, D), wsel)],
            out_specs=pl.BlockSpec(memory_space=pl.ANY),
            scratch_shapes=[pltpu.VMEM((2, bm * ROW_TILES, LANES), f32), pltpu.VMEM((2, bm * ROW_TILES, LANES), f32),
                            pltpu.VMEM((D, F), bf16), pltpu.VMEM((D, F), bf16), pltpu.VMEM((F, D), bf16),
                            pltpu.SemaphoreType.DMA((2,)), pltpu.SemaphoreType.DMA((2,))]),
        compiler_params=_cparams("arbitrary"),
        name="moe",
    )(blk_e, n_used, row_tok, row_tok, row_dst, h2t.reshape(-1, ROW_TILES, LANES),
      w_gate, b_gate.reshape(E, 1, F), w_up, b_up.reshape(E, 1, F), w_down, b_down.reshape(E, 1, D))


def _combine_kernel(y0, y1, y2, y3, route_ref, x1_ref, ga2, gpost, o_ref):
    route = route_ref[...]
    tm = o_ref.shape[0]
    y = jnp.zeros(o_ref.shape, f32)
    for k, yk in enumerate((y0, y1, y2, y3)):
        y = y + route[:, 2 * TOP_K + k:2 * TOP_K + k + 1] * _load_tile_rows(yk, tm)
    o_ref[...] = x1_ref[...] + ga2[0] * _rms(y, gpost[...])


def _combine(y4, route, x1, ga2, g_post2, S):
    T, D = x1.shape
    tm = min(MOE_ROWS, S)
    tiles_per_seq = S // tm
    nt = T // tm
    y4 = y4.reshape(-1, LANES)
    y_spec = lambda k: pl.BlockSpec((tm * ROW_TILES, LANES), lambda i: (k * nt + i, 0))
    return pl.pallas_call(
        _combine_kernel,
        out_shape=jax.ShapeDtypeStruct((T, D), f32),
        grid=(nt,),
        in_specs=[y_spec(k) for k in range(TOP_K)]
        + [pl.BlockSpec((tm, LANES), lambda i: (i, 0)),
           pl.BlockSpec((tm, D), lambda i: (i, 0)),
           pl.BlockSpec((1, 1, D), lambda i: (i // tiles_per_seq, 0, 0)),
           pl.BlockSpec((1, D), lambda i: (0, 0))],
        out_specs=pl.BlockSpec((tm, D), lambda i: (i, 0)),
        compiler_params=_cparams("parallel"),
        name="combine",
    )(y4, y4, y4, y4, route, x1, ga2, g_post2.reshape(1, D))


def kernel(x, c, w_ada, b_ada, g_mix_pre, g_mix_post, w_in, b_forget, pe_k, pe_v, w_cmp_k, w_cmp_v, w_fox_proj, w_nsa_proj, w_mix_out, rel_bias, g_ffn_pre, g_ffn_post, w_router, b_router, w_gate, b_gate, w_up, b_up, w_down, b_down):
    B, S, D = x.shape
    T = B * S
    for l in range(w_ada.shape[0]):
        x2 = x.reshape(T, D)
        ada = _ada(c, w_ada[l], b_ada[l])
        sh1, sc1, ga1, sh2, sc2, ga2 = [a.reshape(B, 1, D) for a in jnp.split(ada, 6, axis=-1)]
        fq, fk, fv, nq, cm, ksl, vsl, kwn, vwn, mg, sm = _inproj(x2, sc1, sh1, g_mix_pre[l], w_in[l], b_forget[l], S)
        o_fox = _fox(fq, fk, fv, B, S)
        kc, vc = _compress(cm, pe_k[l], pe_v[l], w_cmp_k[l], w_cmp_v[l], B, S)
        bias_slc, bias_win, pat_cmp = _biasgen(rel_bias, S)
        o_cmp, selb = _cmpsel(nq, kc, vc, pat_cmp, B, S)
        o_slc = _nsa_flash("slc", nq, selb, ksl, vsl, bias_slc, B, S)
        o_win = _nsa_flash("win", nq, None, kwn, vwn, bias_win, B, S)
        x1, h2, route, cnt = _post(x2, o_fox.reshape(T, -1), o_cmp.reshape(T, -1), o_slc.reshape(T, -1),
                                   o_win.reshape(T, -1), mg, sm, ga1, sc2, sh2, g_mix_post[l], g_ffn_pre[l],
                                   w_fox_proj[l], w_nsa_proj[l], w_mix_out[l], w_router[l], b_router[l], S)
        counts = cnt[0, :N_EXPERTS].astype(i32)
        nblk = (counts + MOE_BM - 1) // MOE_BM
        blk_end = jnp.cumsum(nblk)
        pad_start = (blk_end - nblk) * MOE_BM
        top_i = route[:, :TOP_K].astype(i32)
        A = T * TOP_K
        nbt = -(-A // MOE_BM) + N_EXPERTS + 1
        n_used = blk_end[-1:].astype(i32)
        blk_e = jnp.minimum(jnp.sum(jnp.arange(nbt)[:, None] >= blk_end[None, :], axis=1), N_EXPERTS - 1).astype(i32)
        a_sorted = jnp.sort((top_i * A + jnp.arange(A, dtype=i32).reshape(T, TOP_K)).reshape(-1)) % A
        grp_start = jnp.cumsum(counts) - counts
        j = jnp.arange(MOE_BM, dtype=i32)[None, :]
        b = jnp.arange(nbt, dtype=i32)[:, None]
        r_in_e = b * MOE_BM + j - pad_start[blk_e][:, None]
        valid = (b < n_used[0]) & (r_in_e < counts[blk_e][:, None])
        row_a = a_sorted[jnp.clip(grp_start[blk_e][:, None] + r_in_e, 0, A - 1)]
        row_tok = jnp.where(valid, row_a // TOP_K, 0)
        row_dst = jnp.where(valid, (row_a % TOP_K) * T + row_a // TOP_K, A + (b % 2) * MOE_BM + j)
        y4 = _moe(blk_e, n_used, row_tok.reshape(nbt, 1, MOE_BM), row_dst.reshape(nbt, 1, MOE_BM), h2,
                  w_gate[l], b_gate[l], w_up[l], b_up[l], w_down[l], b_down[l], A + 2 * MOE_BM)
        x = _combine(y4, route, x1, ga2, g_ffn_post[l], S).reshape(B, S, D)
    return x
```

```python
import functools

import numpy as np
import jax
import jax.numpy as jnp
from jax import lax
from jax.experimental import pallas as pl
from jax.experimental.pallas import tpu as pltpu

f32 = jnp.float32
bf16 = jnp.bfloat16
i32 = jnp.int32

D_MODEL = 1024
HEAD_DIM = 64
FOX_HEADS = 8
NSA_HEADS = 8
NSA_KV_GROUPS = 2
NSA_HPG = NSA_HEADS // NSA_KV_GROUPS
FOX_WIDTH = FOX_HEADS * HEAD_DIM
NSA_WIDTH = NSA_HEADS * HEAD_DIM
NSA_KV_WIDTH = NSA_KV_GROUPS * HEAD_DIM
CMP_BLOCK = 32
CMP_STRIDE = 16
SEL_BLOCK = 64
SEL_TOPN = 16
WINDOW = 512
REL_BUCKETS = 32
REL_MAX_DIST = 128
N_EXPERTS = 32
TOP_K = 4
SWIGLU_LIMIT = 7.0
SWIGLU_ALPHA = 1.702
RMS_EPS = 1e-6
NEG = -1e30
BIG = 1e9
IN_SIZES = (FOX_WIDTH, FOX_WIDTH, FOX_WIDTH, FOX_HEADS, NSA_WIDTH,
            NSA_KV_WIDTH, NSA_KV_WIDTH, NSA_KV_WIDTH, NSA_KV_WIDTH, NSA_KV_WIDTH, NSA_KV_WIDTH,
            3 * NSA_HEADS, D_MODEL, D_MODEL)

LANES = 128
VMEM_LIMIT = 56 * 1024 * 1024
FOX_TILE = 512
FOX_STEP_HEADS = 8
SLC_TILE = 512
TILES_PER_TRIP = 3
WIN_TILE = 512
CMP_TILE = 1024
ROW_TILE = 512
MOE_BM = 512
MOE_ROWS = 512
DISPATCH_ROWS = 512
SEL_MASK = 1e9

_NT = (((1,), (1,)), ((), ()))


def _cparams(*sem):
    return pltpu.CompilerParams(dimension_semantics=sem, vmem_limit_bytes=VMEM_LIMIT)


def _const_spec(shape):
    nd = len(shape)
    return pl.BlockSpec(shape, lambda *_: (0,) * nd, pipeline_mode=pl.Buffered(1))


def _split3(a):
    a1 = a.astype(bf16)
    r1 = a - a1.astype(f32)
    a2 = r1.astype(bf16)
    a3 = (r1 - a2.astype(f32)).astype(bf16)
    return a1, a2, a3


ROW_TILES = D_MODEL // LANES


def _store_tile_rows(ref, val):
    n = val.shape[0]
    for c in range(ROW_TILES):
        ref[pl.ds(c, n, stride=ROW_TILES), :] = val[:, c * LANES:(c + 1) * LANES]


def _load_tile_rows(ref, n):
    return jnp.concatenate([ref[pl.ds(c, n, stride=ROW_TILES), :] for c in range(ROW_TILES)], axis=1)


def _loop_tiles(n, body, carry):
    u = TILES_PER_TRIP

    def trip(jj, c):
        for t in range(u):
            c = body(u * jj + t, c)
        return c

    carry = lax.fori_loop(0, n // u, trip, carry)
    return lax.fori_loop((n // u) * u, n, body, carry)


def _rms(x, g):
    ms = jnp.mean(x * x, axis=-1, keepdims=True)
    return x * lax.rsqrt(ms + RMS_EPS) * g


def _ada_kernel(c_ref, w_ref, b_ref, o_ref):
    c = c_ref[...]
    s = c * jax.nn.sigmoid(c)
    w = w_ref[...]
    s1, s2, _ = _split3(s)
    w1, w2, _ = _split3(w)
    acc = jnp.dot(s1, w1, preferred_element_type=f32)
    acc += jnp.dot(s1, w2, preferred_element_type=f32)
    acc += jnp.dot(s2, w1, preferred_element_type=f32)
    o_ref[...] = acc + b_ref[...]


def _ada(c, w_ada, b_ada):
    B, D = c.shape
    N = w_ada.shape[1]
    tn = 1024
    return pl.pallas_call(
        _ada_kernel,
        out_shape=jax.ShapeDtypeStruct((B, N), f32),
        grid=(N // tn,),
        in_specs=[pl.BlockSpec((B, D), lambda j: (0, 0)),
                  pl.BlockSpec((D, tn), lambda j: (0, j)),
                  pl.BlockSpec((1, tn), lambda j: (0, j))],
        out_specs=pl.BlockSpec((B, tn), lambda j: (0, j)),
        compiler_params=_cparams("arbitrary"),
        name="ada",
    )(c, w_ada, b_ada.reshape(1, N))


def _inproj_kernel(x_ref, sc_ref, sh_ref, g_ref, bfg_ref, tri_ref, esel_ref,
                   wfq, wfk, wfv, wnq, wcm, wkv, wmg, wsm,
                   bq, bv, bvs,
                   ofq, ofk, ofv, onq, ocm, oksl, ovsl, okwn, ovwn, omg, osm,
                   carry_ref, *, tm, tiles_per_seq):
    i = pl.program_id(0)
    x = x_ref[...]
    h = _rms(x, g_ref[...]) * (1.0 + sc_ref[0]) + sh_ref[0]
    hb = h.astype(bf16)

    def proj(w):
        return jnp.dot(hb, w[...], preferred_element_type=f32)

    low_half = lax.broadcasted_iota(i32, (tm, LANES), 1) < HEAD_DIM

    def place(tile, src_hi, dst_hi):
        t = tile if src_hi == dst_hi else pltpu.roll(tile, HEAD_DIM, 1)
        return jnp.where(low_half, 0.0, t) if dst_hi else jnp.where(low_half, t, 0.0)

    def tiles(c):
        return [c[:, j * LANES:(j + 1) * LANES] for j in range(c.shape[1] // LANES)]

    def heads(c, even_hi, odd_hi):
        out = []
        for t in tiles(c):
            out += [place(t, False, even_hi), place(t, True, odd_hi)]
        return jnp.concatenate(out, axis=1)

    def group_both(t):
        return jnp.concatenate([place(t, False, False), place(t, False, True),
                                place(t, True, False), place(t, True, True)], axis=1)

    ofq[...] = (heads(proj(wfq), False, False) + bq[...]).astype(bf16)
    ofv[...] = (heads(proj(wfv), False, True) + bv[...]).astype(bf16)
    onq[...] = heads(proj(wnq), True, True).astype(bf16)
    ocm[...] = proj(wcm).astype(bf16)
    ksl_c, vsl_c, kwn_c, vwn_c = tiles(proj(wkv))
    ovsl[...] = (group_both(vsl_c) + bvs[...]).astype(bf16)
    okwn[...] = heads(kwn_c, True, True).astype(bf16)
    ovwn[...] = (group_both(vwn_c) + bvs[...]).astype(bf16)
    omg[...] = jax.nn.sigmoid(proj(wmg)).astype(bf16)

    row = lax.broadcasted_iota(i32, (tm, 2 * LANES), 0)
    lane = lax.broadcasted_iota(i32, (tm, 2 * LANES), 1)
    blk = ((i % tiles_per_seq) * tm + row) // SEL_BLOCK
    onehot = jnp.where((lane & (LANES - 1)) == blk, 1.0, 0.0)
    oksl[...] = (heads(ksl_c, True, True) + onehot).astype(bf16)

    sm = proj(wsm)
    osm[...] = sm
    z = sm + bfg_ref[...]
    lane1 = lax.broadcasted_iota(i32, (tm, LANES), 1)
    logf = jnp.where(lane1 < FOX_HEADS, jnp.minimum(z, 0.0) - jnp.log(1.0 + jnp.exp(-jnp.abs(z))), 0.0)

    @pl.when(i % tiles_per_seq == 0)
    def _():
        carry_ref[...] = jnp.zeros_like(carry_ref)

    tri = tri_ref[...]
    cum = carry_ref[0:1, :]
    for piece in _split3(logf):
        cum = cum + jnp.dot(tri, piece, preferred_element_type=f32)
    carry_ref[0:1, :] = cum[tm - 1:tm, :]
    ncat = jnp.concatenate(_split3(-cum), axis=1)
    ofk[...] = (heads(proj(wfk), False, False)
                + jnp.dot(ncat, esel_ref[...], preferred_element_type=f32)).astype(bf16)


def _inproj(x2, sc1, sh1, g_pre, w_in, b_forget, S):
    T, D = x2.shape
    tm = min(ROW_TILE, S)
    tiles_per_seq = S // tm
    offs = np.cumsum(IN_SIZES)[:-1].tolist()
    (wfq, wfk, wfv, wff, wnq, wkcm, wvcm, wksl, wvsl, wkwn, wvwn, wng, wmgf, wmgn) = jnp.split(w_in, offs, axis=1)
    scale = HEAD_DIM ** -0.5
    cast = lambda w: w.astype(bf16)
    weights = [
        cast(wfq * scale),
        cast(wfk),
        cast(wfv),
        cast(wnq * scale),
        cast(jnp.concatenate([wkcm, wvcm], axis=1)),
        cast(jnp.concatenate([wksl, wvsl, wkwn, wvwn], axis=1)),
        cast(jnp.concatenate([wmgf, wmgn], axis=1)),
        cast(jnp.concatenate([wff, wng, jnp.zeros((D, LANES - FOX_HEADS - 3 * NSA_HEADS), f32)], axis=1)),
    ]
    G2 = NSA_KV_GROUPS * LANES
    widths = [FOX_HEADS * LANES] * 3 + [NSA_HEADS * LANES, 2 * NSA_KV_WIDTH, G2, 2 * G2, G2, 2 * G2, 2 * D, LANES]
    bq = np.zeros((1, FOX_HEADS * LANES), np.float32)
    bv = np.zeros((1, FOX_HEADS * LANES), np.float32)
    for h in range(FOX_HEADS):
        bq[0, h * LANES + HEAD_DIM:h * LANES + HEAD_DIM + 3] = 1.0
        bv[0, h * LANES + (HEAD_DIM if h % 2 == 0 else 0)] = 1.0
    bvs = np.zeros((1, NSA_KV_GROUPS * 2 * LANES), np.float32)
    for g in range(NSA_KV_GROUPS):
        bvs[0, g * 2 * LANES + HEAD_DIM] = 1.0
        bvs[0, g * 2 * LANES + LANES] = 1.0
    esel = np.zeros((3 * LANES, FOX_HEADS * LANES), np.float32)
    for j in range(3):
        for h in range(FOX_HEADS):
            esel[j * LANES + h, h * LANES + HEAD_DIM + j] = 1.0
    tri = np.tril(np.ones((tm, tm), np.float32))
    bfg = jnp.concatenate([b_forget, jnp.zeros((LANES - FOX_HEADS,), f32)]).reshape(1, LANES)

    out_dtypes = [bf16] * 10 + [f32]
    row_spec = lambda n: pl.BlockSpec((tm, n), lambda i: (i, 0))
    mod_spec = pl.BlockSpec((1, 1, D), lambda i: (i // tiles_per_seq, 0, 0))
    consts = [jnp.asarray(tri, bf16), jnp.asarray(esel, bf16)]
    biases = [jnp.asarray(bq), jnp.asarray(bv), jnp.asarray(bvs)]
    outs = pl.pallas_call(
        functools.partial(_inproj_kernel, tm=tm, tiles_per_seq=tiles_per_seq),
        out_shape=[jax.ShapeDtypeStruct((T, n), dt) for n, dt in zip(widths, out_dtypes)],
        grid=(T // tm,),
        in_specs=[row_spec(D), mod_spec, mod_spec, _const_spec((1, D)), _const_spec((1, LANES))]
        + [_const_spec(c.shape) for c in consts]
        + [_const_spec(w.shape) for w in weights]
        + [_const_spec(b.shape) for b in biases],
        out_specs=[row_spec(n) for n in widths],
        scratch_shapes=[pltpu.VMEM((8, LANES), f32)],
        compiler_params=_cparams("arbitrary"),
        name="inproj",
    )(x2, sc1, sh1, g_pre.reshape(1, D), bfg, *consts, *weights, *biases)
    return outs


def _fox_kernel(q_ref, k_ref, v_ref, o_ref, *, tq, nh):
    i = pl.program_id(2)
    row = lax.broadcasted_iota(i32, (tq, tq), 0)
    col = lax.broadcasted_iota(i32, (tq, tq), 1)
    qs = [q_ref[0, :, hh * LANES:(hh + 1) * LANES] for hh in range(nh)]

    def tile(j, carry, diag):
        start = pl.multiple_of(j * tq, tq)
        new = []
        for hh in range(nh):
            m, acc = carry[hh]
            k = k_ref[0, pl.ds(start, tq), hh * LANES:(hh + 1) * LANES]
            v = v_ref[0, pl.ds(start, tq), hh * LANES:(hh + 1) * LANES]
            s = lax.dot_general(qs[hh], k, _NT, preferred_element_type=f32)
            if diag:
                s = jnp.where(col <= row, s, NEG)
            m_new = jnp.maximum(m, jnp.max(s, axis=-1, keepdims=True))
            p = jnp.exp((s - m_new).astype(bf16))
            acc = jnp.exp(m - m_new) * acc + jnp.dot(p, v, preferred_element_type=f32)
            new.append((m_new, acc))
        return tuple(new)

    carry = tuple((jnp.full((tq, 1), NEG, f32), jnp.zeros((tq, LANES), f32)) for _ in range(nh))
    carry = tile(i, carry, True)
    carry = _loop_tiles(i, lambda j, c: tile(j, c, False), carry)
    o_ref[0] = jnp.concatenate([_pair_out(carry[2 * pr][1], carry[2 * pr + 1][1]) for pr in range(nh // 2)],
                               axis=1).astype(bf16)


def _fox(fq, fk, fv, B, S):
    tq = min(FOX_TILE, S)
    nq = S // tq
    nh = FOX_STEP_HEADS
    q3 = fq.reshape(B, S, FOX_HEADS * LANES)
    k3 = fk.reshape(B, S, FOX_HEADS * LANES)
    v3 = fv.reshape(B, S, FOX_HEADS * LANES)
    return pl.pallas_call(
        functools.partial(_fox_kernel, tq=tq, nh=nh),
        out_shape=jax.ShapeDtypeStruct((B, S, FOX_WIDTH), bf16),
        grid=(B, FOX_HEADS // nh, nq),
        in_specs=[pl.BlockSpec((1, tq, nh * LANES), lambda b, hp, i: (b, i, hp)),
                  pl.BlockSpec((1, S, nh * LANES), lambda b, hp, i: (b, 0, hp)),
                  pl.BlockSpec((1, S, nh * LANES), lambda b, hp, i: (b, 0, hp))],
        out_specs=pl.BlockSpec((1, tq, nh // 2 * LANES), lambda b, hp, i: (b, i, hp)),
        compiler_params=_cparams("parallel", "parallel", "arbitrary"),
        name="fox",
    )(q3, k3, v3)


def _compress_kernel(x_ref, pea_ref, peb_ref, wa_ref, wb_ref, okc, ovc, *, nc):
    x = x_ref[0].astype(f32)
    xa = (x + pea_ref[...]).astype(bf16)
    xb = (x + peb_ref[...]).astype(bf16)
    a = jnp.dot(xa, wa_ref[...], preferred_element_type=f32)
    b = jnp.dot(xb, wb_ref[...], preferred_element_type=f32)
    out = a + pltpu.roll(b, nc - 1, 0)
    okc[0] = out[:, :2 * LANES].astype(bf16)
    ovc[0] = out[:, 2 * LANES:].astype(bf16)


def _compress(cm, pe_k, pe_v, w_cmp_k, w_cmp_v, B, S):
    nc = S // CMP_STRIDE
    half = CMP_BLOCK // 2
    win = half * 2 * LANES
    x = cm.reshape(B, nc, win)
    wk = w_cmp_k.reshape(CMP_BLOCK, HEAD_DIM, HEAD_DIM)
    wv = w_cmp_v.reshape(CMP_BLOCK, HEAD_DIM, HEAD_DIM)
    H = HEAD_DIM

    def build(wk_h, wv_h):
        z = jnp.zeros((half, H, H), f32)

        def rows(cols):
            return jnp.concatenate([cols.get(c, z) for c in range(12)], axis=2)

        w = jnp.concatenate([rows({1: wk_h}),
                             rows({3: wk_h}),
                             rows({4: wv_h, 7: wv_h}),
                             rows({8: wv_h, 11: wv_h})],
                            axis=1)
        return w.reshape(win, 6 * LANES).astype(bf16)

    wa = build(wk[:half], wv[:half])
    wb = build(wk[half:], wv[half:])

    def pe_row(pk, pv):
        return jnp.concatenate([pk, pk, pv, pv], axis=1).reshape(1, win)

    pea = pe_row(pe_k[:half], pe_v[:half])
    peb = pe_row(pe_k[half:], pe_v[half:])
    return pl.pallas_call(
        functools.partial(_compress_kernel, nc=nc),
        out_shape=[jax.ShapeDtypeStruct((B, nc, 2 * LANES), bf16),
                   jax.ShapeDtypeStruct((B, nc, 4 * LANES), bf16)],
        grid=(B,),
        in_specs=[pl.BlockSpec((1, nc, win), lambda b: (b, 0, 0)),
                  _const_spec((1, win)), _const_spec((1, win)),
                  _const_spec((win, 6 * LANES)), _const_spec((win, 6 * LANES))],
        out_specs=[pl.BlockSpec((1, nc, 2 * LANES), lambda b: (b, 0, 0)),
                   pl.BlockSpec((1, nc, 4 * LANES), lambda b: (b, 0, 0))],
        compiler_params=_cparams("parallel"),
        name="compress",
    )(x, pea, peb, wa, wb)


def _cmpsel_kernel(q_ref, kc_ref, vc_ref, pat_ref, ov_ref, ocmp, osel, *, tq, nc, n_sel, top_n, past):
    i = pl.program_id(1)
    kc = kc_ref[0]
    c0 = i * (tq // CMP_STRIDE) - past
    wio = lax.broadcasted_iota(i32, (LANES, nc), 0)
    cio = lax.broadcasted_iota(i32, (LANES, nc), 1)
    shift = jnp.where(cio == wio + c0, 1.0, 0.0).astype(bf16)
    t = i * tq + lax.broadcasted_iota(i32, (tq, nc), 0)
    cend = lax.broadcasted_iota(i32, (tq, nc), 1) * CMP_STRIDE + (CMP_BLOCK - 1)
    valid = cend <= t
    pcs = jnp.zeros((tq, nc), f32)
    outs = []
    for hh in range(NSA_HPG):
        q = q_ref[0, :, hh * LANES:(hh + 1) * LANES]
        pat = pat_ref[hh]
        pat_hi = pat.astype(bf16)
        pat_lo = (pat - pat_hi.astype(f32)).astype(bf16)
        cb = (jnp.dot(pat_hi, shift, preferred_element_type=f32)
              + jnp.dot(pat_lo, shift, preferred_element_type=f32))
        lc = jnp.where(valid, lax.dot_general(q, kc, _NT, preferred_element_type=f32) + cb, NEG)
        m = jnp.max(lc, axis=-1, keepdims=True)
        p = jnp.where(valid, jnp.exp(lc - m), 0.0)
        l = jnp.sum(p, axis=-1, keepdims=True)
        pc = p * jnp.where(l > 0.0, 1.0 / l, 0.0)
        pcs = pcs + pc
        v = vc_ref[0, :, (hh % 2) * LANES:(hh % 2 + 1) * LANES]
        outs.append(jnp.dot(pc.astype(bf16), v, preferred_element_type=f32))
    ocmp[0] = jnp.concatenate([outs[0] + outs[1], outs[2] + outs[3]], axis=1).astype(bf16)

    hi = pcs.astype(bf16)
    lo = (pcs - hi.astype(f32)).astype(bf16)
    ov = ov_ref[...]
    imp = (lax.dot_general(ov, hi, _NT, preferred_element_type=f32)
           + lax.dot_general(ov, lo, _NT, preferred_element_type=f32))
    jio = lax.broadcasted_iota(i32, (n_sel, tq), 0)
    t = i * tq + lax.broadcasted_iota(i32, (n_sel, tq), 1)
    cur = t // SEL_BLOCK
    forced = (jio == 0) | (jio == cur) | (jio == cur - 1)
    score = jnp.where(forced, BIG, jnp.where(jio <= cur, imp, -BIG))
    picked = jnp.zeros((n_sel, tq), f32)
    jf = jio.astype(f32)
    for _ in range(top_n):
        best = jnp.max(score, axis=0, keepdims=True)
        jbest = jnp.min(jnp.where(score == best, jf, float(n_sel)), axis=0, keepdims=True)
        hit = jf == jbest
        picked = jnp.where(hit, 1.0, picked)
        score = jnp.where(hit, NEG, score)
    selb = jnp.where(picked > 0.0, 0.0, -SEL_MASK)
    padded = jnp.concatenate([selb, jnp.zeros((LANES - n_sel, tq), f32)], axis=0)
    osel[0, 0] = padded.T.astype(bf16)


def _bucket_bounds():
    n = np.arange(0, 4 * REL_MAX_DIST)
    max_exact = REL_BUCKETS // 2
    nf = np.maximum(n, 1).astype(np.float32)
    large = max_exact + (np.log(nf / np.float32(max_exact)) / np.float32(np.log(REL_MAX_DIST / max_exact))
                         * np.float32(REL_BUCKETS - max_exact)).astype(np.int32)
    bucket = np.where(n < max_exact, n, np.minimum(large, REL_BUCKETS - 1))
    return [int(n[bucket > b].min()) for b in range(REL_BUCKETS - 1)]


_BOUNDS = _bucket_bounds()
_CMP_PAST = (_BOUNDS[-1] + CMP_BLOCK - 1 + CMP_STRIDE - 1) // CMP_STRIDE - 1


def _rel_bias_of(d, rb_ref, h):
    far = rb_ref[(REL_BUCKETS - 1) * NSA_HEADS + h]
    v = jnp.zeros(d.shape, f32)
    for b in reversed(range(REL_BUCKETS - 1)):
        v = jnp.where(d < _BOUNDS[b], rb_ref[b * NSA_HEADS + h] - far, v)
    return v


def _biasgen_kernel(rb_ref, oslc, owin, ocmp, *, ts, tw, tc, nk):
    h = pl.program_id(0)

    def tile(t, off, window):
        lo, hi = off - (t - 1), off + (t - 1)
        if hi < 0 or (window is not None and lo >= window):
            return jnp.full((t, t), NEG, f32)
        d = lax.broadcasted_iota(i32, (t, t), 0) - lax.broadcasted_iota(i32, (t, t), 1) + off
        val = _rel_bias_of(d, rb_ref, h) if lo < _BOUNDS[-1] else jnp.zeros((t, t), f32)
        if lo < 0:
            val = jnp.where(d >= 0, val, NEG)
        if window is not None and hi >= window:
            val = jnp.where(d < window, val, NEG)
        return val

    oslc[0, 0, 0] = tile(ts, 0, None)
    oslc[0, 1, 0] = tile(ts, ts, None)
    oslc[0, 2, 0] = jnp.full((ts, ts), NEG, f32)
    for v in range(nk):
        for cc in range(nk):
            owin[0, v, 0, :, cc * tw:(cc + 1) * tw] = tile(tw, (v - cc) * tw, WINDOW)
    rr = lax.broadcasted_iota(i32, (tc, LANES), 0)
    w = lax.broadcasted_iota(i32, (tc, LANES), 1)
    d = rr - CMP_STRIDE * (w - _CMP_PAST) - (CMP_BLOCK - 1)
    ocmp[0] = jnp.where(d >= 0, _rel_bias_of(d, rb_ref, h), 0.0)


def _biasgen(rel_bias, S):
    ts, tw, tc = min(SLC_TILE, S), min(WIN_TILE, S), min(CMP_TILE, S)
    nk = WINDOW // tw + 1
    assert WINDOW % tw == 0 and S >= nk * tw and min(ts, tw) + 1 >= _BOUNDS[-1]
    assert tc // CMP_STRIDE + _CMP_PAST <= LANES
    G = NSA_KV_GROUPS
    return pl.pallas_call(
        functools.partial(_biasgen_kernel, ts=ts, tw=tw, tc=tc, nk=nk),
        out_shape=[jax.ShapeDtypeStruct((G, 3, NSA_HPG, ts, ts), f32),
                   jax.ShapeDtypeStruct((G, nk, NSA_HPG, tw, nk * tw), f32),
                   jax.ShapeDtypeStruct((NSA_HEADS, tc, LANES), f32)],
        grid=(NSA_HEADS,),
        in_specs=[pl.BlockSpec(memory_space=pltpu.SMEM)],
        out_specs=[pl.BlockSpec((1, 3, 1, ts, ts), lambda h: (h // NSA_HPG, 0, h % NSA_HPG, 0, 0)),
                   pl.BlockSpec((1, nk, 1, tw, nk * tw), lambda h: (h // NSA_HPG, 0, h % NSA_HPG, 0, 0)),
                   pl.BlockSpec((1, tc, LANES), lambda h: (h, 0, 0))],
        compiler_params=_cparams("arbitrary"),
        name="biasgen",
    )(rel_bias.reshape(-1))


def _cmpsel(nq_arr, kc, vc, pat, B, S):
    tq = min(CMP_TILE, S)
    nqt = S // tq
    nc = S // CMP_STRIDE
    n_sel = S // SEL_BLOCK
    top_n = min(SEL_TOPN, n_sel)
    G = NSA_KV_GROUPS
    assert n_sel <= HEAD_DIM
    c = np.arange(nc)[None, :]
    j = np.arange(n_sel)[:, None]
    ov = ((c * CMP_STRIDE < j * SEL_BLOCK + SEL_BLOCK) & (c * CMP_STRIDE + CMP_BLOCK > j * SEL_BLOCK)
          & (c < nc - 1)).astype(np.float32)
    q3 = nq_arr.reshape(B, S, NSA_HEADS * LANES)
    return pl.pallas_call(
        functools.partial(_cmpsel_kernel, tq=tq, nc=nc, n_sel=n_sel, top_n=top_n, past=_CMP_PAST),
        out_shape=[jax.ShapeDtypeStruct((B, S, NSA_HEADS * HEAD_DIM), bf16),
                   jax.ShapeDtypeStruct((B, G, S, LANES), bf16)],
        grid=(G, nqt, B),
        in_specs=[pl.BlockSpec((1, tq, NSA_HPG * LANES), lambda g, i, b: (b, i, g)),
                  pl.BlockSpec((1, nc, LANES), lambda g, i, b: (b, 0, g)),
                  pl.BlockSpec((1, nc, 2 * LANES), lambda g, i, b: (b, 0, g)),
                  pl.BlockSpec((NSA_HPG, tq, LANES), lambda g, i, b: (g, 0, 0)),
                  _const_spec((n_sel, nc))],
        out_specs=[pl.BlockSpec((1, tq, 2 * LANES), lambda g, i, b: (b, i, g)),
                   pl.BlockSpec((1, 1, tq, LANES), lambda g, i, b: (b, g, i, 0))],
        compiler_params=_cparams("parallel", "arbitrary", "arbitrary"),
        name="cmpsel",
    )(q3, kc, vc, pat, jnp.asarray(ov, bf16))


def _pair_out(acc_e, acc_o):
    lane = lax.broadcasted_iota(i32, acc_e.shape, 1)
    return jnp.where(lane < HEAD_DIM, acc_e / acc_e[:, HEAD_DIM:HEAD_DIM + 1], acc_o / acc_o[:, 0:1])


def _slc_kernel(q_ref, sb_ref, k_ref, v_ref, bias_ref, o_ref, *, tq):
    i = pl.program_id(2)
    sb = sb_ref[0, 0]
    qs = [q_ref[0, :, h * LANES:(h + 1) * LANES] + sb for h in range(NSA_HPG)]

    def tile(j, carry, bias_idx):
        start = pl.multiple_of(j * tq, tq)
        k = k_ref[0, pl.ds(start, tq), :]
        v = v_ref[0, pl.ds(start, tq), :]
        new = []
        for h in range(NSA_HPG):
            m, acc = carry[h]
            s = lax.dot_general(qs[h], k, _NT, preferred_element_type=f32)
            if bias_idx is not None:
                s = s + bias_ref[0, bias_idx, h]
            m_new = jnp.maximum(m, jnp.max(s, axis=-1, keepdims=True))
            p = jnp.exp((s - m_new).astype(bf16))
            vh = v[:, (h % 2) * LANES:(h % 2 + 1) * LANES]
            acc = jnp.exp(m - m_new) * acc + jnp.dot(p, vh, preferred_element_type=f32)
            new.append((m_new, acc))
        return tuple(new)

    carry = tuple((jnp.full((tq, 1), NEG, f32), jnp.zeros((tq, LANES), f32)) for _ in range(NSA_HPG))
    carry = tile(i, carry, 0)
    carry = tile(jnp.maximum(i - 1, 0), carry, jnp.where(i >= 1, 1, 2))
    carry = _loop_tiles(jnp.maximum(i - 1, 0), lambda j, c: tile(j, c, None), carry)
    o_ref[0] = jnp.concatenate([_pair_out(carry[0][1], carry[1][1]),
                                _pair_out(carry[2][1], carry[3][1])], axis=1).astype(bf16)


def _win_kernel(q_ref, k_ref, v_ref, bias_ref, o_ref, *, tq, nk):
    i = pl.program_id(2)
    start = pl.multiple_of(jnp.maximum(i - (nk - 1), 0) * tq, tq)
    k = k_ref[0, pl.ds(start, nk * tq), :]
    v = v_ref[0, pl.ds(start, nk * tq), :]
    accs = []
    for h in range(NSA_HPG):
        q = q_ref[0, :, h * LANES:(h + 1) * LANES]
        s = lax.dot_general(q, k, _NT, preferred_element_type=f32) + bias_ref[0, 0, h]
        p = jnp.exp((s - jnp.max(s, axis=-1, keepdims=True)).astype(bf16))
        accs.append(jnp.dot(p, v[:, (h % 2) * LANES:(h % 2 + 1) * LANES], preferred_element_type=f32))
    o_ref[0] = jnp.concatenate([_pair_out(accs[0], accs[1]), _pair_out(accs[2], accs[3])], axis=1).astype(bf16)


def _nsa_flash(kind, nq_arr, selb, k_arr, v_arr, bias, B, S):
    tq = bias.shape[3]
    nqt = S // tq
    G = NSA_KV_GROUPS
    q3 = nq_arr.reshape(B, S, NSA_HEADS * LANES)
    k3 = k_arr.reshape(B, S, G * LANES)
    v3 = v_arr.reshape(B, S, G * 2 * LANES)
    q_spec = pl.BlockSpec((1, tq, NSA_HPG * LANES), lambda g, b, i: (b, i, g))
    k_spec = pl.BlockSpec((1, S, LANES), lambda g, b, i: (b, 0, g))
    v_spec = pl.BlockSpec((1, S, 2 * LANES), lambda g, b, i: (b, 0, g))
    if kind == "slc":
        kern = functools.partial(_slc_kernel, tq=tq)
        extra_specs = [pl.BlockSpec((1, 1, tq, LANES), lambda g, b, i: (b, g, i, 0))]
        extra = [selb]
        b_spec = pl.BlockSpec((1, 3, NSA_HPG, tq, tq), lambda g, b, i: (g, 0, 0, 0, 0),
                              pipeline_mode=pl.Buffered(1))
    else:
        nk = bias.shape[1]
        kern = functools.partial(_win_kernel, tq=tq, nk=nk)
        extra_specs, extra = [], []
        b_spec = pl.BlockSpec((1, 1, NSA_HPG, tq, nk * tq), lambda g, b, i: (g, jnp.minimum(i, nk - 1), 0, 0, 0))
    return pl.pallas_call(
        kern,
        out_shape=jax.ShapeDtypeStruct((B, S, NSA_HEADS * HEAD_DIM), bf16),
        grid=(G, B, nqt),
        in_specs=[q_spec] + extra_specs + [k_spec, v_spec, b_spec],
        out_specs=pl.BlockSpec((1, tq, 2 * LANES), lambda g, b, i: (b, i, g)),
        compiler_params=_cparams("parallel", "parallel", "arbitrary"),
        name=kind,
    )(q3, *extra, k3, v3, bias)


def _post_kernel(x_ref, ofox, ocmp, oslc, owin, mg_ref, sm_ref, ga1, sc2, sh2, gpost, gpre,
                 wfp, wnp_, wmo, wr, br, eg, tris,
                 x1_ref, h2_ref, route_ref, cnt_ref, carry_ref, *, tm):
    i = pl.program_id(0)
    W = NSA_WIDTH
    n = tm

    def slab(r0):
        rs = pl.ds(r0, n)
        gates = jax.nn.sigmoid(sm_ref[rs, :]).astype(bf16)
        gx = jnp.dot(gates, eg[...], preferred_element_type=f32)
        nsa = (gx[:, :W] * ocmp[rs, :].astype(f32) + gx[:, W:2 * W] * oslc[rs, :].astype(f32)
               + gx[:, 2 * W:] * owin[rs, :].astype(f32))
        y_nsa = jnp.dot(nsa.astype(bf16), wnp_[...], preferred_element_type=f32)
        y_fox = jnp.dot(ofox[rs, :], wfp[...], preferred_element_type=f32)
        mg = mg_ref[rs, :].astype(f32)
        mix = (mg[:, :D_MODEL] * y_fox + mg[:, D_MODEL:] * y_nsa).astype(bf16)
        mixed = jnp.dot(mix, wmo[...], preferred_element_type=f32)
        x1 = x_ref[rs, :] + ga1[0] * _rms(mixed, gpost[...])
        x1_ref[rs, :] = x1
        h2 = _rms(x1, gpre[...]) * (1.0 + sc2[0]) + sh2[0]
        _store_tile_rows(h2_ref.at[pl.ds(r0 * ROW_TILES, n * ROW_TILES), :], h2)

        lane = lax.broadcasted_iota(i32, (n, LANES), 1)
        logits = jnp.dot(h2.astype(bf16), wr[...], preferred_element_type=f32) + br[...]
        l = jnp.where(lane < N_EXPERTS, logits, NEG)
        vals, idxs = [], []
        for _ in range(TOP_K):
            m = jnp.max(l, axis=-1, keepdims=True)
            idx = jnp.min(jnp.where(l == m, lane, LANES), axis=-1, keepdims=True)
            vals.append(m)
            idxs.append(idx)
            l = jnp.where(lane == idx, NEG, l)
        es = [jnp.exp(v - vals[0]) for v in vals]
        den = es[0] + es[1] + es[2] + es[3]
        hot = [lane == idx for idx in idxs]
        cnt = sum(jnp.where(h, 1.0, 0.0) for h in hot)
        base = jnp.dot(tris[...], cnt.astype(bf16), preferred_element_type=f32) + carry_ref[0:1, :]
        route = jnp.zeros((n, LANES), f32)
        for k in range(TOP_K):
            pos = jnp.sum(jnp.where(hot[k], base, 0.0), axis=-1, keepdims=True)
            route = jnp.where(lane == k, idxs[k].astype(f32), route)
            route = jnp.where(lane == TOP_K + k, pos, route)
            route = jnp.where(lane == 2 * TOP_K + k, es[k] / den, route)
        route_ref[rs, :] = route
        return jnp.sum(cnt, axis=0, keepdims=True)

    @pl.when(i == 0)
    def _():
        carry_ref[...] = jnp.zeros_like(carry_ref)

    new_carry = carry_ref[0:1, :] + slab(0)
    carry_ref[0:1, :] = new_carry
    cnt_ref[...] = jnp.broadcast_to(new_carry, cnt_ref.shape)


def _post(x2, ofox, ocmp, oslc, owin, mg, sm, ga1, sc2, sh2, g_post, g_pre2,
          w_fox_proj, w_nsa_proj, w_mix_out, w_router, b_router, S):
    T, D = x2.shape
    tm = min(ROW_TILE, S)
    tiles_per_seq = S // tm
    W = NSA_WIDTH
    eg = np.zeros((LANES, 3 * W), np.float32)
    for h in range(NSA_HEADS):
        for k in range(3):
            eg[FOX_HEADS + 3 * h + k, k * W + h * HEAD_DIM:k * W + (h + 1) * HEAD_DIM] = 1.0
    wr = jnp.concatenate([w_router, jnp.zeros((D, LANES - N_EXPERTS), f32)], axis=1).astype(bf16)
    br = jnp.concatenate([b_router, jnp.zeros((LANES - N_EXPERTS,), f32)]).reshape(1, LANES)
    row = lambda n: pl.BlockSpec((tm, n), lambda i: (i, 0))
    mod = pl.BlockSpec((1, 1, D), lambda i: (i // tiles_per_seq, 0, 0))
    consts = [w_fox_proj.astype(bf16), w_nsa_proj.astype(bf16), w_mix_out.astype(bf16), wr, br,
              jnp.asarray(eg, bf16), jnp.asarray(np.tril(np.ones((tm, tm), np.float32), -1), bf16)]
    return pl.pallas_call(
        functools.partial(_post_kernel, tm=tm),
        out_shape=[jax.ShapeDtypeStruct((T, D), f32), jax.ShapeDtypeStruct((T * ROW_TILES, LANES), f32),
                   jax.ShapeDtypeStruct((T, LANES), f32), jax.ShapeDtypeStruct((8, LANES), f32)],
        grid=(T // tm,),
        in_specs=[row(D), row(FOX_WIDTH), row(W), row(W), row(W), row(2 * D), row(LANES),
                  mod, mod, mod, _const_spec((1, D)), _const_spec((1, D))]
        + [_const_spec(c.shape) for c in consts],
        out_specs=[row(D), pl.BlockSpec((tm * ROW_TILES, LANES), lambda i: (i, 0)), row(LANES),
                   pl.BlockSpec((8, LANES), lambda i: (0, 0))],
        scratch_shapes=[pltpu.VMEM((8, LANES), f32)],
        compiler_params=_cparams("arbitrary"),
        name="post",
    )(x2, ofox, ocmp, oslc, owin, mg, sm, ga1, sc2, sh2, g_post.reshape(1, D), g_pre2.reshape(1, D), *consts)


def _dispatch_kernel(plo_ref, phi_ref, dest_ref, h_ref, xs_hbm, zbuf, sem, zsem, *, tm):
    i = pl.program_id(0)
    rt = ROW_TILES
    for r in range(tm):
        for k in range(TOP_K):
            a = r * TOP_K + k
            pltpu.make_async_copy(h_ref.at[pl.ds(r * rt, rt), :], xs_hbm.at[dest_ref[0, 0, a]], sem).start(priority=a % 2)
    for _ in range(TOP_K):
        pltpu.make_async_copy(h_ref, h_ref, sem).wait()

    @pl.when(i == pl.num_programs(0) - 1)
    def _():
        zbuf[...] = jnp.zeros(zbuf.shape, f32)
        bm = zbuf.shape[0]
        first_free = phi_ref[N_EXPERTS - 1] // bm
        n_blocks = xs_hbm.shape[0] // bm

        def fill(e, c):
            lax.fori_loop(plo_ref[e], phi_ref[e],
                          lambda p, cc: (pltpu.make_async_copy(zbuf.at[0], xs_hbm.at[p], zsem).start(), cc)[1], 0)
            return c

        def drain(e, c):
            lax.fori_loop(plo_ref[e], phi_ref[e],
                          lambda p, cc: (pltpu.make_async_copy(zbuf.at[0], zbuf.at[0], zsem).wait(), cc)[1], 0)
            return c

        lax.fori_loop(0, N_EXPERTS, fill, 0)
        lax.fori_loop(0, N_EXPERTS, drain, 0)
        lax.fori_loop(first_free, n_blocks,
                      lambda b, cc: (pltpu.make_async_copy(zbuf, xs_hbm.at[pl.ds(b * bm, bm)], zsem).start(), cc)[1], 0)
        lax.fori_loop(first_free, n_blocks,
                      lambda b, cc: (pltpu.make_async_copy(zbuf, zbuf, zsem).wait(), cc)[1], 0)


def _dispatch(pad_lo, pad_hi, dest, h2t, n_rows):
    tm = DISPATCH_ROWS
    nt = dest.shape[0]
    return pl.pallas_call(
        functools.partial(_dispatch_kernel, tm=tm),
        out_shape=jax.ShapeDtypeStruct((n_rows, ROW_TILES, LANES), f32),
        grid_spec=pltpu.PrefetchScalarGridSpec(
            num_scalar_prefetch=2, grid=(nt,),
            in_specs=[pl.BlockSpec((1, 1, tm * TOP_K), lambda i, lo, hi: (i, 0, 0), memory_space=pltpu.SMEM),
                      pl.BlockSpec((tm * ROW_TILES, LANES), lambda i, lo, hi: (i, 0))],
            out_specs=pl.BlockSpec(memory_space=pl.ANY),
            scratch_shapes=[pltpu.VMEM((MOE_BM, ROW_TILES, LANES), f32), pltpu.SemaphoreType.DMA,
                            pltpu.SemaphoreType.DMA]),
        compiler_params=_cparams("arbitrary"),
        name="dispatch",
    )(pad_lo, pad_hi, dest, h2t)


def _moe_kernel(be_ref, nu_ref, dstp_ref, xs_ref, wg, bg, wu, bu, wd, bd,
                y_hbm, ybuf, wgb, wub, wdb, ssem, *, bm):
    i = pl.program_id(0)
    nu = nu_ref[0]
    rt = ROW_TILES

    def rows(r):
        return pl.ds(r * rt, rt) if isinstance(r, int) else pl.ds(pl.multiple_of(r * rt, rt), rt)

    def scatter_copy(r, s):
        return pltpu.make_async_copy(ybuf.at[s, rows(r), :], y_hbm.at[dstp_ref[0, 0, r]], ssem.at[s])

    def wait_scatter(s):
        pltpu.make_async_copy(ybuf.at[s], ybuf.at[s], ssem.at[s]).wait()

    @pl.when(i == 0)
    def _():
        ybuf[1] = jnp.zeros((bm * rt, LANES), f32)
        n_real = y_hbm.shape[0] - 2 * bm

        def issue(r, c):
            for half in range(2):
                pltpu.make_async_copy(ybuf.at[1, rows(r), :], y_hbm.at[n_real + half * bm + r],
                                      ssem.at[1]).start()
            return c

        lax.fori_loop(0, bm, issue, 0)
        for half in range(2):
            wait_scatter(1)

    def step(slot):
        other = 1 - slot

        @pl.when(i >= 1)
        def _():
            wait_scatter(slot)

        @pl.when(jnp.logical_or(i == 0, be_ref[i] != be_ref[jnp.maximum(i - 1, 0)]))
        def _():
            wgb[...] = wg[0].astype(bf16)
            wub[...] = wu[0].astype(bf16)
            wdb[...] = wd[0].astype(bf16)

        for r in range(bm):
            scatter_copy(r, other).start(priority=r % 2)
        x = _load_tile_rows(xs_ref, bm).astype(bf16)
        g = jnp.dot(x, wgb[...], preferred_element_type=f32) + bg[0]
        u = jnp.dot(x, wub[...], preferred_element_type=f32) + bu[0]
        gt = jnp.minimum(g, SWIGLU_LIMIT)
        up = jnp.clip(u, -SWIGLU_LIMIT, SWIGLU_LIMIT)
        a = (gt * jax.nn.sigmoid(SWIGLU_ALPHA * gt) * (up + 1.0)).astype(bf16)
        _store_tile_rows(ybuf.at[slot], jnp.dot(a, wdb[...], preferred_element_type=f32) + bd[0])

    def drain(slot):
        other = 1 - slot
        wait_scatter(slot)

        def issue(r, c):
            scatter_copy(r, other).start()
            return c

        lax.fori_loop(0, bm, issue, 0)
        wait_scatter(other)

    for s in range(2):
        pl.when(jnp.logical_and(i < nu, i % 2 == s))(functools.partial(step, s))
        pl.when(jnp.logical_and(i == nu, i % 2 == s))(functools.partial(drain, s))


def _moe(blk_e, n_used, row_dst, xs, w_gate, b_gate, w_up, b_up, w_down, b_down, n_rows):
    D = D_MODEL
    E, _, F = w_gate.shape
    nbt = row_dst.shape[0]
    bm = MOE_BM
    wsel = lambda i, be, nu: (be[jnp.minimum(i, nu[0] - 1)], 0, 0)
    idx_spec = lambda f: pl.BlockSpec((1, 1, bm), lambda i, be, nu: (f(i), 0, 0), memory_space=pltpu.SMEM)
    return pl.pallas_call(
        functools.partial(_moe_kernel, bm=bm),
        out_shape=jax.ShapeDtypeStruct((n_rows, ROW_TILES, LANES), f32),
        grid_spec=pltpu.PrefetchScalarGridSpec(
            num_scalar_prefetch=2, grid=(nbt,),
            in_specs=[idx_spec(lambda i: jnp.maximum(i - 1, 0)),
                      pl.BlockSpec((bm * ROW_TILES, LANES), lambda i, be, nu: (jnp.minimum(i, nu[0] - 1), 0)),
                      pl.BlockSpec((1, D, F), wsel), pl.BlockSpec((1, 1, F), wsel),
                      pl.BlockSpec((1, D, F), wsel), pl.BlockSpec((1, 1, F), wsel),
                      pl.BlockSpec((1, F, D), wsel), pl.BlockSpec((1, 1, D), wsel)],
            out_specs=pl.BlockSpec(memory_space=pl.ANY),
            scratch_shapes=[pltpu.VMEM((2, bm * ROW_TILES, LANES), f32),
                            pltpu.VMEM((D, F), bf16), pltpu.VMEM((D, F), bf16), pltpu.VMEM((F, D), bf16),
                            pltpu.SemaphoreType.DMA((2,))]),
        compiler_params=_cparams("arbitrary"),
        name="moe",
    )(blk_e, n_used, row_dst, xs.reshape(-1, LANES),
      w_gate, b_gate.reshape(E, 1, F), w_up, b_up.reshape(E, 1, F), w_down, b_down.reshape(E, 1, D))


def _combine_kernel(y0, y1, y2, y3, route_ref, x1_ref, ga2, gpost, o_ref):
    route = route_ref[...]
    tm = o_ref.shape[0]
    y = jnp.zeros(o_ref.shape, f32)
    for k, yk in enumerate((y0, y1, y2, y3)):
        y = y + route[:, 2 * TOP_K + k:2 * TOP_K + k + 1] * _load_tile_rows(yk, tm)
    o_ref[...] = x1_ref[...] + ga2[0] * _rms(y, gpost[...])


def _combine(y4, route, x1, ga2, g_post2, S):
    T, D = x1.shape
    tm = min(MOE_ROWS, S)
    tiles_per_seq = S // tm
    nt = T // tm
    y4 = y4.reshape(-1, LANES)
    y_spec = lambda k: pl.BlockSpec((tm * ROW_TILES, LANES), lambda i: (k * nt + i, 0))
    return pl.pallas_call(
        _combine_kernel,
        out_shape=jax.ShapeDtypeStruct((T, D), f32),
        grid=(nt,),
        in_specs=[y_spec(k) for k in range(TOP_K)]
        + [pl.BlockSpec((tm, LANES), lambda i: (i, 0)),
           pl.BlockSpec((tm, D), lambda i: (i, 0)),
           pl.BlockSpec((1, 1, D), lambda i: (i // tiles_per_seq, 0, 0)),
           pl.BlockSpec((1, D), lambda i: (0, 0))],
        out_specs=pl.BlockSpec((tm, D), lambda i: (i, 0)),
        compiler_params=_cparams("parallel"),
        name="combine",
    )(y4, y4, y4, y4, route, x1, ga2, g_post2.reshape(1, D))


def kernel(x, c, w_ada, b_ada, g_mix_pre, g_mix_post, w_in, b_forget, pe_k, pe_v, w_cmp_k, w_cmp_v, w_fox_proj, w_nsa_proj, w_mix_out, rel_bias, g_ffn_pre, g_ffn_post, w_router, b_router, w_gate, b_gate, w_up, b_up, w_down, b_down):
    B, S, D = x.shape
    T = B * S
    for l in range(w_ada.shape[0]):
        x2 = x.reshape(T, D)
        ada = _ada(c, w_ada[l], b_ada[l])
        sh1, sc1, ga1, sh2, sc2, ga2 = [a.reshape(B, 1, D) for a in jnp.split(ada, 6, axis=-1)]
        fq, fk, fv, nq, cm, ksl, vsl, kwn, vwn, mg, sm = _inproj(x2, sc1, sh1, g_mix_pre[l], w_in[l], b_forget[l], S)
        o_fox = _fox(fq, fk, fv, B, S)
        kc, vc = _compress(cm, pe_k[l], pe_v[l], w_cmp_k[l], w_cmp_v[l], B, S)
        bias_slc, bias_win, pat_cmp = _biasgen(rel_bias, S)
        o_cmp, selb = _cmpsel(nq, kc, vc, pat_cmp, B, S)
        o_slc = _nsa_flash("slc", nq, selb, ksl, vsl, bias_slc, B, S)
        o_win = _nsa_flash("win", nq, None, kwn, vwn, bias_win, B, S)
        x1, h2, route, cnt = _post(x2, o_fox.reshape(T, -1), o_cmp.reshape(T, -1), o_slc.reshape(T, -1),
                                   o_win.reshape(T, -1), mg, sm, ga1, sc2, sh2, g_mix_post[l], g_ffn_pre[l],
                                   w_fox_proj[l], w_nsa_proj[l], w_mix_out[l], w_router[l], b_router[l], S)
        counts = cnt[0, :N_EXPERTS].astype(i32)
        nblk = (counts + MOE_BM - 1) // MOE_BM
        blk_end = jnp.cumsum(nblk)
        pad_start = (blk_end - nblk) * MOE_BM
        top_i = route[:, :TOP_K].astype(i32)
        A = T * TOP_K
        nbt = -(-A // MOE_BM) + N_EXPERTS + 1
        n_used = blk_end[-1:].astype(i32)
        blk_e = jnp.minimum(jnp.sum(jnp.arange(nbt)[:, None] >= blk_end[None, :], axis=1), N_EXPERTS - 1).astype(i32)
        a_sorted = jnp.sort((top_i * A + jnp.arange(A, dtype=i32).reshape(T, TOP_K)).reshape(-1)) % A
        grp_start = jnp.cumsum(counts) - counts
        j = jnp.arange(MOE_BM, dtype=i32)[None, :]
        b = jnp.arange(nbt, dtype=i32)[:, None]
        r_in_e = b * MOE_BM + j - pad_start[blk_e][:, None]
        valid = (b < n_used[0]) & (r_in_e < counts[blk_e][:, None])
        row_a = a_sorted[jnp.clip(grp_start[blk_e][:, None] + r_in_e, 0, A - 1)]
        row_dst = jnp.where(valid, (row_a % TOP_K) * T + row_a // TOP_K, A + (b % 2) * MOE_BM + j)
        dest = pad_start[top_i] + route[:, TOP_K:2 * TOP_K].astype(i32)
        xs = _dispatch((pad_start + counts).astype(i32), (pad_start + nblk * MOE_BM).astype(i32),
                       dest.reshape(T // DISPATCH_ROWS, 1, DISPATCH_ROWS * TOP_K), h2, (nbt - 1) * MOE_BM)
        y4 = _moe(blk_e, n_used, row_dst.reshape(nbt, 1, MOE_BM), xs,
                  w_gate[l], b_gate[l], w_up[l], b_up[l], w_down[l], b_down[l], A + 2 * MOE_BM)
        x = _combine(y4, route, x1, ga2, g_ffn_post[l], S).reshape(B, S, D)
    return x
```
